```python
import math
import jax
import jax.numpy as jnp
from jax import lax
import numpy as np

D_MODEL = 1024
BATCH = 8
SEQ = 2048
DEPTH = 1

ATTN_GROUPS = ((128, 1), (512, 4), (2048, 16))
N_GROUPS = len(ATTN_GROUPS)
HEADS_PER_GROUP = 4
HEAD_DIM = D_MODEL // 8
N_ATTN_HEADS = N_GROUPS * HEADS_PER_GROUP
ATTN_QKV_WIDTH = N_ATTN_HEADS * HEAD_DIM
ATTN_OUT_WIDTH = HEADS_PER_GROUP * HEAD_DIM
Q_BLOCK = 128
ALIBI_MAX_EXP = 8.0

HYENA_WIDTH = D_MODEL
HYENA_ORDER = 2
SHORT_CONV = 3
FILTER_BANDS = 16
FILTER_EMB = 1 + 2 * FILTER_BANDS
FILTER_HIDDEN = 64
FILTER_INNER = 2
N_DIR = 2
DECAY_TARGET = 1e-2
FAST_DECAY_PCT = 0.3
SLOW_DECAY_PCT = 1.5
HYENA_PROJ_WIDTH = (HYENA_ORDER + 1) * HYENA_WIDTH

N_BRANCH = 2
D_FF = 4 * D_MODEL
IN_WIDTH = 3 * ATTN_QKV_WIDTH + HYENA_PROJ_WIDTH + N_BRANCH * D_MODEL
DEEPNORM_ALPHA = (2 * DEPTH) ** 0.25
DEEPNORM_BETA = (8 * DEPTH) ** -0.25
LN_EPS = 1e-5

kernel_name = 'hybrid_dilated_attn_hyena_encoder'


def layer_norm(x, g, b):
    xf = x.astype(jnp.float32)
    mu = jnp.mean(xf, axis=-1, keepdims=True)
    var = jnp.mean(jnp.square(xf - mu), axis=-1, keepdims=True)
    y = (xf - mu) * lax.rsqrt(var + LN_EPS) * g.astype(jnp.float32) + b.astype(jnp.float32)
    return y.astype(x.dtype)


def alibi_slopes(n):
    return jnp.exp2(-ALIBI_MAX_EXP * jnp.arange(1, n + 1, dtype=jnp.float32) / n)


def dilated_window_attention(q, k, v, slopes, dilation, n_off):
    bsz, seq_len, n_heads, dh = q.shape
    stream = seq_len // dilation
    qb = math.gcd(stream, Q_BLOCK)
    nb = stream // qb
    kc = qb + 2 * n_off
    qr = q.reshape(bsz, nb, qb, dilation, n_heads, dh)
    pad = ((0, 0), (n_off, n_off), (0, 0), (0, 0), (0, 0))
    kp = jnp.pad(k.reshape(bsz, stream, dilation, n_heads, dh), pad)
    vp = jnp.pad(v.reshape(bsz, stream, dilation, n_heads, dh), pad)
    idx = np.arange(nb)[:, None] * qb + np.arange(kc)[None, :]
    kb = kp[:, idx]
    vb = vp[:, idx]
    off = np.arange(kc)[None, :] - n_off - np.arange(qb)[:, None]
    key_pos = idx - n_off
    valid = (np.abs(off) <= n_off)[None] & ((key_pos >= 0) & (key_pos < stream))[:, None, :]
    dist = jnp.asarray(np.abs(off) * dilation, jnp.float32)
    bias = -slopes[:, None, None] * dist
    scores = jnp.einsum('bnqrhd,bnkrhd->bnhrqk', qr, kb) * (1.0 / math.sqrt(dh))
    scores = scores + bias[None, None, :, None]
    scores = jnp.where(valid[None, :, None, None], scores, -jnp.inf)
    m = jnp.max(scores, axis=-1, keepdims=True)
    p = jnp.exp(scores - m)
    den = jnp.sum(p, axis=-1, keepdims=True)
    out = jnp.einsum('bnhrqk,bnkrhd->bnqrhd', p / den, vb)
    lse = (m + jnp.log(den))[..., 0]
    out = out.reshape(bsz, seq_len, n_heads, dh)
    lse = lse.transpose(0, 1, 4, 3, 2).reshape(bsz, seq_len, n_heads)
    return out, lse


def short_conv(u, w, b):
    c = u.shape[-1]
    y = lax.conv_general_dilated(u, w[:, None, :].astype(u.dtype), window_strides=(1,),
                                 padding=((SHORT_CONV // 2, SHORT_CONV // 2),),
                                 dimension_numbers=('NWC', 'WIO', 'NWC'), feature_group_count=c)
    return y + b


def hyena_filters(seq_len, w0, b0, w_inner, b_inner, w_last, freq):
    f32 = jnp.float32
    t = jnp.linspace(0.0, 1.0, seq_len, dtype=f32)[:, None]
    bands = jnp.linspace(1e-4, FILTER_BANDS - 1, FILTER_BANDS, dtype=f32)
    ang = (2.0 * math.pi / seq_len) * jnp.arange(seq_len, dtype=f32)[:, None] * bands
    feats = jnp.concatenate([t, jnp.cos(ang), -jnp.sin(ang)], axis=-1)
    freq = freq.astype(f32)
    hid = jnp.sin(freq * (feats @ w0.astype(f32) + b0.astype(f32)))
    for i in range(FILTER_INNER):
        hid = jnp.sin(freq * (hid @ w_inner[i].astype(f32) + b_inner[i].astype(f32)))
    filt = (hid @ w_last.astype(f32)).reshape(seq_len, N_DIR, HYENA_ORDER, HYENA_WIDTH)
    deltas = jnp.abs(jnp.linspace(math.log(DECAY_TARGET) / SLOW_DECAY_PCT,
                                  math.log(DECAY_TARGET) / FAST_DECAY_PCT, HYENA_WIDTH, dtype=f32))
    filt = filt * jnp.exp(-t[:, :, None, None] * deltas)
    fwd, bwd = filt[:, 0], filt[:, 1]
    two_sided = jnp.concatenate([fwd, jnp.zeros_like(fwd[:1]), bwd[:0:-1]], axis=0)
    two_sided = two_sided * lax.rsqrt(jnp.sum(jnp.square(two_sided), axis=0, keepdims=True))
    return jnp.fft.rfft(two_sided, n=2 * seq_len, axis=0)


def long_conv(z, filt_f):
    seq_len = z.shape[1]
    n = 2 * seq_len
    y = jnp.fft.irfft(jnp.fft.rfft(z, n=n, axis=1) * filt_f, n=n, axis=1)
    return y[:, :seq_len]


def hybrid_mixer(u, w_in, b_in, conv_w, conv_b, filt_w0, filt_b0, filt_w_inner, filt_b_inner,
                 filt_w_out, filt_freq, hyena_skip, w_branch_attn, w_branch_hyena, w_out):
    f32 = jnp.float32
    bsz, seq_len, _ = u.shape
    proj = jnp.einsum('bsd,de->bse', u, w_in) + b_in
    cuts = [ATTN_QKV_WIDTH, 2 * ATTN_QKV_WIDTH, 3 * ATTN_QKV_WIDTH,
            3 * ATTN_QKV_WIDTH + HYENA_PROJ_WIDTH]
    q, k, v, hy, gate_logits = jnp.split(proj, cuts, axis=-1)

    def heads(a):
        return a.astype(f32).reshape(bsz, seq_len, N_GROUPS, HEADS_PER_GROUP, HEAD_DIM)
    q, k, v = heads(q), heads(k), heads(v)
    slopes = alibi_slopes(N_ATTN_HEADS).reshape(N_GROUPS, HEADS_PER_GROUP)
    outs, lses = [], []
    for g, (window, dilation) in enumerate(ATTN_GROUPS):
        o, l = dilated_window_attention(q[:, :, g], k[:, :, g], v[:, :, g], slopes[g],
                                        dilation, (window // 2) // dilation)
        outs.append(o)
        lses.append(l)
    group_w = jax.nn.softmax(jnp.stack(lses, axis=0), axis=0)
    y_attn = jnp.einsum('gbsh,gbshd->bshd', group_w, jnp.stack(outs, axis=0))
    y_attn = y_attn.reshape(bsz, seq_len, ATTN_OUT_WIDTH).astype(u.dtype)

    parts = jnp.split(short_conv(hy, conv_w, conv_b).astype(f32), HYENA_ORDER + 1, axis=-1)
    filt_f = hyena_filters(seq_len, filt_w0, filt_b0, filt_w_inner, filt_b_inner, filt_w_out, filt_freq)
    z = parts[0]
    for o in range(HYENA_ORDER):
        z = parts[o + 1] * (long_conv(z, filt_f[:, o]) + hyena_skip[o].astype(f32) * z)
    y_hyena = z.astype(u.dtype)

    g_attn, g_hyena = jnp.split(jax.nn.sigmoid(gate_logits), N_BRANCH, axis=-1)
    merged = g_attn * (y_attn @ w_branch_attn) + g_hyena * (y_hyena @ w_branch_hyena)
    return merged @ w_out


def sqrelu_mlp(u, w1, b1, w2, b2):
    hid = jax.nn.relu(u @ w1 + b1)
    return (hid * hid) @ w2 + b2


def setup_inputs(seed: int = 0) -> dict:
    key = jax.random.key(seed)
    ks = jax.random.split(key, 23)
    f32 = jnp.float32

    def nrm(k, shape, scale):
        return scale * jax.random.normal(k, shape, f32)

    nl = DEPTH
    return {
        'x': nrm(ks[0], (BATCH, SEQ, D_MODEL), 1.0),
        'w_in': nrm(ks[1], (nl, D_MODEL, IN_WIDTH), D_MODEL ** -0.5),
        'b_in': nrm(ks[2], (nl, IN_WIDTH), 0.02),
        'conv_w': nrm(ks[3], (nl, SHORT_CONV, HYENA_PROJ_WIDTH), SHORT_CONV ** -0.5),
        'conv_b': nrm(ks[4], (nl, HYENA_PROJ_WIDTH), 0.02),
        'filt_w0': nrm(ks[5], (nl, FILTER_EMB, FILTER_HIDDEN), FILTER_EMB ** -0.5),
        'filt_b0': nrm(ks[6], (nl, FILTER_HIDDEN), 0.1),
        'filt_w_inner': nrm(ks[7], (nl, FILTER_INNER, FILTER_HIDDEN, FILTER_HIDDEN), FILTER_HIDDEN ** -0.5),
        'filt_b_inner': nrm(ks[8], (nl, FILTER_INNER, FILTER_HIDDEN), 0.1),
        'filt_w_out': nrm(ks[9], (nl, FILTER_HIDDEN, N_DIR * HYENA_ORDER * HYENA_WIDTH), FILTER_HIDDEN ** -0.5),
        'filt_freq': 1.0 + nrm(ks[10], (nl, FILTER_HIDDEN), 0.01),
        'hyena_skip': nrm(ks[11], (nl, HYENA_ORDER, HYENA_WIDTH), 0.5),
        'w_branch_attn': nrm(ks[12], (nl, ATTN_OUT_WIDTH, D_MODEL), ATTN_OUT_WIDTH ** -0.5),
        'w_branch_hyena': nrm(ks[13], (nl, HYENA_WIDTH, D_MODEL), HYENA_WIDTH ** -0.5),
        'w_out': nrm(ks[14], (nl, D_MODEL, D_MODEL), DEEPNORM_BETA * D_MODEL ** -0.5),
        'ln1_g': 1.0 + nrm(ks[15], (nl, D_MODEL), 0.01),
        'ln1_b': nrm(ks[16], (nl, D_MODEL), 0.01),
        'w_ff1': nrm(ks[17], (nl, D_MODEL, D_FF), D_MODEL ** -0.5),
        'b_ff1': nrm(ks[18], (nl, D_FF), 0.01),
        'w_ff2': nrm(ks[19], (nl, D_FF, D_MODEL), DEEPNORM_BETA * D_FF ** -0.5),
        'b_ff2': nrm(ks[20], (nl, D_MODEL), 0.01),
        'ln2_g': 1.0 + nrm(ks[21], (nl, D_MODEL), 0.01),
        'ln2_b': nrm(ks[22], (nl, D_MODEL), 0.01),
    }


def reference(x, w_in, b_in, conv_w, conv_b, filt_w0, filt_b0, filt_w_inner, filt_b_inner,
              filt_w_out, filt_freq, hyena_skip, w_branch_attn, w_branch_hyena, w_out,
              ln1_g, ln1_b, w_ff1, b_ff1, w_ff2, b_ff2, ln2_g, ln2_b):
    h = x
    for layer in range(DEPTH):
        mix = hybrid_mixer(h, w_in[layer], b_in[layer], conv_w[layer], conv_b[layer],
                           filt_w0[layer], filt_b0[layer], filt_w_inner[layer], filt_b_inner[layer],
                           filt_w_out[layer], filt_freq[layer], hyena_skip[layer],
                           w_branch_attn[layer], w_branch_hyena[layer], w_out[layer])
        h = layer_norm(DEEPNORM_ALPHA * h + mix, ln1_g[layer], ln1_b[layer])
        ff = sqrelu_mlp(h, w_ff1[layer], b_ff1[layer], w_ff2[layer], b_ff2[layer])
        h = layer_norm(DEEPNORM_ALPHA * h + ff, ln2_g[layer], ln2_b[layer])
    return h
```

```python
import functools
import math

import jax
import jax.numpy as jnp
import numpy as np
from jax import lax
from jax.experimental import pallas as pl
from jax.experimental.pallas import tpu as pltpu

F32 = jnp.float32
BF16 = jnp.bfloat16

ATTN_GROUPS = ((128, 1), (512, 4), (2048, 16))
N_GROUPS = len(ATTN_GROUPS)
HEADS_PER_GROUP = 4
HEAD_DIM = 128
GROUP_WIDTH = HEADS_PER_GROUP * HEAD_DIM
ATTN_QKV_WIDTH = N_GROUPS * GROUP_WIDTH
Q_BLOCK = 128
ALIBI_MAX_EXP = 8.0
HYENA_ORDER = 2
FILTER_BANDS = 16
FILTER_EMB = 1 + 2 * FILTER_BANDS
FILTER_HIDDEN = 64
FILTER_INNER = 2
DECAY_TARGET = 1e-2
FAST_DECAY_PCT = 0.3
SLOW_DECAY_PCT = 1.5
LN_EPS = 1e-5

LANES = 128
VMEM_LIMIT_BYTES = 56 * 1024 * 1024

FFT_N1 = 8
FFT_N2 = 512
FFT_KEPT = (0, 4, 1, 2, 3)
RSQRT2 = 1.0 / math.sqrt(2.0)


def _dot(a, b, precision=None):
    return jnp.dot(a, b, preferred_element_type=F32, precision=precision)


def _dot_nt(a, b):
    return lax.dot_general(a, b, (((1,), (1,)), ((), ())), preferred_element_type=F32)


@functools.lru_cache(maxsize=None)
def _fft_constants(seq_len):
    n_fft = 2 * seq_len
    assert n_fft == FFT_N1 * FFT_N2
    k2 = np.arange(FFT_N2)
    ang = -2.0 * np.pi * ((np.outer(k2, k2) % FFT_N2) / FFT_N2)
    fr, fi = np.cos(ang), np.sin(ang)
    dft_rows = np.concatenate([fr, fi], axis=0).astype(np.float32)
    dft_cols = np.concatenate([fr, fi], axis=1).astype(np.float32)
    tw = []
    for k1 in (1, 2, 3, 4):
        a = -2.0 * np.pi * (k1 * k2) / n_fft
        tw.append(np.broadcast_to(np.cos(a)[:, None], (FFT_N2, LANES)))
        tw.append(np.broadcast_to(np.sin(a)[:, None], (FFT_N2, LANES)))
    tw = np.stack(tw, axis=0).astype(np.float32)
    return dft_rows, dft_cols, tw


@functools.lru_cache(maxsize=None)
def _filter_constants(seq_len, width):
    n_fft = 2 * seq_len
    t = np.linspace(0.0, 1.0, seq_len)
    bands = np.linspace(1e-4, FILTER_BANDS - 1, FILTER_BANDS)
    ang = (2.0 * np.pi / seq_len) * np.arange(seq_len)[:, None] * bands
    feats = np.concatenate([t[:, None], np.cos(ang), -np.sin(ang)], axis=-1)
    src = np.concatenate([np.arange(seq_len), [0], np.arange(seq_len - 1, 0, -1)])
    feats_ext = np.zeros((n_fft, LANES), np.float32)
    feats_ext[:, :FILTER_EMB] = feats[src]
    t_mask = np.zeros((n_fft, 2), np.float32)
    t_mask[:, 0] = t[src]
    t_mask[:, 1] = 1.0
    t_mask[seq_len, 1] = 0.0
    deltas = np.abs(np.linspace(math.log(DECAY_TARGET) / SLOW_DECAY_PCT,
                                math.log(DECAY_TARGET) / FAST_DECAY_PCT, width))
    deltas = np.tile(deltas[None, :], (1, HYENA_ORDER)).astype(np.float32)
    return feats_ext, t_mask, deltas


@functools.lru_cache(maxsize=None)
def _attn_distance_tiles():
    n_off = Q_BLOCK // 2
    qi = np.arange(Q_BLOCK)[:, None]
    kj = np.arange(2 * Q_BLOCK)[None, :]
    tiles = []
    for shift in (0, n_off, 2 * n_off):
        off = np.abs(kj - shift - qi).astype(np.float32)
        tiles.append(np.where(off <= n_off, off, np.inf))
    off = np.abs(kj - qi).astype(np.float32)
    t3 = np.where(off <= n_off, off, np.inf)
    t3[:, Q_BLOCK:] = np.inf
    tiles.append(t3)
    return np.stack(tiles, axis=0).astype(np.float32)


def _alibi_slopes():
    n = N_GROUPS * HEADS_PER_GROUP
    return [2.0 ** (-ALIBI_MAX_EXP * j / n) for j in range(1, n + 1)]


def _radix8_half(z0, z1, z2, z3):
    s02, d02 = z0 + z2, z0 - z2
    s13, d13 = z1 + z3, z1 - z3
    ss, dd = s13 * RSQRT2, d13 * RSQRT2
    return {0: (s02 + s13, None), 4: (s02 - s13, None), 2: (d02, -d13),
            1: (z0 + dd, -z2 - ss), 3: (z0 - dd, z2 - ss)}


def _twiddle(tw_ref, k1, width):
    reps = width // LANES
    re = tw_ref[2 * (k1 - 1)]
    im = tw_ref[2 * (k1 - 1) + 1]
    if reps > 1:
        re = jnp.concatenate([re] * reps, axis=1)
        im = jnp.concatenate([im] * reps, axis=1)
    return re, im


def _slice_rows(k1):
    if k1 == 0:
        return 0, None
    if k1 == 4:
        return FFT_N2, None
    base = 2 * FFT_N2 * k1
    return base, base + FFT_N2


def _forward_dft(a_ref, k1, dft_rows_ref, tw_ref, width):
    n2 = FFT_N2
    r0, i0 = _slice_rows(k1)
    ar = a_ref[r0:r0 + n2, :]
    if k1 == 0:
        res = _dot(dft_rows_ref[...], ar.astype(BF16))
        return res[:n2], res[n2:]
    twr, twi = _twiddle(tw_ref, k1, width)
    if i0 is None:
        br, bi = ar * twr, ar * twi
    else:
        ai = a_ref[i0:i0 + n2, :]
        br, bi = ar * twr - ai * twi, ar * twi + ai * twr
    rhs = jnp.concatenate([br.astype(BF16), bi.astype(BF16)], axis=1)
    res = _dot(dft_rows_ref[...], rhs)
    xr = res[:n2, :width] - res[n2:, width:]
    xi = res[n2:, :width] + res[:n2, width:]
    return xr, xi


def _inverse_dft(pr, pi, k1, dft_rows_ref, dft_cols_ref, tw_ref, c_ref, width):
    n2 = FFT_N2
    r0, i0 = _slice_rows(k1)
    if k1 == 0:
        lhs = jnp.concatenate([pr.astype(BF16), pi.astype(BF16)], axis=0)
        c_ref[r0:r0 + n2, :] = _dot(dft_cols_ref[...], lhs)
        return
    rhs = jnp.concatenate([pr.astype(BF16), pi.astype(BF16)], axis=1)
    res = _dot(dft_rows_ref[...], rhs)
    qr = res[:n2, :width] + res[n2:, width:]
    qi = res[:n2, width:] - res[n2:, :width]
    twr, twi = _twiddle(tw_ref, k1, width)
    c_ref[r0:r0 + n2, :] = qr * twr + qi * twi
    if i0 is not None:
        c_ref[i0:i0 + n2, :] = qi * twr - qr * twi


FILT_CW = 256
FILT_ROWS = 256
A_CHUNK = 32


def _filter_kernel(feats_ref, tmask_ref, w0_ref, b0_ref, wi_ref, bi_ref, freq_ref, wf_ref, wb_ref,
                   delta_ref, dft_rows_ref, tw_ref, h_ref, hid_ref, ts_ref, a_ref):
    n_fft = feats_ref.shape[0]
    seq_len = n_fft // 2
    hp = lax.Precision.HIGHEST

    @pl.when(pl.program_id(0) == 0)
    def _():
        def body(i, carry):
            rows = pl.ds(pl.multiple_of(i * FILT_ROWS, FILT_ROWS), FILT_ROWS)
            freq = freq_ref[...]
            hid = jnp.sin(freq * (_dot(feats_ref[rows, :], w0_ref[...], hp) + b0_ref[...]))
            for layer in range(FILTER_INNER):
                hid = jnp.sin(freq * (_dot(hid, wi_ref[layer], hp) + bi_ref[layer]))
            hid_ref[rows, :] = hid
            return carry
        lax.fori_loop(0, n_fft // FILT_ROWS, body, 0)

    def filt_body(i, ssq):
        start = pl.multiple_of(i * FILT_ROWS, FILT_ROWS)
        rows = pl.ds(start, FILT_ROWS)
        hid = hid_ref[rows, :]
        fwd = _dot(hid, wf_ref[...], hp)
        bwd = _dot(hid, wb_ref[...], hp)
        raw = jnp.where(start < seq_len, fwd, bwd)
        tm = tmask_ref[rows, :]
        val = raw * jnp.exp(-tm[:, 0:1] * delta_ref[...]) * tm[:, 1:2]
        ts_ref[rows, :] = val
        return ssq + jnp.sum(val * val, axis=0, keepdims=True)
    ssq = lax.fori_loop(0, n_fft // FILT_ROWS, filt_body, jnp.zeros((1, FILT_CW), F32))
    scale = lax.rsqrt(ssq)

    def a_body(i, carry):
        off = pl.multiple_of(i * A_CHUNK, A_CHUNK)
        blk = [ts_ref[pl.ds(t1 * FFT_N2 + off, A_CHUNK), :] for t1 in range(FFT_N1)]
        lo = _radix8_half(*blk[:4])
        hi = _radix8_half(*blk[4:])
        for k1 in FFT_KEPT:
            sign = 1.0 if k1 % 2 == 0 else -1.0
            r0, i0 = _slice_rows(k1)
            a_ref[pl.ds(r0 + off, A_CHUNK), :] = (lo[k1][0] + sign * hi[k1][0]) * scale
            if i0 is not None:
                a_ref[pl.ds(i0 + off, A_CHUNK), :] = (lo[k1][1] + sign * hi[k1][1]) * scale
        return carry
    lax.fori_loop(0, FFT_N2 // A_CHUNK, a_body, 0)

    for idx, k1 in enumerate(FFT_KEPT):
        xr, xi = _forward_dft(a_ref, k1, dft_rows_ref, tw_ref, FILT_CW)
        s = (1.0 if k1 in (0, 4) else 2.0) / n_fft
        h_ref[idx, 0:FFT_N2, :] = xr * s
        h_ref[idx, FFT_N2:2 * FFT_N2, :] = xi * s


def _filter_spectrum(seq_len, width, filt_w0, filt_b0, filt_w_inner, filt_b_inner, filt_w_out, filt_freq):
    n_fft = 2 * seq_len
    hid = LANES
    feats_ext, t_mask, deltas = _filter_constants(seq_len, width)
    dft_rows, _, tw = _fft_constants(seq_len)
    dft_rows = jnp.asarray(dft_rows).astype(BF16)
    pad_h = hid - FILTER_HIDDEN
    w0 = jnp.pad(filt_w0.astype(F32), ((0, LANES - FILTER_EMB), (0, pad_h)))
    b0 = jnp.pad(filt_b0.astype(F32), (0, pad_h))[None, :]
    wi = jnp.pad(filt_w_inner.astype(F32), ((0, 0), (0, pad_h), (0, pad_h)))
    bi = jnp.pad(filt_b_inner.astype(F32), ((0, 0), (0, pad_h)))[:, None, :]
    freq = jnp.pad(filt_freq.astype(F32), (0, pad_h))[None, :]
    wl = jnp.pad(filt_w_out.astype(F32), ((0, pad_h), (0, 0)))
    n_cols = HYENA_ORDER * width
    steps = n_cols // FILT_CW
    full = lambda *shape: pl.BlockSpec(shape, lambda j: (0,) * len(shape))
    return pl.pallas_call(
        _filter_kernel,
        out_shape=jax.ShapeDtypeStruct((len(FFT_KEPT), 2 * FFT_N2, n_cols), F32),
        grid=(steps,),
        in_specs=[
            full(n_fft, LANES), full(n_fft, 2), full(LANES, hid), full(1, hid),
            full(FILTER_INNER, hid, hid), full(FILTER_INNER, 1, hid), full(1, hid),
            pl.BlockSpec((hid, FILT_CW), lambda j: (0, j)),
            pl.BlockSpec((hid, FILT_CW), lambda j: (0, steps + j)),
            pl.BlockSpec((1, FILT_CW), lambda j: (0, j)),
            full(2 * FFT_N2, FFT_N2), full(8, FFT_N2, LANES),
        ],
        out_specs=pl.BlockSpec((len(FFT_KEPT), 2 * FFT_N2, FILT_CW), lambda j: (0, 0, j)),
        scratch_shapes=[pltpu.VMEM((n_fft, hid), F32), pltpu.VMEM((n_fft, FILT_CW), F32),
                        pltpu.VMEM((n_fft, FILT_CW), F32)],
        compiler_params=pltpu.CompilerParams(dimension_semantics=("arbitrary",),
                                             vmem_limit_bytes=VMEM_LIMIT_BYTES),
        name="hyena_filter",
    )(feats_ext, t_mask, w0, b0, wi, bi, freq, wl, wl, deltas, dft_rows, tw)


HY_CW = 256
CONV_ROWS = 64
PAD_ROWS = 8


def _hyena_kernel(x_ref, wv_ref, w1_ref, w2_ref, bv_ref, b1_ref, b2_ref, cwv_ref, cw1_ref, cw2_ref,
                  cbv_ref, cb1_ref, cb2_ref, skip_ref, h0_ref, h1_ref, dft_rows_ref, dft_cols_ref, tw_ref,
                  out_ref, u_ref, p_ref, a_ref, c_ref):
    seq_len = x_ref.shape[0]
    width = HY_CW
    n2 = FFT_N2
    xb = x_ref[...]

    zeros = jnp.zeros((PAD_ROWS, width), F32)
    u_ref[0:PAD_ROWS, :] = zeros
    u_ref[PAD_ROWS + seq_len:2 * PAD_ROWS + seq_len, :] = zeros
    parts = ((wv_ref, bv_ref, cwv_ref, cbv_ref), (w1_ref, b1_ref, cw1_ref, cb1_ref),
             (w2_ref, b2_ref, cw2_ref, cb2_ref))
    for p, (w_ref, b_ref, cw_ref, cb_ref) in enumerate(parts):
        u_ref[PAD_ROWS:PAD_ROWS + seq_len, :] = _dot(xb, w_ref[...]) + b_ref[...]

        def conv_body(i, carry, p=p, cw_ref=cw_ref, cb_ref=cb_ref):
            start = pl.multiple_of(i * CONV_ROWS, CONV_ROWS)
            win_rows = CONV_ROWS + 2 * PAD_ROWS
            win = u_ref[pl.ds(start, win_rows), :]
            mid = slice(PAD_ROWS, PAD_ROWS + CONV_ROWS)
            prev = pltpu.roll(win, 1, axis=0)[mid]
            nxt = pltpu.roll(win, win_rows - 1, axis=0)[mid]
            cw = cw_ref[...]
            p_ref[p, pl.ds(start, CONV_ROWS), :] = (
                cw[0:1] * prev + cw[1:2] * win[mid] + cw[2:3] * nxt + cb_ref[...])
            return carry
        lax.fori_loop(0, seq_len // CONV_ROWS, conv_body, 0)

    for order, h_ref in enumerate((h0_ref, h1_ref)):
        def a_body(i, carry):
            off = pl.multiple_of(i * A_CHUNK, A_CHUNK)
            blk = [p_ref[0, pl.ds(t1 * n2 + off, A_CHUNK), :] for t1 in range(FFT_N1 // 2)]
            a = _radix8_half(*blk)
            for k1 in FFT_KEPT:
                r0, i0 = _slice_rows(k1)
                a_ref[pl.ds(r0 + off, A_CHUNK), :] = a[k1][0]
                if i0 is not None:
                    a_ref[pl.ds(i0 + off, A_CHUNK), :] = a[k1][1]
            return carry
        lax.fori_loop(0, n2 // A_CHUNK, a_body, 0)

        for idx, k1 in enumerate(FFT_KEPT):
            xr, xi = _forward_dft(a_ref, k1, dft_rows_ref, tw_ref, width)
            hr = h_ref[idx, 0:n2, :]
            hi = h_ref[idx, n2:2 * n2, :]
            pr = xr * hr - xi * hi
            pi = xr * hi + xi * hr
            _inverse_dft(pr, pi, k1, dft_rows_ref, dft_cols_ref, tw_ref, c_ref, width)

        skip = skip_ref[order:order + 1, :]

        def y_body(i, carry, order=order, skip=skip):
            off = pl.multiple_of(i * A_CHUNK, A_CHUNK)
            ld = lambda r: c_ref[pl.ds(r + off, A_CHUNK), :]
            c0, c4 = ld(0), ld(n2)
            c1r, c1i = ld(2 * n2), ld(3 * n2)
            c2r, c2i = ld(4 * n2), ld(5 * n2)
            c3r, c3i = ld(6 * n2), ld(7 * n2)
            e, o = c0 + c4, c0 - c4
            ys = (e + c1r + c2r + c3r,
                  o + (c1r - c1i - c3r - c3i) * RSQRT2 - c2i,
                  e - c1i - c2r + c3i,
                  o + (c3r - c3i - c1r - c1i) * RSQRT2 + c2i)
            for t1, y in enumerate(ys):
                rows = pl.ds(t1 * n2 + off, A_CHUNK)
                z = p_ref[0, rows, :]
                res = p_ref[order + 1, rows, :] * (y + skip * z)
                if order + 1 < HYENA_ORDER:
                    p_ref[0, rows, :] = res
                else:
                    out_ref[rows, :] = res.astype(out_ref.dtype)
            return carry
        lax.fori_loop(0, n2 // A_CHUNK, y_body, 0)


def _hyena_branch(xb, w_hy, b_hy, conv_w, conv_b, hyena_skip, spectrum):
    bsz, seq_len, d_model = xb.shape
    width = hyena_skip.shape[-1]
    nblk = width // HY_CW
    dft_rows, dft_cols, tw = _fft_constants(seq_len)
    dft_rows = jnp.asarray(dft_rows).astype(BF16)
    dft_cols = jnp.asarray(dft_cols).astype(BF16)
    n_kept = len(FFT_KEPT)
    col = lambda part: (lambda c, b: (0, part * nblk + c))
    const = lambda *shape: pl.BlockSpec(shape, lambda c, b: (0,) * len(shape))
    spec_h = lambda order: pl.BlockSpec((n_kept, 2 * FFT_N2, HY_CW), lambda c, b: (0, 0, order * nblk + c),
                                        pipeline_mode=pl.Buffered(1))
    in_specs = [pl.BlockSpec((None, seq_len, d_model), lambda c, b: (b, 0, 0))]
    in_specs += [pl.BlockSpec((d_model, HY_CW), col(p)) for p in range(3)]
    in_specs += [pl.BlockSpec((1, HY_CW), col(p)) for p in range(3)]
    in_specs += [pl.BlockSpec((3, HY_CW), col(p)) for p in range(3)]
    in_specs += [pl.BlockSpec((1, HY_CW), col(p)) for p in range(3)]
    in_specs += [pl.BlockSpec((HYENA_ORDER, HY_CW), lambda c, b: (0, c)), spec_h(0), spec_h(1),
                 const(2 * FFT_N2, FFT_N2), const(FFT_N2, 2 * FFT_N2), const(8, FFT_N2, LANES)]
    return pl.pallas_call(
        _hyena_kernel,
        out_shape=jax.ShapeDtypeStruct((bsz, seq_len, width), BF16),
        grid=(nblk, bsz),
        in_specs=in_specs,
        out_specs=pl.BlockSpec((None, seq_len, HY_CW), lambda c, b: (b, 0, c)),
        scratch_shapes=[pltpu.VMEM((seq_len + 2 * PAD_ROWS, HY_CW), F32),
                        pltpu.VMEM((3, seq_len, HY_CW), F32),
                        pltpu.VMEM((FFT_N1 * FFT_N2, HY_CW), F32),
                        pltpu.VMEM((FFT_N1 * FFT_N2, HY_CW), F32)],
        compiler_params=pltpu.CompilerParams(dimension_semantics=("arbitrary", "arbitrary"),
                                             vmem_limit_bytes=VMEM_LIMIT_BYTES),
        name="hyena_branch",
    )(xb, w_hy, w_hy, w_hy, b_hy, b_hy, b_hy, conv_w, conv_w, conv_w, conv_b, conv_b, conv_b,
      hyena_skip, spectrum, spectrum, dft_rows, dft_cols, tw)


def _attn_kernel(x_ref, w_ref, b_ref, dist_ref, out_ref, qkv_ref, acc_ref, m_ref, l_ref):
    seq_len = x_ref.shape[0]
    xb = x_ref[...]
    slopes = _alibi_slopes()
    q_scale = 1.0 / math.sqrt(HEAD_DIM)

    def attend(head, q_rows, k_rows, variant, coef, first):
        q = qkv_ref[0, head, q_rows, :].astype(BF16)
        k = qkv_ref[1, head, k_rows, :].astype(BF16)
        v = qkv_ref[2, head, k_rows, :].astype(BF16)
        kc = k.shape[0]
        dist = dist_ref[variant]
        if kc < dist.shape[1]:
            dist = dist[:, :kc]
        s = _dot_nt(q, k) + dist * coef
        m_blk = jnp.max(s, axis=-1, keepdims=True)
        if first:
            m_new = jnp.broadcast_to(m_blk, (Q_BLOCK, HEAD_DIM))
        else:
            m_old = m_ref[head, q_rows, :]
            m_new = jnp.maximum(m_old, m_blk)
        p = jnp.exp(s - m_new[:, 0:1])
        l_blk = jnp.sum(p, axis=-1, keepdims=True)
        pv = _dot(p.astype(BF16), v)
        if first:
            l_ref[head, q_rows, :] = jnp.broadcast_to(l_blk, (Q_BLOCK, HEAD_DIM))
            acc_ref[head, q_rows, :] = pv
        else:
            alpha = jnp.exp(m_old - m_new)
            l_ref[head, q_rows, :] = alpha * l_ref[head, q_rows, :] + l_blk
            acc_ref[head, q_rows, :] = alpha * acc_ref[head, q_rows, :] + pv
        m_ref[head, q_rows, :] = m_new

    for g, (window, dil) in enumerate(ATTN_GROUPS):
        n_off = (window // 2) // dil
        assert n_off == Q_BLOCK // 2
        stream = seq_len // dil
        nb = stream // Q_BLOCK
        for part in range(3):
            c0 = part * ATTN_QKV_WIDTH + g * GROUP_WIDTH
            proj = _dot(xb, w_ref[:, c0:c0 + GROUP_WIDTH]) + b_ref[:, c0:c0 + GROUP_WIDTH]
            if part == 0:
                proj = proj * q_scale
            for h in range(HEADS_PER_GROUP):
                qkv_ref[part, h] = proj[:, h * HEAD_DIM:(h + 1) * HEAD_DIM]

        for h in range(HEADS_PER_GROUP):
            coef = -slopes[g * HEADS_PER_GROUP + h] * dil
            if nb == 1:
                def body(c, carry, h=h, coef=coef, dil=dil, first=(g == 0)):
                    rows = pl.ds(c, Q_BLOCK, stride=dil) if dil > 1 else pl.ds(c, Q_BLOCK)
                    attend(h, rows, rows, 3, coef, first)
                    return carry
                lax.fori_loop(0, dil, body, 0)
            else:
                def body(i, carry, h=h, coef=coef, dil=dil, nb=nb, stream=stream, first=(g == 0)):
                    c = i // nb
                    n = i % nb
                    lo = jnp.clip(n * Q_BLOCK - n_off, 0, stream - 2 * Q_BLOCK)
                    variant = jnp.where(n == 0, 0, jnp.where(n == nb - 1, 2, 1))
                    if dil > 1:
                        q_rows = pl.ds(c + n * Q_BLOCK * dil, Q_BLOCK, stride=dil)
                        k_rows = pl.ds(c + lo * dil, 2 * Q_BLOCK, stride=dil)
                    else:
                        q_rows = pl.ds(pl.multiple_of(n * Q_BLOCK, Q_BLOCK), Q_BLOCK)
                        k_rows = pl.ds(pl.multiple_of(lo, n_off), 2 * Q_BLOCK)
                    attend(h, q_rows, k_rows, variant, coef, first)
                    return carry
                lax.fori_loop(0, dil * nb, body, 0)

    for h in range(HEADS_PER_GROUP):
        def norm_body(i, carry, h=h):
            rows = pl.ds(pl.multiple_of(i * Q_BLOCK, Q_BLOCK), Q_BLOCK)
            out_ref[rows, h * HEAD_DIM:(h + 1) * HEAD_DIM] = (
                acc_ref[h, rows, :] / l_ref[h, rows, :]).astype(out_ref.dtype)
            return carry
        lax.fori_loop(0, seq_len // Q_BLOCK, norm_body, 0)


def _attention_branch(xb, w_qkv, b_qkv):
    bsz, seq_len, d_model = xb.shape
    dist = _attn_distance_tiles()
    qkv_w = 3 * ATTN_QKV_WIDTH
    return pl.pallas_call(
        _attn_kernel,
        out_shape=jax.ShapeDtypeStruct((bsz, seq_len, GROUP_WIDTH), BF16),
        grid=(bsz,),
        in_specs=[pl.BlockSpec((None, seq_len, d_model), lambda b: (b, 0, 0)),
                  pl.BlockSpec((d_model, qkv_w), lambda b: (0, 0), pipeline_mode=pl.Buffered(1)),
                  pl.BlockSpec((1, qkv_w), lambda b: (0, 0)),
                  pl.BlockSpec(dist.shape, lambda b: (0, 0, 0))],
        out_specs=pl.BlockSpec((None, seq_len, GROUP_WIDTH), lambda b: (b, 0, 0)),
        scratch_shapes=[pltpu.VMEM((3, HEADS_PER_GROUP, seq_len, HEAD_DIM), F32),
                        pltpu.VMEM((HEADS_PER_GROUP, seq_len, HEAD_DIM), F32),
                        pltpu.VMEM((HEADS_PER_GROUP, seq_len, HEAD_DIM), F32),
                        pltpu.VMEM((HEADS_PER_GROUP, seq_len, HEAD_DIM), F32)],
        compiler_params=pltpu.CompilerParams(dimension_semantics=("arbitrary",),
                                             vmem_limit_bytes=VMEM_LIMIT_BYTES),
        name="dilated_attention",
    )(xb, w_qkv, b_qkv, dist)


ROW_TILE = 512


def _layer_norm(r, g, b):
    mu = jnp.mean(r, axis=-1, keepdims=True)
    d = r - mu
    var = jnp.mean(d * d, axis=-1, keepdims=True)
    return d * lax.rsqrt(var + LN_EPS) * g + b


def _merge_kernel(alpha, x_ref, ya_ref, yh_ref, wg_ref, bg_ref, wa_ref, wh_ref, wo_ref, g_ref, b_ref, out_ref):
    x = x_ref[...]
    d_model = x.shape[-1]
    gates = jax.nn.sigmoid(_dot(x.astype(BF16), wg_ref[...]) + bg_ref[...])
    merged = (gates[:, :d_model] * _dot(ya_ref[...], wa_ref[...])
              + gates[:, d_model:] * _dot(yh_ref[...], wh_ref[...]))
    mix = _dot(merged.astype(BF16), wo_ref[...])
    out_ref[...] = _layer_norm(alpha * x + mix, g_ref[...], b_ref[...])


def _ffn_kernel(alpha, h_ref, w1_ref, b1_ref, w2_ref, b2_ref, g_ref, b_ref, out_ref):
    h = h_ref[...]
    hid = jnp.maximum(_dot(h.astype(BF16), w1_ref[...]) + b1_ref[...], 0.0)
    ff = _dot((hid * hid).astype(BF16), w2_ref[...]) + b2_ref[...]
    out_ref[...] = _layer_norm(alpha * h + ff, g_ref[...], b_ref[...])


def _row_tiled_call(kernel_fn, name, rows, d_model, tiled, resident):
    in_specs = [pl.BlockSpec((ROW_TILE, a.shape[1]), lambda i: (i, 0)) for a in tiled]
    in_specs += [pl.BlockSpec(a.shape, lambda i: (0, 0), pipeline_mode=pl.Buffered(1)) for a in resident]
    return pl.pallas_call(
        kernel_fn,
        out_shape=jax.ShapeDtypeStruct((rows, d_model), F32),
        grid=(rows // ROW_TILE,),
        in_specs=in_specs,
        out_specs=pl.BlockSpec((ROW_TILE, d_model), lambda i: (i, 0)),
        compiler_params=pltpu.CompilerParams(dimension_semantics=("arbitrary",),
                                             vmem_limit_bytes=VMEM_LIMIT_BYTES),
        name=name,
    )(*tiled, *resident)


def kernel(x, w_in, b_in, conv_w, conv_b, filt_w0, filt_b0, filt_w_inner, filt_b_inner, filt_w_out,
           filt_freq, hyena_skip, w_branch_attn, w_branch_hyena, w_out, ln1_g, ln1_b, w_ff1, b_ff1,
           w_ff2, b_ff2, ln2_g, ln2_b):
    bsz, seq_len, d_model = x.shape
    depth = w_in.shape[0]
    alpha = (2 * depth) ** 0.25
    rows = bsz * seq_len
    qkv_w = 3 * ATTN_QKV_WIDTH
    hy_w = (HYENA_ORDER + 1) * hyena_skip.shape[-1]
    row = lambda a: a.astype(F32)[None, :]
    h = x
    for layer in range(depth):
        w_l = w_in[layer].astype(BF16)
        b_l = b_in[layer].astype(F32)[None, :]
        hb = h.astype(BF16)
        spectrum = _filter_spectrum(seq_len, hyena_skip.shape[-1], filt_w0[layer], filt_b0[layer],
                                    filt_w_inner[layer], filt_b_inner[layer], filt_w_out[layer],
                                    filt_freq[layer])
        y_attn = _attention_branch(hb, w_l[:, :qkv_w], b_l[:, :qkv_w])
        y_hyena = _hyena_branch(hb, w_l[:, qkv_w:qkv_w + hy_w], b_l[:, qkv_w:qkv_w + hy_w],
                                conv_w[layer].astype(F32), conv_b[layer].astype(F32)[None, :],
                                hyena_skip[layer].astype(F32), spectrum)
        h1 = _row_tiled_call(
            functools.partial(_merge_kernel, alpha), "merge_ln", rows, d_model,
            [h.reshape(rows, d_model), y_attn.reshape(rows, -1), y_hyena.reshape(rows, -1)],
            [w_l[:, qkv_w + hy_w:], b_l[:, qkv_w + hy_w:], w_branch_attn[layer].astype(BF16),
             w_branch_hyena[layer].astype(BF16), w_out[layer].astype(BF16), row(ln1_g[layer]),
             row(ln1_b[layer])])
        h2 = _row_tiled_call(
            functools.partial(_ffn_kernel, alpha), "ffn_ln", rows, d_model, [h1],
            [w_ff1[layer].astype(BF16), row(b_ff1[layer]), w_ff2[layer].astype(BF16), row(b_ff2[layer]),
             row(ln2_g[layer]), row(ln2_b[layer])])
        h = h2.reshape(bsz, seq_len, d_model)
    return h
```

```python
import functools
import math

import jax
import jax.numpy as jnp
import numpy as np
from jax import lax
from jax.experimental import pallas as pl
from jax.experimental.pallas import tpu as pltpu

F32 = jnp.float32
BF16 = jnp.bfloat16

ATTN_GROUPS = ((128, 1), (512, 4), (2048, 16))
N_GROUPS = len(ATTN_GROUPS)
HEADS_PER_GROUP = 4
HEAD_DIM = 128
GROUP_WIDTH = HEADS_PER_GROUP * HEAD_DIM
ATTN_QKV_WIDTH = N_GROUPS * GROUP_WIDTH
Q_BLOCK = 128
ALIBI_MAX_EXP = 8.0
HYENA_ORDER = 2
FILTER_BANDS = 16
FILTER_EMB = 1 + 2 * FILTER_BANDS
FILTER_HIDDEN = 64
FILTER_INNER = 2
DECAY_TARGET = 1e-2
FAST_DECAY_PCT = 0.3
SLOW_DECAY_PCT = 1.5
LN_EPS = 1e-5

LANES = 128
VMEM_LIMIT_BYTES = 56 * 1024 * 1024

FFT_N1 = 8
FFT_N2 = 512
FFT_KEPT = (0, 4, 1, 2, 3)
RSQRT2 = 1.0 / math.sqrt(2.0)


def _dot(a, b, precision=None):
    return jnp.dot(a, b, preferred_element_type=F32, precision=precision)


def _split_bf16(a):
    hi = a.astype(BF16)
    return hi, (a - hi.astype(F32)).astype(BF16)


def _dot_nt(a, b):
    return lax.dot_general(a, b, (((1,), (1,)), ((), ())), preferred_element_type=F32)


@functools.lru_cache(maxsize=None)
def _fft_constants(seq_len):
    n_fft = 2 * seq_len
    assert n_fft == FFT_N1 * FFT_N2
    k2 = np.arange(FFT_N2)
    ang = -2.0 * np.pi * ((np.outer(k2, k2) % FFT_N2) / FFT_N2)
    fr, fi = np.cos(ang), np.sin(ang)
    dft_rows = np.concatenate([fr, fi], axis=0).astype(np.float32)
    dft_cols = np.concatenate([fr, fi], axis=1).astype(np.float32)
    tw = []
    for k1 in (1, 2, 3, 4):
        a = -2.0 * np.pi * (k1 * k2) / n_fft
        tw.append(np.broadcast_to(np.cos(a)[:, None], (FFT_N2, LANES)))
        tw.append(np.broadcast_to(np.sin(a)[:, None], (FFT_N2, LANES)))
    tw = np.stack(tw, axis=0).astype(np.float32)
    return dft_rows, dft_cols, tw


@functools.lru_cache(maxsize=None)
def _filter_constants(seq_len, width):
    n_fft = 2 * seq_len
    t = np.linspace(0.0, 1.0, seq_len)
    bands = np.linspace(1e-4, FILTER_BANDS - 1, FILTER_BANDS)
    ang = (2.0 * np.pi / seq_len) * np.arange(seq_len)[:, None] * bands
    feats = np.concatenate([t[:, None], np.cos(ang), -np.sin(ang)], axis=-1)
    src = np.concatenate([np.arange(seq_len), [0], np.arange(seq_len - 1, 0, -1)])
    feats_ext = np.zeros((n_fft, LANES), np.float32)
    feats_ext[:, :FILTER_EMB] = feats[src]
    t_mask = np.zeros((n_fft, 2), np.float32)
    t_mask[:, 0] = t[src]
    t_mask[:, 1] = 1.0
    t_mask[seq_len, 1] = 0.0
    deltas = np.abs(np.linspace(math.log(DECAY_TARGET) / SLOW_DECAY_PCT,
                                math.log(DECAY_TARGET) / FAST_DECAY_PCT, width))
    deltas = np.tile(deltas[None, :], (1, HYENA_ORDER)).astype(np.float32)
    return feats_ext, t_mask, deltas


@functools.lru_cache(maxsize=None)
def _attn_distance_tiles():
    n_off = Q_BLOCK // 2
    qi = np.arange(Q_BLOCK)[:, None]
    kj = np.arange(2 * Q_BLOCK)[None, :]
    tiles = []
    for shift in (0, n_off, 2 * n_off):
        off = np.abs(kj - shift - qi).astype(np.float32)
        tiles.append(np.where(off <= n_off, off, np.inf))
    off = np.abs(kj - qi).astype(np.float32)
    t3 = np.where(off <= n_off, off, np.inf)
    t3[:, Q_BLOCK:] = np.inf
    tiles.append(t3)
    return np.stack(tiles, axis=0).astype(np.float32)


def _alibi_slopes():
    n = N_GROUPS * HEADS_PER_GROUP
    return [2.0 ** (-ALIBI_MAX_EXP * j / n) for j in range(1, n + 1)]


def _radix8_half(z0, z1, z2, z3):
    s02, d02 = z0 + z2, z0 - z2
    s13, d13 = z1 + z3, z1 - z3
    ss, dd = s13 * RSQRT2, d13 * RSQRT2
    return {0: (s02 + s13, None), 4: (s02 - s13, None), 2: (d02, -d13),
            1: (z0 + dd, -z2 - ss), 3: (z0 - dd, z2 - ss)}


def _twiddle(tw_ref, k1, width):
    reps = width // LANES
    re = tw_ref[2 * (k1 - 1)]
    im = tw_ref[2 * (k1 - 1) + 1]
    if reps > 1:
        re = jnp.concatenate([re] * reps, axis=1)
        im = jnp.concatenate([im] * reps, axis=1)
    return re, im


def _slice_rows(k1):
    if k1 == 0:
        return 0, None
    if k1 == 4:
        return FFT_N2, None
    base = 2 * FFT_N2 * k1
    return base, base + FFT_N2


def _forward_dft(a_ref, k1, dft_rows_ref, tw_ref, width):
    n2 = FFT_N2
    r0, i0 = _slice_rows(k1)
    ar = a_ref[r0:r0 + n2, :]
    if k1 == 0:
        res = _dot(dft_rows_ref[...], ar.astype(BF16))
        return res[:n2], res[n2:]
    twr, twi = _twiddle(tw_ref, k1, width)
    if i0 is None:
        br, bi = ar * twr, ar * twi
    else:
        ai = a_ref[i0:i0 + n2, :]
        br, bi = ar * twr - ai * twi, ar * twi + ai * twr
    rhs = jnp.concatenate([br.astype(BF16), bi.astype(BF16)], axis=1)
    res = _dot(dft_rows_ref[...], rhs)
    xr = res[:n2, :width] - res[n2:, width:]
    xi = res[n2:, :width] + res[:n2, width:]
    return xr, xi


def _inverse_dft(pr, pi, k1, dft_rows_ref, dft_cols_ref, tw_ref, c_ref, width):
    n2 = FFT_N2
    r0, i0 = _slice_rows(k1)
    if k1 == 0:
        lhs = jnp.concatenate([pr.astype(BF16), pi.astype(BF16)], axis=0)
        c_ref[r0:r0 + n2, :] = _dot(dft_cols_ref[...], lhs)
        return
    rhs = jnp.concatenate([pr.astype(BF16), pi.astype(BF16)], axis=1)
    res = _dot(dft_rows_ref[...], rhs)
    qr = res[:n2, :width] + res[n2:, width:]
    qi = res[:n2, width:] - res[n2:, :width]
    twr, twi = _twiddle(tw_ref, k1, width)
    c_ref[r0:r0 + n2, :] = qr * twr + qi * twi
    if i0 is not None:
        c_ref[i0:i0 + n2, :] = qi * twr - qr * twi


FILT_CW = 256
FILT_ROWS = 256
A_CHUNK = 32


def _filter_kernel(feats_ref, tmask_ref, w0_ref, b0_ref, wi_ref, bi_ref, freq_ref, wf_ref, wb_ref,
                   delta_ref, dft_rows_ref, tw_ref, h_ref, hid_ref, ts_ref, a_ref):
    n_fft = feats_ref.shape[0]
    seq_len = n_fft // 2
    hp = lax.Precision.HIGHEST

    @pl.when(pl.program_id(0) == 0)
    def _():
        def body(i, carry):
            rows = pl.ds(pl.multiple_of(i * FILT_ROWS, FILT_ROWS), FILT_ROWS)
            freq = freq_ref[...]
            hid = jnp.sin(freq * (_dot(feats_ref[rows, :], w0_ref[...], hp) + b0_ref[...]))
            for layer in range(FILTER_INNER):
                hid = jnp.sin(freq * (_dot(hid, wi_ref[layer], hp) + bi_ref[layer]))
            hid_ref[rows, :] = hid
            return carry
        lax.fori_loop(0, n_fft // FILT_ROWS, body, 0)

    ssq = jnp.zeros((1, FILT_CW), F32)
    half_steps = seq_len // FILT_ROWS
    for half, w_ref in enumerate((wf_ref, wb_ref)):
        w_hi, w_lo = _split_bf16(w_ref[...])

        def filt_body(i, ssq, half=half, w_hi=w_hi, w_lo=w_lo):
            rows = pl.ds(pl.multiple_of((half * half_steps + i) * FILT_ROWS, FILT_ROWS), FILT_ROWS)
            h_hi, h_lo = _split_bf16(hid_ref[rows, :])
            raw = _dot(h_hi, w_hi) + (_dot(h_hi, w_lo) + _dot(h_lo, w_hi))
            tm = tmask_ref[rows, :]
            val = raw * jnp.exp(-tm[:, 0:1] * delta_ref[...]) * tm[:, 1:2]
            ts_ref[rows, :] = val
            return ssq + jnp.sum(val * val, axis=0, keepdims=True)
        ssq = lax.fori_loop(0, half_steps, filt_body, ssq)
    scale = lax.rsqrt(ssq)

    def a_body(i, carry):
        off = pl.multiple_of(i * A_CHUNK, A_CHUNK)
        blk = [ts_ref[pl.ds(t1 * FFT_N2 + off, A_CHUNK), :] for t1 in range(FFT_N1)]
        lo = _radix8_half(*blk[:4])
        hi = _radix8_half(*blk[4:])
        for k1 in FFT_KEPT:
            sign = 1.0 if k1 % 2 == 0 else -1.0
            r0, i0 = _slice_rows(k1)
            a_ref[pl.ds(r0 + off, A_CHUNK), :] = (lo[k1][0] + sign * hi[k1][0]) * scale
            if i0 is not None:
                a_ref[pl.ds(i0 + off, A_CHUNK), :] = (lo[k1][1] + sign * hi[k1][1]) * scale
        return carry
    lax.fori_loop(0, FFT_N2 // A_CHUNK, a_body, 0)

    for idx, k1 in enumerate(FFT_KEPT):
        xr, xi = _forward_dft(a_ref, k1, dft_rows_ref, tw_ref, FILT_CW)
        s = (1.0 if k1 in (0, 4) else 2.0) / n_fft
        h_ref[idx, 0:FFT_N2, :] = xr * s
        h_ref[idx, FFT_N2:2 * FFT_N2, :] = xi * s


def _filter_spectrum(seq_len, width, filt_w0, filt_b0, filt_w_inner, filt_b_inner, filt_w_out, filt_freq):
    n_fft = 2 * seq_len
    hid = LANES
    feats_ext, t_mask, deltas = _filter_constants(seq_len, width)
    dft_rows, _, tw = _fft_constants(seq_len)
    dft_rows = jnp.asarray(dft_rows).astype(BF16)
    pad_h = hid - FILTER_HIDDEN
    w0 = jnp.pad(filt_w0.astype(F32), ((0, LANES - FILTER_EMB), (0, pad_h)))
    b0 = jnp.pad(filt_b0.astype(F32), (0, pad_h))[None, :]
    wi = jnp.pad(filt_w_inner.astype(F32), ((0, 0), (0, pad_h), (0, pad_h)))
    bi = jnp.pad(filt_b_inner.astype(F32), ((0, 0), (0, pad_h)))[:, None, :]
    freq = jnp.pad(filt_freq.astype(F32), (0, pad_h))[None, :]
    wl = jnp.pad(filt_w_out.astype(F32), ((0, pad_h), (0, 0)))
    n_cols = HYENA_ORDER * width
    steps = n_cols // FILT_CW
    full = lambda *shape: pl.BlockSpec(shape, lambda j: (0,) * len(shape))
    return pl.pallas_call(
        _filter_kernel,
        out_shape=jax.ShapeDtypeStruct((len(FFT_KEPT), 2 * FFT_N2, n_cols), F32),
        grid=(steps,),
        in_specs=[
            full(n_fft, LANES), full(n_fft, 2), full(LANES, hid), full(1, hid),
            full(FILTER_INNER, hid, hid), full(FILTER_INNER, 1, hid), full(1, hid),
            pl.BlockSpec((hid, FILT_CW), lambda j: (0, j)),
            pl.BlockSpec((hid, FILT_CW), lambda j: (0, steps + j)),
            pl.BlockSpec((1, FILT_CW), lambda j: (0, j)),
            full(2 * FFT_N2, FFT_N2), full(8, FFT_N2, LANES),
        ],
        out_specs=pl.BlockSpec((len(FFT_KEPT), 2 * FFT_N2, FILT_CW), lambda j: (0, 0, j)),
        scratch_shapes=[pltpu.VMEM((n_fft, hid), F32), pltpu.VMEM((n_fft, FILT_CW), F32),
                        pltpu.VMEM((n_fft, FILT_CW), F32)],
        compiler_params=pltpu.CompilerParams(dimension_semantics=("arbitrary",),
                                             vmem_limit_bytes=VMEM_LIMIT_BYTES),
        name="hyena_filter",
    )(feats_ext, t_mask, w0, b0, wi, bi, freq, wl, wl, deltas, dft_rows, tw)


HY_CW = 256
CONV_ROWS = 64
PAD_ROWS = 8


def _hyena_kernel(x_ref, wv_ref, w1_ref, w2_ref, bv_ref, b1_ref, b2_ref, cwv_ref, cw1_ref, cw2_ref,
                  cbv_ref, cb1_ref, cb2_ref, skip_ref, h0_ref, h1_ref, dft_rows_ref, dft_cols_ref, tw_ref,
                  out_ref, u_ref, p_ref, a_ref, c_ref):
    seq_len = x_ref.shape[0]
    width = HY_CW
    n2 = FFT_N2
    xb = x_ref[...]

    zeros = jnp.zeros((PAD_ROWS, width), F32)
    u_ref[0:PAD_ROWS, :] = zeros
    u_ref[PAD_ROWS + seq_len:2 * PAD_ROWS + seq_len, :] = zeros
    parts = ((wv_ref, bv_ref, cwv_ref, cbv_ref), (w1_ref, b1_ref, cw1_ref, cb1_ref),
             (w2_ref, b2_ref, cw2_ref, cb2_ref))
    for p, (w_ref, b_ref, cw_ref, cb_ref) in enumerate(parts):
        u_ref[PAD_ROWS:PAD_ROWS + seq_len, :] = _dot(xb, w_ref[...]) + b_ref[...]

        def conv_body(i, carry, p=p, cw_ref=cw_ref, cb_ref=cb_ref):
            start = pl.multiple_of(i * CONV_ROWS, CONV_ROWS)
            win_rows = CONV_ROWS + 2 * PAD_ROWS
            win = u_ref[pl.ds(start, win_rows), :]
            mid = slice(PAD_ROWS, PAD_ROWS + CONV_ROWS)
            prev = pltpu.roll(win, 1, axis=0)[mid]
            nxt = pltpu.roll(win, win_rows - 1, axis=0)[mid]
            cw = cw_ref[...]
            p_ref[p, pl.ds(start, CONV_ROWS), :] = (
                cw[0:1] * prev + cw[1:2] * win[mid] + cw[2:3] * nxt + cb_ref[...])
            return carry
        lax.fori_loop(0, seq_len // CONV_ROWS, conv_body, 0)

    for order, h_ref in enumerate((h0_ref, h1_ref)):
        def a_body(i, carry):
            off = pl.multiple_of(i * A_CHUNK, A_CHUNK)
            blk = [p_ref[0, pl.ds(t1 * n2 + off, A_CHUNK), :] for t1 in range(FFT_N1 // 2)]
            a = _radix8_half(*blk)
            for k1 in FFT_KEPT:
                r0, i0 = _slice_rows(k1)
                a_ref[pl.ds(r0 + off, A_CHUNK), :] = a[k1][0]
                if i0 is not None:
                    a_ref[pl.ds(i0 + off, A_CHUNK), :] = a[k1][1]
            return carry
        lax.fori_loop(0, n2 // A_CHUNK, a_body, 0)

        for idx, k1 in enumerate(FFT_KEPT):
            xr, xi = _forward_dft(a_ref, k1, dft_rows_ref, tw_ref, width)
            hr = h_ref[idx, 0:n2, :]
            hi = h_ref[idx, n2:2 * n2, :]
            pr = xr * hr - xi * hi
            pi = xr * hi + xi * hr
            _inverse_dft(pr, pi, k1, dft_rows_ref, dft_cols_ref, tw_ref, c_ref, width)

        skip = skip_ref[order:order + 1, :]

        def y_body(i, carry, order=order, skip=skip):
            off = pl.multiple_of(i * A_CHUNK, A_CHUNK)
            ld = lambda r: c_ref[pl.ds(r + off, A_CHUNK), :]
            c0, c4 = ld(0), ld(n2)
            c1r, c1i = ld(2 * n2), ld(3 * n2)
            c2r, c2i = ld(4 * n2), ld(5 * n2)
            c3r, c3i = ld(6 * n2), ld(7 * n2)
            e, o = c0 + c4, c0 - c4
            ys = (e + c1r + c2r + c3r,
                  o + (c1r - c1i - c3r - c3i) * RSQRT2 - c2i,
                  e - c1i - c2r + c3i,
                  o + (c3r - c3i - c1r - c1i) * RSQRT2 + c2i)
            for t1, y in enumerate(ys):
                rows = pl.ds(t1 * n2 + off, A_CHUNK)
                z = p_ref[0, rows, :]
                res = p_ref[order + 1, rows, :] * (y + skip * z)
                if order + 1 < HYENA_ORDER:
                    p_ref[0, rows, :] = res
                else:
                    out_ref[rows, :] = res.astype(out_ref.dtype)
            return carry
        lax.fori_loop(0, n2 // A_CHUNK, y_body, 0)


def _hyena_branch(xb, w_hy, b_hy, conv_w, conv_b, hyena_skip, spectrum):
    bsz, seq_len, d_model = xb.shape
    width = hyena_skip.shape[-1]
    nblk = width // HY_CW
    dft_rows, dft_cols, tw = _fft_constants(seq_len)
    dft_rows = jnp.asarray(dft_rows).astype(BF16)
    dft_cols = jnp.asarray(dft_cols).astype(BF16)
    n_kept = len(FFT_KEPT)
    col = lambda part: (lambda c, b: (0, part * nblk + c))
    const = lambda *shape: pl.BlockSpec(shape, lambda c, b: (0,) * len(shape))
    spec_h = lambda order: pl.BlockSpec((n_kept, 2 * FFT_N2, HY_CW), lambda c, b: (0, 0, order * nblk + c),
                                        pipeline_mode=pl.Buffered(1))
    in_specs = [pl.BlockSpec((None, seq_len, d_model), lambda c, b: (b, 0, 0))]
    in_specs += [pl.BlockSpec((d_model, HY_CW), col(p)) for p in range(3)]
    in_specs += [pl.BlockSpec((1, HY_CW), col(p)) for p in range(3)]
    in_specs += [pl.BlockSpec((3, HY_CW), col(p)) for p in range(3)]
    in_specs += [pl.BlockSpec((1, HY_CW), col(p)) for p in range(3)]
    in_specs += [pl.BlockSpec((HYENA_ORDER, HY_CW), lambda c, b: (0, c)), spec_h(0), spec_h(1),
                 const(2 * FFT_N2, FFT_N2), const(FFT_N2, 2 * FFT_N2), const(8, FFT_N2, LANES)]
    return pl.pallas_call(
        _hyena_kernel,
        out_shape=jax.ShapeDtypeStruct((bsz, seq_len, width), BF16),
        grid=(nblk, bsz),
        in_specs=in_specs,
        out_specs=pl.BlockSpec((None, seq_len, HY_CW), lambda c, b: (b, 0, c)),
        scratch_shapes=[pltpu.VMEM((seq_len + 2 * PAD_ROWS, HY_CW), F32),
                        pltpu.VMEM((3, seq_len, HY_CW), F32),
                        pltpu.VMEM((FFT_N1 * FFT_N2, HY_CW), F32),
                        pltpu.VMEM((FFT_N1 * FFT_N2, HY_CW), F32)],
        compiler_params=pltpu.CompilerParams(dimension_semantics=("arbitrary", "arbitrary"),
                                             vmem_limit_bytes=VMEM_LIMIT_BYTES),
        name="hyena_branch",
    )(xb, w_hy, w_hy, w_hy, b_hy, b_hy, b_hy, conv_w, conv_w, conv_w, conv_b, conv_b, conv_b,
      hyena_skip, spectrum, spectrum, dft_rows, dft_cols, tw)


ATTN_UNROLL = 1


def _attn_kernel(x_ref, w_ref, b_ref, dist_ref, out_ref, qkv_ref, acc_ref, m_ref, l_ref):
    seq_len = x_ref.shape[0]
    xb = x_ref[...]
    slopes = _alibi_slopes()
    q_scale = 1.0 / math.sqrt(HEAD_DIM)

    def attend(blocks, coefs, first, last):
        chains = [(qr, kr, dist, h) for (qr, kr, dist) in blocks for h in range(HEADS_PER_GROUP)]
        idx = range(len(chains))
        s = [_dot_nt(qkv_ref[0, h, qr, :].astype(BF16), qkv_ref[1, h, kr, :].astype(BF16))
             + dist * coefs[h] for (qr, kr, dist, h) in chains]
        m_blk = [jnp.max(s[i], axis=-1, keepdims=True) for i in idx]
        if first:
            m_new = [jnp.broadcast_to(m_blk[i], (Q_BLOCK, HEAD_DIM)) for i in idx]
        else:
            m_old = [m_ref[h, qr, :] for (qr, _, _, h) in chains]
            m_new = [jnp.maximum(m_old[i], m_blk[i]) for i in idx]
        p = [jnp.exp(s[i] - m_new[i][:, 0:1]) for i in idx]
        l_new = [jnp.sum(p[i], axis=-1, keepdims=True) for i in idx]
        acc = [_dot(p[i].astype(BF16), qkv_ref[2, h, kr, :].astype(BF16))
               for i, (_, kr, _, h) in enumerate(chains)]
        if not first:
            alpha = [jnp.exp(m_old[i] - m_new[i]) for i in idx]
            l_new = [alpha[i] * l_ref[h, qr, :] + l_new[i] for i, (qr, _, _, h) in enumerate(chains)]
            acc = [alpha[i] * acc_ref[h, qr, :] + acc[i] for i, (qr, _, _, h) in enumerate(chains)]
        for i, (qr, _, _, h) in enumerate(chains):
            if last:
                out_ref[qr, h * HEAD_DIM:(h + 1) * HEAD_DIM] = (acc[i] / l_new[i]).astype(out_ref.dtype)
            else:
                l_ref[h, qr, :] = jnp.broadcast_to(l_new[i], (Q_BLOCK, HEAD_DIM))
                acc_ref[h, qr, :] = acc[i]
                m_ref[h, qr, :] = m_new[i]

    order = sorted(range(N_GROUPS), key=lambda g: -ATTN_GROUPS[g][1])
    assert ATTN_GROUPS[order[-1]][1] == 1
    for pos, g in enumerate(order):
        window, dil = ATTN_GROUPS[g]
        first, last = pos == 0, pos == N_GROUPS - 1
        n_off = (window // 2) // dil
        assert n_off == Q_BLOCK // 2
        stream = seq_len // dil
        nb = stream // Q_BLOCK
        for part in range(3):
            c0 = part * ATTN_QKV_WIDTH + g * GROUP_WIDTH
            proj = _dot(xb, w_ref[:, c0:c0 + GROUP_WIDTH]) + b_ref[:, c0:c0 + GROUP_WIDTH]
            if part == 0:
                proj = proj * q_scale
            for h in range(HEADS_PER_GROUP):
                qkv_ref[part, h] = proj[:, h * HEAD_DIM:(h + 1) * HEAD_DIM]
        coefs = [-slopes[g * HEADS_PER_GROUP + h] * dil for h in range(HEADS_PER_GROUP)]

        def block(i, dil=dil, nb=nb, stream=stream):
            if nb == 1:
                rows = pl.ds(i, Q_BLOCK, stride=dil) if dil > 1 else pl.ds(i, Q_BLOCK)
                return rows, rows, dist_ref[3, :, 0:Q_BLOCK]
            c = i // nb
            n = i % nb
            lo = jnp.clip(n * Q_BLOCK - n_off, 0, stream - 2 * Q_BLOCK)
            dist = dist_ref[jnp.where(n == 0, 0, jnp.where(n == nb - 1, 2, 1))]
            if dil > 1:
                return (pl.ds(c + n * Q_BLOCK * dil, Q_BLOCK, stride=dil),
                        pl.ds(c + lo * dil, 2 * Q_BLOCK, stride=dil), dist)
            return (pl.ds(pl.multiple_of(n * Q_BLOCK, Q_BLOCK), Q_BLOCK),
                    pl.ds(pl.multiple_of(lo, n_off), 2 * Q_BLOCK), dist)

        def body(i, carry, block=block, coefs=coefs, first=first, last=last):
            attend([block(i * ATTN_UNROLL + u) for u in range(ATTN_UNROLL)], coefs, first, last)
            return carry
        lax.fori_loop(0, dil * nb // ATTN_UNROLL, body, 0)


def _attention_branch(xb, w_qkv, b_qkv):
    bsz, seq_len, d_model = xb.shape
    dist = _attn_distance_tiles()
    qkv_w = 3 * ATTN_QKV_WIDTH
    return pl.pallas_call(
        _attn_kernel,
        out_shape=jax.ShapeDtypeStruct((bsz, seq_len, GROUP_WIDTH), BF16),
        grid=(bsz,),
        in_specs=[pl.BlockSpec((None, seq_len, d_model), lambda b: (b, 0, 0)),
                  pl.BlockSpec((d_model, qkv_w), lambda b: (0, 0), pipeline_mode=pl.Buffered(1)),
                  pl.BlockSpec((1, qkv_w), lambda b: (0, 0)),
                  pl.BlockSpec(dist.shape, lambda b: (0, 0, 0))],
        out_specs=pl.BlockSpec((None, seq_len, GROUP_WIDTH), lambda b: (b, 0, 0)),
        scratch_shapes=[pltpu.VMEM((3, HEADS_PER_GROUP, seq_len, HEAD_DIM), F32),
                        pltpu.VMEM((HEADS_PER_GROUP, seq_len, HEAD_DIM), F32),
                        pltpu.VMEM((HEADS_PER_GROUP, seq_len, HEAD_DIM), F32),
                        pltpu.VMEM((HEADS_PER_GROUP, seq_len, HEAD_DIM), F32)],
        compiler_params=pltpu.CompilerParams(dimension_semantics=("arbitrary",),
                                             vmem_limit_bytes=VMEM_LIMIT_BYTES),
        name="dilated_attention",
    )(xb, w_qkv, b_qkv, dist)


ROW_TILE = 512


def _layer_norm(r, g, b):
    mu = jnp.mean(r, axis=-1, keepdims=True)
    d = r - mu
    var = jnp.mean(d * d, axis=-1, keepdims=True)
    return d * lax.rsqrt(var + LN_EPS) * g + b


def _merge_kernel(alpha, x_ref, ya_ref, yh_ref, wg_ref, bg_ref, wa_ref, wh_ref, wo_ref, g_ref, b_ref, out_ref):
    x = x_ref[...]
    d_model = x.shape[-1]
    gates = jax.nn.sigmoid(_dot(x.astype(BF16), wg_ref[...]) + bg_ref[...])
    merged = (gates[:, :d_model] * _dot(ya_ref[...], wa_ref[...])
              + gates[:, d_model:] * _dot(yh_ref[...], wh_ref[...]))
    mix = _dot(merged.astype(BF16), wo_ref[...])
    out_ref[...] = _layer_norm(alpha * x + mix, g_ref[...], b_ref[...])


def _ffn_kernel(alpha, h_ref, w1_ref, b1_ref, w2_ref, b2_ref, g_ref, b_ref, out_ref):
    h = h_ref[...]
    hid = jnp.maximum(_dot(h.astype(BF16), w1_ref[...]) + b1_ref[...], 0.0)
    ff = _dot((hid * hid).astype(BF16), w2_ref[...]) + b2_ref[...]
    out_ref[...] = _layer_norm(alpha * h + ff, g_ref[...], b_ref[...])


def _row_tiled_call(kernel_fn, name, rows, d_model, tiled, resident):
    in_specs = [pl.BlockSpec((ROW_TILE, a.shape[1]), lambda i: (i, 0)) for a in tiled]
    in_specs += [pl.BlockSpec(a.shape, lambda i: (0, 0), pipeline_mode=pl.Buffered(1)) for a in resident]
    return pl.pallas_call(
        kernel_fn,
        out_shape=jax.ShapeDtypeStruct((rows, d_model), F32),
        grid=(rows // ROW_TILE,),
        in_specs=in_specs,
        out_specs=pl.BlockSpec((ROW_TILE, d_model), lambda i: (i, 0)),
        compiler_params=pltpu.CompilerParams(dimension_semantics=("arbitrary",),
                                             vmem_limit_bytes=VMEM_LIMIT_BYTES),
        name=name,
    )(*tiled, *resident)


def kernel(x, w_in, b_in, conv_w, conv_b, filt_w0, filt_b0, filt_w_inner, filt_b_inner, filt_w_out,
           filt_freq, hyena_skip, w_branch_attn, w_branch_hyena, w_out, ln1_g, ln1_b, w_ff1, b_ff1,
           w_ff2, b_ff2, ln2_g, ln2_b):
    bsz, seq_len, d_model = x.shape
    depth = w_in.shape[0]
    alpha = (2 * depth) ** 0.25
    rows = bsz * seq_len
    qkv_w = 3 * ATTN_QKV_WIDTH
    hy_w = (HYENA_ORDER + 1) * hyena_skip.shape[-1]
    row = lambda a: a.astype(F32)[None, :]
    h = x
    for layer in range(depth):
        w_l = w_in[layer].astype(BF16)
        b_l = b_in[layer].astype(F32)[None, :]
        hb = h.astype(BF16)
        spectrum = _filter_spectrum(seq_len, hyena_skip.shape[-1], filt_w0[layer], filt_b0[layer],
                                    filt_w_inner[layer], filt_b_inner[layer], filt_w_out[layer],
                                    filt_freq[layer])
        y_attn = _attention_branch(hb, w_l[:, :qkv_w], b_l[:, :qkv_w])
        y_hyena = _hyena_branch(hb, w_l[:, qkv_w:qkv_w + hy_w], b_l[:, qkv_w:qkv_w + hy_w],
                                conv_w[layer].astype(F32), conv_b[layer].astype(F32)[None, :],
                                hyena_skip[layer].astype(F32), spectrum)
        h1 = _row_tiled_call(
            functools.partial(_merge_kernel, alpha), "merge_ln", rows, d_model,
            [h.reshape(rows, d_model), y_attn.reshape(rows, -1), y_hyena.reshape(rows, -1)],
            [w_l[:, qkv_w + hy_w:], b_l[:, qkv_w + hy_w:], w_branch_attn[layer].astype(BF16),
             w_branch_hyena[layer].astype(BF16), w_out[layer].astype(BF16), row(ln1_g[layer]),
             row(ln1_b[layer])])
        h2 = _row_tiled_call(
            functools.partial(_ffn_kernel, alpha), "ffn_ln", rows, d_model, [h1],
            [w_ff1[layer].astype(BF16), row(b_ff1[layer]), w_ff2[layer].astype(BF16), row(b_ff2[layer]),
             row(ln2_g[layer]), row(ln2_b[layer])])
        h = h2.reshape(bsz, seq_len, d_model)
    return h
```

```python
import functools
import math

import jax
import jax.numpy as jnp
import numpy as np
from jax import lax
from jax.experimental import pallas as pl
from jax.experimental.pallas import tpu as pltpu

F32 = jnp.float32
BF16 = jnp.bfloat16

ATTN_GROUPS = ((128, 1), (512, 4), (2048, 16))
N_GROUPS = len(ATTN_GROUPS)
HEADS_PER_GROUP = 4
HEAD_DIM = 128
GROUP_WIDTH = HEADS_PER_GROUP * HEAD_DIM
ATTN_QKV_WIDTH = N_GROUPS * GROUP_WIDTH
Q_BLOCK = 128
ALIBI_MAX_EXP = 8.0
HYENA_ORDER = 2
FILTER_BANDS = 16
FILTER_EMB = 1 + 2 * FILTER_BANDS
FILTER_HIDDEN = 64
FILTER_INNER = 2
DECAY_TARGET = 1e-2
FAST_DECAY_PCT = 0.3
SLOW_DECAY_PCT = 1.5
LN_EPS = 1e-5

LANES = 128
VMEM_LIMIT_BYTES = 56 * 1024 * 1024

FFT_N1 = 16
FFT_N2 = 256
FFT_SLICES = FFT_N1 // 2 + 1
FFT_ROWS = FFT_N1 * FFT_N2
RSQRT2 = 1.0 / math.sqrt(2.0)


def _unit_root(k, n):
    snap = lambda v: float(round(v)) if abs(v - round(v)) < 1e-12 else v
    return snap(math.cos(2.0 * math.pi * k / n)), snap(-math.sin(2.0 * math.pi * k / n))


W16 = tuple(_unit_root(k, FFT_N1) for k in range(FFT_SLICES))


def _dot(a, b, precision=None):
    return jnp.dot(a, b, preferred_element_type=F32, precision=precision)


def _split_bf16(a):
    hi = a.astype(BF16)
    return hi, (a - hi.astype(F32)).astype(BF16)


def _dot_nt(a, b):
    return lax.dot_general(a, b, (((1,), (1,)), ((), ())), preferred_element_type=F32)


@functools.lru_cache(maxsize=None)
def _fft_constants(seq_len):
    n_fft = 2 * seq_len
    assert n_fft == FFT_N1 * FFT_N2
    k2 = np.arange(FFT_N2)
    fwd = []
    for k1 in range(FFT_SLICES):
        ang = -2.0 * np.pi * ((np.outer(FFT_N1 * k2 + k1, k2) % n_fft) / n_fft)
        er, ei = np.cos(ang), np.sin(ang)
        fwd.append(np.block([[er, -ei], [ei, er]]))
    fwd = np.stack(fwd, axis=0).astype(np.float32)
    inv = np.ascontiguousarray(np.transpose(fwd, (0, 2, 1)))
    return fwd, inv


@functools.lru_cache(maxsize=None)
def _filter_constants(seq_len, width):
    n_fft = 2 * seq_len
    t = np.linspace(0.0, 1.0, seq_len)
    bands = np.linspace(1e-4, FILTER_BANDS - 1, FILTER_BANDS)
    ang = (2.0 * np.pi / seq_len) * np.arange(seq_len)[:, None] * bands
    feats = np.concatenate([t[:, None], np.cos(ang), -np.sin(ang)], axis=-1)
    src = np.concatenate([np.arange(seq_len), [0], np.arange(seq_len - 1, 0, -1)])
    feats_ext = np.zeros((n_fft, LANES), np.float32)
    feats_ext[:, :FILTER_EMB] = feats[src]
    t_mask = np.zeros((n_fft, 2), np.float32)
    t_mask[:, 0] = t[src]
    t_mask[:, 1] = 1.0
    t_mask[seq_len, 1] = 0.0
    deltas = np.abs(np.linspace(math.log(DECAY_TARGET) / SLOW_DECAY_PCT,
                                math.log(DECAY_TARGET) / FAST_DECAY_PCT, width))
    deltas = np.tile(deltas[None, :], (1, HYENA_ORDER)).astype(np.float32)
    return feats_ext, t_mask, deltas


@functools.lru_cache(maxsize=None)
def _attn_distance_tiles():
    n_off = Q_BLOCK // 2
    qi = np.arange(Q_BLOCK)[:, None]
    kj = np.arange(2 * Q_BLOCK)[None, :]
    tiles = []
    for shift in (0, n_off, 2 * n_off):
        off = np.abs(kj - shift - qi).astype(np.float32)
        tiles.append(np.where(off <= n_off, off, np.inf))
    off = np.abs(kj - qi).astype(np.float32)
    t3 = np.where(off <= n_off, off, np.inf)
    t3[:, Q_BLOCK:] = np.inf
    tiles.append(t3)
    return np.stack(tiles, axis=0).astype(np.float32)


def _alibi_slopes():
    n = N_GROUPS * HEADS_PER_GROUP
    return [2.0 ** (-ALIBI_MAX_EXP * j / n) for j in range(1, n + 1)]


def _radix8_half(z0, z1, z2, z3):
    s02, d02 = z0 + z2, z0 - z2
    s13, d13 = z1 + z3, z1 - z3
    ss, dd = s13 * RSQRT2, d13 * RSQRT2
    return {0: (s02 + s13, None), 4: (s02 - s13, None), 2: (d02, -d13),
            1: (z0 + dd, -z2 - ss), 3: (z0 - dd, z2 - ss)}


def _add(a, b, sign=1.0):
    if b is None:
        return a
    if a is None:
        return b if sign > 0 else -b
    return a + b if sign > 0 else a - b


def _cmul_const(re, im, wr, wi):
    def scaled(v, w):
        if v is None or w == 0.0:
            return None
        return v if w == 1.0 else (-v if w == -1.0 else v * w)
    return _add(scaled(re, wr), scaled(im, wi), -1.0), _add(scaled(re, wi), scaled(im, wr))


def _half8(d, k):
    k %= 8
    if k <= 4:
        return d[k]
    re, im = d[8 - k]
    return re, (None if im is None else -im)


def _radix16_stage(blocks):
    even = _radix8_half(*blocks[0::2])
    odd = _radix8_half(*blocks[1::2])
    out = []
    for k in range(FFT_SLICES):
        er, ei = _half8(even, k)
        pr, pi = _cmul_const(*_half8(odd, k), *W16[k])
        out.append((_add(er, pr), _add(ei, pi)))
    return out


def _assemble8(c0, c4, c1, c2, c3):
    (c1r, c1i), (c2r, c2i), (c3r, c3i) = c1, c2, c3
    e, o = c0 + c4, c0 - c4
    return [e + c1r + c2r + c3r,
            o + (c1r - c1i - c3r - c3i) * RSQRT2 - c2i,
            e - c1i - c2r + c3i,
            o + (c3r - c3i - c1r - c1i) * RSQRT2 + c2i]


def _inverse_radix16_stage(c):
    pair = lambda u, k: (u[k][0] + u[8 - k][0], u[k][1] - u[8 - k][1])
    even = _assemble8(c[0][0] + c[8][0], c[4][0], pair(c, 1), pair(c, 2), pair(c, 3))
    d = {k: _cmul_const(c[k][0], c[k][1], W16[k][0], -W16[k][1]) for k in range(1, 8)}
    odd = _assemble8(c[0][0] - c[8][0], d[4][0], pair(d, 1), pair(d, 2), pair(d, 3))
    return [blk for pair_ in zip(even, odd) for blk in pair_]


def _slice_rows(k1):
    if k1 == 0:
        return 0, FFT_N2
    if k1 == FFT_N1 // 2:
        return FFT_N2, FFT_N2
    return 2 * FFT_N2 * k1, 2 * FFT_N2


def _store_slices(ref, off, rows, values, scale=None):
    for k1, (re, im) in enumerate(values):
        r0, n = _slice_rows(k1)
        parts = (re,) if n == FFT_N2 else (re, im)
        for j, part in enumerate(parts):
            if scale is not None:
                part = part * scale
            ref[pl.ds(r0 + j * FFT_N2 + off, rows), :] = part.astype(ref.dtype)


def _load_slices(ref, off, rows):
    out = []
    for k1 in range(FFT_SLICES):
        r0, n = _slice_rows(k1)
        re = ref[pl.ds(r0 + off, rows), :]
        out.append((re, ref[pl.ds(r0 + FFT_N2 + off, rows), :] if n > FFT_N2 else None))
    return out


def _forward_dft(b_ref, k1, mf_ref):
    r0, n = _slice_rows(k1)
    m = mf_ref[k1] if n > FFT_N2 else mf_ref[k1, :, 0:FFT_N2]
    return _dot(m, b_ref[r0:r0 + n, :])


def _inverse_dft(p, k1, mi_ref, c_ref):
    r0, n = _slice_rows(k1)
    m = mi_ref[k1] if n > FFT_N2 else mi_ref[k1, 0:FFT_N2, :]
    c_ref[r0:r0 + n, :] = _dot(m, p)


FILT_CW = 256
FILT_ROWS = 256
A_CHUNK = 16


def _filter_kernel(feats_ref, tmask_ref, w0_ref, b0_ref, wi_ref, bi_ref, freq_ref, wf_ref, wb_ref,
                   delta_ref, mf_ref, h_ref, hid_ref, ts_ref, b_ref):
    n_fft = feats_ref.shape[0]
    seq_len = n_fft // 2
    hp = lax.Precision.HIGHEST

    @pl.when(pl.program_id(0) == 0)
    def _():
        def body(i, carry):
            rows = pl.ds(pl.multiple_of(i * FILT_ROWS, FILT_ROWS), FILT_ROWS)
            freq = freq_ref[...]
            hid = jnp.sin(freq * (_dot(feats_ref[rows, :], w0_ref[...], hp) + b0_ref[...]))
            for layer in range(FILTER_INNER):
                hid = jnp.sin(freq * (_dot(hid, wi_ref[layer], hp) + bi_ref[layer]))
            hid_ref[rows, :] = hid
            return carry
        lax.fori_loop(0, n_fft // FILT_ROWS, body, 0)

    ssq = jnp.zeros((1, FILT_CW), F32)
    half_steps = seq_len // FILT_ROWS
    for half, w_ref in enumerate((wf_ref, wb_ref)):
        w_hi, w_lo = _split_bf16(w_ref[...])

        def filt_body(i, ssq, half=half, w_hi=w_hi, w_lo=w_lo):
            rows = pl.ds(pl.multiple_of((half * half_steps + i) * FILT_ROWS, FILT_ROWS), FILT_ROWS)
            h_hi, h_lo = _split_bf16(hid_ref[rows, :])
            raw = _dot(h_hi, w_hi) + (_dot(h_hi, w_lo) + _dot(h_lo, w_hi))
            tm = tmask_ref[rows, :]
            val = raw * jnp.exp(-tm[:, 0:1] * delta_ref[...]) * tm[:, 1:2]
            ts_ref[rows, :] = val
            return ssq + jnp.sum(val * val, axis=0, keepdims=True)
        ssq = lax.fori_loop(0, half_steps, filt_body, ssq)
    scale = lax.rsqrt(ssq)

    def a_body(i, carry):
        off = pl.multiple_of(i * A_CHUNK, A_CHUNK)
        blk = [ts_ref[pl.ds(t1 * FFT_N2 + off, A_CHUNK), :] for t1 in range(FFT_N1)]
        lo = _radix16_stage(blk[:FFT_N1 // 2])
        hi = _radix16_stage(blk[FFT_N1 // 2:])
        vals = [(_add(lo[k][0], hi[k][0], 1.0 if k % 2 == 0 else -1.0),
                 _add(lo[k][1], hi[k][1], 1.0 if k % 2 == 0 else -1.0)) for k in range(FFT_SLICES)]
        _store_slices(b_ref, off, A_CHUNK, vals, scale)
        return carry
    lax.fori_loop(0, FFT_N2 // A_CHUNK, a_body, 0)

    for k1 in range(FFT_SLICES):
        s = (1.0 if k1 in (0, FFT_N1 // 2) else 2.0) / n_fft
        h_ref[k1] = _forward_dft(b_ref, k1, mf_ref) * s


def _filter_spectrum(seq_len, width, filt_w0, filt_b0, filt_w_inner, filt_b_inner, filt_w_out, filt_freq):
    n_fft = 2 * seq_len
    hid = LANES
    feats_ext, t_mask, deltas = _filter_constants(seq_len, width)
    mf = jnp.asarray(_fft_constants(seq_len)[0]).astype(BF16)
    pad_h = hid - FILTER_HIDDEN
    w0 = jnp.pad(filt_w0.astype(F32), ((0, LANES - FILTER_EMB), (0, pad_h)))
    b0 = jnp.pad(filt_b0.astype(F32), (0, pad_h))[None, :]
    wi = jnp.pad(filt_w_inner.astype(F32), ((0, 0), (0, pad_h), (0, pad_h)))
    bi = jnp.pad(filt_b_inner.astype(F32), ((0, 0), (0, pad_h)))[:, None, :]
    freq = jnp.pad(filt_freq.astype(F32), (0, pad_h))[None, :]
    wl = jnp.pad(filt_w_out.astype(F32), ((0, pad_h), (0, 0)))
    n_cols = HYENA_ORDER * width
    steps = n_cols // FILT_CW
    full = lambda *shape: pl.BlockSpec(shape, lambda j: (0,) * len(shape))
    return pl.pallas_call(
        _filter_kernel,
        out_shape=jax.ShapeDtypeStruct((FFT_SLICES, 2 * FFT_N2, n_cols), F32),
        grid=(steps,),
        in_specs=[
            full(n_fft, LANES), full(n_fft, 2), full(LANES, hid), full(1, hid),
            full(FILTER_INNER, hid, hid), full(FILTER_INNER, 1, hid), full(1, hid),
            pl.BlockSpec((hid, FILT_CW), lambda j: (0, j)),
            pl.BlockSpec((hid, FILT_CW), lambda j: (0, steps + j)),
            pl.BlockSpec((1, FILT_CW), lambda j: (0, j)),
            full(FFT_SLICES, 2 * FFT_N2, 2 * FFT_N2),
        ],
        out_specs=pl.BlockSpec((FFT_SLICES, 2 * FFT_N2, FILT_CW), lambda j: (0, 0, j)),
        scratch_shapes=[pltpu.VMEM((n_fft, hid), F32), pltpu.VMEM((n_fft, FILT_CW), F32),
                        pltpu.VMEM((FFT_ROWS, FILT_CW), BF16)],
        compiler_params=pltpu.CompilerParams(dimension_semantics=("arbitrary",),
                                             vmem_limit_bytes=VMEM_LIMIT_BYTES),
        name="hyena_filter",
    )(feats_ext, t_mask, w0, b0, wi, bi, freq, wl, wl, deltas, mf)


HY_CW = 256
CONV_ROWS = 64
PAD_ROWS = 8


def _hyena_kernel(x_ref, wv_ref, w1_ref, w2_ref, bv_ref, b1_ref, b2_ref, cwv_ref, cw1_ref, cw2_ref,
                  cbv_ref, cb1_ref, cb2_ref, skip_ref, h0_ref, h1_ref, mf_ref, mi_ref,
                  out_ref, u_ref, p_ref, b_ref, c_ref):
    seq_len = x_ref.shape[0]
    width = HY_CW
    n2 = FFT_N2
    xb = x_ref[...]

    zeros = jnp.zeros((PAD_ROWS, width), F32)
    for j in range(u_ref.shape[0]):
        u_ref[j, 0:PAD_ROWS, :] = zeros
        u_ref[j, PAD_ROWS + seq_len:2 * PAD_ROWS + seq_len, :] = zeros
    parts = ((wv_ref, bv_ref, cwv_ref, cbv_ref), (w1_ref, b1_ref, cw1_ref, cb1_ref),
             (w2_ref, b2_ref, cw2_ref, cb2_ref))

    def project(p, slot):
        w_ref, bias_ref, _, _ = parts[p]
        u_ref[slot, PAD_ROWS:PAD_ROWS + seq_len, :] = _dot(xb, w_ref[...]) + bias_ref[...]

    def short_conv(p, slot):
        _, _, cw_ref, cb_ref = parts[p]

        def conv_body(i, carry):
            start = pl.multiple_of(i * CONV_ROWS, CONV_ROWS)
            win_rows = CONV_ROWS + 2 * PAD_ROWS
            win = u_ref[slot, pl.ds(start, win_rows), :]
            mid = slice(PAD_ROWS, PAD_ROWS + CONV_ROWS)
            prev = pltpu.roll(win, 1, axis=0)[mid]
            nxt = pltpu.roll(win, win_rows - 1, axis=0)[mid]
            cw = cw_ref[...]
            p_ref[p, pl.ds(start, CONV_ROWS), :] = (
                cw[0:1] * prev + cw[1:2] * win[mid] + cw[2:3] * nxt + cb_ref[...])
            return carry
        lax.fori_loop(0, seq_len // CONV_ROWS, conv_body, 0)

    def spectral_product(h_ref):
        for k1 in range(FFT_SLICES):
            x = _forward_dft(b_ref, k1, mf_ref)
            xr, xi = x[:n2], x[n2:]
            hr = h_ref[k1, 0:n2, :]
            hi = h_ref[k1, n2:2 * n2, :]
            prod = jnp.concatenate([(xr * hr - xi * hi).astype(BF16), (xr * hi + xi * hr).astype(BF16)], axis=0)
            _inverse_dft(prod, k1, mi_ref, c_ref)

    def gated_blocks(order, off):
        skip = skip_ref[order:order + 1, :]
        ys = _inverse_radix16_stage(_load_slices(c_ref, off, A_CHUNK))
        out = []
        for t1, y in enumerate(ys):
            rows = pl.ds(t1 * n2 + off, A_CHUNK)
            out.append(p_ref[order + 1, rows, :] * (y + skip * p_ref[0, rows, :]))
        return out

    project(0, 0)
    short_conv(0, 0)

    def a_body(i, carry):
        off = pl.multiple_of(i * A_CHUNK, A_CHUNK)
        blk = [p_ref[0, pl.ds(t1 * n2 + off, A_CHUNK), :] for t1 in range(FFT_N1 // 2)]
        _store_slices(b_ref, off, A_CHUNK, _radix16_stage(blk))
        return carry
    lax.fori_loop(0, n2 // A_CHUNK, a_body, 0)

    project(1, 0)
    project(2, 1)
    spectral_product(h0_ref)
    short_conv(1, 0)
    short_conv(2, 1)

    def mid_body(i, carry):
        off = pl.multiple_of(i * A_CHUNK, A_CHUNK)
        z = gated_blocks(0, off)
        for t1, blk in enumerate(z):
            p_ref[0, pl.ds(t1 * n2 + off, A_CHUNK), :] = blk
        _store_slices(b_ref, off, A_CHUNK, _radix16_stage(z))
        return carry
    lax.fori_loop(0, n2 // A_CHUNK, mid_body, 0)

    spectral_product(h1_ref)

    def out_body(i, carry):
        off = pl.multiple_of(i * A_CHUNK, A_CHUNK)
        for t1, blk in enumerate(gated_blocks(1, off)):
            out_ref[pl.ds(t1 * n2 + off, A_CHUNK), :] = blk.astype(out_ref.dtype)
        return carry
    lax.fori_loop(0, n2 // A_CHUNK, out_body, 0)


def _hyena_branch(xb, w_hy, b_hy, conv_w, conv_b, hyena_skip, spectrum):
    bsz, seq_len, d_model = xb.shape
    width = hyena_skip.shape[-1]
    nblk = width // HY_CW
    mf, mi = (jnp.asarray(m).astype(BF16) for m in _fft_constants(seq_len))
    col = lambda part: (lambda c, b: (0, part * nblk + c))
    const = lambda *shape: pl.BlockSpec(shape, lambda c, b: (0,) * len(shape), pipeline_mode=pl.Buffered(1))
    spec_h = lambda order: pl.BlockSpec((FFT_SLICES, 2 * FFT_N2, HY_CW), lambda c, b: (0, 0, order * nblk + c),
                                        pipeline_mode=pl.Buffered(1))
    in_specs = [pl.BlockSpec((None, seq_len, d_model), lambda c, b: (b, 0, 0))]
    in_specs += [pl.BlockSpec((d_model, HY_CW), col(p)) for p in range(3)]
    in_specs += [pl.BlockSpec((1, HY_CW), col(p)) for p in range(3)]
    in_specs += [pl.BlockSpec((3, HY_CW), col(p)) for p in range(3)]
    in_specs += [pl.BlockSpec((1, HY_CW), col(p)) for p in range(3)]
    in_specs += [pl.BlockSpec((HYENA_ORDER, HY_CW), lambda c, b: (0, c)), spec_h(0), spec_h(1),
                 const(FFT_SLICES, 2 * FFT_N2, 2 * FFT_N2), const(FFT_SLICES, 2 * FFT_N2, 2 * FFT_N2)]
    return pl.pallas_call(
        _hyena_kernel,
        out_shape=jax.ShapeDtypeStruct((bsz, seq_len, width), BF16),
        grid=(nblk, bsz),
        in_specs=in_specs,
        out_specs=pl.BlockSpec((None, seq_len, HY_CW), lambda c, b: (b, 0, c)),
        scratch_shapes=[pltpu.VMEM((2, seq_len + 2 * PAD_ROWS, HY_CW), F32),
                        pltpu.VMEM((3, seq_len, HY_CW), F32),
                        pltpu.VMEM((FFT_ROWS, HY_CW), BF16),
                        pltpu.VMEM((FFT_ROWS, HY_CW), F32)],
        compiler_params=pltpu.CompilerParams(dimension_semantics=("arbitrary", "arbitrary"),
                                             vmem_limit_bytes=VMEM_LIMIT_BYTES),
        name="hyena_branch",
    )(xb, w_hy, w_hy, w_hy, b_hy, b_hy, b_hy, conv_w, conv_w, conv_w, conv_b, conv_b, conv_b,
      hyena_skip, spectrum, spectrum, mf, mi)


ATTN_UNROLL = 1


def _attn_kernel(x_ref, w_ref, b_ref, dist_ref, out_ref, qkv_ref, acc_ref, m_ref, l_ref):
    seq_len = x_ref.shape[0]
    xb = x_ref[...]
    slopes = _alibi_slopes()
    q_scale = 1.0 / math.sqrt(HEAD_DIM)

    def attend(blocks, coefs, first, last):
        chains = [(qr, kr, dist, h) for (qr, kr, dist) in blocks for h in range(HEADS_PER_GROUP)]
        idx = range(len(chains))
        s = [_dot_nt(qkv_ref[0, h, qr, :].astype(BF16), qkv_ref[1, h, kr, :].astype(BF16))
             + dist * coefs[h] for (qr, kr, dist, h) in chains]
        m_blk = [jnp.max(s[i], axis=-1, keepdims=True) for i in idx]
        if first:
            m_new = [jnp.broadcast_to(m_blk[i], (Q_BLOCK, HEAD_DIM)) for i in idx]
        else:
            m_old = [m_ref[h, qr, :] for (qr, _, _, h) in chains]
            m_new = [jnp.maximum(m_old[i], m_blk[i]) for i in idx]
        p = [jnp.exp(s[i] - m_new[i][:, 0:1]) for i in idx]
        l_new = [jnp.sum(p[i], axis=-1, keepdims=True) for i in idx]
        acc = [_dot(p[i].astype(BF16), qkv_ref[2, h, kr, :].astype(BF16))
               for i, (_, kr, _, h) in enumerate(chains)]
        if not first:
            alpha = [jnp.exp(m_old[i] - m_new[i]) for i in idx]
            l_new = [alpha[i] * l_ref[h, qr, :] + l_new[i] for i, (qr, _, _, h) in enumerate(chains)]
            acc = [alpha[i] * acc_ref[h, qr, :] + acc[i] for i, (qr, _, _, h) in enumerate(chains)]
        for i, (qr, _, _, h) in enumerate(chains):
            if last:
                out_ref[qr, h * HEAD_DIM:(h + 1) * HEAD_DIM] = (acc[i] / l_new[i]).astype(out_ref.dtype)
            else:
                l_ref[h, qr, :] = jnp.broadcast_to(l_new[i], (Q_BLOCK, HEAD_DIM))
                acc_ref[h, qr, :] = acc[i]
                m_ref[h, qr, :] = m_new[i]

    order = sorted(range(N_GROUPS), key=lambda g: -ATTN_GROUPS[g][1])
    assert ATTN_GROUPS[order[-1]][1] == 1
    for pos, g in enumerate(order):
        window, dil = ATTN_GROUPS[g]
        first, last = pos == 0, pos == N_GROUPS - 1
        n_off = (window // 2) // dil
        assert n_off == Q_BLOCK // 2
        stream = seq_len // dil
        nb = stream // Q_BLOCK
        for part in range(3):
            c0 = part * ATTN_QKV_WIDTH + g * GROUP_WIDTH
            proj = _dot(xb, w_ref[:, c0:c0 + GROUP_WIDTH]) + b_ref[:, c0:c0 + GROUP_WIDTH]
            if part == 0:
                proj = proj * q_scale
            for h in range(HEADS_PER_GROUP):
                qkv_ref[part, h] = proj[:, h * HEAD_DIM:(h + 1) * HEAD_DIM]
        coefs = [-slopes[g * HEADS_PER_GROUP + h] * dil for h in range(HEADS_PER_GROUP)]

        def block(i, dil=dil, nb=nb, stream=stream):
            if nb == 1:
                rows = pl.ds(i, Q_BLOCK, stride=dil) if dil > 1 else pl.ds(i, Q_BLOCK)
                return rows, rows, dist_ref[3, :, 0:Q_BLOCK]
            c = i // nb
            n = i % nb
            lo = jnp.clip(n * Q_BLOCK - n_off, 0, stream - 2 * Q_BLOCK)
            dist = dist_ref[jnp.where(n == 0, 0, jnp.where(n == nb - 1, 2, 1))]
            if dil > 1:
                return (pl.ds(c + n * Q_BLOCK * dil, Q_BLOCK, stride=dil),
                        pl.ds(c + lo * dil, 2 * Q_BLOCK, stride=dil), dist)
            return (pl.ds(pl.multiple_of(n * Q_BLOCK, Q_BLOCK), Q_BLOCK),
                    pl.ds(pl.multiple_of(lo, n_off), 2 * Q_BLOCK), dist)

        def body(i, carry, block=block, coefs=coefs, first=first, last=last):
            attend([block(i * ATTN_UNROLL + u) for u in range(ATTN_UNROLL)], coefs, first, last)
            return carry
        lax.fori_loop(0, dil * nb // ATTN_UNROLL, body, 0)


def _attention_branch(xb, w_qkv, b_qkv):
    bsz, seq_len, d_model = xb.shape
    dist = _attn_distance_tiles()
    qkv_w = 3 * ATTN_QKV_WIDTH
    return pl.pallas_call(
        _attn_kernel,
        out_shape=jax.ShapeDtypeStruct((bsz, seq_len, GROUP_WIDTH), BF16),
        grid=(bsz,),
        in_specs=[pl.BlockSpec((None, seq_len, d_model), lambda b: (b, 0, 0)),
                  pl.BlockSpec((d_model, qkv_w), lambda b: (0, 0), pipeline_mode=pl.Buffered(1)),
                  pl.BlockSpec((1, qkv_w), lambda b: (0, 0)),
                  pl.BlockSpec(dist.shape, lambda b: (0, 0, 0))],
        out_specs=pl.BlockSpec((None, seq_len, GROUP_WIDTH), lambda b: (b, 0, 0)),
        scratch_shapes=[pltpu.VMEM((3, HEADS_PER_GROUP, seq_len, HEAD_DIM), F32),
                        pltpu.VMEM((HEADS_PER_GROUP, seq_len, HEAD_DIM), F32),
                        pltpu.VMEM((HEADS_PER_GROUP, seq_len, HEAD_DIM), F32),
                        pltpu.VMEM((HEADS_PER_GROUP, seq_len, HEAD_DIM), F32)],
        compiler_params=pltpu.CompilerParams(dimension_semantics=("arbitrary",),
                                             vmem_limit_bytes=VMEM_LIMIT_BYTES),
        name="dilated_attention",
    )(xb, w_qkv, b_qkv, dist)


ROW_TILE = 512


def _layer_norm(r, g, b):
    mu = jnp.mean(r, axis=-1, keepdims=True)
    d = r - mu
    var = jnp.mean(d * d, axis=-1, keepdims=True)
    return d * lax.rsqrt(var + LN_EPS) * g + b


def _merge_kernel(alpha, x_ref, ya_ref, yh_ref, wg_ref, bg_ref, wa_ref, wh_ref, wo_ref, g_ref, b_ref, out_ref):
    x = x_ref[...]
    d_model = x.shape[-1]
    gates = jax.nn.sigmoid(_dot(x.astype(BF16), wg_ref[...]) + bg_ref[...])
    merged = (gates[:, :d_model] * _dot(ya_ref[...], wa_ref[...])
              + gates[:, d_model:] * _dot(yh_ref[...], wh_ref[...]))
    mix = _dot(merged.astype(BF16), wo_ref[...])
    out_ref[...] = _layer_norm(alpha * x + mix, g_ref[...], b_ref[...])


def _ffn_kernel(alpha, h_ref, w1_ref, b1_ref, w2_ref, b2_ref, g_ref, b_ref, out_ref):
    h = h_ref[...]
    hid = jnp.maximum(_dot(h.astype(BF16), w1_ref[...]) + b1_ref[...], 0.0)
    ff = _dot((hid * hid).astype(BF16), w2_ref[...]) + b2_ref[...]
    out_ref[...] = _layer_norm(alpha * h + ff, g_ref[...], b_ref[...])


def _row_tiled_call(kernel_fn, name, rows, d_model, tiled, resident):
    in_specs = [pl.BlockSpec((ROW_TILE, a.shape[1]), lambda i: (i, 0)) for a in tiled]
    in_specs += [pl.BlockSpec(a.shape, lambda i: (0, 0), pipeline_mode=pl.Buffered(1)) for a in resident]
    return pl.pallas_call(
        kernel_fn,
        out_shape=jax.ShapeDtypeStruct((rows, d_model), F32),
        grid=(rows // ROW_TILE,),
        in_specs=in_specs,
        out_specs=pl.BlockSpec((ROW_TILE, d_model), lambda i: (i, 0)),
        compiler_params=pltpu.CompilerParams(dimension_semantics=("arbitrary",),
                                             vmem_limit_bytes=VMEM_LIMIT_BYTES),
        name=name,
    )(*tiled, *resident)


def kernel(x, w_in, b_in, conv_w, conv_b, filt_w0, filt_b0, filt_w_inner, filt_b_inner, filt_w_out,
           filt_freq, hyena_skip, w_branch_attn, w_branch_hyena, w_out, ln1_g, ln1_b, w_ff1, b_ff1,
           w_ff2, b_ff2, ln2_g, ln2_b):
    bsz, seq_len, d_model = x.shape
    depth = w_in.shape[0]
    alpha = (2 * depth) ** 0.25
    rows = bsz * seq_len
    qkv_w = 3 * ATTN_QKV_WIDTH
    hy_w = (HYENA_ORDER + 1) * hyena_skip.shape[-1]
    row = lambda a: a.astype(F32)[None, :]
    h = x
    for layer in range(depth):
        w_l = w_in[layer].astype(BF16)
        b_l = b_in[layer].astype(F32)[None, :]
        hb = h.astype(BF16)
        spectrum = _filter_spectrum(seq_len, hyena_skip.shape[-1], filt_w0[layer], filt_b0[layer],
                                    filt_w_inner[layer], filt_b_inner[layer], filt_w_out[layer],
                                    filt_freq[layer])
        y_attn = _attention_branch(hb, w_l[:, :qkv_w], b_l[:, :qkv_w])
        y_hyena = _hyena_branch(hb, w_l[:, qkv_w:qkv_w + hy_w], b_l[:, qkv_w:qkv_w + hy_w],
                                conv_w[layer].astype(F32), conv_b[layer].astype(F32)[None, :],
                                hyena_skip[layer].astype(F32), spectrum)
        h1 = _row_tiled_call(
            functools.partial(_merge_kernel, alpha), "merge_ln", rows, d_model,
            [h.reshape(rows, d_model), y_attn.reshape(rows, -1), y_hyena.reshape(rows, -1)],
            [w_l[:, qkv_w + hy_w:], b_l[:, qkv_w + hy_w:], w_branch_attn[layer].astype(BF16),
             w_branch_hyena[layer].astype(BF16), w_out[layer].astype(BF16), row(ln1_g[layer]),
             row(ln1_b[layer])])
        h2 = _row_tiled_call(
            functools.partial(_ffn_kernel, alpha), "ffn_ln", rows, d_model, [h1],
            [w_ff1[layer].astype(BF16), row(b_ff1[layer]), w_ff2[layer].astype(BF16), row(b_ff2[layer]),
             row(ln2_g[layer]), row(ln2_b[layer])])
        h = h2.reshape(bsz, seq_len, d_model)
    return h
```

```python
import functools
import math

import jax
import jax.numpy as jnp
import numpy as np
from jax import lax
from jax.experimental import pallas as pl
from jax.experimental.pallas import tpu as pltpu

F32 = jnp.float32
BF16 = jnp.bfloat16

ATTN_GROUPS = ((128, 1), (512, 4), (2048, 16))
N_GROUPS = len(ATTN_GROUPS)
HEADS_PER_GROUP = 4
HEAD_DIM = 128
GROUP_WIDTH = HEADS_PER_GROUP * HEAD_DIM
ATTN_QKV_WIDTH = N_GROUPS * GROUP_WIDTH
Q_BLOCK = 128
ALIBI_MAX_EXP = 8.0
HYENA_ORDER = 2
FILTER_BANDS = 16
FILTER_EMB = 1 + 2 * FILTER_BANDS
FILTER_HIDDEN = 64
FILTER_INNER = 2
DECAY_TARGET = 1e-2
FAST_DECAY_PCT = 0.3
SLOW_DECAY_PCT = 1.5
LN_EPS = 1e-5

LANES = 128
VMEM_LIMIT_BYTES = 58 * 1024 * 1024

FFT_N1 = 16
FFT_N2 = 256
FFT_SLICES = FFT_N1 // 2 + 1
FFT_ROWS = FFT_N1 * FFT_N2
RSQRT2 = 1.0 / math.sqrt(2.0)


def _unit_root(k, n):
    snap = lambda v: float(round(v)) if abs(v - round(v)) < 1e-12 else v
    return snap(math.cos(2.0 * math.pi * k / n)), snap(-math.sin(2.0 * math.pi * k / n))


W16 = tuple(_unit_root(k, FFT_N1) for k in range(FFT_SLICES))


def _dot(a, b, precision=None):
    return jnp.dot(a, b, preferred_element_type=F32, precision=precision)


def _split_bf16(a):
    hi = a.astype(BF16)
    return hi, (a - hi.astype(F32)).astype(BF16)


def _dot_nt(a, b):
    return lax.dot_general(a, b, (((1,), (1,)), ((), ())), preferred_element_type=F32)


@functools.lru_cache(maxsize=None)
def _fft_constants(seq_len):
    n_fft = 2 * seq_len
    assert n_fft == FFT_N1 * FFT_N2
    k2 = np.arange(FFT_N2)
    fwd = []
    for k1 in range(FFT_SLICES):
        ang = -2.0 * np.pi * ((np.outer(FFT_N1 * k2 + k1, k2) % n_fft) / n_fft)
        er, ei = np.cos(ang), np.sin(ang)
        fwd.append(np.block([[er, -ei], [ei, er]]))
    fwd = np.stack(fwd, axis=0).astype(np.float32)
    inv = np.ascontiguousarray(np.transpose(fwd, (0, 2, 1)))
    return fwd, inv


@functools.lru_cache(maxsize=None)
def _filter_constants(seq_len, width):
    n_fft = 2 * seq_len
    t = np.linspace(0.0, 1.0, seq_len)
    bands = np.linspace(1e-4, FILTER_BANDS - 1, FILTER_BANDS)
    ang = (2.0 * np.pi / seq_len) * np.arange(seq_len)[:, None] * bands
    feats = np.concatenate([t[:, None], np.cos(ang), -np.sin(ang)], axis=-1)
    src = np.concatenate([np.arange(seq_len), [0], np.arange(seq_len - 1, 0, -1)])
    feats_ext = np.zeros((n_fft, LANES), np.float32)
    feats_ext[:, :FILTER_EMB] = feats[src]
    t_mask = np.zeros((n_fft, 2), np.float32)
    t_mask[:, 0] = t[src]
    t_mask[:, 1] = 1.0
    t_mask[seq_len, 1] = 0.0
    deltas = np.abs(np.linspace(math.log(DECAY_TARGET) / SLOW_DECAY_PCT,
                                math.log(DECAY_TARGET) / FAST_DECAY_PCT, width))
    deltas = np.tile(deltas[None, :], (1, HYENA_ORDER)).astype(np.float32)
    return feats_ext, t_mask, deltas


@functools.lru_cache(maxsize=None)
def _attn_distance_tiles():
    n_off = Q_BLOCK // 2
    qi = np.arange(Q_BLOCK)[:, None]
    kj = np.arange(2 * Q_BLOCK)[None, :]
    tiles = []
    for shift in (0, n_off, 2 * n_off):
        off = np.abs(kj - shift - qi).astype(np.float32)
        tiles.append(np.where(off <= n_off, off, np.inf))
    off = np.abs(kj - qi).astype(np.float32)
    t3 = np.where(off <= n_off, off, np.inf)
    t3[:, Q_BLOCK:] = np.inf
    tiles.append(t3)
    return np.stack(tiles, axis=0).astype(np.float32)


def _alibi_slopes():
    n = N_GROUPS * HEADS_PER_GROUP
    return [2.0 ** (-ALIBI_MAX_EXP * j / n) for j in range(1, n + 1)]


def _radix8_half(z0, z1, z2, z3):
    s02, d02 = z0 + z2, z0 - z2
    s13, d13 = z1 + z3, z1 - z3
    ss, dd = s13 * RSQRT2, d13 * RSQRT2
    return {0: (s02 + s13, None), 4: (s02 - s13, None), 2: (d02, -d13),
            1: (z0 + dd, -z2 - ss), 3: (z0 - dd, z2 - ss)}


def _add(a, b, sign=1.0):
    if b is None:
        return a
    if a is None:
        return b if sign > 0 else -b
    return a + b if sign > 0 else a - b


def _cmul_const(re, im, wr, wi):
    def scaled(v, w):
        if v is None or w == 0.0:
            return None
        return v if w == 1.0 else (-v if w == -1.0 else v * w)
    return _add(scaled(re, wr), scaled(im, wi), -1.0), _add(scaled(re, wi), scaled(im, wr))


def _half8(d, k):
    k %= 8
    if k <= 4:
        return d[k]
    re, im = d[8 - k]
    return re, (None if im is None else -im)


def _radix16_stage(blocks):
    even = _radix8_half(*blocks[0::2])
    odd = _radix8_half(*blocks[1::2])
    out = []
    for k in range(FFT_SLICES):
        er, ei = _half8(even, k)
        pr, pi = _cmul_const(*_half8(odd, k), *W16[k])
        out.append((_add(er, pr), _add(ei, pi)))
    return out


def _assemble8(c0, c4, c1, c2, c3):
    (c1r, c1i), (c2r, c2i), (c3r, c3i) = c1, c2, c3
    e, o = c0 + c4, c0 - c4
    return [e + c1r + c2r + c3r,
            o + (c1r - c1i - c3r - c3i) * RSQRT2 - c2i,
            e - c1i - c2r + c3i,
            o + (c3r - c3i - c1r - c1i) * RSQRT2 + c2i]


def _inverse_radix16_stage(c):
    pair = lambda u, k: (u[k][0] + u[8 - k][0], u[k][1] - u[8 - k][1])
    even = _assemble8(c[0][0] + c[8][0], c[4][0], pair(c, 1), pair(c, 2), pair(c, 3))
    d = {k: _cmul_const(c[k][0], c[k][1], W16[k][0], -W16[k][1]) for k in range(1, 8)}
    odd = _assemble8(c[0][0] - c[8][0], d[4][0], pair(d, 1), pair(d, 2), pair(d, 3))
    return [blk for pair_ in zip(even, odd) for blk in pair_]


def _slice_rows(k1):
    if k1 == 0:
        return 0, FFT_N2
    if k1 == FFT_N1 // 2:
        return FFT_N2, FFT_N2
    return 2 * FFT_N2 * k1, 2 * FFT_N2


def _store_slices(ref, off, rows, values, scale=None):
    for k1, (re, im) in enumerate(values):
        r0, n = _slice_rows(k1)
        parts = (re,) if n == FFT_N2 else (re, im)
        for j, part in enumerate(parts):
            if scale is not None:
                part = part * scale
            ref[pl.ds(r0 + j * FFT_N2 + off, rows), :] = part.astype(ref.dtype)


def _load_slices(ref, off, rows):
    out = []
    for k1 in range(FFT_SLICES):
        r0, n = _slice_rows(k1)
        re = ref[pl.ds(r0 + off, rows), :]
        out.append((re, ref[pl.ds(r0 + FFT_N2 + off, rows), :] if n > FFT_N2 else None))
    return out


def _forward_dft(b_ref, k1, mf_ref):
    r0, n = _slice_rows(k1)
    m = mf_ref[k1] if n > FFT_N2 else mf_ref[k1, :, 0:FFT_N2]
    return _dot(m, b_ref[r0:r0 + n, :])


def _inverse_dft(p, k1, mi_ref, c_ref):
    r0, n = _slice_rows(k1)
    m = mi_ref[k1] if n > FFT_N2 else mi_ref[k1, 0:FFT_N2, :]
    c_ref[r0:r0 + n, :] = _dot(m, p)


FILT_CW = 256
FILT_ROWS = 256
A_CHUNK = 16


def _filter_kernel(feats_ref, tmask_ref, w0_ref, b0_ref, wi_ref, bi_ref, freq_ref, wf_ref, wb_ref,
                   delta_ref, mf_ref, h_ref, hid_ref, ts_ref, b_ref):
    n_fft = feats_ref.shape[0]
    seq_len = n_fft // 2
    hp = lax.Precision.HIGHEST

    @pl.when(pl.program_id(0) == 0)
    def _():
        def body(i, carry):
            rows = pl.ds(pl.multiple_of(i * FILT_ROWS, FILT_ROWS), FILT_ROWS)
            freq = freq_ref[...]
            hid = jnp.sin(freq * (_dot(feats_ref[rows, :], w0_ref[...], hp) + b0_ref[...]))
            for layer in range(FILTER_INNER):
                hid = jnp.sin(freq * (_dot(hid, wi_ref[layer], hp) + bi_ref[layer]))
            hid_ref[rows, :] = hid
            return carry
        lax.fori_loop(0, n_fft // FILT_ROWS, body, 0)

    ssq = jnp.zeros((1, FILT_CW), F32)
    half_steps = seq_len // FILT_ROWS
    for half, w_ref in enumerate((wf_ref, wb_ref)):
        w_hi, w_lo = _split_bf16(w_ref[...])

        def filt_body(i, ssq, half=half, w_hi=w_hi, w_lo=w_lo):
            rows = pl.ds(pl.multiple_of((half * half_steps + i) * FILT_ROWS, FILT_ROWS), FILT_ROWS)
            h_hi, h_lo = _split_bf16(hid_ref[rows, :])
            raw = _dot(h_hi, w_hi) + (_dot(h_hi, w_lo) + _dot(h_lo, w_hi))
            tm = tmask_ref[rows, :]
            val = raw * jnp.exp(-tm[:, 0:1] * delta_ref[...]) * tm[:, 1:2]
            ts_ref[rows, :] = val
            return ssq + jnp.sum(val * val, axis=0, keepdims=True)
        ssq = lax.fori_loop(0, half_steps, filt_body, ssq)
    scale = lax.rsqrt(ssq)

    def a_body(i, carry):
        off = pl.multiple_of(i * A_CHUNK, A_CHUNK)
        blk = [ts_ref[pl.ds(t1 * FFT_N2 + off, A_CHUNK), :] for t1 in range(FFT_N1)]
        lo = _radix16_stage(blk[:FFT_N1 // 2])
        hi = _radix16_stage(blk[FFT_N1 // 2:])
        vals = [(_add(lo[k][0], hi[k][0], 1.0 if k % 2 == 0 else -1.0),
                 _add(lo[k][1], hi[k][1], 1.0 if k % 2 == 0 else -1.0)) for k in range(FFT_SLICES)]
        _store_slices(b_ref, off, A_CHUNK, vals, scale)
        return carry
    lax.fori_loop(0, FFT_N2 // A_CHUNK, a_body, 0)

    for k1 in range(FFT_SLICES):
        s = (1.0 if k1 in (0, FFT_N1 // 2) else 2.0) / n_fft
        h_ref[k1] = _forward_dft(b_ref, k1, mf_ref) * s


def _filter_spectrum(seq_len, width, filt_w0, filt_b0, filt_w_inner, filt_b_inner, filt_w_out, filt_freq):
    n_fft = 2 * seq_len
    hid = LANES
    feats_ext, t_mask, deltas = _filter_constants(seq_len, width)
    mf = jnp.asarray(_fft_constants(seq_len)[0]).astype(BF16)
    pad_h = hid - FILTER_HIDDEN
    w0 = jnp.pad(filt_w0.astype(F32), ((0, LANES - FILTER_EMB), (0, pad_h)))
    b0 = jnp.pad(filt_b0.astype(F32), (0, pad_h))[None, :]
    wi = jnp.pad(filt_w_inner.astype(F32), ((0, 0), (0, pad_h), (0, pad_h)))
    bi = jnp.pad(filt_b_inner.astype(F32), ((0, 0), (0, pad_h)))[:, None, :]
    freq = jnp.pad(filt_freq.astype(F32), (0, pad_h))[None, :]
    wl = jnp.pad(filt_w_out.astype(F32), ((0, pad_h), (0, 0)))
    n_cols = HYENA_ORDER * width
    steps = n_cols // FILT_CW
    full = lambda *shape: pl.BlockSpec(shape, lambda j: (0,) * len(shape))
    return pl.pallas_call(
        _filter_kernel,
        out_shape=jax.ShapeDtypeStruct((FFT_SLICES, 2 * FFT_N2, n_cols), F32),
        grid=(steps,),
        in_specs=[
            full(n_fft, LANES), full(n_fft, 2), full(LANES, hid), full(1, hid),
            full(FILTER_INNER, hid, hid), full(FILTER_INNER, 1, hid), full(1, hid),
            pl.BlockSpec((hid, FILT_CW), lambda j: (0, j)),
            pl.BlockSpec((hid, FILT_CW), lambda j: (0, steps + j)),
            pl.BlockSpec((1, FILT_CW), lambda j: (0, j)),
            full(FFT_SLICES, 2 * FFT_N2, 2 * FFT_N2),
        ],
        out_specs=pl.BlockSpec((FFT_SLICES, 2 * FFT_N2, FILT_CW), lambda j: (0, 0, j)),
        scratch_shapes=[pltpu.VMEM((n_fft, hid), F32), pltpu.VMEM((n_fft, FILT_CW), F32),
                        pltpu.VMEM((FFT_ROWS, FILT_CW), BF16)],
        compiler_params=pltpu.CompilerParams(dimension_semantics=("arbitrary",),
                                             vmem_limit_bytes=VMEM_LIMIT_BYTES),
        name="hyena_filter",
    )(feats_ext, t_mask, w0, b0, wi, bi, freq, wl, wl, deltas, mf)


HY_CW = 256
CONV_ROWS = 64
PAD_ROWS = 8


def _hyena_kernel(x_ref, wv_ref, w1_ref, w2_ref, bv_ref, b1_ref, b2_ref, cwv_ref, cw1_ref, cw2_ref,
                  cbv_ref, cb1_ref, cb2_ref, skip_ref, h0_ref, h1_ref, mf_ref, mi_ref,
                  out_ref, u_ref, p_ref, b_ref, c_ref):
    seq_len = x_ref.shape[0]
    width = HY_CW
    n2 = FFT_N2
    xb = x_ref[...]

    n_slabs = width // LANES
    zeros = jnp.zeros((PAD_ROWS, LANES), F32)
    for slot in range(u_ref.shape[0]):
        for j in range(n_slabs):
            u_ref[slot, j, 0:PAD_ROWS, :] = zeros
            u_ref[slot, j, PAD_ROWS + seq_len:2 * PAD_ROWS + seq_len, :] = zeros
    parts = ((wv_ref, bv_ref, cwv_ref, cbv_ref), (w1_ref, b1_ref, cw1_ref, cb1_ref),
             (w2_ref, b2_ref, cw2_ref, cb2_ref))

    def project(p, slot):
        w_ref, bias_ref, _, _ = parts[p]
        u = _dot(xb, w_ref[...]) + bias_ref[...]
        for j in range(n_slabs):
            u_ref[slot, j, PAD_ROWS:PAD_ROWS + seq_len, :] = u[:, j * LANES:(j + 1) * LANES]

    def short_conv(p, slot):
        _, _, cw_ref, cb_ref = parts[p]

        def conv_body(i, carry):
            start = pl.multiple_of(i * CONV_ROWS, CONV_ROWS)
            cw = cw_ref[...]
            cb = cb_ref[...]
            for j in range(n_slabs):
                lanes = slice(j * LANES, (j + 1) * LANES)
                taps = [u_ref[slot, j, pl.ds(start + PAD_ROWS - 1 + k, CONV_ROWS, stride=1), :]
                        for k in range(3)]
                p_ref[p, pl.ds(start, CONV_ROWS), lanes] = (
                    cw[0:1, lanes] * taps[0] + cw[1:2, lanes] * taps[1] + cw[2:3, lanes] * taps[2]
                    + cb[:, lanes])
            return carry
        lax.fori_loop(0, seq_len // CONV_ROWS, conv_body, 0, unroll=True)

    def spectral_product(h_ref):
        for k1 in range(FFT_SLICES):
            x = _forward_dft(b_ref, k1, mf_ref)
            xr, xi = x[:n2], x[n2:]
            hr = h_ref[k1, 0:n2, :]
            hi = h_ref[k1, n2:2 * n2, :]
            prod = jnp.concatenate([(xr * hr - xi * hi).astype(BF16), (xr * hi + xi * hr).astype(BF16)], axis=0)
            _inverse_dft(prod, k1, mi_ref, c_ref)

    def gated_blocks(order, off):
        skip = skip_ref[order:order + 1, :]
        ys = _inverse_radix16_stage(_load_slices(c_ref, off, A_CHUNK))
        out = []
        for t1, y in enumerate(ys):
            rows = pl.ds(t1 * n2 + off, A_CHUNK)
            out.append(p_ref[order + 1, rows, :] * (y + skip * p_ref[0, rows, :]))
        return out

    project(0, 0)
    short_conv(0, 0)

    def a_body(i, carry):
        off = pl.multiple_of(i * A_CHUNK, A_CHUNK)
        blk = [p_ref[0, pl.ds(t1 * n2 + off, A_CHUNK), :] for t1 in range(FFT_N1 // 2)]
        _store_slices(b_ref, off, A_CHUNK, _radix16_stage(blk))
        return carry
    lax.fori_loop(0, n2 // A_CHUNK, a_body, 0, unroll=True)

    project(1, 0)
    project(2, 1)
    spectral_product(h0_ref)
    short_conv(1, 0)
    short_conv(2, 1)

    def mid_body(i, carry):
        off = pl.multiple_of(i * A_CHUNK, A_CHUNK)
        z = gated_blocks(0, off)
        for t1, blk in enumerate(z):
            p_ref[0, pl.ds(t1 * n2 + off, A_CHUNK), :] = blk
        _store_slices(b_ref, off, A_CHUNK, _radix16_stage(z))
        return carry
    lax.fori_loop(0, n2 // A_CHUNK, mid_body, 0)

    spectral_product(h1_ref)

    def out_body(i, carry):
        off = pl.multiple_of(i * A_CHUNK, A_CHUNK)
        for t1, blk in enumerate(gated_blocks(1, off)):
            out_ref[pl.ds(t1 * n2 + off, A_CHUNK), :] = blk.astype(out_ref.dtype)
        return carry
    lax.fori_loop(0, n2 // A_CHUNK, out_body, 0)


def _hyena_branch(xb, w_hy, b_hy, conv_w, conv_b, hyena_skip, spectrum):
    bsz, seq_len, d_model = xb.shape
    width = hyena_skip.shape[-1]
    nblk = width // HY_CW
    mf, mi = (jnp.asarray(m).astype(BF16) for m in _fft_constants(seq_len))
    col = lambda part: (lambda c, b: (0, part * nblk + c))
    const = lambda *shape: pl.BlockSpec(shape, lambda c, b: (0,) * len(shape), pipeline_mode=pl.Buffered(1))
    spec_h = lambda order: pl.BlockSpec((FFT_SLICES, 2 * FFT_N2, HY_CW), lambda c, b: (0, 0, order * nblk + c),
                                        pipeline_mode=pl.Buffered(1))
    in_specs = [pl.BlockSpec((None, seq_len, d_model), lambda c, b: (b, 0, 0))]
    in_specs += [pl.BlockSpec((d_model, HY_CW), col(p)) for p in range(3)]
    in_specs += [pl.BlockSpec((1, HY_CW), col(p)) for p in range(3)]
    in_specs += [pl.BlockSpec((3, HY_CW), col(p)) for p in range(3)]
    in_specs += [pl.BlockSpec((1, HY_CW), col(p)) for p in range(3)]
    in_specs += [pl.BlockSpec((HYENA_ORDER, HY_CW), lambda c, b: (0, c)), spec_h(0), spec_h(1),
                 const(FFT_SLICES, 2 * FFT_N2, 2 * FFT_N2), const(FFT_SLICES, 2 * FFT_N2, 2 * FFT_N2)]
    return pl.pallas_call(
        _hyena_kernel,
        out_shape=jax.ShapeDtypeStruct((bsz, seq_len, width), BF16),
        grid=(nblk, bsz),
        in_specs=in_specs,
        out_specs=pl.BlockSpec((None, seq_len, HY_CW), lambda c, b: (b, 0, c)),
        scratch_shapes=[pltpu.VMEM((2, HY_CW // LANES, seq_len + 2 * PAD_ROWS, LANES), F32),
                        pltpu.VMEM((3, seq_len, HY_CW), F32),
                        pltpu.VMEM((FFT_ROWS, HY_CW), BF16),
                        pltpu.VMEM((FFT_ROWS, HY_CW), F32)],
        compiler_params=pltpu.CompilerParams(dimension_semantics=("arbitrary", "arbitrary"),
                                             vmem_limit_bytes=VMEM_LIMIT_BYTES),
        name="hyena_branch",
    )(xb, w_hy, w_hy, w_hy, b_hy, b_hy, b_hy, conv_w, conv_w, conv_w, conv_b, conv_b, conv_b,
      hyena_skip, spectrum, spectrum, mf, mi)


ATTN_UNROLL = 1


def _attn_kernel(x_ref, w_ref, b_ref, dist_ref, out_ref, qkv_ref, acc_ref, m_ref, l_ref):
    seq_len = x_ref.shape[0]
    xb = x_ref[...]
    slopes = _alibi_slopes()
    q_scale = 1.0 / math.sqrt(HEAD_DIM)

    def attend(blocks, coefs, first, last):
        chains = [(qr, kr, dist, h) for (qr, kr, dist) in blocks for h in range(HEADS_PER_GROUP)]
        idx = range(len(chains))
        s = [_dot_nt(qkv_ref[0, h, qr, :].astype(BF16), qkv_ref[1, h, kr, :].astype(BF16))
             + dist * coefs[h] for (qr, kr, dist, h) in chains]
        m_blk = [jnp.max(s[i], axis=-1, keepdims=True) for i in idx]
        if first:
            m_new = [jnp.broadcast_to(m_blk[i], (Q_BLOCK, HEAD_DIM)) for i in idx]
        else:
            m_old = [m_ref[h, qr, :] for (qr, _, _, h) in chains]
            m_new = [jnp.maximum(m_old[i], m_blk[i]) for i in idx]
        p = [jnp.exp(s[i] - m_new[i][:, 0:1]) for i in idx]
        l_new = [jnp.sum(p[i], axis=-1, keepdims=True) for i in idx]
        acc = [_dot(p[i].astype(BF16), qkv_ref[2, h, kr, :].astype(BF16))
               for i, (_, kr, _, h) in enumerate(chains)]
        if not first:
            alpha = [jnp.exp(m_old[i] - m_new[i]) for i in idx]
            l_new = [alpha[i] * l_ref[h, qr, :] + l_new[i] for i, (qr, _, _, h) in enumerate(chains)]
            acc = [alpha[i] * acc_ref[h, qr, :] + acc[i] for i, (qr, _, _, h) in enumerate(chains)]
        for i, (qr, _, _, h) in enumerate(chains):
            if last:
                out_ref[qr, h * HEAD_DIM:(h + 1) * HEAD_DIM] = (acc[i] / l_new[i]).astype(out_ref.dtype)
            else:
                l_ref[h, qr, :] = jnp.broadcast_to(l_new[i], (Q_BLOCK, HEAD_DIM))
                acc_ref[h, qr, :] = acc[i]
                m_ref[h, qr, :] = m_new[i]

    order = sorted(range(N_GROUPS), key=lambda g: -ATTN_GROUPS[g][1])
    assert ATTN_GROUPS[order[-1]][1] == 1
    for pos, g in enumerate(order):
        window, dil = ATTN_GROUPS[g]
        first, last = pos == 0, pos == N_GROUPS - 1
        n_off = (window // 2) // dil
        assert n_off == Q_BLOCK // 2
        stream = seq_len // dil
        nb = stream // Q_BLOCK
        for part in range(3):
            c0 = part * ATTN_QKV_WIDTH + g * GROUP_WIDTH
            proj = _dot(xb, w_ref[:, c0:c0 + GROUP_WIDTH]) + b_ref[:, c0:c0 + GROUP_WIDTH]
            if part == 0:
                proj = proj * q_scale
            for h in range(HEADS_PER_GROUP):
                qkv_ref[part, h] = proj[:, h * HEAD_DIM:(h + 1) * HEAD_DIM]
        coefs = [-slopes[g * HEADS_PER_GROUP + h] * dil for h in range(HEADS_PER_GROUP)]

        def block(i, dil=dil, nb=nb, stream=stream):
            if nb == 1:
                rows = pl.ds(i, Q_BLOCK, stride=dil) if dil > 1 else pl.ds(i, Q_BLOCK)
                return rows, rows, dist_ref[3, :, 0:Q_BLOCK]
            c = i // nb
            n = i % nb
            lo = jnp.clip(n * Q_BLOCK - n_off, 0, stream - 2 * Q_BLOCK)
            dist = dist_ref[jnp.where(n == 0, 0, jnp.where(n == nb - 1, 2, 1))]
            if dil > 1:
                return (pl.ds(c + n * Q_BLOCK * dil, Q_BLOCK, stride=dil),
                        pl.ds(c + lo * dil, 2 * Q_BLOCK, stride=dil), dist)
            return (pl.ds(pl.multiple_of(n * Q_BLOCK, Q_BLOCK), Q_BLOCK),
                    pl.ds(pl.multiple_of(lo, n_off), 2 * Q_BLOCK), dist)

        def body(i, carry, block=block, coefs=coefs, first=first, last=last):
            attend([block(i * ATTN_UNROLL + u) for u in range(ATTN_UNROLL)], coefs, first, last)
            return carry
        lax.fori_loop(0, dil * nb // ATTN_UNROLL, body, 0)


def _attention_branch(xb, w_qkv, b_qkv):
    bsz, seq_len, d_model = xb.shape
    dist = _attn_distance_tiles()
    qkv_w = 3 * ATTN_QKV_WIDTH
    return pl.pallas_call(
        _attn_kernel,
        out_shape=jax.ShapeDtypeStruct((bsz, seq_len, GROUP_WIDTH), BF16),
        grid=(bsz,),
        in_specs=[pl.BlockSpec((None, seq_len, d_model), lambda b: (b, 0, 0)),
                  pl.BlockSpec((d_model, qkv_w), lambda b: (0, 0), pipeline_mode=pl.Buffered(1)),
                  pl.BlockSpec((1, qkv_w), lambda b: (0, 0)),
                  pl.BlockSpec(dist.shape, lambda b: (0, 0, 0))],
        out_specs=pl.BlockSpec((None, seq_len, GROUP_WIDTH), lambda b: (b, 0, 0)),
        scratch_shapes=[pltpu.VMEM((3, HEADS_PER_GROUP, seq_len, HEAD_DIM), F32),
                        pltpu.VMEM((HEADS_PER_GROUP, seq_len, HEAD_DIM), F32),
                        pltpu.VMEM((HEADS_PER_GROUP, seq_len, HEAD_DIM), F32),
                        pltpu.VMEM((HEADS_PER_GROUP, seq_len, HEAD_DIM), F32)],
        compiler_params=pltpu.CompilerParams(dimension_semantics=("arbitrary",),
                                             vmem_limit_bytes=VMEM_LIMIT_BYTES),
        name="dilated_attention",
    )(xb, w_qkv, b_qkv, dist)


ROW_TILE = 512


def _layer_norm(r, g, b):
    mu = jnp.mean(r, axis=-1, keepdims=True)
    d = r - mu
    var = jnp.mean(d * d, axis=-1, keepdims=True)
    return d * lax.rsqrt(var + LN_EPS) * g + b


def _merge_kernel(alpha, x_ref, ya_ref, yh_ref, wg_ref, bg_ref, wa_ref, wh_ref, wo_ref, g_ref, b_ref, out_ref):
    x = x_ref[...]
    d_model = x.shape[-1]
    gates = jax.nn.sigmoid(_dot(x.astype(BF16), wg_ref[...]) + bg_ref[...])
    merged = (gates[:, :d_model] * _dot(ya_ref[...], wa_ref[...])
              + gates[:, d_model:] * _dot(yh_ref[...], wh_ref[...]))
    mix = _dot(merged.astype(BF16), wo_ref[...])
    out_ref[...] = _layer_norm(alpha * x + mix, g_ref[...], b_ref[...])


def _ffn_kernel(alpha, h_ref, w1_ref, b1_ref, w2_ref, b2_ref, g_ref, b_ref, out_ref):
    h = h_ref[...]
    hid = jnp.maximum(_dot(h.astype(BF16), w1_ref[...]) + b1_ref[...], 0.0)
    ff = _dot((hid * hid).astype(BF16), w2_ref[...]) + b2_ref[...]
    out_ref[...] = _layer_norm(alpha * h + ff, g_ref[...], b_ref[...])


def _row_tiled_call(kernel_fn, name, rows, d_model, tiled, resident):
    in_specs = [pl.BlockSpec((ROW_TILE, a.shape[1]), lambda i: (i, 0)) for a in tiled]
    in_specs += [pl.BlockSpec(a.shape, lambda i: (0, 0), pipeline_mode=pl.Buffered(1)) for a in resident]
    return pl.pallas_call(
        kernel_fn,
        out_shape=jax.ShapeDtypeStruct((rows, d_model), F32),
        grid=(rows // ROW_TILE,),
        in_specs=in_specs,
        out_specs=pl.BlockSpec((ROW_TILE, d_model), lambda i: (i, 0)),
        compiler_params=pltpu.CompilerParams(dimension_semantics=("arbitrary",),
                                             vmem_limit_bytes=VMEM_LIMIT_BYTES),
        name=name,
    )(*tiled, *resident)


def kernel(x, w_in, b_in, conv_w, conv_b, filt_w0, filt_b0, filt_w_inner, filt_b_inner, filt_w_out,
           filt_freq, hyena_skip, w_branch_attn, w_branch_hyena, w_out, ln1_g, ln1_b, w_ff1, b_ff1,
           w_ff2, b_ff2, ln2_g, ln2_b):
    bsz, seq_len, d_model = x.shape
    depth = w_in.shape[0]
    alpha = (2 * depth) ** 0.25
    rows = bsz * seq_len
    qkv_w = 3 * ATTN_QKV_WIDTH
    hy_w = (HYENA_ORDER + 1) * hyena_skip.shape[-1]
    row = lambda a: a.astype(F32)[None, :]
    h = x
    for layer in range(depth):
        w_l = w_in[layer].astype(BF16)
        b_l = b_in[layer].astype(F32)[None, :]
        hb = h.astype(BF16)
        spectrum = _filter_spectrum(seq_len, hyena_skip.shape[-1], filt_w0[layer], filt_b0[layer],
                                    filt_w_inner[layer], filt_b_inner[layer], filt_w_out[layer],
                                    filt_freq[layer])
        y_attn = _attention_branch(hb, w_l[:, :qkv_w], b_l[:, :qkv_w])
        y_hyena = _hyena_branch(hb, w_l[:, qkv_w:qkv_w + hy_w], b_l[:, qkv_w:qkv_w + hy_w],
                                conv_w[layer].astype(F32), conv_b[layer].astype(F32)[None, :],
                                hyena_skip[layer].astype(F32), spectrum)
        h1 = _row_tiled_call(
            functools.partial(_merge_kernel, alpha), "merge_ln", rows, d_model,
            [h.reshape(rows, d_model), y_attn.reshape(rows, -1), y_hyena.reshape(rows, -1)],
            [w_l[:, qkv_w + hy_w:], b_l[:, qkv_w + hy_w:], w_branch_attn[layer].astype(BF16),
             w_branch_hyena[layer].astype(BF16), w_out[layer].astype(BF16), row(ln1_g[layer]),
             row(ln1_b[layer])])
        h2 = _row_tiled_call(
            functools.partial(_ffn_kernel, alpha), "ffn_ln", rows, d_model, [h1],
            [w_ff1[layer].astype(BF16), row(b_ff1[layer]), w_ff2[layer].astype(BF16), row(b_ff2[layer]),
             row(ln2_g[layer]), row(ln2_b[layer])])
        h = h2.reshape(bsz, seq_len, d_model)
    return h
```

```python
import functools
import math

import jax
import jax.numpy as jnp
import numpy as np
from jax import lax
from jax.experimental import pallas as pl
from jax.experimental.pallas import tpu as pltpu

F32 = jnp.float32
BF16 = jnp.bfloat16

ATTN_GROUPS = ((128, 1), (512, 4), (2048, 16))
N_GROUPS = len(ATTN_GROUPS)
HEADS_PER_GROUP = 4
HEAD_DIM = 128
GROUP_WIDTH = HEADS_PER_GROUP * HEAD_DIM
ATTN_QKV_WIDTH = N_GROUPS * GROUP_WIDTH
Q_BLOCK = 128
ALIBI_MAX_EXP = 8.0
HYENA_ORDER = 2
FILTER_BANDS = 16
FILTER_EMB = 1 + 2 * FILTER_BANDS
FILTER_HIDDEN = 64
FILTER_INNER = 2
DECAY_TARGET = 1e-2
FAST_DECAY_PCT = 0.3
SLOW_DECAY_PCT = 1.5
LN_EPS = 1e-5

LANES = 128
VMEM_LIMIT_BYTES = 58 * 1024 * 1024

FFT_N1 = 16
FFT_N2 = 256
FFT_SLICES = FFT_N1 // 2 + 1
FFT_ROWS = FFT_N1 * FFT_N2
RSQRT2 = 1.0 / math.sqrt(2.0)


def _unit_root(k, n):
    snap = lambda v: float(round(v)) if abs(v - round(v)) < 1e-12 else v
    return snap(math.cos(2.0 * math.pi * k / n)), snap(-math.sin(2.0 * math.pi * k / n))


W16 = tuple(_unit_root(k, FFT_N1) for k in range(FFT_SLICES))


def _dot(a, b, precision=None):
    return jnp.dot(a, b, preferred_element_type=F32, precision=precision)


def _split_bf16(a):
    hi = a.astype(BF16)
    return hi, (a - hi.astype(F32)).astype(BF16)


def _dot_nt(a, b):
    return lax.dot_general(a, b, (((1,), (1,)), ((), ())), preferred_element_type=F32)


@functools.lru_cache(maxsize=None)
def _fft_constants(seq_len):
    n_fft = 2 * seq_len
    assert n_fft == FFT_N1 * FFT_N2
    k2 = np.arange(FFT_N2)
    fwd = []
    for k1 in range(FFT_SLICES):
        ang = -2.0 * np.pi * ((np.outer(FFT_N1 * k2 + k1, k2) % n_fft) / n_fft)
        er, ei = np.cos(ang), np.sin(ang)
        fwd.append(np.block([[er, -ei], [ei, er]]))
    fwd = np.stack(fwd, axis=0).astype(np.float32)
    inv = np.ascontiguousarray(np.transpose(fwd, (0, 2, 1)))
    return fwd, inv


@functools.lru_cache(maxsize=None)
def _filter_constants(seq_len, width):
    n_fft = 2 * seq_len
    t = np.linspace(0.0, 1.0, seq_len)
    bands = np.linspace(1e-4, FILTER_BANDS - 1, FILTER_BANDS)
    ang = (2.0 * np.pi / seq_len) * np.arange(seq_len)[:, None] * bands
    feats = np.concatenate([t[:, None], np.cos(ang), -np.sin(ang)], axis=-1)
    src = np.concatenate([np.arange(seq_len), [0], np.arange(seq_len - 1, 0, -1)])
    feats_ext = np.zeros((n_fft, LANES), np.float32)
    feats_ext[:, :FILTER_EMB] = feats[src]
    t_mask = np.zeros((n_fft, 2), np.float32)
    t_mask[:, 0] = t[src]
    t_mask[:, 1] = 1.0
    t_mask[seq_len, 1] = 0.0
    deltas = np.abs(np.linspace(math.log(DECAY_TARGET) / SLOW_DECAY_PCT,
                                math.log(DECAY_TARGET) / FAST_DECAY_PCT, width))
    deltas = np.tile(deltas[None, :], (1, HYENA_ORDER)).astype(np.float32)
    return feats_ext, t_mask, deltas


@functools.lru_cache(maxsize=None)
def _attn_distance_tiles():
    n_off = Q_BLOCK // 2
    qi = np.arange(Q_BLOCK)[:, None]
    kj = np.arange(2 * Q_BLOCK)[None, :]
    tiles = []
    for shift in (0, n_off, 2 * n_off):
        off = np.abs(kj - shift - qi).astype(np.float32)
        tiles.append(np.where(off <= n_off, off, np.inf))
    off = np.abs(kj - qi).astype(np.float32)
    t3 = np.where(off <= n_off, off, np.inf)
    t3[:, Q_BLOCK:] = np.inf
    tiles.append(t3)
    return np.stack(tiles, axis=0).astype(np.float32)


def _alibi_slopes():
    n = N_GROUPS * HEADS_PER_GROUP
    return [2.0 ** (-ALIBI_MAX_EXP * j / n) for j in range(1, n + 1)]


def _radix8_half(z0, z1, z2, z3):
    s02, d02 = z0 + z2, z0 - z2
    s13, d13 = z1 + z3, z1 - z3
    ss, dd = s13 * RSQRT2, d13 * RSQRT2
    return {0: (s02 + s13, None), 4: (s02 - s13, None), 2: (d02, -d13),
            1: (z0 + dd, -z2 - ss), 3: (z0 - dd, z2 - ss)}


def _add(a, b, sign=1.0):
    if b is None:
        return a
    if a is None:
        return b if sign > 0 else -b
    return a + b if sign > 0 else a - b


def _cmul_const(re, im, wr, wi):
    def scaled(v, w):
        if v is None or w == 0.0:
            return None
        return v if w == 1.0 else (-v if w == -1.0 else v * w)
    return _add(scaled(re, wr), scaled(im, wi), -1.0), _add(scaled(re, wi), scaled(im, wr))


def _half8(d, k):
    k %= 8
    if k <= 4:
        return d[k]
    re, im = d[8 - k]
    return re, (None if im is None else -im)


def _radix16_stage(blocks):
    even = _radix8_half(*blocks[0::2])
    odd = _radix8_half(*blocks[1::2])
    out = []
    for k in range(FFT_SLICES):
        er, ei = _half8(even, k)
        pr, pi = _cmul_const(*_half8(odd, k), *W16[k])
        out.append((_add(er, pr), _add(ei, pi)))
    return out


def _assemble8(c0, c4, c1, c2, c3):
    (c1r, c1i), (c2r, c2i), (c3r, c3i) = c1, c2, c3
    e, o = c0 + c4, c0 - c4
    return [e + c1r + c2r + c3r,
            o + (c1r - c1i - c3r - c3i) * RSQRT2 - c2i,
            e - c1i - c2r + c3i,
            o + (c3r - c3i - c1r - c1i) * RSQRT2 + c2i]


def _inverse_radix16_stage(c):
    pair = lambda u, k: (u[k][0] + u[8 - k][0], u[k][1] - u[8 - k][1])
    even = _assemble8(c[0][0] + c[8][0], c[4][0], pair(c, 1), pair(c, 2), pair(c, 3))
    d = {k: _cmul_const(c[k][0], c[k][1], W16[k][0], -W16[k][1]) for k in range(1, 8)}
    odd = _assemble8(c[0][0] - c[8][0], d[4][0], pair(d, 1), pair(d, 2), pair(d, 3))
    return [blk for pair_ in zip(even, odd) for blk in pair_]


def _slice_rows(k1):
    if k1 == 0:
        return 0, FFT_N2
    if k1 == FFT_N1 // 2:
        return FFT_N2, FFT_N2
    return 2 * FFT_N2 * k1, 2 * FFT_N2


def _store_slices(ref, off, rows, values, scale=None):
    for k1, (re, im) in enumerate(values):
        r0, n = _slice_rows(k1)
        parts = (re,) if n == FFT_N2 else (re, im)
        for j, part in enumerate(parts):
            if scale is not None:
                part = part * scale
            ref[pl.ds(r0 + j * FFT_N2 + off, rows), :] = part.astype(ref.dtype)


def _load_slices(ref, off, rows):
    out = []
    for k1 in range(FFT_SLICES):
        r0, n = _slice_rows(k1)
        re = ref[pl.ds(r0 + off, rows), :]
        out.append((re, ref[pl.ds(r0 + FFT_N2 + off, rows), :] if n > FFT_N2 else None))
    return out


def _forward_dft(b_ref, k1, mf_ref):
    r0, n = _slice_rows(k1)
    m = mf_ref[k1] if n > FFT_N2 else mf_ref[k1, :, 0:FFT_N2]
    return _dot(m, b_ref[r0:r0 + n, :])


def _inverse_dft(p, k1, mi_ref, c_ref):
    r0, n = _slice_rows(k1)
    m = mi_ref[k1] if n > FFT_N2 else mi_ref[k1, 0:FFT_N2, :]
    c_ref[r0:r0 + n, :] = _dot(m, p)


FILT_CW = 256
FILT_ROWS = 256
A_CHUNK = 16


def _filter_kernel(feats_ref, tmask_ref, w0_ref, b0_ref, wi_ref, bi_ref, freq_ref, wf_ref, wb_ref,
                   delta_ref, mf_ref, h_ref, hid_ref, ts_ref, b_ref):
    n_fft = feats_ref.shape[0]
    seq_len = n_fft // 2
    hp = lax.Precision.HIGHEST

    @pl.when(pl.program_id(0) == 0)
    def _():
        def body(i, carry):
            rows = pl.ds(pl.multiple_of(i * FILT_ROWS, FILT_ROWS), FILT_ROWS)
            freq = freq_ref[...]
            hid = jnp.sin(freq * (_dot(feats_ref[rows, :], w0_ref[...], hp) + b0_ref[...]))
            for layer in range(FILTER_INNER):
                hid = jnp.sin(freq * (_dot(hid, wi_ref[layer], hp) + bi_ref[layer]))
            hid_ref[rows, :] = hid
            return carry
        lax.fori_loop(0, n_fft // FILT_ROWS, body, 0)

    ssq = jnp.zeros((1, FILT_CW), F32)
    half_steps = seq_len // FILT_ROWS
    for half, w_ref in enumerate((wf_ref, wb_ref)):
        w_hi, w_lo = _split_bf16(w_ref[...])

        def filt_body(i, ssq, half=half, w_hi=w_hi, w_lo=w_lo):
            rows = pl.ds(pl.multiple_of((half * half_steps + i) * FILT_ROWS, FILT_ROWS), FILT_ROWS)
            h_hi, h_lo = _split_bf16(hid_ref[rows, :])
            raw = _dot(h_hi, w_hi) + (_dot(h_hi, w_lo) + _dot(h_lo, w_hi))
            tm = tmask_ref[rows, :]
            val = raw * jnp.exp(-tm[:, 0:1] * delta_ref[...]) * tm[:, 1:2]
            ts_ref[rows, :] = val
            return ssq + jnp.sum(val * val, axis=0, keepdims=True)
        ssq = lax.fori_loop(0, half_steps, filt_body, ssq)
    scale = lax.rsqrt(ssq)

    def a_body(i, carry):
        off = pl.multiple_of(i * A_CHUNK, A_CHUNK)
        blk = [ts_ref[pl.ds(t1 * FFT_N2 + off, A_CHUNK), :] for t1 in range(FFT_N1)]
        lo = _radix16_stage(blk[:FFT_N1 // 2])
        hi = _radix16_stage(blk[FFT_N1 // 2:])
        vals = [(_add(lo[k][0], hi[k][0], 1.0 if k % 2 == 0 else -1.0),
                 _add(lo[k][1], hi[k][1], 1.0 if k % 2 == 0 else -1.0)) for k in range(FFT_SLICES)]
        _store_slices(b_ref, off, A_CHUNK, vals, scale)
        return carry
    lax.fori_loop(0, FFT_N2 // A_CHUNK, a_body, 0)

    for k1 in range(FFT_SLICES):
        s = (1.0 if k1 in (0, FFT_N1 // 2) else 2.0) / n_fft
        h_ref[k1] = _forward_dft(b_ref, k1, mf_ref) * s


def _filter_spectrum(seq_len, width, filt_w0, filt_b0, filt_w_inner, filt_b_inner, filt_w_out, filt_freq):
    n_fft = 2 * seq_len
    hid = LANES
    feats_ext, t_mask, deltas = _filter_constants(seq_len, width)
    mf = jnp.asarray(_fft_constants(seq_len)[0]).astype(BF16)
    pad_h = hid - FILTER_HIDDEN
    w0 = jnp.pad(filt_w0.astype(F32), ((0, LANES - FILTER_EMB), (0, pad_h)))
    b0 = jnp.pad(filt_b0.astype(F32), (0, pad_h))[None, :]
    wi = jnp.pad(filt_w_inner.astype(F32), ((0, 0), (0, pad_h), (0, pad_h)))
    bi = jnp.pad(filt_b_inner.astype(F32), ((0, 0), (0, pad_h)))[:, None, :]
    freq = jnp.pad(filt_freq.astype(F32), (0, pad_h))[None, :]
    wl = jnp.pad(filt_w_out.astype(F32), ((0, pad_h), (0, 0)))
    n_cols = HYENA_ORDER * width
    steps = n_cols // FILT_CW
    full = lambda *shape: pl.BlockSpec(shape, lambda j: (0,) * len(shape))
    return pl.pallas_call(
        _filter_kernel,
        out_shape=jax.ShapeDtypeStruct((FFT_SLICES, 2 * FFT_N2, n_cols), F32),
        grid=(steps,),
        in_specs=[
            full(n_fft, LANES), full(n_fft, 2), full(LANES, hid), full(1, hid),
            full(FILTER_INNER, hid, hid), full(FILTER_INNER, 1, hid), full(1, hid),
            pl.BlockSpec((hid, FILT_CW), lambda j: (0, j)),
            pl.BlockSpec((hid, FILT_CW), lambda j: (0, steps + j)),
            pl.BlockSpec((1, FILT_CW), lambda j: (0, j)),
            full(FFT_SLICES, 2 * FFT_N2, 2 * FFT_N2),
        ],
        out_specs=pl.BlockSpec((FFT_SLICES, 2 * FFT_N2, FILT_CW), lambda j: (0, 0, j)),
        scratch_shapes=[pltpu.VMEM((n_fft, hid), F32), pltpu.VMEM((n_fft, FILT_CW), F32),
                        pltpu.VMEM((FFT_ROWS, FILT_CW), BF16)],
        compiler_params=pltpu.CompilerParams(dimension_semantics=("arbitrary",),
                                             vmem_limit_bytes=VMEM_LIMIT_BYTES),
        name="hyena_filter",
    )(feats_ext, t_mask, w0, b0, wi, bi, freq, wl, wl, deltas, mf)


HY_CW = 256
CONV_ROWS = 64
PAD_ROWS = 8


def _hyena_kernel(x_ref, wv_ref, w1_ref, w2_ref, bv_ref, b1_ref, b2_ref, cwv_ref, cw1_ref, cw2_ref,
                  cbv_ref, cb1_ref, cb2_ref, skip_ref, h0_ref, h1_ref, mf_ref, mi_ref,
                  out_ref, u_ref, p_ref, b_ref, c_ref):
    seq_len = x_ref.shape[0]
    width = HY_CW
    n2 = FFT_N2
    xb = x_ref[...]

    n_slabs = width // LANES
    zeros = jnp.zeros((PAD_ROWS, LANES), F32)
    for slot in range(u_ref.shape[0]):
        for j in range(n_slabs):
            u_ref[slot, j, 0:PAD_ROWS, :] = zeros
            u_ref[slot, j, PAD_ROWS + seq_len:2 * PAD_ROWS + seq_len, :] = zeros
    parts = ((wv_ref, bv_ref, cwv_ref, cbv_ref), (w1_ref, b1_ref, cw1_ref, cb1_ref),
             (w2_ref, b2_ref, cw2_ref, cb2_ref))

    def project(p, slot):
        w_ref, bias_ref, _, _ = parts[p]
        u = _dot(xb, w_ref[...]) + bias_ref[...]
        for j in range(n_slabs):
            u_ref[slot, j, PAD_ROWS:PAD_ROWS + seq_len, :] = u[:, j * LANES:(j + 1) * LANES]

    def short_conv(p, slot):
        _, _, cw_ref, cb_ref = parts[p]

        def conv_body(i, carry):
            start = pl.multiple_of(i * CONV_ROWS, CONV_ROWS)
            cw = cw_ref[...]
            cb = cb_ref[...]
            for j in range(n_slabs):
                lanes = slice(j * LANES, (j + 1) * LANES)
                taps = [u_ref[slot, j, pl.ds(start + PAD_ROWS - 1 + k, CONV_ROWS, stride=1), :]
                        for k in range(3)]
                p_ref[p, pl.ds(start, CONV_ROWS), lanes] = (
                    cw[0:1, lanes] * taps[0] + cw[1:2, lanes] * taps[1] + cw[2:3, lanes] * taps[2]
                    + cb[:, lanes])
            return carry
        lax.fori_loop(0, seq_len // CONV_ROWS, conv_body, 0, unroll=True)

    def spectral_product(h_ref):
        for k1 in range(FFT_SLICES):
            x = _forward_dft(b_ref, k1, mf_ref)
            xr, xi = x[:n2], x[n2:]
            hr = h_ref[k1, 0:n2, :]
            hi = h_ref[k1, n2:2 * n2, :]
            prod = jnp.concatenate([(xr * hr - xi * hi).astype(BF16), (xr * hi + xi * hr).astype(BF16)], axis=0)
            _inverse_dft(prod, k1, mi_ref, c_ref)

    def gated_blocks(order, off):
        skip = skip_ref[order:order + 1, :]
        ys = _inverse_radix16_stage(_load_slices(c_ref, off, A_CHUNK))
        out = []
        for t1, y in enumerate(ys):
            rows = pl.ds(t1 * n2 + off, A_CHUNK)
            out.append(p_ref[order + 1, rows, :] * (y + skip * p_ref[0, rows, :]))
        return out

    project(0, 0)
    short_conv(0, 0)

    def a_body(i, carry):
        off = pl.multiple_of(i * A_CHUNK, A_CHUNK)
        blk = [p_ref[0, pl.ds(t1 * n2 + off, A_CHUNK), :] for t1 in range(FFT_N1 // 2)]
        _store_slices(b_ref, off, A_CHUNK, _radix16_stage(blk))
        return carry
    lax.fori_loop(0, n2 // A_CHUNK, a_body, 0, unroll=True)

    project(1, 0)
    project(2, 1)
    spectral_product(h0_ref)
    short_conv(1, 0)
    short_conv(2, 1)

    def mid_body(i, carry):
        off = pl.multiple_of(i * A_CHUNK, A_CHUNK)
        z = gated_blocks(0, off)
        for t1, blk in enumerate(z):
            p_ref[0, pl.ds(t1 * n2 + off, A_CHUNK), :] = blk
        _store_slices(b_ref, off, A_CHUNK, _radix16_stage(z))
        return carry
    lax.fori_loop(0, n2 // A_CHUNK, mid_body, 0)

    spectral_product(h1_ref)

    def out_body(i, carry):
        off = pl.multiple_of(i * A_CHUNK, A_CHUNK)
        for t1, blk in enumerate(gated_blocks(1, off)):
            out_ref[pl.ds(t1 * n2 + off, A_CHUNK), :] = blk.astype(out_ref.dtype)
        return carry
    lax.fori_loop(0, n2 // A_CHUNK, out_body, 0)


def _hyena_branch(xb, w_in, b_in, col0, conv_w, conv_b, hyena_skip, spectrum):
    bsz, seq_len, d_model = xb.shape
    width = hyena_skip.shape[-1]
    nblk = width // HY_CW
    mf, mi = (jnp.asarray(m).astype(BF16) for m in _fft_constants(seq_len))
    col = lambda part: (lambda c, b: (0, part * nblk + c))
    blk0, rem = divmod(col0, HY_CW)
    assert rem == 0
    in_col = lambda part: (lambda c, b: (0, blk0 + part * nblk + c))
    const = lambda *shape: pl.BlockSpec(shape, lambda c, b: (0,) * len(shape), pipeline_mode=pl.Buffered(1))
    spec_h = lambda order: pl.BlockSpec((FFT_SLICES, 2 * FFT_N2, HY_CW), lambda c, b: (0, 0, order * nblk + c),
                                        pipeline_mode=pl.Buffered(1))
    in_specs = [pl.BlockSpec((None, seq_len, d_model), lambda c, b: (b, 0, 0))]
    in_specs += [pl.BlockSpec((d_model, HY_CW), in_col(p)) for p in range(3)]
    in_specs += [pl.BlockSpec((1, HY_CW), in_col(p)) for p in range(3)]
    in_specs += [pl.BlockSpec((3, HY_CW), col(p)) for p in range(3)]
    in_specs += [pl.BlockSpec((1, HY_CW), col(p)) for p in range(3)]
    in_specs += [pl.BlockSpec((HYENA_ORDER, HY_CW), lambda c, b: (0, c)), spec_h(0), spec_h(1),
                 const(FFT_SLICES, 2 * FFT_N2, 2 * FFT_N2), const(FFT_SLICES, 2 * FFT_N2, 2 * FFT_N2)]
    return pl.pallas_call(
        _hyena_kernel,
        out_shape=jax.ShapeDtypeStruct((bsz, seq_len, width), BF16),
        grid=(nblk, bsz),
        in_specs=in_specs,
        out_specs=pl.BlockSpec((None, seq_len, HY_CW), lambda c, b: (b, 0, c)),
        scratch_shapes=[pltpu.VMEM((2, HY_CW // LANES, seq_len + 2 * PAD_ROWS, LANES), F32),
                        pltpu.VMEM((3, seq_len, HY_CW), F32),
                        pltpu.VMEM((FFT_ROWS, HY_CW), BF16),
                        pltpu.VMEM((FFT_ROWS, HY_CW), F32)],
        compiler_params=pltpu.CompilerParams(dimension_semantics=("arbitrary", "arbitrary"),
                                             vmem_limit_bytes=VMEM_LIMIT_BYTES),
        name="hyena_branch",
    )(xb, w_in, w_in, w_in, b_in, b_in, b_in, conv_w, conv_w, conv_w, conv_b, conv_b, conv_b,
      hyena_skip, spectrum, spectrum, mf, mi)


def _attn_kernel(x_ref, w_ref, b_ref, dist_ref, out_ref, qkv_ref, acc_ref, m_ref, l_ref,
                 s_ref, p_ref, al_ref, lb_ref):
    seq_len = x_ref.shape[0]
    xb = x_ref[...]
    slopes = _alibi_slopes()
    q_scale = 1.0 / math.sqrt(HEAD_DIM)
    heads = range(HEADS_PER_GROUP)

    def stage_scores(blk, coefs):
        qr, kr, dist, kc = blk
        for h in heads:
            s_ref[h, :, 0:kc] = _dot_nt(qkv_ref[0, h, qr, :].astype(BF16),
                                        qkv_ref[1, h, kr, :].astype(BF16)) + dist * coefs[h]

    def stage_softmax(blk, first, last):
        qr, _, _, kc = blk
        s = [s_ref[h, :, 0:kc] for h in heads]
        m_blk = [jnp.max(s[h], axis=-1, keepdims=True) for h in heads]
        if first:
            m_new = [jnp.broadcast_to(m_blk[h], (Q_BLOCK, HEAD_DIM)) for h in heads]
        else:
            m_old = [m_ref[h, qr, :] for h in heads]
            m_new = [jnp.maximum(m_old[h], m_blk[h]) for h in heads]
        p = [jnp.exp(s[h] - m_new[h][:, 0:1]) for h in heads]
        l_blk = [jnp.sum(p[h], axis=-1, keepdims=True) for h in heads]
        for h in heads:
            p_ref[h, :, 0:kc] = p[h].astype(BF16)
            lb_ref[h] = jnp.broadcast_to(l_blk[h], (Q_BLOCK, HEAD_DIM))
            if not first:
                al_ref[h] = jnp.exp(m_old[h] - m_new[h])
            if not last:
                m_ref[h, qr, :] = m_new[h]

    def stage_output(blk, first, last):
        qr, kr, _, kc = blk
        acc = [_dot(p_ref[h, :, 0:kc], qkv_ref[2, h, kr, :].astype(BF16)) for h in heads]
        l_new = [lb_ref[h] for h in heads]
        if not first:
            alpha = [al_ref[h] for h in heads]
            l_new = [alpha[h] * l_ref[h, qr, :] + l_new[h] for h in heads]
            acc = [alpha[h] * acc_ref[h, qr, :] + acc[h] for h in heads]
        for h in heads:
            if last:
                out_ref[qr, h * HEAD_DIM:(h + 1) * HEAD_DIM] = (acc[h] / l_new[h]).astype(out_ref.dtype)
            else:
                l_ref[h, qr, :] = l_new[h]
                acc_ref[h, qr, :] = acc[h]

    order = sorted(range(N_GROUPS), key=lambda g: -ATTN_GROUPS[g][1])
    assert ATTN_GROUPS[order[-1]][1] == 1
    for pos, g in enumerate(order):
        window, dil = ATTN_GROUPS[g]
        first, last = pos == 0, pos == N_GROUPS - 1
        n_off = (window // 2) // dil
        assert n_off == Q_BLOCK // 2
        stream = seq_len // dil
        nb = stream // Q_BLOCK
        n_blocks = dil * nb
        for part in range(3):
            c0 = part * ATTN_QKV_WIDTH + g * GROUP_WIDTH
            proj = _dot(xb, w_ref[:, c0:c0 + GROUP_WIDTH]) + b_ref[:, c0:c0 + GROUP_WIDTH]
            if part == 0:
                proj = proj * q_scale
            for h in heads:
                qkv_ref[part, h] = proj[:, h * HEAD_DIM:(h + 1) * HEAD_DIM]
        coefs = [-slopes[g * HEADS_PER_GROUP + h] * dil for h in heads]

        def block(i, dil=dil, nb=nb, stream=stream):
            if nb == 1:
                rows = pl.ds(i, Q_BLOCK, stride=dil) if dil > 1 else pl.ds(i, Q_BLOCK)
                return rows, rows, dist_ref[3, :, 0:Q_BLOCK], Q_BLOCK
            c = i // nb
            n = i % nb
            lo = jnp.clip(n * Q_BLOCK - n_off, 0, stream - 2 * Q_BLOCK)
            dist = dist_ref[jnp.where(n == 0, 0, jnp.where(n == nb - 1, 2, 1))]
            if dil > 1:
                return (pl.ds(c + n * Q_BLOCK * dil, Q_BLOCK, stride=dil),
                        pl.ds(c + lo * dil, 2 * Q_BLOCK, stride=dil), dist, 2 * Q_BLOCK)
            return (pl.ds(pl.multiple_of(n * Q_BLOCK, Q_BLOCK), Q_BLOCK),
                    pl.ds(pl.multiple_of(lo, n_off), 2 * Q_BLOCK), dist, 2 * Q_BLOCK)

        def steady(i, carry, block=block, coefs=coefs, first=first, last=last):
            stage_output(block(i - 2), first, last)
            stage_softmax(block(i - 1), first, last)
            stage_scores(block(i), coefs)
            return carry

        assert n_blocks >= 3
        stage_scores(block(0), coefs)
        stage_softmax(block(0), first, last)
        stage_scores(block(1), coefs)
        lax.fori_loop(2, n_blocks, steady, 0)
        stage_output(block(n_blocks - 2), first, last)
        stage_softmax(block(n_blocks - 1), first, last)
        stage_output(block(n_blocks - 1), first, last)


def _attention_branch(xb, w_qkv, b_qkv):
    bsz, seq_len, d_model = xb.shape
    dist = _attn_distance_tiles()
    qkv_w = 3 * ATTN_QKV_WIDTH
    return pl.pallas_call(
        _attn_kernel,
        out_shape=jax.ShapeDtypeStruct((bsz, seq_len, GROUP_WIDTH), BF16),
        grid=(bsz,),
        in_specs=[pl.BlockSpec((None, seq_len, d_model), lambda b: (b, 0, 0)),
                  pl.BlockSpec((d_model, qkv_w), lambda b: (0, 0), pipeline_mode=pl.Buffered(1)),
                  pl.BlockSpec((1, qkv_w), lambda b: (0, 0)),
                  pl.BlockSpec(dist.shape, lambda b: (0, 0, 0))],
        out_specs=pl.BlockSpec((None, seq_len, GROUP_WIDTH), lambda b: (b, 0, 0)),
        scratch_shapes=[pltpu.VMEM((3, HEADS_PER_GROUP, seq_len, HEAD_DIM), F32),
                        pltpu.VMEM((HEADS_PER_GROUP, seq_len, HEAD_DIM), F32),
                        pltpu.VMEM((HEADS_PER_GROUP, seq_len, HEAD_DIM), F32),
                        pltpu.VMEM((HEADS_PER_GROUP, seq_len, HEAD_DIM), F32),
                        pltpu.VMEM((HEADS_PER_GROUP, Q_BLOCK, 2 * Q_BLOCK), F32),
                        pltpu.VMEM((HEADS_PER_GROUP, Q_BLOCK, 2 * Q_BLOCK), BF16),
                        pltpu.VMEM((HEADS_PER_GROUP, Q_BLOCK, HEAD_DIM), F32),
                        pltpu.VMEM((HEADS_PER_GROUP, Q_BLOCK, HEAD_DIM), F32)],
        compiler_params=pltpu.CompilerParams(dimension_semantics=("arbitrary",),
                                             vmem_limit_bytes=VMEM_LIMIT_BYTES),
        name="dilated_attention",
    )(xb, w_qkv, b_qkv, dist)


ROW_TILE = 512


def _layer_norm(r, g, b):
    mu = jnp.mean(r, axis=-1, keepdims=True)
    d = r - mu
    var = jnp.mean(d * d, axis=-1, keepdims=True)
    return d * lax.rsqrt(var + LN_EPS) * g + b


def _merge_kernel(alpha, x_ref, ya_ref, yh_ref, wg_ref, bg_ref, wa_ref, wh_ref, wo_ref, g_ref, b_ref, out_ref):
    x = x_ref[...]
    d_model = x.shape[-1]
    gates = jax.nn.sigmoid(_dot(x.astype(BF16), wg_ref[...]) + bg_ref[...])
    merged = (gates[:, :d_model] * _dot(ya_ref[...], wa_ref[...])
              + gates[:, d_model:] * _dot(yh_ref[...], wh_ref[...]))
    mix = _dot(merged.astype(BF16), wo_ref[...])
    out_ref[...] = _layer_norm(alpha * x + mix, g_ref[...], b_ref[...])


def _ffn_kernel(alpha, h_ref, w1_ref, b1_ref, w2_ref, b2_ref, g_ref, b_ref, out_ref):
    h = h_ref[...]
    hid = jnp.maximum(_dot(h.astype(BF16), w1_ref[...]) + b1_ref[...], 0.0)
    ff = _dot((hid * hid).astype(BF16), w2_ref[...]) + b2_ref[...]
    out_ref[...] = _layer_norm(alpha * h + ff, g_ref[...], b_ref[...])


def _row_tiled_call(kernel_fn, name, rows, d_model, tiled, resident):
    in_specs = [pl.BlockSpec((ROW_TILE, a.shape[1]), lambda i: (i, 0)) for a in tiled]
    in_specs += [pl.BlockSpec(a.shape, lambda i: (0, 0), pipeline_mode=pl.Buffered(1)) for a in resident]
    return pl.pallas_call(
        kernel_fn,
        out_shape=jax.ShapeDtypeStruct((rows, d_model), F32),
        grid=(rows // ROW_TILE,),
        in_specs=in_specs,
        out_specs=pl.BlockSpec((ROW_TILE, d_model), lambda i: (i, 0)),
        compiler_params=pltpu.CompilerParams(dimension_semantics=("arbitrary",),
                                             vmem_limit_bytes=VMEM_LIMIT_BYTES),
        name=name,
    )(*tiled, *resident)


def kernel(x, w_in, b_in, conv_w, conv_b, filt_w0, filt_b0, filt_w_inner, filt_b_inner, filt_w_out,
           filt_freq, hyena_skip, w_branch_attn, w_branch_hyena, w_out, ln1_g, ln1_b, w_ff1, b_ff1,
           w_ff2, b_ff2, ln2_g, ln2_b):
    bsz, seq_len, d_model = x.shape
    depth = w_in.shape[0]
    alpha = (2 * depth) ** 0.25
    rows = bsz * seq_len
    qkv_w = 3 * ATTN_QKV_WIDTH
    hy_w = (HYENA_ORDER + 1) * hyena_skip.shape[-1]
    row = lambda a: a.astype(F32)[None, :]
    h = x
    for layer in range(depth):
        w_l = w_in[layer].astype(BF16)
        b_l = b_in[layer].astype(F32)[None, :]
        hb = h.astype(BF16)
        spectrum = _filter_spectrum(seq_len, hyena_skip.shape[-1], filt_w0[layer], filt_b0[layer],
                                    filt_w_inner[layer], filt_b_inner[layer], filt_w_out[layer],
                                    filt_freq[layer])
        y_attn = _attention_branch(hb, w_l, b_l)
        y_hyena = _hyena_branch(hb, w_l, b_l, qkv_w,
                                conv_w[layer].astype(F32), conv_b[layer].astype(F32)[None, :],
                                hyena_skip[layer].astype(F32), spectrum)
        h1 = _row_tiled_call(
            functools.partial(_merge_kernel, alpha), "merge_ln", rows, d_model,
            [h.reshape(rows, d_model), y_attn.reshape(rows, -1), y_hyena.reshape(rows, -1)],
            [w_l[:, qkv_w + hy_w:], b_l[:, qkv_w + hy_w:], w_branch_attn[layer].astype(BF16),
             w_branch_hyena[layer].astype(BF16), w_out[layer].astype(BF16), row(ln1_g[layer]),
             row(ln1_b[layer])])
        h2 = _row_tiled_call(
            functools.partial(_ffn_kernel, alpha), "ffn_ln", rows, d_model, [h1],
            [w_ff1[layer].astype(BF16), row(b_ff1[layer]), w_ff2[layer].astype(BF16), row(b_ff2[layer]),
             row(ln2_g[layer]), row(ln2_b[layer])])
        h = h2.reshape(bsz, seq_len, d_model)
    return h
```

```python
import functools
import math

import jax
import jax.numpy as jnp
import numpy as np
from jax import lax
from jax.experimental import pallas as pl
from jax.experimental.pallas import tpu as pltpu

F32 = jnp.float32
BF16 = jnp.bfloat16

ATTN_GROUPS = ((128, 1), (512, 4), (2048, 16))
N_GROUPS = len(ATTN_GROUPS)
HEADS_PER_GROUP = 4
HEAD_DIM = 128
GROUP_WIDTH = HEADS_PER_GROUP * HEAD_DIM
ATTN_QKV_WIDTH = N_GROUPS * GROUP_WIDTH
Q_BLOCK = 128
ALIBI_MAX_EXP = 8.0
HYENA_ORDER = 2
FILTER_BANDS = 16
FILTER_EMB = 1 + 2 * FILTER_BANDS
FILTER_HIDDEN = 64
FILTER_INNER = 2
DECAY_TARGET = 1e-2
FAST_DECAY_PCT = 0.3
SLOW_DECAY_PCT = 1.5
LN_EPS = 1e-5

LANES = 128
VMEM_LIMIT_BYTES = 58 * 1024 * 1024

FFT_N1 = 16
FFT_N2 = 256
FFT_SLICES = FFT_N1 // 2 + 1
FFT_ROWS = FFT_N1 * FFT_N2
RSQRT2 = 1.0 / math.sqrt(2.0)


def _unit_root(k, n):
    snap = lambda v: float(round(v)) if abs(v - round(v)) < 1e-12 else v
    return snap(math.cos(2.0 * math.pi * k / n)), snap(-math.sin(2.0 * math.pi * k / n))


W16 = tuple(_unit_root(k, FFT_N1) for k in range(FFT_SLICES))


def _dot(a, b, precision=None):
    return jnp.dot(a, b, preferred_element_type=F32, precision=precision)


def _split_bf16(a):
    hi = a.astype(BF16)
    return hi, (a - hi.astype(F32)).astype(BF16)


def _dot_nt(a, b):
    return lax.dot_general(a, b, (((1,), (1,)), ((), ())), preferred_element_type=F32)


@functools.lru_cache(maxsize=None)
def _fft_constants(seq_len):
    n_fft = 2 * seq_len
    assert n_fft == FFT_N1 * FFT_N2
    k2 = np.arange(FFT_N2)
    fwd = []
    for k1 in range(FFT_SLICES):
        ang = -2.0 * np.pi * ((np.outer(FFT_N1 * k2 + k1, k2) % n_fft) / n_fft)
        er, ei = np.cos(ang), np.sin(ang)
        fwd.append(np.block([[er, -ei], [ei, er]]))
    fwd = np.stack(fwd, axis=0).astype(np.float32)
    inv = np.ascontiguousarray(np.transpose(fwd, (0, 2, 1)))
    return fwd, inv


@functools.lru_cache(maxsize=None)
def _filter_constants(seq_len, width):
    n_fft = 2 * seq_len
    t = np.linspace(0.0, 1.0, seq_len)
    bands = np.linspace(1e-4, FILTER_BANDS - 1, FILTER_BANDS)
    ang = (2.0 * np.pi / seq_len) * np.arange(seq_len)[:, None] * bands
    feats = np.concatenate([t[:, None], np.cos(ang), -np.sin(ang)], axis=-1)
    src = np.concatenate([np.arange(seq_len), [0], np.arange(seq_len - 1, 0, -1)])
    feats_ext = np.zeros((seq_len, 2 * LANES), np.float32)
    feats_ext[:, :FILTER_EMB] = feats[src[:seq_len]]
    feats_ext[:, LANES:LANES + FILTER_EMB] = feats[src[seq_len:]]
    t_mask = np.zeros((n_fft, 2), np.float32)
    t_mask[:, 0] = t[src]
    t_mask[:, 1] = 1.0
    t_mask[seq_len, 1] = 0.0
    deltas = np.abs(np.linspace(math.log(DECAY_TARGET) / SLOW_DECAY_PCT,
                                math.log(DECAY_TARGET) / FAST_DECAY_PCT, width))
    deltas = np.tile(deltas[None, :], (1, HYENA_ORDER)).astype(np.float32)
    return feats_ext, t_mask, deltas


@functools.lru_cache(maxsize=None)
def _attn_distance_tiles():
    n_off = Q_BLOCK // 2
    qi = np.arange(Q_BLOCK)[:, None]
    kj = np.arange(2 * Q_BLOCK)[None, :]
    tiles = []
    for shift in (0, n_off, 2 * n_off):
        off = np.abs(kj - shift - qi).astype(np.float32)
        tiles.append(np.where(off <= n_off, off, np.inf))
    off = np.abs(kj - qi).astype(np.float32)
    t3 = np.where(off <= n_off, off, np.inf)
    t3[:, Q_BLOCK:] = np.inf
    tiles.append(t3)
    return np.stack(tiles, axis=0).astype(np.float32)


def _alibi_slopes():
    n = N_GROUPS * HEADS_PER_GROUP
    return [2.0 ** (-ALIBI_MAX_EXP * j / n) for j in range(1, n + 1)]


def _radix8_half(z0, z1, z2, z3):
    s02, d02 = z0 + z2, z0 - z2
    s13, d13 = z1 + z3, z1 - z3
    ss, dd = s13 * RSQRT2, d13 * RSQRT2
    return {0: (s02 + s13, None), 4: (s02 - s13, None), 2: (d02, -d13),
            1: (z0 + dd, -z2 - ss), 3: (z0 - dd, z2 - ss)}


def _add(a, b, sign=1.0):
    if b is None:
        return a
    if a is None:
        return b if sign > 0 else -b
    return a + b if sign > 0 else a - b


def _cmul_const(re, im, wr, wi):
    def scaled(v, w):
        if v is None or w == 0.0:
            return None
        return v if w == 1.0 else (-v if w == -1.0 else v * w)
    if re is not None and im is not None and wr != 0.0 and abs(wr) == abs(wi):
        if wi == wr:
            return (re - im) * wr, (re + im) * wr
        return (re + im) * wr, (im - re) * wr
    return _add(scaled(re, wr), scaled(im, wi), -1.0), _add(scaled(re, wi), scaled(im, wr))


def _radix16_stage(blocks):
    even = _radix8_half(*blocks[0::2])
    odd = _radix8_half(*blocks[1::2])
    half = FFT_N1 // 2
    out = [None] * FFT_SLICES
    out[0] = (even[0][0] + odd[0][0], None)
    out[half] = (even[0][0] - odd[0][0], None)
    out[half // 2] = (even[4][0], -odd[4][0])
    for k in range(1, half // 2):
        (er, ei), (pr, pi) = even[k], _cmul_const(*odd[k], *W16[k])
        out[k] = (er + pr, ei + pi)
        out[half - k] = (er - pr, pi - ei)
    return out


def _assemble8(c0, c4, c1, c2, c3):
    (c1r, c1i), (c2r, c2i), (c3r, c3i) = c1, c2, c3
    e, o = c0 + c4, c0 - c4
    return [e + c1r + c2r + c3r,
            o + (c1r - c1i - c3r - c3i) * RSQRT2 - c2i,
            e - c1i - c2r + c3i,
            o + (c3r - c3i - c1r - c1i) * RSQRT2 + c2i]


def _inverse_radix16_stage(c):
    half = FFT_N1 // 2
    plus = [(c[k][0] + c[half - k][0], c[k][1] - c[half - k][1]) for k in range(1, half // 2)]
    minus = [_cmul_const(c[k][0] - c[half - k][0], c[k][1] + c[half - k][1], W16[k][0], -W16[k][1])
             for k in range(1, half // 2)]
    mid = c[half // 2]
    even = _assemble8(c[0][0] + c[half][0], mid[0], *plus)
    odd = _assemble8(c[0][0] - c[half][0], -mid[1], *minus)
    return [blk for pair_ in zip(even, odd) for blk in pair_]


def _slice_rows(k1):
    if k1 == 0:
        return 0, FFT_N2
    if k1 == FFT_N1 // 2:
        return FFT_N2, FFT_N2
    return 2 * FFT_N2 * k1, 2 * FFT_N2


def _store_slices(ref, off, rows, values, scale=None):
    for k1, (re, im) in enumerate(values):
        r0, n = _slice_rows(k1)
        parts = (re,) if n == FFT_N2 else (re, im)
        for j, part in enumerate(parts):
            if scale is not None:
                part = part * scale
            ref[pl.ds(r0 + j * FFT_N2 + off, rows), :] = part.astype(ref.dtype)


def _load_slices(ref, off, rows):
    out = []
    for k1 in range(FFT_SLICES):
        r0, n = _slice_rows(k1)
        re = ref[pl.ds(r0 + off, rows), :]
        out.append((re, ref[pl.ds(r0 + FFT_N2 + off, rows), :] if n > FFT_N2 else None))
    return out


def _forward_dft(b_ref, k1, mf_ref):
    r0, n = _slice_rows(k1)
    m = mf_ref[k1] if n > FFT_N2 else mf_ref[k1, :, 0:FFT_N2]
    return _dot(m, b_ref[r0:r0 + n, :])


def _inverse_dft(p, k1, mi_ref, c_ref):
    r0, n = _slice_rows(k1)
    m = mi_ref[k1] if n > FFT_N2 else mi_ref[k1, 0:FFT_N2, :]
    c_ref[r0:r0 + n, :] = _dot(m, p)


FILT_CW = 256
FILT_ROWS = 256
A_CHUNK = 16


def _filter_kernel(feats_ref, tmask_ref, w0_ref, b0_ref, wi_ref, bi_ref, freq_ref, wf_ref, wb_ref,
                   delta_ref, mf_ref, h_ref, hid_ref, ts_ref, b_ref):
    seq_len = feats_ref.shape[0]
    n_fft = 2 * seq_len
    hp = lax.Precision.HIGHEST

    @pl.when(pl.program_id(0) == 0)
    def _():
        def body(i, carry):
            rows = pl.ds(pl.multiple_of(i * FILT_ROWS, FILT_ROWS), FILT_ROWS)
            freq = freq_ref[...]
            hid = jnp.sin(freq * (_dot(feats_ref[rows, :], w0_ref[...], hp) + b0_ref[...]))
            for layer in range(FILTER_INNER):
                hid = jnp.sin(freq * (_dot(hid, wi_ref[layer], hp) + bi_ref[layer]))
            hid_ref[rows, :] = hid
            return carry
        lax.fori_loop(0, seq_len // FILT_ROWS, body, 0)

    ssq = jnp.zeros((1, FILT_CW), F32)
    half_steps = seq_len // FILT_ROWS
    for half, w_ref in enumerate((wf_ref, wb_ref)):
        w_hi, w_lo = _split_bf16(w_ref[...])

        def filt_body(i, ssq, half=half, w_hi=w_hi, w_lo=w_lo):
            start = pl.multiple_of(i * FILT_ROWS, FILT_ROWS)
            rows = pl.ds(pl.multiple_of(half * seq_len + start, FILT_ROWS), FILT_ROWS)
            h_hi, h_lo = _split_bf16(hid_ref[pl.ds(start, FILT_ROWS), :])
            raw = _dot(h_hi, w_hi) + (_dot(h_hi, w_lo) + _dot(h_lo, w_hi))
            tm = tmask_ref[rows, :]
            val = raw * jnp.exp(-tm[:, 0:1] * delta_ref[...]) * tm[:, 1:2]
            ts_ref[rows, :] = val
            return ssq + jnp.sum(val * val, axis=0, keepdims=True)
        ssq = lax.fori_loop(0, half_steps, filt_body, ssq)
    scale = lax.rsqrt(ssq)

    def a_body(i, carry):
        off = pl.multiple_of(i * A_CHUNK, A_CHUNK)
        blk = [ts_ref[pl.ds(t1 * FFT_N2 + off, A_CHUNK), :] for t1 in range(FFT_N1)]
        lo = _radix16_stage(blk[:FFT_N1 // 2])
        hi = _radix16_stage(blk[FFT_N1 // 2:])
        vals = [(_add(lo[k][0], hi[k][0], 1.0 if k % 2 == 0 else -1.0),
                 _add(lo[k][1], hi[k][1], 1.0 if k % 2 == 0 else -1.0)) for k in range(FFT_SLICES)]
        _store_slices(b_ref, off, A_CHUNK, vals, scale)
        return carry
    lax.fori_loop(0, FFT_N2 // A_CHUNK, a_body, 0)

    for k1 in range(FFT_SLICES):
        s = (1.0 if k1 in (0, FFT_N1 // 2) else 2.0) / n_fft
        h_ref[k1] = _forward_dft(b_ref, k1, mf_ref) * s


def _filter_spectrum(seq_len, width, filt_w0, filt_b0, filt_w_inner, filt_b_inner, filt_w_out, filt_freq):
    n_fft = 2 * seq_len
    hid = LANES
    feats_ext, t_mask, deltas = _filter_constants(seq_len, width)
    mf = jnp.asarray(_fft_constants(seq_len)[0]).astype(BF16)
    assert 2 * FILTER_HIDDEN == hid and FILTER_EMB <= LANES
    n_cols = HYENA_ORDER * width
    steps = n_cols // FILT_CW
    zeros = lambda r, c: jnp.zeros((r, c), F32)
    w0p = jnp.pad(filt_w0.astype(F32), ((0, LANES - FILTER_EMB), (0, 0)))
    w0 = jnp.block([[w0p, zeros(LANES, FILTER_HIDDEN)], [zeros(LANES, FILTER_HIDDEN), w0p]])
    b0 = jnp.tile(filt_b0.astype(F32), 2)[None, :]
    wi = jnp.stack([jnp.block([[w, zeros(FILTER_HIDDEN, FILTER_HIDDEN)], [zeros(FILTER_HIDDEN, FILTER_HIDDEN), w]])
                    for w in filt_w_inner.astype(F32)])
    bi = jnp.tile(filt_b_inner.astype(F32), (1, 2))[:, None, :]
    freq = jnp.tile(filt_freq.astype(F32), 2)[None, :]
    w_out = filt_w_out.astype(F32)
    wl = jnp.concatenate([jnp.pad(w_out[:, :n_cols], ((0, FILTER_HIDDEN), (0, 0))),
                          jnp.pad(w_out[:, n_cols:], ((FILTER_HIDDEN, 0), (0, 0)))], axis=1)
    full = lambda *shape: pl.BlockSpec(shape, lambda j: (0,) * len(shape))
    return pl.pallas_call(
        _filter_kernel,
        out_shape=jax.ShapeDtypeStruct((FFT_SLICES, 2 * FFT_N2, n_cols), F32),
        grid=(steps,),
        in_specs=[
            full(seq_len, 2 * LANES), full(n_fft, 2), full(2 * LANES, hid), full(1, hid),
            full(FILTER_INNER, hid, hid), full(FILTER_INNER, 1, hid), full(1, hid),
            pl.BlockSpec((hid, FILT_CW), lambda j: (0, j)),
            pl.BlockSpec((hid, FILT_CW), lambda j: (0, steps + j)),
            pl.BlockSpec((1, FILT_CW), lambda j: (0, j)),
            full(FFT_SLICES, 2 * FFT_N2, 2 * FFT_N2),
        ],
        out_specs=pl.BlockSpec((FFT_SLICES, 2 * FFT_N2, FILT_CW), lambda j: (0, 0, j)),
        scratch_shapes=[pltpu.VMEM((seq_len, hid), F32), pltpu.VMEM((n_fft, FILT_CW), F32),
                        pltpu.VMEM((FFT_ROWS, FILT_CW), BF16)],
        compiler_params=pltpu.CompilerParams(dimension_semantics=("arbitrary",),
                                             vmem_limit_bytes=VMEM_LIMIT_BYTES),
        name="hyena_filter",
    )(feats_ext, t_mask, w0, b0, wi, bi, freq, wl, wl, deltas, mf)


HY_CW = 256
CONV_ROWS = 64
PAD_ROWS = 8


def _hyena_kernel(x_ref, wv_ref, w1_ref, w2_ref, bv_ref, b1_ref, b2_ref, cwv_ref, cw1_ref, cw2_ref,
                  cbv_ref, cb1_ref, cb2_ref, skip_ref, h0_ref, h1_ref, mf_ref, mi_ref,
                  out_ref, u_ref, p_ref, b_ref, c_ref):
    seq_len = x_ref.shape[0]
    width = HY_CW
    n2 = FFT_N2
    xb = x_ref[...]

    n_slabs = width // LANES
    zeros = jnp.zeros((PAD_ROWS, LANES), F32)
    for slot in range(u_ref.shape[0]):
        for j in range(n_slabs):
            u_ref[slot, j, 0:PAD_ROWS, :] = zeros
            u_ref[slot, j, PAD_ROWS + seq_len:2 * PAD_ROWS + seq_len, :] = zeros
    parts = ((wv_ref, bv_ref, cwv_ref, cbv_ref), (w1_ref, b1_ref, cw1_ref, cb1_ref),
             (w2_ref, b2_ref, cw2_ref, cb2_ref))

    def project(p, slot):
        w_ref, bias_ref, _, _ = parts[p]
        u = _dot(xb, w_ref[...]) + bias_ref[...]
        for j in range(n_slabs):
            u_ref[slot, j, PAD_ROWS:PAD_ROWS + seq_len, :] = u[:, j * LANES:(j + 1) * LANES]

    def short_conv(p, slot):
        _, _, cw_ref, cb_ref = parts[p]

        def conv_body(i, carry):
            start = pl.multiple_of(i * CONV_ROWS, CONV_ROWS)
            cw = cw_ref[...]
            cb = cb_ref[...]
            for j in range(n_slabs):
                lanes = slice(j * LANES, (j + 1) * LANES)
                taps = [u_ref[slot, j, pl.ds(start + PAD_ROWS - 1 + k, CONV_ROWS, stride=1), :]
                        for k in range(3)]
                p_ref[p, pl.ds(start, CONV_ROWS), lanes] = (
                    cw[0:1, lanes] * taps[0] + cw[1:2, lanes] * taps[1] + cw[2:3, lanes] * taps[2]
                    + cb[:, lanes])
            return carry
        lax.fori_loop(0, seq_len // CONV_ROWS, conv_body, 0, unroll=True)

    def spectral_product(h_ref):
        for k1 in range(FFT_SLICES):
            x = _forward_dft(b_ref, k1, mf_ref)
            xr, xi = x[:n2], x[n2:]
            hr = h_ref[k1, 0:n2, :]
            hi = h_ref[k1, n2:2 * n2, :]
            prod = jnp.concatenate([(xr * hr - xi * hi).astype(BF16), (xr * hi + xi * hr).astype(BF16)], axis=0)
            _inverse_dft(prod, k1, mi_ref, c_ref)

    def gated_blocks(order, off):
        skip = skip_ref[order:order + 1, :]
        ys = _inverse_radix16_stage(_load_slices(c_ref, off, A_CHUNK))
        out = []
        for t1, y in enumerate(ys):
            rows = pl.ds(t1 * n2 + off, A_CHUNK)
            out.append(p_ref[order + 1, rows, :] * (y + skip * p_ref[0, rows, :]))
        return out

    project(0, 0)
    short_conv(0, 0)

    def a_body(i, carry):
        off = pl.multiple_of(i * A_CHUNK, A_CHUNK)
        blk = [p_ref[0, pl.ds(t1 * n2 + off, A_CHUNK), :] for t1 in range(FFT_N1 // 2)]
        _store_slices(b_ref, off, A_CHUNK, _radix16_stage(blk))
        return carry
    lax.fori_loop(0, n2 // A_CHUNK, a_body, 0, unroll=True)

    project(1, 0)
    project(2, 1)
    spectral_product(h0_ref)
    short_conv(1, 0)
    short_conv(2, 1)

    def mid_body(i, carry):
        off = pl.multiple_of(i * A_CHUNK, A_CHUNK)
        z = gated_blocks(0, off)
        for t1, blk in enumerate(z):
            p_ref[0, pl.ds(t1 * n2 + off, A_CHUNK), :] = blk
        _store_slices(b_ref, off, A_CHUNK, _radix16_stage(z))
        return carry
    lax.fori_loop(0, n2 // A_CHUNK, mid_body, 0)

    spectral_product(h1_ref)

    def out_body(i, carry):
        off = pl.multiple_of(i * A_CHUNK, A_CHUNK)
        for t1, blk in enumerate(gated_blocks(1, off)):
            out_ref[pl.ds(t1 * n2 + off, A_CHUNK), :] = blk.astype(out_ref.dtype)
        return carry
    lax.fori_loop(0, n2 // A_CHUNK, out_body, 0)


def _hyena_branch(xb, w_in, b_in, col0, conv_w, conv_b, hyena_skip, spectrum):
    bsz, seq_len, d_model = xb.shape
    width = hyena_skip.shape[-1]
    nblk = width // HY_CW
    mf, mi = (jnp.asarray(m).astype(BF16) for m in _fft_constants(seq_len))
    col = lambda part: (lambda c, b: (0, part * nblk + c))
    blk0, rem = divmod(col0, HY_CW)
    assert rem == 0
    in_col = lambda part: (lambda c, b: (0, blk0 + part * nblk + c))
    const = lambda *shape: pl.BlockSpec(shape, lambda c, b: (0,) * len(shape), pipeline_mode=pl.Buffered(1))
    spec_h = lambda order: pl.BlockSpec((FFT_SLICES, 2 * FFT_N2, HY_CW), lambda c, b: (0, 0, order * nblk + c),
                                        pipeline_mode=pl.Buffered(1))
    in_specs = [pl.BlockSpec((None, seq_len, d_model), lambda c, b: (b, 0, 0))]
    in_specs += [pl.BlockSpec((d_model, HY_CW), in_col(p)) for p in range(3)]
    in_specs += [pl.BlockSpec((1, HY_CW), in_col(p)) for p in range(3)]
    in_specs += [pl.BlockSpec((3, HY_CW), col(p)) for p in range(3)]
    in_specs += [pl.BlockSpec((1, HY_CW), col(p)) for p in range(3)]
    in_specs += [pl.BlockSpec((HYENA_ORDER, HY_CW), lambda c, b: (0, c)), spec_h(0), spec_h(1),
                 const(FFT_SLICES, 2 * FFT_N2, 2 * FFT_N2), const(FFT_SLICES, 2 * FFT_N2, 2 * FFT_N2)]
    return pl.pallas_call(
        _hyena_kernel,
        out_shape=jax.ShapeDtypeStruct((bsz, seq_len, width), BF16),
        grid=(nblk, bsz),
        in_specs=in_specs,
        out_specs=pl.BlockSpec((None, seq_len, HY_CW), lambda c, b: (b, 0, c)),
        scratch_shapes=[pltpu.VMEM((2, HY_CW // LANES, seq_len + 2 * PAD_ROWS, LANES), F32),
                        pltpu.VMEM((3, seq_len, HY_CW), F32),
                        pltpu.VMEM((FFT_ROWS, HY_CW), BF16),
                        pltpu.VMEM((FFT_ROWS, HY_CW), F32)],
        compiler_params=pltpu.CompilerParams(dimension_semantics=("arbitrary", "arbitrary"),
                                             vmem_limit_bytes=VMEM_LIMIT_BYTES),
        name="hyena_branch",
    )(xb, w_in, w_in, w_in, b_in, b_in, b_in, conv_w, conv_w, conv_w, conv_b, conv_b, conv_b,
      hyena_skip, spectrum, spectrum, mf, mi)


def _attn_kernel(x_ref, w_ref, b_ref, dist_ref, out_ref, qkv_ref, acc_ref, m_ref, l_ref,
                 s_ref, p_ref, al_ref, lb_ref):
    seq_len = x_ref.shape[0]
    xb = x_ref[...]
    slopes = _alibi_slopes()
    q_scale = 1.0 / math.sqrt(HEAD_DIM)
    heads = range(HEADS_PER_GROUP)

    def stage_scores(blk, coefs):
        qr, kr, dist, kc = blk
        for h in heads:
            s_ref[h, :, 0:kc] = _dot_nt(qkv_ref[0, h, qr, :].astype(BF16),
                                        qkv_ref[1, h, kr, :].astype(BF16)) + dist * coefs[h]

    def stage_softmax(blk, first, last):
        qr, _, _, kc = blk
        s = [s_ref[h, :, 0:kc] for h in heads]
        m_blk = [jnp.max(s[h], axis=-1, keepdims=True) for h in heads]
        if first:
            m_new = [jnp.broadcast_to(m_blk[h], (Q_BLOCK, HEAD_DIM)) for h in heads]
        else:
            m_old = [m_ref[h, qr, :] for h in heads]
            m_new = [jnp.maximum(m_old[h], m_blk[h]) for h in heads]
        p = [jnp.exp(s[h] - m_new[h][:, 0:1]) for h in heads]
        l_blk = [jnp.sum(p[h], axis=-1, keepdims=True) for h in heads]
        for h in heads:
            p_ref[h, :, 0:kc] = p[h].astype(BF16)
            lb_ref[h] = jnp.broadcast_to(l_blk[h], (Q_BLOCK, HEAD_DIM))
            if not first:
                al_ref[h] = jnp.exp(m_old[h] - m_new[h])
            if not last:
                m_ref[h, qr, :] = m_new[h]

    def stage_output(blk, first, last):
        qr, kr, _, kc = blk
        acc = [_dot(p_ref[h, :, 0:kc], qkv_ref[2, h, kr, :].astype(BF16)) for h in heads]
        l_new = [lb_ref[h] for h in heads]
        if not first:
            alpha = [al_ref[h] for h in heads]
            l_new = [alpha[h] * l_ref[h, qr, :] + l_new[h] for h in heads]
            acc = [alpha[h] * acc_ref[h, qr, :] + acc[h] for h in heads]
        for h in heads:
            if last:
                out_ref[qr, h * HEAD_DIM:(h + 1) * HEAD_DIM] = (acc[h] / l_new[h]).astype(out_ref.dtype)
            else:
                l_ref[h, qr, :] = l_new[h]
                acc_ref[h, qr, :] = acc[h]

    order = sorted(range(N_GROUPS), key=lambda g: -ATTN_GROUPS[g][1])
    assert ATTN_GROUPS[order[-1]][1] == 1
    for pos, g in enumerate(order):
        window, dil = ATTN_GROUPS[g]
        first, last = pos == 0, pos == N_GROUPS - 1
        n_off = (window // 2) // dil
        assert n_off == Q_BLOCK // 2
        stream = seq_len // dil
        nb = stream // Q_BLOCK
        n_blocks = dil * nb
        for part in range(3):
            c0 = part * ATTN_QKV_WIDTH + g * GROUP_WIDTH
            proj = _dot(xb, w_ref[:, c0:c0 + GROUP_WIDTH]) + b_ref[:, c0:c0 + GROUP_WIDTH]
            if part == 0:
                proj = proj * q_scale
            for h in heads:
                qkv_ref[part, h] = proj[:, h * HEAD_DIM:(h + 1) * HEAD_DIM]
        coefs = [-slopes[g * HEADS_PER_GROUP + h] * dil for h in heads]

        def block(i, dil=dil, nb=nb, stream=stream):
            if nb == 1:
                rows = pl.ds(i, Q_BLOCK, stride=dil) if dil > 1 else pl.ds(i, Q_BLOCK)
                return rows, rows, dist_ref[3, :, 0:Q_BLOCK], Q_BLOCK
            c = i // nb
            n = i % nb
            lo = jnp.clip(n * Q_BLOCK - n_off, 0, stream - 2 * Q_BLOCK)
            dist = dist_ref[jnp.where(n == 0, 0, jnp.where(n == nb - 1, 2, 1))]
            if dil > 1:
                return (pl.ds(c + n * Q_BLOCK * dil, Q_BLOCK, stride=dil),
                        pl.ds(c + lo * dil, 2 * Q_BLOCK, stride=dil), dist, 2 * Q_BLOCK)
            return (pl.ds(pl.multiple_of(n * Q_BLOCK, Q_BLOCK), Q_BLOCK),
                    pl.ds(pl.multiple_of(lo, n_off), 2 * Q_BLOCK), dist, 2 * Q_BLOCK)

        def steady(i, carry, block=block, coefs=coefs, first=first, last=last):
            stage_output(block(i - 2), first, last)
            stage_softmax(block(i - 1), first, last)
            stage_scores(block(i), coefs)
            return carry

        assert n_blocks >= 3
        stage_scores(block(0), coefs)
        stage_softmax(block(0), first, last)
        stage_scores(block(1), coefs)
        lax.fori_loop(2, n_blocks, steady, 0)
        stage_output(block(n_blocks - 2), first, last)
        stage_softmax(block(n_blocks - 1), first, last)
        stage_output(block(n_blocks - 1), first, last)


def _attention_branch(xb, w_qkv, b_qkv):
    bsz, seq_len, d_model = xb.shape
    dist = _attn_distance_tiles()
    qkv_w = 3 * ATTN_QKV_WIDTH
    return pl.pallas_call(
        _attn_kernel,
        out_shape=jax.ShapeDtypeStruct((bsz, seq_len, GROUP_WIDTH), BF16),
        grid=(bsz,),
        in_specs=[pl.BlockSpec((None, seq_len, d_model), lambda b: (b, 0, 0)),
                  pl.BlockSpec((d_model, qkv_w), lambda b: (0, 0), pipeline_mode=pl.Buffered(1)),
                  pl.BlockSpec((1, qkv_w), lambda b: (0, 0)),
                  pl.BlockSpec(dist.shape, lambda b: (0, 0, 0))],
        out_specs=pl.BlockSpec((None, seq_len, GROUP_WIDTH), lambda b: (b, 0, 0)),
        scratch_shapes=[pltpu.VMEM((3, HEADS_PER_GROUP, seq_len, HEAD_DIM), F32),
                        pltpu.VMEM((HEADS_PER_GROUP, seq_len, HEAD_DIM), F32),
                        pltpu.VMEM((HEADS_PER_GROUP, seq_len, HEAD_DIM), F32),
                        pltpu.VMEM((HEADS_PER_GROUP, seq_len, HEAD_DIM), F32),
                        pltpu.VMEM((HEADS_PER_GROUP, Q_BLOCK, 2 * Q_BLOCK), F32),
                        pltpu.VMEM((HEADS_PER_GROUP, Q_BLOCK, 2 * Q_BLOCK), BF16),
                        pltpu.VMEM((HEADS_PER_GROUP, Q_BLOCK, HEAD_DIM), F32),
                        pltpu.VMEM((HEADS_PER_GROUP, Q_BLOCK, HEAD_DIM), F32)],
        compiler_params=pltpu.CompilerParams(dimension_semantics=("arbitrary",),
                                             vmem_limit_bytes=VMEM_LIMIT_BYTES),
        name="dilated_attention",
    )(xb, w_qkv, b_qkv, dist)


ROW_TILE = 1024
SUB_ROWS = 512


def _layer_norm(r, g, b):
    mu = jnp.mean(r, axis=-1, keepdims=True)
    d = r - mu
    var = jnp.mean(d * d, axis=-1, keepdims=True)
    return d * lax.rsqrt(var + LN_EPS) * g + b


def _sub_tiles(ref):
    return [slice(s * SUB_ROWS, (s + 1) * SUB_ROWS) for s in range(ref.shape[0] // SUB_ROWS)]


def _merge_kernel(alpha, x_ref, ya_ref, yh_ref, wg_ref, bg_ref, wa_ref, wh_ref, wo_ref, g_ref, b_ref, out_ref):
    d_model = x_ref.shape[-1]
    tiles = _sub_tiles(x_ref)
    x = [x_ref[r, :] for r in tiles]
    gates = [jax.nn.sigmoid(_dot(xs.astype(BF16), wg_ref[...]) + bg_ref[...]) for xs in x]
    merged = [gt[:, :d_model] * _dot(ya_ref[r, :], wa_ref[...])
              + gt[:, d_model:] * _dot(yh_ref[r, :], wh_ref[...]) for gt, r in zip(gates, tiles)]
    mix = [_dot(mg.astype(BF16), wo_ref[...]) for mg in merged]
    for r, xs, mx in zip(tiles, x, mix):
        out_ref[r, :] = _layer_norm(alpha * xs + mx, g_ref[...], b_ref[...])


def _ffn_kernel(alpha, h_ref, w1_ref, b1_ref, w2_ref, b2_ref, g_ref, b_ref, out_ref):
    tiles = _sub_tiles(h_ref)
    h = [h_ref[r, :] for r in tiles]
    hid = [jnp.maximum(_dot(hs.astype(BF16), w1_ref[...]) + b1_ref[...], 0.0) for hs in h]
    ff = [_dot((hd * hd).astype(BF16), w2_ref[...]) + b2_ref[...] for hd in hid]
    for r, hs, fs in zip(tiles, h, ff):
        out_ref[r, :] = _layer_norm(alpha * hs + fs, g_ref[...], b_ref[...])


def _row_tiled_call(kernel_fn, name, rows, d_model, tiled, resident):
    in_specs = [pl.BlockSpec((ROW_TILE, a.shape[1]), lambda i: (i, 0)) for a in tiled]
    in_specs += [pl.BlockSpec(a.shape, lambda i: (0, 0), pipeline_mode=pl.Buffered(1)) for a in resident]
    return pl.pallas_call(
        kernel_fn,
        out_shape=jax.ShapeDtypeStruct((rows, d_model), F32),
        grid=(rows // ROW_TILE,),
        in_specs=in_specs,
        out_specs=pl.BlockSpec((ROW_TILE, d_model), lambda i: (i, 0)),
        compiler_params=pltpu.CompilerParams(dimension_semantics=("arbitrary",),
                                             vmem_limit_bytes=VMEM_LIMIT_BYTES),
        name=name,
    )(*tiled, *resident)


def kernel(x, w_in, b_in, conv_w, conv_b, filt_w0, filt_b0, filt_w_inner, filt_b_inner, filt_w_out,
           filt_freq, hyena_skip, w_branch_attn, w_branch_hyena, w_out, ln1_g, ln1_b, w_ff1, b_ff1,
           w_ff2, b_ff2, ln2_g, ln2_b):
    bsz, seq_len, d_model = x.shape
    depth = w_in.shape[0]
    alpha = (2 * depth) ** 0.25
    rows = bsz * seq_len
    qkv_w = 3 * ATTN_QKV_WIDTH
    hy_w = (HYENA_ORDER + 1) * hyena_skip.shape[-1]
    row = lambda a: a.astype(F32)[None, :]
    h = x
    for layer in range(depth):
        w_l = w_in[layer].astype(BF16)
        b_l = b_in[layer].astype(F32)[None, :]
        hb = h.astype(BF16)
        spectrum = _filter_spectrum(seq_len, hyena_skip.shape[-1], filt_w0[layer], filt_b0[layer],
                                    filt_w_inner[layer], filt_b_inner[layer], filt_w_out[layer],
                                    filt_freq[layer])
        y_attn = _attention_branch(hb, w_l, b_l)
        y_hyena = _hyena_branch(hb, w_l, b_l, qkv_w,
                                conv_w[layer].astype(F32), conv_b[layer].astype(F32)[None, :],
                                hyena_skip[layer].astype(F32), spectrum)
        h1 = _row_tiled_call(
            functools.partial(_merge_kernel, alpha), "merge_ln", rows, d_model,
            [h.reshape(rows, d_model), y_attn.reshape(rows, -1), y_hyena.reshape(rows, -1)],
            [w_l[:, qkv_w + hy_w:], b_l[:, qkv_w + hy_w:], w_branch_attn[layer].astype(BF16),
             w_branch_hyena[layer].astype(BF16), w_out[layer].astype(BF16), row(ln1_g[layer]),
             row(ln1_b[layer])])
        h2 = _row_tiled_call(
            functools.partial(_ffn_kernel, alpha), "ffn_ln", rows, d_model, [h1],
            [w_ff1[layer].astype(BF16), row(b_ff1[layer]), w_ff2[layer].astype(BF16), row(b_ff2[layer]),
             row(ln2_g[layer]), row(ln2_b[layer])])
        h = h2.reshape(bsz, seq_len, d_model)
    return h
```

```python
import functools
import math

import jax
import jax.numpy as jnp
import numpy as np
from jax import lax
from jax.experimental import pallas as pl
from jax.experimental.pallas import tpu as pltpu

F32 = jnp.float32
BF16 = jnp.bfloat16

ATTN_GROUPS = ((128, 1), (512, 4), (2048, 16))
N_GROUPS = len(ATTN_GROUPS)
HEADS_PER_GROUP = 4
HEAD_DIM = 128
GROUP_WIDTH = HEADS_PER_GROUP * HEAD_DIM
ATTN_QKV_WIDTH = N_GROUPS * GROUP_WIDTH
Q_BLOCK = 128
ALIBI_MAX_EXP = 8.0
HYENA_ORDER = 2
FILTER_BANDS = 16
FILTER_EMB = 1 + 2 * FILTER_BANDS
FILTER_HIDDEN = 64
FILTER_INNER = 2
DECAY_TARGET = 1e-2
FAST_DECAY_PCT = 0.3
SLOW_DECAY_PCT = 1.5
LN_EPS = 1e-5

LANES = 128
VMEM_LIMIT_BYTES = 58 * 1024 * 1024

FFT_N1 = 16
FFT_N2 = 256
FFT_SLICES = FFT_N1 // 2 + 1
FFT_ROWS = FFT_N1 * FFT_N2
RSQRT2 = 1.0 / math.sqrt(2.0)


def _unit_root(k, n):
    snap = lambda v: float(round(v)) if abs(v - round(v)) < 1e-12 else v
    return snap(math.cos(2.0 * math.pi * k / n)), snap(-math.sin(2.0 * math.pi * k / n))


W16 = tuple(_unit_root(k, FFT_N1) for k in range(FFT_SLICES))


def _dot(a, b, precision=None):
    return jnp.dot(a, b, preferred_element_type=F32, precision=precision)


def _split_bf16(a):
    hi = a.astype(BF16)
    return hi, (a - hi.astype(F32)).astype(BF16)


def _dot_nt(a, b):
    return lax.dot_general(a, b, (((1,), (1,)), ((), ())), preferred_element_type=F32)


@functools.lru_cache(maxsize=None)
def _fft_constants(seq_len):
    n_fft = 2 * seq_len
    assert n_fft == FFT_N1 * FFT_N2
    k2 = np.arange(FFT_N2)
    fwd = []
    for k1 in range(FFT_SLICES):
        ang = -2.0 * np.pi * ((np.outer(FFT_N1 * k2 + k1, k2) % n_fft) / n_fft)
        er, ei = np.cos(ang), np.sin(ang)
        fwd.append(np.block([[er, -ei], [ei, er]]))
    fwd = np.stack(fwd, axis=0).astype(np.float32)
    inv = np.ascontiguousarray(np.transpose(fwd, (0, 2, 1)))
    return fwd, inv


@functools.lru_cache(maxsize=None)
def _filter_constants(seq_len, width):
    n_fft = 2 * seq_len
    t = np.linspace(0.0, 1.0, seq_len)
    bands = np.linspace(1e-4, FILTER_BANDS - 1, FILTER_BANDS)
    ang = (2.0 * np.pi / seq_len) * np.arange(seq_len)[:, None] * bands
    feats = np.concatenate([t[:, None], np.cos(ang), -np.sin(ang)], axis=-1)
    src = np.concatenate([np.arange(seq_len), [0], np.arange(seq_len - 1, 0, -1)])
    feats_ext = np.zeros((seq_len, 2 * LANES), np.float32)
    feats_ext[:, :FILTER_EMB] = feats[src[:seq_len]]
    feats_ext[:, LANES:LANES + FILTER_EMB] = feats[src[seq_len:]]
    t_mask = np.zeros((n_fft, 2), np.float32)
    t_mask[:, 0] = t[src]
    t_mask[:, 1] = 1.0
    t_mask[seq_len, 1] = 0.0
    deltas = np.abs(np.linspace(math.log(DECAY_TARGET) / SLOW_DECAY_PCT,
                                math.log(DECAY_TARGET) / FAST_DECAY_PCT, width))
    deltas = np.tile(deltas[None, :], (1, HYENA_ORDER)).astype(np.float32)
    return feats_ext, t_mask, deltas


@functools.lru_cache(maxsize=None)
def _attn_distance_tiles():
    n_off = Q_BLOCK // 2
    qi = np.arange(Q_BLOCK)[:, None]
    kj = np.arange(2 * Q_BLOCK)[None, :]
    tiles = []
    for shift in (0, n_off, 2 * n_off):
        off = np.abs(kj - shift - qi).astype(np.float32)
        tiles.append(np.where(off <= n_off, off, np.inf))
    off = np.abs(kj - qi).astype(np.float32)
    t3 = np.where(off <= n_off, off, np.inf)
    t3[:, Q_BLOCK:] = np.inf
    tiles.append(t3)
    return np.stack(tiles, axis=0).astype(np.float32)


def _alibi_slopes():
    n = N_GROUPS * HEADS_PER_GROUP
    return [2.0 ** (-ALIBI_MAX_EXP * j / n) for j in range(1, n + 1)]


def _radix8_half(z0, z1, z2, z3):
    s02, d02 = z0 + z2, z0 - z2
    s13, d13 = z1 + z3, z1 - z3
    ss, dd = s13 * RSQRT2, d13 * RSQRT2
    return {0: (s02 + s13, None), 4: (s02 - s13, None), 2: (d02, -d13),
            1: (z0 + dd, -z2 - ss), 3: (z0 - dd, z2 - ss)}


def _add(a, b, sign=1.0):
    if b is None:
        return a
    if a is None:
        return b if sign > 0 else -b
    return a + b if sign > 0 else a - b


def _cmul_const(re, im, wr, wi):
    def scaled(v, w):
        if v is None or w == 0.0:
            return None
        return v if w == 1.0 else (-v if w == -1.0 else v * w)
    if re is not None and im is not None and wr != 0.0 and abs(wr) == abs(wi):
        if wi == wr:
            return (re - im) * wr, (re + im) * wr
        return (re + im) * wr, (im - re) * wr
    return _add(scaled(re, wr), scaled(im, wi), -1.0), _add(scaled(re, wi), scaled(im, wr))


def _radix16_stage(blocks):
    even = _radix8_half(*blocks[0::2])
    odd = _radix8_half(*blocks[1::2])
    half = FFT_N1 // 2
    out = [None] * FFT_SLICES
    out[0] = (even[0][0] + odd[0][0], None)
    out[half] = (even[0][0] - odd[0][0], None)
    out[half // 2] = (even[4][0], -odd[4][0])
    for k in range(1, half // 2):
        (er, ei), (pr, pi) = even[k], _cmul_const(*odd[k], *W16[k])
        out[k] = (er + pr, ei + pi)
        out[half - k] = (er - pr, pi - ei)
    return out


def _assemble8(c0, c4, c1, c2, c3):
    (c1r, c1i), (c2r, c2i), (c3r, c3i) = c1, c2, c3
    e, o = c0 + c4, c0 - c4
    return [e + c1r + c2r + c3r,
            o + (c1r - c1i - c3r - c3i) * RSQRT2 - c2i,
            e - c1i - c2r + c3i,
            o + (c3r - c3i - c1r - c1i) * RSQRT2 + c2i]


def _inverse_radix16_stage(c):
    half = FFT_N1 // 2
    plus = [(c[k][0] + c[half - k][0], c[k][1] - c[half - k][1]) for k in range(1, half // 2)]
    minus = [_cmul_const(c[k][0] - c[half - k][0], c[k][1] + c[half - k][1], W16[k][0], -W16[k][1])
             for k in range(1, half // 2)]
    mid = c[half // 2]
    even = _assemble8(c[0][0] + c[half][0], mid[0], *plus)
    odd = _assemble8(c[0][0] - c[half][0], -mid[1], *minus)
    return [blk for pair_ in zip(even, odd) for blk in pair_]


def _slice_rows(k1):
    if k1 == 0:
        return 0, FFT_N2
    if k1 == FFT_N1 // 2:
        return FFT_N2, FFT_N2
    return 2 * FFT_N2 * k1, 2 * FFT_N2


def _store_slices(ref, off, rows, values, scale=None):
    for k1, (re, im) in enumerate(values):
        r0, n = _slice_rows(k1)
        parts = (re,) if n == FFT_N2 else (re, im)
        for j, part in enumerate(parts):
            if scale is not None:
                part = part * scale
            ref[pl.ds(r0 + j * FFT_N2 + off, rows), :] = part.astype(ref.dtype)


def _load_slices(ref, off, rows):
    out = []
    for k1 in range(FFT_SLICES):
        r0, n = _slice_rows(k1)
        re = ref[pl.ds(r0 + off, rows), :]
        out.append((re, ref[pl.ds(r0 + FFT_N2 + off, rows), :] if n > FFT_N2 else None))
    return out


def _forward_dft(b_ref, k1, mf_ref):
    r0, n = _slice_rows(k1)
    m = mf_ref[k1] if n > FFT_N2 else mf_ref[k1, :, 0:FFT_N2]
    return _dot(m, b_ref[r0:r0 + n, :])


def _inverse_dft(p, k1, mi_ref, c_ref):
    r0, n = _slice_rows(k1)
    m = mi_ref[k1] if n > FFT_N2 else mi_ref[k1, 0:FFT_N2, :]
    c_ref[r0:r0 + n, :] = _dot(m, p)


FILT_CW = 256
FILT_ROWS = 256
A_CHUNK = 16


def _filter_kernel(feats_ref, tmask_ref, w0_ref, b0_ref, wi_ref, bi_ref, freq_ref, wf_ref, wb_ref,
                   delta_ref, mf_ref, h_ref, hid_ref, ts_ref, b_ref):
    seq_len = feats_ref.shape[0]
    n_fft = 2 * seq_len
    hp = lax.Precision.HIGHEST

    @pl.when(pl.program_id(0) == 0)
    def _():
        def body(i, carry):
            rows = pl.ds(pl.multiple_of(i * FILT_ROWS, FILT_ROWS), FILT_ROWS)
            freq = freq_ref[...]
            hid = jnp.sin(freq * (_dot(feats_ref[rows, :], w0_ref[...], hp) + b0_ref[...]))
            for layer in range(FILTER_INNER):
                hid = jnp.sin(freq * (_dot(hid, wi_ref[layer], hp) + bi_ref[layer]))
            hid_ref[rows, :] = hid
            return carry
        lax.fori_loop(0, seq_len // FILT_ROWS, body, 0)

    ssq = jnp.zeros((1, FILT_CW), F32)
    half_steps = seq_len // FILT_ROWS
    for half, w_ref in enumerate((wf_ref, wb_ref)):
        w_hi, w_lo = _split_bf16(w_ref[...])

        def filt_body(i, ssq, half=half, w_hi=w_hi, w_lo=w_lo):
            start = pl.multiple_of(i * FILT_ROWS, FILT_ROWS)
            rows = pl.ds(pl.multiple_of(half * seq_len + start, FILT_ROWS), FILT_ROWS)
            h_hi, h_lo = _split_bf16(hid_ref[pl.ds(start, FILT_ROWS), :])
            raw = _dot(h_hi, w_hi) + (_dot(h_hi, w_lo) + _dot(h_lo, w_hi))
            tm = tmask_ref[rows, :]
            val = raw * jnp.exp(-tm[:, 0:1] * delta_ref[...]) * tm[:, 1:2]
            ts_ref[rows, :] = val
            return ssq + jnp.sum(val * val, axis=0, keepdims=True)
        ssq = lax.fori_loop(0, half_steps, filt_body, ssq)
    scale = lax.rsqrt(ssq)

    def a_body(i, carry):
        off = pl.multiple_of(i * A_CHUNK, A_CHUNK)
        blk = [ts_ref[pl.ds(t1 * FFT_N2 + off, A_CHUNK), :] for t1 in range(FFT_N1)]
        lo = _radix16_stage(blk[:FFT_N1 // 2])
        hi = _radix16_stage(blk[FFT_N1 // 2:])
        vals = [(_add(lo[k][0], hi[k][0], 1.0 if k % 2 == 0 else -1.0),
                 _add(lo[k][1], hi[k][1], 1.0 if k % 2 == 0 else -1.0)) for k in range(FFT_SLICES)]
        _store_slices(b_ref, off, A_CHUNK, vals, scale)
        return carry
    lax.fori_loop(0, FFT_N2 // A_CHUNK, a_body, 0)

    for k1 in range(FFT_SLICES):
        s = (1.0 if k1 in (0, FFT_N1 // 2) else 2.0) / n_fft
        h_ref[k1] = _forward_dft(b_ref, k1, mf_ref) * s


def _filter_spectrum(seq_len, width, filt_w0, filt_b0, filt_w_inner, filt_b_inner, filt_w_out, filt_freq):
    n_fft = 2 * seq_len
    hid = LANES
    feats_ext, t_mask, deltas = _filter_constants(seq_len, width)
    mf = jnp.asarray(_fft_constants(seq_len)[0]).astype(BF16)
    assert 2 * FILTER_HIDDEN == hid and FILTER_EMB <= LANES
    n_cols = HYENA_ORDER * width
    steps = n_cols // FILT_CW
    zeros = lambda r, c: jnp.zeros((r, c), F32)
    w0p = jnp.pad(filt_w0.astype(F32), ((0, LANES - FILTER_EMB), (0, 0)))
    w0 = jnp.block([[w0p, zeros(LANES, FILTER_HIDDEN)], [zeros(LANES, FILTER_HIDDEN), w0p]])
    b0 = jnp.tile(filt_b0.astype(F32), 2)[None, :]
    wi = jnp.stack([jnp.block([[w, zeros(FILTER_HIDDEN, FILTER_HIDDEN)], [zeros(FILTER_HIDDEN, FILTER_HIDDEN), w]])
                    for w in filt_w_inner.astype(F32)])
    bi = jnp.tile(filt_b_inner.astype(F32), (1, 2))[:, None, :]
    freq = jnp.tile(filt_freq.astype(F32), 2)[None, :]
    w_out = filt_w_out.astype(F32)
    wl = jnp.concatenate([jnp.pad(w_out[:, :n_cols], ((0, FILTER_HIDDEN), (0, 0))),
                          jnp.pad(w_out[:, n_cols:], ((FILTER_HIDDEN, 0), (0, 0)))], axis=1)
    full = lambda *shape: pl.BlockSpec(shape, lambda j: (0,) * len(shape))
    return pl.pallas_call(
        _filter_kernel,
        out_shape=jax.ShapeDtypeStruct((FFT_SLICES, 2 * FFT_N2, n_cols), F32),
        grid=(steps,),
        in_specs=[
            full(seq_len, 2 * LANES), full(n_fft, 2), full(2 * LANES, hid), full(1, hid),
            full(FILTER_INNER, hid, hid), full(FILTER_INNER, 1, hid), full(1, hid),
            pl.BlockSpec((hid, FILT_CW), lambda j: (0, j)),
            pl.BlockSpec((hid, FILT_CW), lambda j: (0, steps + j)),
            pl.BlockSpec((1, FILT_CW), lambda j: (0, j)),
            full(FFT_SLICES, 2 * FFT_N2, 2 * FFT_N2),
        ],
        out_specs=pl.BlockSpec((FFT_SLICES, 2 * FFT_N2, FILT_CW), lambda j: (0, 0, j)),
        scratch_shapes=[pltpu.VMEM((seq_len, hid), F32), pltpu.VMEM((n_fft, FILT_CW), F32),
                        pltpu.VMEM((FFT_ROWS, FILT_CW), BF16)],
        compiler_params=pltpu.CompilerParams(dimension_semantics=("arbitrary",),
                                             vmem_limit_bytes=VMEM_LIMIT_BYTES),
        name="hyena_filter",
    )(feats_ext, t_mask, w0, b0, wi, bi, freq, wl, wl, deltas, mf)


HY_CW = 256
CONV_ROWS = 64
PAD_ROWS = 8


def _hyena_kernel(x_ref, wv_ref, w1_ref, w2_ref, bv_ref, b1_ref, b2_ref, cwv_ref, cw1_ref, cw2_ref,
                  cbv_ref, cb1_ref, cb2_ref, skip_ref, h0_ref, h1_ref, mf_ref, mi_ref,
                  out_ref, u_ref, p_ref, b_ref, c_ref):
    seq_len = x_ref.shape[0]
    width = HY_CW
    n2 = FFT_N2
    xb = x_ref[...]

    n_slabs = width // LANES
    zeros = jnp.zeros((PAD_ROWS, LANES), F32)
    for slot in range(u_ref.shape[0]):
        for j in range(n_slabs):
            u_ref[slot, j, 0:PAD_ROWS, :] = zeros
            u_ref[slot, j, PAD_ROWS + seq_len:2 * PAD_ROWS + seq_len, :] = zeros
    parts = ((wv_ref, bv_ref, cwv_ref, cbv_ref), (w1_ref, b1_ref, cw1_ref, cb1_ref),
             (w2_ref, b2_ref, cw2_ref, cb2_ref))

    def project(p, slot):
        w_ref, bias_ref, _, _ = parts[p]
        u = _dot(xb, w_ref[...]) + bias_ref[...]
        for j in range(n_slabs):
            u_ref[slot, j, PAD_ROWS:PAD_ROWS + seq_len, :] = u[:, j * LANES:(j + 1) * LANES]

    def short_conv(p, slot):
        _, _, cw_ref, cb_ref = parts[p]

        def conv_body(i, carry):
            start = pl.multiple_of(i * CONV_ROWS, CONV_ROWS)
            cw = cw_ref[...]
            cb = cb_ref[...]
            for j in range(n_slabs):
                lanes = slice(j * LANES, (j + 1) * LANES)
                taps = [u_ref[slot, j, pl.ds(start + PAD_ROWS - 1 + k, CONV_ROWS, stride=1), :]
                        for k in range(3)]
                p_ref[p, pl.ds(start, CONV_ROWS), lanes] = (
                    cw[0:1, lanes] * taps[0] + cw[1:2, lanes] * taps[1] + cw[2:3, lanes] * taps[2]
                    + cb[:, lanes])
            return carry
        lax.fori_loop(0, seq_len // CONV_ROWS, conv_body, 0, unroll=True)

    def spectral_product(h_ref):
        for k1 in range(FFT_SLICES):
            x = _forward_dft(b_ref, k1, mf_ref)
            xr, xi = x[:n2], x[n2:]
            hr = h_ref[k1, 0:n2, :]
            hi = h_ref[k1, n2:2 * n2, :]
            prod = jnp.concatenate([(xr * hr - xi * hi).astype(BF16), (xr * hi + xi * hr).astype(BF16)], axis=0)
            _inverse_dft(prod, k1, mi_ref, c_ref)

    def gated_blocks(order, off):
        skip = skip_ref[order:order + 1, :]
        ys = _inverse_radix16_stage(_load_slices(c_ref, off, A_CHUNK))
        out = []
        for t1, y in enumerate(ys):
            rows = pl.ds(t1 * n2 + off, A_CHUNK)
            out.append(p_ref[order + 1, rows, :] * (y + skip * p_ref[0, rows, :]))
        return out

    project(0, 0)
    short_conv(0, 0)

    def a_body(i, carry):
        off = pl.multiple_of(i * A_CHUNK, A_CHUNK)
        blk = [p_ref[0, pl.ds(t1 * n2 + off, A_CHUNK), :] for t1 in range(FFT_N1 // 2)]
        _store_slices(b_ref, off, A_CHUNK, _radix16_stage(blk))
        return carry
    lax.fori_loop(0, n2 // A_CHUNK, a_body, 0, unroll=True)

    project(1, 0)
    project(2, 1)
    spectral_product(h0_ref)
    short_conv(1, 0)
    short_conv(2, 1)

    def mid_body(i, carry):
        off = pl.multiple_of(i * A_CHUNK, A_CHUNK)
        z = gated_blocks(0, off)
        for t1, blk in enumerate(z):
            p_ref[0, pl.ds(t1 * n2 + off, A_CHUNK), :] = blk
        _store_slices(b_ref, off, A_CHUNK, _radix16_stage(z))
        return carry
    lax.fori_loop(0, n2 // A_CHUNK, mid_body, 0)

    spectral_product(h1_ref)

    def out_body(i, carry):
        off = pl.multiple_of(i * A_CHUNK, A_CHUNK)
        for t1, blk in enumerate(gated_blocks(1, off)):
            out_ref[pl.ds(t1 * n2 + off, A_CHUNK), :] = blk.astype(out_ref.dtype)
        return carry
    lax.fori_loop(0, n2 // A_CHUNK, out_body, 0)


def _hyena_branch(xb, w_in, b_in, col0, conv_w, conv_b, hyena_skip, spectrum):
    bsz, seq_len, d_model = xb.shape
    width = hyena_skip.shape[-1]
    nblk = width // HY_CW
    mf, mi = (jnp.asarray(m).astype(BF16) for m in _fft_constants(seq_len))
    col = lambda part: (lambda c, b: (0, part * nblk + c))
    blk0, rem = divmod(col0, HY_CW)
    assert rem == 0
    in_col = lambda part: (lambda c, b: (0, blk0 + part * nblk + c))
    const = lambda *shape: pl.BlockSpec(shape, lambda c, b: (0,) * len(shape), pipeline_mode=pl.Buffered(1))
    spec_h = lambda order: pl.BlockSpec((FFT_SLICES, 2 * FFT_N2, HY_CW), lambda c, b: (0, 0, order * nblk + c),
                                        pipeline_mode=pl.Buffered(1))
    in_specs = [pl.BlockSpec((None, seq_len, d_model), lambda c, b: (b, 0, 0))]
    in_specs += [pl.BlockSpec((d_model, HY_CW), in_col(p)) for p in range(3)]
    in_specs += [pl.BlockSpec((1, HY_CW), in_col(p)) for p in range(3)]
    in_specs += [pl.BlockSpec((3, HY_CW), col(p)) for p in range(3)]
    in_specs += [pl.BlockSpec((1, HY_CW), col(p)) for p in range(3)]
    in_specs += [pl.BlockSpec((HYENA_ORDER, HY_CW), lambda c, b: (0, c)), spec_h(0), spec_h(1),
                 const(FFT_SLICES, 2 * FFT_N2, 2 * FFT_N2), const(FFT_SLICES, 2 * FFT_N2, 2 * FFT_N2)]
    return pl.pallas_call(
        _hyena_kernel,
        out_shape=jax.ShapeDtypeStruct((bsz, seq_len, width), BF16),
        grid=(nblk, bsz),
        in_specs=in_specs,
        out_specs=pl.BlockSpec((None, seq_len, HY_CW), lambda c, b: (b, 0, c)),
        scratch_shapes=[pltpu.VMEM((2, HY_CW // LANES, seq_len + 2 * PAD_ROWS, LANES), F32),
                        pltpu.VMEM((3, seq_len, HY_CW), F32),
                        pltpu.VMEM((FFT_ROWS, HY_CW), BF16),
                        pltpu.VMEM((FFT_ROWS, HY_CW), F32)],
        compiler_params=pltpu.CompilerParams(dimension_semantics=("arbitrary", "arbitrary"),
                                             vmem_limit_bytes=VMEM_LIMIT_BYTES),
        name="hyena_branch",
    )(xb, w_in, w_in, w_in, b_in, b_in, b_in, conv_w, conv_w, conv_w, conv_b, conv_b, conv_b,
      hyena_skip, spectrum, spectrum, mf, mi)


def _attn_kernel(x_ref, w_ref, b_ref, dist_ref, out_ref, qkv_ref, acc_ref, m_ref, l_ref,
                 s_ref, p_ref, al_ref):
    seq_len = x_ref.shape[0]
    xb = x_ref[...]
    slopes = _alibi_slopes()
    q_scale = 1.0 / math.sqrt(HEAD_DIM)
    heads = range(HEADS_PER_GROUP)

    def stage_scores(blk, coefs):
        qr, kr, dist, kc = blk
        for h in heads:
            s_ref[h, :, 0:kc] = _dot_nt(qkv_ref[0, h, qr, :].astype(BF16),
                                        qkv_ref[1, h, kr, :].astype(BF16)) + dist * coefs[h]

    def stage_softmax(blk, first, last):
        qr, _, _, kc = blk
        s = [s_ref[h, :, 0:kc] for h in heads]
        m_blk = [jnp.max(s[h], axis=-1, keepdims=True) for h in heads]
        if first:
            m_new = [jnp.broadcast_to(m_blk[h], (Q_BLOCK, HEAD_DIM)) for h in heads]
        else:
            m_old = [m_ref[h, qr, :] for h in heads]
            m_new = [jnp.maximum(m_old[h], m_blk[h]) for h in heads]
        m_wide = [jnp.concatenate([m_new[h]] * (kc // HEAD_DIM), axis=1) for h in heads]
        for h in heads:
            p_ref[h, :, 0:kc] = jnp.exp(s[h] - m_wide[h]).astype(BF16)
            if not first:
                al_ref[h] = jnp.exp(m_old[h] - m_new[h])
            if not last:
                m_ref[h, qr, :] = m_new[h]

    def stage_output(blk, first, last):
        qr, kr, _, kc = blk
        ones = jnp.ones((kc, HEAD_DIM), BF16)
        pv = [_dot(p_ref[h, :, 0:kc], jnp.concatenate([qkv_ref[2, h, kr, :].astype(BF16), ones], axis=1))
              for h in heads]
        acc = [r[:, :HEAD_DIM] for r in pv]
        l_new = [r[:, HEAD_DIM:] for r in pv]
        if not first:
            alpha = [al_ref[h] for h in heads]
            l_new = [alpha[h] * l_ref[h, qr, :] + l_new[h] for h in heads]
            acc = [alpha[h] * acc_ref[h, qr, :] + acc[h] for h in heads]
        for h in heads:
            if last:
                out_ref[qr, h * HEAD_DIM:(h + 1) * HEAD_DIM] = (acc[h] / l_new[h]).astype(out_ref.dtype)
            else:
                l_ref[h, qr, :] = l_new[h]
                acc_ref[h, qr, :] = acc[h]

    order = sorted(range(N_GROUPS), key=lambda g: -ATTN_GROUPS[g][1])
    assert ATTN_GROUPS[order[-1]][1] == 1
    for pos, g in enumerate(order):
        window, dil = ATTN_GROUPS[g]
        first, last = pos == 0, pos == N_GROUPS - 1
        n_off = (window // 2) // dil
        assert n_off == Q_BLOCK // 2
        stream = seq_len // dil
        nb = stream // Q_BLOCK
        n_blocks = dil * nb
        for part in range(3):
            c0 = part * ATTN_QKV_WIDTH + g * GROUP_WIDTH
            proj = _dot(xb, w_ref[:, c0:c0 + GROUP_WIDTH]) + b_ref[:, c0:c0 + GROUP_WIDTH]
            if part == 0:
                proj = proj * q_scale
            for h in heads:
                qkv_ref[part, h] = proj[:, h * HEAD_DIM:(h + 1) * HEAD_DIM]
        coefs = [-slopes[g * HEADS_PER_GROUP + h] * dil for h in heads]

        def block(i, dil=dil, nb=nb, stream=stream):
            if nb == 1:
                rows = pl.ds(i, Q_BLOCK, stride=dil) if dil > 1 else pl.ds(i, Q_BLOCK)
                return rows, rows, dist_ref[3, :, 0:Q_BLOCK], Q_BLOCK
            c = i // nb
            n = i % nb
            lo = jnp.clip(n * Q_BLOCK - n_off, 0, stream - 2 * Q_BLOCK)
            dist = dist_ref[jnp.where(n == 0, 0, jnp.where(n == nb - 1, 2, 1))]
            if dil > 1:
                return (pl.ds(c + n * Q_BLOCK * dil, Q_BLOCK, stride=dil),
                        pl.ds(c + lo * dil, 2 * Q_BLOCK, stride=dil), dist, 2 * Q_BLOCK)
            return (pl.ds(pl.multiple_of(n * Q_BLOCK, Q_BLOCK), Q_BLOCK),
                    pl.ds(pl.multiple_of(lo, n_off), 2 * Q_BLOCK), dist, 2 * Q_BLOCK)

        def steady(i, carry, block=block, coefs=coefs, first=first, last=last):
            stage_output(block(i - 2), first, last)
            stage_softmax(block(i - 1), first, last)
            stage_scores(block(i), coefs)
            return carry

        assert n_blocks >= 3
        stage_scores(block(0), coefs)
        stage_softmax(block(0), first, last)
        stage_scores(block(1), coefs)
        lax.fori_loop(2, n_blocks, steady, 0)
        stage_output(block(n_blocks - 2), first, last)
        stage_softmax(block(n_blocks - 1), first, last)
        stage_output(block(n_blocks - 1), first, last)


def _attention_branch(xb, w_qkv, b_qkv):
    bsz, seq_len, d_model = xb.shape
    dist = _attn_distance_tiles()
    qkv_w = 3 * ATTN_QKV_WIDTH
    return pl.pallas_call(
        _attn_kernel,
        out_shape=jax.ShapeDtypeStruct((bsz, seq_len, GROUP_WIDTH), BF16),
        grid=(bsz,),
        in_specs=[pl.BlockSpec((None, seq_len, d_model), lambda b: (b, 0, 0)),
                  pl.BlockSpec((d_model, qkv_w), lambda b: (0, 0), pipeline_mode=pl.Buffered(1)),
                  pl.BlockSpec((1, qkv_w), lambda b: (0, 0)),
                  pl.BlockSpec(dist.shape, lambda b: (0, 0, 0))],
        out_specs=pl.BlockSpec((None, seq_len, GROUP_WIDTH), lambda b: (b, 0, 0)),
        scratch_shapes=[pltpu.VMEM((3, HEADS_PER_GROUP, seq_len, HEAD_DIM), F32),
                        pltpu.VMEM((HEADS_PER_GROUP, seq_len, HEAD_DIM), F32),
                        pltpu.VMEM((HEADS_PER_GROUP, seq_len, HEAD_DIM), F32),
                        pltpu.VMEM((HEADS_PER_GROUP, seq_len, HEAD_DIM), F32),
                        pltpu.VMEM((HEADS_PER_GROUP, Q_BLOCK, 2 * Q_BLOCK), F32),
                        pltpu.VMEM((HEADS_PER_GROUP, Q_BLOCK, 2 * Q_BLOCK), BF16),
                        pltpu.VMEM((HEADS_PER_GROUP, Q_BLOCK, HEAD_DIM), F32)],
        compiler_params=pltpu.CompilerParams(dimension_semantics=("arbitrary",),
                                             vmem_limit_bytes=VMEM_LIMIT_BYTES),
        name="dilated_attention",
    )(xb, w_qkv, b_qkv, dist)


ROW_TILE = 1024
SUB_ROWS = 512


def _layer_norm(r, g, b):
    mu = jnp.mean(r, axis=-1, keepdims=True)
    d = r - mu
    var = jnp.mean(d * d, axis=-1, keepdims=True)
    return d * lax.rsqrt(var + LN_EPS) * g + b


def _sub_tiles(ref):
    return [slice(s * SUB_ROWS, (s + 1) * SUB_ROWS) for s in range(ref.shape[0] // SUB_ROWS)]


def _merge_kernel(alpha, x_ref, ya_ref, yh_ref, wg_ref, bg_ref, wa_ref, wh_ref, wo_ref, g_ref, b_ref, out_ref):
    d_model = x_ref.shape[-1]
    tiles = _sub_tiles(x_ref)
    x = [x_ref[r, :] for r in tiles]
    gates = [jax.nn.sigmoid(_dot(xs.astype(BF16), wg_ref[...]) + bg_ref[...]) for xs in x]
    merged = [gt[:, :d_model] * _dot(ya_ref[r, :], wa_ref[...])
              + gt[:, d_model:] * _dot(yh_ref[r, :], wh_ref[...]) for gt, r in zip(gates, tiles)]
    mix = [_dot(mg.astype(BF16), wo_ref[...]) for mg in merged]
    for r, xs, mx in zip(tiles, x, mix):
        out_ref[r, :] = _layer_norm(alpha * xs + mx, g_ref[...], b_ref[...])


def _ffn_kernel(alpha, h_ref, w1_ref, b1_ref, w2_ref, b2_ref, g_ref, b_ref, out_ref):
    tiles = _sub_tiles(h_ref)
    h = [h_ref[r, :] for r in tiles]
    hid = [jnp.maximum(_dot(hs.astype(BF16), w1_ref[...]) + b1_ref[...], 0.0) for hs in h]
    ff = [_dot((hd * hd).astype(BF16), w2_ref[...]) + b2_ref[...] for hd in hid]
    for r, hs, fs in zip(tiles, h, ff):
        out_ref[r, :] = _layer_norm(alpha * hs + fs, g_ref[...], b_ref[...])


def _row_tiled_call(kernel_fn, name, rows, d_model, tiled, resident):
    in_specs = [pl.BlockSpec((ROW_TILE, a.shape[1]), lambda i: (i, 0)) for a in tiled]
    in_specs += [pl.BlockSpec(a.shape, lambda i: (0, 0), pipeline_mode=pl.Buffered(1)) for a in resident]
    return pl.pallas_call(
        kernel_fn,
        out_shape=jax.ShapeDtypeStruct((rows, d_model), F32),
        grid=(rows // ROW_TILE,),
        in_specs=in_specs,
        out_specs=pl.BlockSpec((ROW_TILE, d_model), lambda i: (i, 0)),
        compiler_params=pltpu.CompilerParams(dimension_semantics=("arbitrary",),
                                             vmem_limit_bytes=VMEM_LIMIT_BYTES),
        name=name,
    )(*tiled, *resident)


def kernel(x, w_in, b_in, conv_w, conv_b, filt_w0, filt_b0, filt_w_inner, filt_b_inner, filt_w_out,
           filt_freq, hyena_skip, w_branch_attn, w_branch_hyena, w_out, ln1_g, ln1_b, w_ff1, b_ff1,
           w_ff2, b_ff2, ln2_g, ln2_b):
    bsz, seq_len, d_model = x.shape
    depth = w_in.shape[0]
    alpha = (2 * depth) ** 0.25
    rows = bsz * seq_len
    qkv_w = 3 * ATTN_QKV_WIDTH
    hy_w = (HYENA_ORDER + 1) * hyena_skip.shape[-1]
    row = lambda a: a.astype(F32)[None, :]
    h = x
    for layer in range(depth):
        w_l = w_in[layer].astype(BF16)
        b_l = b_in[layer].astype(F32)[None, :]
        hb = h.astype(BF16)
        spectrum = _filter_spectrum(seq_len, hyena_skip.shape[-1], filt_w0[layer], filt_b0[layer],
                                    filt_w_inner[layer], filt_b_inner[layer], filt_w_out[layer],
                                    filt_freq[layer])
        y_attn = _attention_branch(hb, w_l, b_l)
        y_hyena = _hyena_branch(hb, w_l, b_l, qkv_w,
                                conv_w[layer].astype(F32), conv_b[layer].astype(F32)[None, :],
                                hyena_skip[layer].astype(F32), spectrum)
        h1 = _row_tiled_call(
            functools.partial(_merge_kernel, alpha), "merge_ln", rows, d_model,
            [h.reshape(rows, d_model), y_attn.reshape(rows, -1), y_hyena.reshape(rows, -1)],
            [w_l[:, qkv_w + hy_w:], b_l[:, qkv_w + hy_w:], w_branch_attn[layer].astype(BF16),
             w_branch_hyena[layer].astype(BF16), w_out[layer].astype(BF16), row(ln1_g[layer]),
             row(ln1_b[layer])])
        h2 = _row_tiled_call(
            functools.partial(_ffn_kernel, alpha), "ffn_ln", rows, d_model, [h1],
            [w_ff1[layer].astype(BF16), row(b_ff1[layer]), w_ff2[layer].astype(BF16), row(b_ff2[layer]),
             row(ln2_g[layer]), row(ln2_b[layer])])
        h = h2.reshape(bsz, seq_len, d_model)
    return h
```

```python
import functools
import math

import jax
import jax.numpy as jnp
import numpy as np
from jax import lax
from jax.experimental import pallas as pl
from jax.experimental.pallas import tpu as pltpu

F32 = jnp.float32
BF16 = jnp.bfloat16

ATTN_GROUPS = ((128, 1), (512, 4), (2048, 16))
N_GROUPS = len(ATTN_GROUPS)
HEADS_PER_GROUP = 4
HEAD_DIM = 128
GROUP_WIDTH = HEADS_PER_GROUP * HEAD_DIM
ATTN_QKV_WIDTH = N_GROUPS * GROUP_WIDTH
Q_BLOCK = 128
ALIBI_MAX_EXP = 8.0
HYENA_ORDER = 2
FILTER_BANDS = 16
FILTER_EMB = 1 + 2 * FILTER_BANDS
FILTER_HIDDEN = 64
FILTER_INNER = 2
DECAY_TARGET = 1e-2
FAST_DECAY_PCT = 0.3
SLOW_DECAY_PCT = 1.5
LN_EPS = 1e-5

LANES = 128
VMEM_LIMIT_BYTES = 58 * 1024 * 1024

FFT_N1 = 16
FFT_N2 = 256
FFT_SLICES = FFT_N1 // 2 + 1
FFT_ROWS = FFT_N1 * FFT_N2
RSQRT2 = 1.0 / math.sqrt(2.0)


def _unit_root(k, n):
    snap = lambda v: float(round(v)) if abs(v - round(v)) < 1e-12 else v
    return snap(math.cos(2.0 * math.pi * k / n)), snap(-math.sin(2.0 * math.pi * k / n))


W16 = tuple(_unit_root(k, FFT_N1) for k in range(FFT_SLICES))


def _dot(a, b, precision=None):
    return jnp.dot(a, b, preferred_element_type=F32, precision=precision)


def _split_bf16(a):
    hi = a.astype(BF16)
    return hi, (a - hi.astype(F32)).astype(BF16)


def _dot_nt(a, b):
    return lax.dot_general(a, b, (((1,), (1,)), ((), ())), preferred_element_type=F32)


CAST_ROWS = 256


def _cast_plan(arrays, steps):
    specs, shapes = [], []
    for a in arrays:
        rows, rem = divmod(a.shape[0], steps)
        assert rem == 0 and rows % 16 == 0, a.shape
        specs.append(pl.BlockSpec((rows, a.shape[1]), lambda i: (i, 0)))
        shapes.append(jax.ShapeDtypeStruct(a.shape, BF16))
    return specs, shapes


def _cast_blocks(src_refs, dst_refs):
    for src, dst in zip(src_refs, dst_refs):
        chunk = min(src.shape[0], CAST_ROWS)
        assert src.shape[0] % chunk == 0

        def body(i, carry, src=src, dst=dst, chunk=chunk):
            rows = pl.ds(pl.multiple_of(i * chunk, chunk), chunk)
            dst[rows, :] = src[rows, :].astype(dst.dtype)
            return carry
        lax.fori_loop(0, src.shape[0] // chunk, body, 0)


@functools.lru_cache(maxsize=None)
def _fft_constants(seq_len):
    n_fft = 2 * seq_len
    assert n_fft == FFT_N1 * FFT_N2
    k2 = np.arange(FFT_N2)
    fwd = []
    for k1 in range(FFT_SLICES):
        ang = -2.0 * np.pi * ((np.outer(FFT_N1 * k2 + k1, k2) % n_fft) / n_fft)
        er, ei = np.cos(ang), np.sin(ang)
        fwd.append(np.block([[er, -ei], [ei, er]]))
    fwd = np.stack(fwd, axis=0).astype(np.float32)
    inv = np.ascontiguousarray(np.transpose(fwd, (0, 2, 1)))
    return fwd, inv


@functools.lru_cache(maxsize=None)
def _filter_constants(seq_len, width):
    n_fft = 2 * seq_len
    t = np.linspace(0.0, 1.0, seq_len)
    bands = np.linspace(1e-4, FILTER_BANDS - 1, FILTER_BANDS)
    ang = (2.0 * np.pi / seq_len) * np.arange(seq_len)[:, None] * bands
    feats = np.concatenate([t[:, None], np.cos(ang), -np.sin(ang)], axis=-1)
    src = np.concatenate([np.arange(seq_len), [0], np.arange(seq_len - 1, 0, -1)])
    feats_ext = np.zeros((seq_len, 2 * LANES), np.float32)
    feats_ext[:, :FILTER_EMB] = feats[src[:seq_len]]
    feats_ext[:, LANES:LANES + FILTER_EMB] = feats[src[seq_len:]]
    t_mask = np.zeros((n_fft, 2), np.float32)
    t_mask[:, 0] = t[src]
    t_mask[:, 1] = 1.0
    t_mask[seq_len, 1] = 0.0
    deltas = np.abs(np.linspace(math.log(DECAY_TARGET) / SLOW_DECAY_PCT,
                                math.log(DECAY_TARGET) / FAST_DECAY_PCT, width))
    deltas = np.tile(deltas[None, :], (1, HYENA_ORDER)).astype(np.float32)
    return feats_ext, t_mask, deltas


@functools.lru_cache(maxsize=None)
def _attn_distance_tiles():
    n_off = Q_BLOCK // 2
    qi = np.arange(Q_BLOCK)[:, None]
    kj = np.arange(2 * Q_BLOCK)[None, :]
    tiles = []
    for shift in (0, n_off, 2 * n_off):
        off = np.abs(kj - shift - qi).astype(np.float32)
        tiles.append(np.where(off <= n_off, off, np.inf))
    off = np.abs(kj - qi).astype(np.float32)
    t3 = np.where(off <= n_off, off, np.inf)
    t3[:, Q_BLOCK:] = np.inf
    tiles.append(t3)
    return np.stack(tiles, axis=0).astype(np.float32)


def _alibi_slopes():
    n = N_GROUPS * HEADS_PER_GROUP
    return [2.0 ** (-ALIBI_MAX_EXP * j / n) for j in range(1, n + 1)]


def _radix8_half(z0, z1, z2, z3):
    s02, d02 = z0 + z2, z0 - z2
    s13, d13 = z1 + z3, z1 - z3
    ss, dd = s13 * RSQRT2, d13 * RSQRT2
    return {0: (s02 + s13, None), 4: (s02 - s13, None), 2: (d02, -d13),
            1: (z0 + dd, -z2 - ss), 3: (z0 - dd, z2 - ss)}


def _add(a, b, sign=1.0):
    if b is None:
        return a
    if a is None:
        return b if sign > 0 else -b
    return a + b if sign > 0 else a - b


def _cmul_const(re, im, wr, wi):
    def scaled(v, w):
        if v is None or w == 0.0:
            return None
        return v if w == 1.0 else (-v if w == -1.0 else v * w)
    if re is not None and im is not None and wr != 0.0 and abs(wr) == abs(wi):
        if wi == wr:
            return (re - im) * wr, (re + im) * wr
        return (re + im) * wr, (im - re) * wr
    return _add(scaled(re, wr), scaled(im, wi), -1.0), _add(scaled(re, wi), scaled(im, wr))


def _radix16_stage(blocks):
    even = _radix8_half(*blocks[0::2])
    odd = _radix8_half(*blocks[1::2])
    half = FFT_N1 // 2
    out = [None] * FFT_SLICES
    out[0] = (even[0][0] + odd[0][0], None)
    out[half] = (even[0][0] - odd[0][0], None)
    out[half // 2] = (even[4][0], -odd[4][0])
    for k in range(1, half // 2):
        (er, ei), (pr, pi) = even[k], _cmul_const(*odd[k], *W16[k])
        out[k] = (er + pr, ei + pi)
        out[half - k] = (er - pr, pi - ei)
    return out


def _assemble8(c0, c4, c1, c2, c3):
    (c1r, c1i), (c2r, c2i), (c3r, c3i) = c1, c2, c3
    e, o = c0 + c4, c0 - c4
    return [e + c1r + c2r + c3r,
            o + (c1r - c1i - c3r - c3i) * RSQRT2 - c2i,
            e - c1i - c2r + c3i,
            o + (c3r - c3i - c1r - c1i) * RSQRT2 + c2i]


def _inverse_radix16_stage(c):
    half = FFT_N1 // 2
    plus = [(c[k][0] + c[half - k][0], c[k][1] - c[half - k][1]) for k in range(1, half // 2)]
    minus = [_cmul_const(c[k][0] - c[half - k][0], c[k][1] + c[half - k][1], W16[k][0], -W16[k][1])
             for k in range(1, half // 2)]
    mid = c[half // 2]
    even = _assemble8(c[0][0] + c[half][0], mid[0], *plus)
    odd = _assemble8(c[0][0] - c[half][0], -mid[1], *minus)
    return [blk for pair_ in zip(even, odd) for blk in pair_]


def _slice_rows(k1):
    if k1 == 0:
        return 0, FFT_N2
    if k1 == FFT_N1 // 2:
        return FFT_N2, FFT_N2
    return 2 * FFT_N2 * k1, 2 * FFT_N2


def _store_slices(ref, off, rows, values, scale=None):
    for k1, (re, im) in enumerate(values):
        r0, n = _slice_rows(k1)
        parts = (re,) if n == FFT_N2 else (re, im)
        for j, part in enumerate(parts):
            if scale is not None:
                part = part * scale
            ref[pl.ds(r0 + j * FFT_N2 + off, rows), :] = part.astype(ref.dtype)


def _load_slices(ref, off, rows):
    out = []
    for k1 in range(FFT_SLICES):
        r0, n = _slice_rows(k1)
        re = ref[pl.ds(r0 + off, rows), :]
        out.append((re, ref[pl.ds(r0 + FFT_N2 + off, rows), :] if n > FFT_N2 else None))
    return out


def _forward_dft(b_ref, k1, mf_ref):
    r0, n = _slice_rows(k1)
    m = mf_ref[k1] if n > FFT_N2 else mf_ref[k1, :, 0:FFT_N2]
    return _dot(m, b_ref[r0:r0 + n, :])


def _inverse_dft(p, k1, mi_ref, c_ref):
    r0, n = _slice_rows(k1)
    m = mi_ref[k1] if n > FFT_N2 else mi_ref[k1, 0:FFT_N2, :]
    c_ref[r0:r0 + n, :] = _dot(m, p)


FILT_CW = 256
FILT_ROWS = 256
A_CHUNK = 16


def _filter_kernel(n_cast, feats_ref, tmask_ref, w0_ref, b0_ref, wi_ref, bi_ref, freq_ref, wf_ref, wb_ref,
                   delta_ref, mf_ref, *rest):
    cast_in, (h_ref, *cast_out), (hid_ref, ts_ref, b_ref) = (
        rest[:n_cast], rest[n_cast:2 * n_cast + 1], rest[2 * n_cast + 1:])
    _cast_blocks(cast_in, cast_out)
    seq_len = feats_ref.shape[0]
    n_fft = 2 * seq_len
    hp = lax.Precision.HIGHEST

    @pl.when(pl.program_id(0) == 0)
    def _():
        def body(i, carry):
            rows = pl.ds(pl.multiple_of(i * FILT_ROWS, FILT_ROWS), FILT_ROWS)
            freq = freq_ref[...]
            hid = jnp.sin(freq * (_dot(feats_ref[rows, :], w0_ref[...], hp) + b0_ref[...]))
            for layer in range(FILTER_INNER):
                hid = jnp.sin(freq * (_dot(hid, wi_ref[layer], hp) + bi_ref[layer]))
            hid_ref[rows, :] = hid
            return carry
        lax.fori_loop(0, seq_len // FILT_ROWS, body, 0)

    ssq = jnp.zeros((1, FILT_CW), F32)
    half_steps = seq_len // FILT_ROWS
    for half, w_ref in enumerate((wf_ref, wb_ref)):
        w_hi, w_lo = _split_bf16(w_ref[...])

        def filt_body(i, ssq, half=half, w_hi=w_hi, w_lo=w_lo):
            start = pl.multiple_of(i * FILT_ROWS, FILT_ROWS)
            rows = pl.ds(pl.multiple_of(half * seq_len + start, FILT_ROWS), FILT_ROWS)
            h_hi, h_lo = _split_bf16(hid_ref[pl.ds(start, FILT_ROWS), :])
            raw = _dot(h_hi, w_hi) + (_dot(h_hi, w_lo) + _dot(h_lo, w_hi))
            tm = tmask_ref[rows, :]
            val = raw * jnp.exp(-tm[:, 0:1] * delta_ref[...]) * tm[:, 1:2]
            ts_ref[rows, :] = val
            return ssq + jnp.sum(val * val, axis=0, keepdims=True)
        ssq = lax.fori_loop(0, half_steps, filt_body, ssq)
    scale = lax.rsqrt(ssq)

    def a_body(i, carry):
        off = pl.multiple_of(i * A_CHUNK, A_CHUNK)
        blk = [ts_ref[pl.ds(t1 * FFT_N2 + off, A_CHUNK), :] for t1 in range(FFT_N1)]
        lo = _radix16_stage(blk[:FFT_N1 // 2])
        hi = _radix16_stage(blk[FFT_N1 // 2:])
        vals = [(_add(lo[k][0], hi[k][0], 1.0 if k % 2 == 0 else -1.0),
                 _add(lo[k][1], hi[k][1], 1.0 if k % 2 == 0 else -1.0)) for k in range(FFT_SLICES)]
        _store_slices(b_ref, off, A_CHUNK, vals, scale)
        return carry
    lax.fori_loop(0, FFT_N2 // A_CHUNK, a_body, 0)

    for k1 in range(FFT_SLICES):
        s = (1.0 if k1 in (0, FFT_N1 // 2) else 2.0) / n_fft
        h_ref[k1] = _forward_dft(b_ref, k1, mf_ref) * s


def _filter_spectrum(seq_len, width, filt_w0, filt_b0, filt_w_inner, filt_b_inner, filt_w_out, filt_freq,
                     casts=()):
    n_fft = 2 * seq_len
    hid = LANES
    feats_ext, t_mask, deltas = _filter_constants(seq_len, width)
    mf = jnp.asarray(_fft_constants(seq_len)[0]).astype(BF16)
    assert 2 * FILTER_HIDDEN == hid and FILTER_EMB <= LANES
    n_cols = HYENA_ORDER * width
    steps = n_cols // FILT_CW
    zeros = lambda r, c: jnp.zeros((r, c), F32)
    w0p = jnp.pad(filt_w0.astype(F32), ((0, LANES - FILTER_EMB), (0, 0)))
    w0 = jnp.block([[w0p, zeros(LANES, FILTER_HIDDEN)], [zeros(LANES, FILTER_HIDDEN), w0p]])
    b0 = jnp.tile(filt_b0.astype(F32), 2)[None, :]
    wi = jnp.stack([jnp.block([[w, zeros(FILTER_HIDDEN, FILTER_HIDDEN)], [zeros(FILTER_HIDDEN, FILTER_HIDDEN), w]])
                    for w in filt_w_inner.astype(F32)])
    bi = jnp.tile(filt_b_inner.astype(F32), (1, 2))[:, None, :]
    freq = jnp.tile(filt_freq.astype(F32), 2)[None, :]
    w_out = filt_w_out.astype(F32)
    wl = jnp.concatenate([jnp.pad(w_out[:, :n_cols], ((0, FILTER_HIDDEN), (0, 0))),
                          jnp.pad(w_out[:, n_cols:], ((FILTER_HIDDEN, 0), (0, 0)))], axis=1)
    full = lambda *shape: pl.BlockSpec(shape, lambda j: (0,) * len(shape))
    cast_specs, cast_shapes = _cast_plan(casts, steps)
    spectrum, *cast_out = pl.pallas_call(
        functools.partial(_filter_kernel, len(casts)),
        out_shape=[jax.ShapeDtypeStruct((FFT_SLICES, 2 * FFT_N2, n_cols), F32)] + cast_shapes,
        grid=(steps,),
        in_specs=[
            full(seq_len, 2 * LANES), full(n_fft, 2), full(2 * LANES, hid), full(1, hid),
            full(FILTER_INNER, hid, hid), full(FILTER_INNER, 1, hid), full(1, hid),
            pl.BlockSpec((hid, FILT_CW), lambda j: (0, j)),
            pl.BlockSpec((hid, FILT_CW), lambda j: (0, steps + j)),
            pl.BlockSpec((1, FILT_CW), lambda j: (0, j)),
            full(FFT_SLICES, 2 * FFT_N2, 2 * FFT_N2),
        ] + cast_specs,
        out_specs=[pl.BlockSpec((FFT_SLICES, 2 * FFT_N2, FILT_CW), lambda j: (0, 0, j))] + cast_specs,
        scratch_shapes=[pltpu.VMEM((seq_len, hid), F32), pltpu.VMEM((n_fft, FILT_CW), F32),
                        pltpu.VMEM((FFT_ROWS, FILT_CW), BF16)],
        compiler_params=pltpu.CompilerParams(dimension_semantics=("arbitrary",),
                                             vmem_limit_bytes=VMEM_LIMIT_BYTES),
        name="hyena_filter",
    )(feats_ext, t_mask, w0, b0, wi, bi, freq, wl, wl, deltas, mf, *casts)
    return spectrum, cast_out


HY_CW = 256
CONV_ROWS = 64
PAD_ROWS = 8


def _hyena_kernel(x_ref, wv_ref, w1_ref, w2_ref, bv_ref, b1_ref, b2_ref, cwv_ref, cw1_ref, cw2_ref,
                  cbv_ref, cb1_ref, cb2_ref, skip_ref, h0_ref, h1_ref, mf_ref, mi_ref,
                  out_ref, u_ref, p_ref, b_ref, c_ref):
    seq_len = x_ref.shape[0]
    width = HY_CW
    n2 = FFT_N2
    xb = x_ref[...]

    n_slabs = width // LANES
    zeros = jnp.zeros((PAD_ROWS, LANES), F32)
    for slot in range(u_ref.shape[0]):
        for j in range(n_slabs):
            u_ref[slot, j, 0:PAD_ROWS, :] = zeros
            u_ref[slot, j, PAD_ROWS + seq_len:2 * PAD_ROWS + seq_len, :] = zeros
    parts = ((wv_ref, bv_ref, cwv_ref, cbv_ref), (w1_ref, b1_ref, cw1_ref, cb1_ref),
             (w2_ref, b2_ref, cw2_ref, cb2_ref))

    def project(p, slot):
        w_ref, bias_ref, _, _ = parts[p]
        u = _dot(xb, w_ref[...]) + bias_ref[...]
        for j in range(n_slabs):
            u_ref[slot, j, PAD_ROWS:PAD_ROWS + seq_len, :] = u[:, j * LANES:(j + 1) * LANES]

    def short_conv(p, slot):
        _, _, cw_ref, cb_ref = parts[p]

        def conv_body(i, carry):
            start = pl.multiple_of(i * CONV_ROWS, CONV_ROWS)
            cw = cw_ref[...]
            cb = cb_ref[...]
            for j in range(n_slabs):
                lanes = slice(j * LANES, (j + 1) * LANES)
                taps = [u_ref[slot, j, pl.ds(start + PAD_ROWS - 1 + k, CONV_ROWS, stride=1), :]
                        for k in range(3)]
                p_ref[p, pl.ds(start, CONV_ROWS), lanes] = (
                    cw[0:1, lanes] * taps[0] + cw[1:2, lanes] * taps[1] + cw[2:3, lanes] * taps[2]
                    + cb[:, lanes])
            return carry
        lax.fori_loop(0, seq_len // CONV_ROWS, conv_body, 0, unroll=True)

    def spectral_product(h_ref):
        for k1 in range(FFT_SLICES):
            x = _forward_dft(b_ref, k1, mf_ref)
            xr, xi = x[:n2], x[n2:]
            hr = h_ref[k1, 0:n2, :]
            hi = h_ref[k1, n2:2 * n2, :]
            prod = jnp.concatenate([(xr * hr - xi * hi).astype(BF16), (xr * hi + xi * hr).astype(BF16)], axis=0)
            _inverse_dft(prod, k1, mi_ref, c_ref)

    def gated_blocks(order, off):
        skip = skip_ref[order:order + 1, :]
        ys = _inverse_radix16_stage(_load_slices(c_ref, off, A_CHUNK))
        out = []
        for t1, y in enumerate(ys):
            rows = pl.ds(t1 * n2 + off, A_CHUNK)
            out.append(p_ref[order + 1, rows, :] * (y + skip * p_ref[0, rows, :]))
        return out

    project(0, 0)
    short_conv(0, 0)

    def a_body(i, carry):
        off = pl.multiple_of(i * A_CHUNK, A_CHUNK)
        blk = [p_ref[0, pl.ds(t1 * n2 + off, A_CHUNK), :] for t1 in range(FFT_N1 // 2)]
        _store_slices(b_ref, off, A_CHUNK, _radix16_stage(blk))
        return carry
    lax.fori_loop(0, n2 // A_CHUNK, a_body, 0, unroll=True)

    project(1, 0)
    project(2, 1)
    spectral_product(h0_ref)
    short_conv(1, 0)
    short_conv(2, 1)

    def mid_body(i, carry):
        off = pl.multiple_of(i * A_CHUNK, A_CHUNK)
        z = gated_blocks(0, off)
        for t1, blk in enumerate(z):
            p_ref[0, pl.ds(t1 * n2 + off, A_CHUNK), :] = blk
        _store_slices(b_ref, off, A_CHUNK, _radix16_stage(z))
        return carry
    lax.fori_loop(0, n2 // A_CHUNK, mid_body, 0)

    spectral_product(h1_ref)

    def out_body(i, carry):
        off = pl.multiple_of(i * A_CHUNK, A_CHUNK)
        for t1, blk in enumerate(gated_blocks(1, off)):
            out_ref[pl.ds(t1 * n2 + off, A_CHUNK), :] = blk.astype(out_ref.dtype)
        return carry
    lax.fori_loop(0, n2 // A_CHUNK, out_body, 0)


def _hyena_branch(xb, w_in, b_in, col0, conv_w, conv_b, hyena_skip, spectrum):
    bsz, seq_len, d_model = xb.shape
    width = hyena_skip.shape[-1]
    nblk = width // HY_CW
    mf, mi = (jnp.asarray(m).astype(BF16) for m in _fft_constants(seq_len))
    col = lambda part: (lambda c, b: (0, part * nblk + c))
    blk0, rem = divmod(col0, HY_CW)
    assert rem == 0
    in_col = lambda part: (lambda c, b: (0, blk0 + part * nblk + c))
    const = lambda *shape: pl.BlockSpec(shape, lambda c, b: (0,) * len(shape), pipeline_mode=pl.Buffered(1))
    spec_h = lambda order: pl.BlockSpec((FFT_SLICES, 2 * FFT_N2, HY_CW), lambda c, b: (0, 0, order * nblk + c),
                                        pipeline_mode=pl.Buffered(1))
    in_specs = [pl.BlockSpec((None, seq_len, d_model), lambda c, b: (b, 0, 0))]
    in_specs += [pl.BlockSpec((d_model, HY_CW), in_col(p)) for p in range(3)]
    in_specs += [pl.BlockSpec((1, HY_CW), in_col(p)) for p in range(3)]
    in_specs += [pl.BlockSpec((3, HY_CW), col(p)) for p in range(3)]
    in_specs += [pl.BlockSpec((1, HY_CW), col(p)) for p in range(3)]
    in_specs += [pl.BlockSpec((HYENA_ORDER, HY_CW), lambda c, b: (0, c)), spec_h(0), spec_h(1),
                 const(FFT_SLICES, 2 * FFT_N2, 2 * FFT_N2), const(FFT_SLICES, 2 * FFT_N2, 2 * FFT_N2)]
    return pl.pallas_call(
        _hyena_kernel,
        out_shape=jax.ShapeDtypeStruct((bsz, seq_len, width), BF16),
        grid=(nblk, bsz),
        in_specs=in_specs,
        out_specs=pl.BlockSpec((None, seq_len, HY_CW), lambda c, b: (b, 0, c)),
        scratch_shapes=[pltpu.VMEM((2, HY_CW // LANES, seq_len + 2 * PAD_ROWS, LANES), F32),
                        pltpu.VMEM((3, seq_len, HY_CW), F32),
                        pltpu.VMEM((FFT_ROWS, HY_CW), BF16),
                        pltpu.VMEM((FFT_ROWS, HY_CW), F32)],
        compiler_params=pltpu.CompilerParams(dimension_semantics=("arbitrary", "arbitrary"),
                                             vmem_limit_bytes=VMEM_LIMIT_BYTES),
        name="hyena_branch",
    )(xb, w_in, w_in, w_in, b_in, b_in, b_in, conv_w, conv_w, conv_w, conv_b, conv_b, conv_b,
      hyena_skip, spectrum, spectrum, mf, mi)


def _attn_kernel(x_ref, w_ref, b_ref, dist_ref, out_ref, qkv_ref, acc_ref, m_ref, l_ref,
                 s_ref, p_ref, al_ref):
    seq_len = x_ref.shape[0]
    xb = x_ref[...]
    slopes = _alibi_slopes()
    q_scale = 1.0 / math.sqrt(HEAD_DIM)
    heads = range(HEADS_PER_GROUP)

    def stage_scores(blk, coefs):
        qr, kr, dist, kc = blk
        for h in heads:
            s_ref[h, :, 0:kc] = _dot_nt(qkv_ref[0, h, qr, :].astype(BF16),
                                        qkv_ref[1, h, kr, :].astype(BF16)) + dist * coefs[h]

    def stage_softmax(blk, first, last):
        qr, _, _, kc = blk
        s = [s_ref[h, :, 0:kc] for h in heads]
        m_blk = [jnp.max(s[h], axis=-1, keepdims=True) for h in heads]
        if first:
            m_new = [jnp.broadcast_to(m_blk[h], (Q_BLOCK, HEAD_DIM)) for h in heads]
        else:
            m_old = [m_ref[h, qr, :] for h in heads]
            m_new = [jnp.maximum(m_old[h], m_blk[h]) for h in heads]
        m_wide = [jnp.concatenate([m_new[h]] * (kc // HEAD_DIM), axis=1) for h in heads]
        for h in heads:
            p_ref[h, :, 0:kc] = jnp.exp(s[h] - m_wide[h]).astype(BF16)
            if not first:
                al_ref[h] = jnp.exp(m_old[h] - m_new[h])
            if not last:
                m_ref[h, qr, :] = m_new[h]

    def stage_output(blk, first, last):
        qr, kr, _, kc = blk
        ones = jnp.ones((kc, HEAD_DIM), BF16)
        pv = [_dot(p_ref[h, :, 0:kc], jnp.concatenate([qkv_ref[2, h, kr, :].astype(BF16), ones], axis=1))
              for h in heads]
        acc = [r[:, :HEAD_DIM] for r in pv]
        l_new = [r[:, HEAD_DIM:] for r in pv]
        if not first:
            alpha = [al_ref[h] for h in heads]
            l_new = [alpha[h] * l_ref[h, qr, :] + l_new[h] for h in heads]
            acc = [alpha[h] * acc_ref[h, qr, :] + acc[h] for h in heads]
        for h in heads:
            if last:
                out_ref[qr, h * HEAD_DIM:(h + 1) * HEAD_DIM] = (acc[h] / l_new[h]).astype(out_ref.dtype)
            else:
                l_ref[h, qr, :] = l_new[h]
                acc_ref[h, qr, :] = acc[h]

    order = sorted(range(N_GROUPS), key=lambda g: -ATTN_GROUPS[g][1])
    assert ATTN_GROUPS[order[-1]][1] == 1
    for pos, g in enumerate(order):
        window, dil = ATTN_GROUPS[g]
        first, last = pos == 0, pos == N_GROUPS - 1
        n_off = (window // 2) // dil
        assert n_off == Q_BLOCK // 2
        stream = seq_len // dil
        nb = stream // Q_BLOCK
        n_blocks = dil * nb
        for part in range(3):
            c0 = part * ATTN_QKV_WIDTH + g * GROUP_WIDTH
            proj = _dot(xb, w_ref[:, c0:c0 + GROUP_WIDTH]) + b_ref[:, c0:c0 + GROUP_WIDTH]
            if part == 0:
                proj = proj * q_scale
            for h in heads:
                qkv_ref[part, h] = proj[:, h * HEAD_DIM:(h + 1) * HEAD_DIM]
        coefs = [-slopes[g * HEADS_PER_GROUP + h] * dil for h in heads]

        def block(i, dil=dil, nb=nb, stream=stream):
            if nb == 1:
                rows = pl.ds(i, Q_BLOCK, stride=dil) if dil > 1 else pl.ds(i, Q_BLOCK)
                return rows, rows, dist_ref[3, :, 0:Q_BLOCK], Q_BLOCK
            c = i // nb
            n = i % nb
            lo = jnp.clip(n * Q_BLOCK - n_off, 0, stream - 2 * Q_BLOCK)
            dist = dist_ref[jnp.where(n == 0, 0, jnp.where(n == nb - 1, 2, 1))]
            if dil > 1:
                return (pl.ds(c + n * Q_BLOCK * dil, Q_BLOCK, stride=dil),
                        pl.ds(c + lo * dil, 2 * Q_BLOCK, stride=dil), dist, 2 * Q_BLOCK)
            return (pl.ds(pl.multiple_of(n * Q_BLOCK, Q_BLOCK), Q_BLOCK),
                    pl.ds(pl.multiple_of(lo, n_off), 2 * Q_BLOCK), dist, 2 * Q_BLOCK)

        def steady(i, carry, block=block, coefs=coefs, first=first, last=last):
            stage_output(block(i - 2), first, last)
            stage_softmax(block(i - 1), first, last)
            stage_scores(block(i), coefs)
            return carry

        assert n_blocks >= 3
        stage_scores(block(0), coefs)
        stage_softmax(block(0), first, last)
        stage_scores(block(1), coefs)
        lax.fori_loop(2, n_blocks, steady, 0)
        stage_output(block(n_blocks - 2), first, last)
        stage_softmax(block(n_blocks - 1), first, last)
        stage_output(block(n_blocks - 1), first, last)


def _attention_branch(xb, w_qkv, b_qkv):
    bsz, seq_len, d_model = xb.shape
    dist = _attn_distance_tiles()
    qkv_w = 3 * ATTN_QKV_WIDTH
    return pl.pallas_call(
        _attn_kernel,
        out_shape=jax.ShapeDtypeStruct((bsz, seq_len, GROUP_WIDTH), BF16),
        grid=(bsz,),
        in_specs=[pl.BlockSpec((None, seq_len, d_model), lambda b: (b, 0, 0)),
                  pl.BlockSpec((d_model, qkv_w), lambda b: (0, 0), pipeline_mode=pl.Buffered(1)),
                  pl.BlockSpec((1, qkv_w), lambda b: (0, 0)),
                  pl.BlockSpec(dist.shape, lambda b: (0, 0, 0))],
        out_specs=pl.BlockSpec((None, seq_len, GROUP_WIDTH), lambda b: (b, 0, 0)),
        scratch_shapes=[pltpu.VMEM((3, HEADS_PER_GROUP, seq_len, HEAD_DIM), F32),
                        pltpu.VMEM((HEADS_PER_GROUP, seq_len, HEAD_DIM), F32),
                        pltpu.VMEM((HEADS_PER_GROUP, seq_len, HEAD_DIM), F32),
                        pltpu.VMEM((HEADS_PER_GROUP, seq_len, HEAD_DIM), F32),
                        pltpu.VMEM((HEADS_PER_GROUP, Q_BLOCK, 2 * Q_BLOCK), F32),
                        pltpu.VMEM((HEADS_PER_GROUP, Q_BLOCK, 2 * Q_BLOCK), BF16),
                        pltpu.VMEM((HEADS_PER_GROUP, Q_BLOCK, HEAD_DIM), F32)],
        compiler_params=pltpu.CompilerParams(dimension_semantics=("arbitrary",),
                                             vmem_limit_bytes=VMEM_LIMIT_BYTES),
        name="dilated_attention",
    )(xb, w_qkv, b_qkv, dist)


ROW_TILE = 1024
SUB_ROWS = 512


def _layer_norm(r, g, b):
    mu = jnp.mean(r, axis=-1, keepdims=True)
    d = r - mu
    var = jnp.mean(d * d, axis=-1, keepdims=True)
    return d * lax.rsqrt(var + LN_EPS) * g + b


def _sub_tiles(ref):
    return [slice(s * SUB_ROWS, (s + 1) * SUB_ROWS) for s in range(ref.shape[0] // SUB_ROWS)]


def _merge_kernel(alpha, x_ref, ya_ref, yh_ref, wg_ref, bg_ref, wa_ref, wh_ref, wo_ref, g_ref, b_ref, out_ref):
    d_model = x_ref.shape[-1]
    tiles = _sub_tiles(x_ref)
    x = [x_ref[r, :] for r in tiles]
    gates = [jax.nn.sigmoid(_dot(xs.astype(BF16), wg_ref[...]) + bg_ref[...]) for xs in x]
    merged = [gt[:, :d_model] * _dot(ya_ref[r, :], wa_ref[...])
              + gt[:, d_model:] * _dot(yh_ref[r, :], wh_ref[...]) for gt, r in zip(gates, tiles)]
    mix = [_dot(mg.astype(BF16), wo_ref[...]) for mg in merged]
    for r, xs, mx in zip(tiles, x, mix):
        out_ref[r, :] = _layer_norm(alpha * xs + mx, g_ref[...], b_ref[...])


def _ffn_kernel(alpha, h_ref, w1_ref, b1_ref, w2_ref, b2_ref, g_ref, b_ref, out_ref):
    tiles = _sub_tiles(h_ref)
    h = [h_ref[r, :] for r in tiles]
    hid = [jnp.maximum(_dot(hs.astype(BF16), w1_ref[...]) + b1_ref[...], 0.0) for hs in h]
    ff = [_dot((hd * hd).astype(BF16), w2_ref[...]) + b2_ref[...] for hd in hid]
    for r, hs, fs in zip(tiles, h, ff):
        out_ref[r, :] = _layer_norm(alpha * hs + fs, g_ref[...], b_ref[...])


def _row_tiled_call(kernel_fn, name, rows, d_model, tiled, resident, casts=()):
    steps = rows // ROW_TILE
    n_main = len(tiled) + len(resident)
    cast_specs, cast_shapes = _cast_plan(casts, steps)

    def body(*refs):
        cast_in, (out_ref, *cast_out) = refs[n_main:n_main + len(casts)], refs[n_main + len(casts):]
        _cast_blocks(cast_in, cast_out)
        kernel_fn(*refs[:n_main], out_ref)

    in_specs = [pl.BlockSpec((ROW_TILE, a.shape[1]), lambda i: (i, 0)) for a in tiled]
    in_specs += [pl.BlockSpec(a.shape, lambda i: (0, 0), pipeline_mode=pl.Buffered(1)) for a in resident]
    out, *cast_out = pl.pallas_call(
        body,
        out_shape=[jax.ShapeDtypeStruct((rows, d_model), F32)] + cast_shapes,
        grid=(steps,),
        in_specs=in_specs + cast_specs,
        out_specs=[pl.BlockSpec((ROW_TILE, d_model), lambda i: (i, 0))] + cast_specs,
        compiler_params=pltpu.CompilerParams(dimension_semantics=("arbitrary",),
                                             vmem_limit_bytes=VMEM_LIMIT_BYTES),
        name=name,
    )(*tiled, *resident, *casts)
    return out, cast_out


def kernel(x, w_in, b_in, conv_w, conv_b, filt_w0, filt_b0, filt_w_inner, filt_b_inner, filt_w_out,
           filt_freq, hyena_skip, w_branch_attn, w_branch_hyena, w_out, ln1_g, ln1_b, w_ff1, b_ff1,
           w_ff2, b_ff2, ln2_g, ln2_b):
    bsz, seq_len, d_model = x.shape
    depth = w_in.shape[0]
    alpha = (2 * depth) ** 0.25
    rows = bsz * seq_len
    qkv_w = 3 * ATTN_QKV_WIDTH
    hy_w = (HYENA_ORDER + 1) * hyena_skip.shape[-1]
    row = lambda a: a.astype(F32)[None, :]
    h = x
    for layer in range(depth):
        w_l = w_in[layer].astype(BF16)
        b_l = b_in[layer].astype(F32)[None, :]
        h_rows = h.reshape(rows, d_model).astype(F32)
        spectrum, (hb, wa_l, wh_l, wo_l) = _filter_spectrum(
            seq_len, hyena_skip.shape[-1], filt_w0[layer], filt_b0[layer], filt_w_inner[layer],
            filt_b_inner[layer], filt_w_out[layer], filt_freq[layer],
            casts=[h_rows, w_branch_attn[layer].astype(F32), w_branch_hyena[layer].astype(F32),
                   w_out[layer].astype(F32)])
        hb = hb.reshape(bsz, seq_len, d_model)
        y_attn = _attention_branch(hb, w_l, b_l)
        y_hyena = _hyena_branch(hb, w_l, b_l, qkv_w,
                                conv_w[layer].astype(F32), conv_b[layer].astype(F32)[None, :],
                                hyena_skip[layer].astype(F32), spectrum)
        h1, (w1_l, w2_l) = _row_tiled_call(
            functools.partial(_merge_kernel, alpha), "merge_ln", rows, d_model,
            [h_rows, y_attn.reshape(rows, -1), y_hyena.reshape(rows, -1)],
            [w_l[:, qkv_w + hy_w:], b_l[:, qkv_w + hy_w:], wa_l, wh_l, wo_l, row(ln1_g[layer]),
             row(ln1_b[layer])],
            casts=[w_ff1[layer].astype(F32), w_ff2[layer].astype(F32)])
        h2, _ = _row_tiled_call(
            functools.partial(_ffn_kernel, alpha), "ffn_ln", rows, d_model, [h1],
            [w1_l, row(b_ff1[layer]), w2_l, row(b_ff2[layer]), row(ln2_g[layer]), row(ln2_b[layer])])
        h = h2.reshape(bsz, seq_len, d_model)
    return h
```

```python
import functools
import math

import jax
import jax.numpy as jnp
import numpy as np
from jax import lax
from jax.experimental import pallas as pl
from jax.experimental.pallas import tpu as pltpu

F32 = jnp.float32
BF16 = jnp.bfloat16

ATTN_GROUPS = ((128, 1), (512, 4), (2048, 16))
N_GROUPS = len(ATTN_GROUPS)
HEADS_PER_GROUP = 4
HEAD_DIM = 128
GROUP_WIDTH = HEADS_PER_GROUP * HEAD_DIM
ATTN_QKV_WIDTH = N_GROUPS * GROUP_WIDTH
Q_BLOCK = 128
ALIBI_MAX_EXP = 8.0
HYENA_ORDER = 2
FILTER_BANDS = 16
FILTER_EMB = 1 + 2 * FILTER_BANDS
FILTER_HIDDEN = 64
FILTER_INNER = 2
DECAY_TARGET = 1e-2
FAST_DECAY_PCT = 0.3
SLOW_DECAY_PCT = 1.5
LN_EPS = 1e-5

LANES = 128
VMEM_LIMIT_BYTES = 58 * 1024 * 1024

FFT_N1 = 16
FFT_N2 = 256
FFT_SLICES = FFT_N1 // 2 + 1
FFT_ROWS = FFT_N1 * FFT_N2
RSQRT2 = 1.0 / math.sqrt(2.0)


def _unit_root(k, n):
    snap = lambda v: float(round(v)) if abs(v - round(v)) < 1e-12 else v
    return snap(math.cos(2.0 * math.pi * k / n)), snap(-math.sin(2.0 * math.pi * k / n))


W16 = tuple(_unit_root(k, FFT_N1) for k in range(FFT_SLICES))


def _dot(a, b, precision=None):
    return jnp.dot(a, b, preferred_element_type=F32, precision=precision)


def _split_bf16(a):
    hi = a.astype(BF16)
    return hi, (a - hi.astype(F32)).astype(BF16)


def _dot_nt(a, b):
    return lax.dot_general(a, b, (((1,), (1,)), ((), ())), preferred_element_type=F32)


CAST_ROWS = 256


def _cast_plan(arrays, steps):
    specs, shapes = [], []
    for a in arrays:
        rows, rem = divmod(a.shape[0], steps)
        assert rem == 0 and rows % 16 == 0, a.shape
        specs.append(pl.BlockSpec((rows, a.shape[1]), lambda i: (i, 0)))
        shapes.append(jax.ShapeDtypeStruct(a.shape, BF16))
    return specs, shapes


def _cast_blocks(src_refs, dst_refs):
    for src, dst in zip(src_refs, dst_refs):
        chunk = min(src.shape[0], CAST_ROWS)
        assert src.shape[0] % chunk == 0

        def body(i, carry, src=src, dst=dst, chunk=chunk):
            rows = pl.ds(pl.multiple_of(i * chunk, chunk), chunk)
            dst[rows, :] = src[rows, :].astype(dst.dtype)
            return carry
        lax.fori_loop(0, src.shape[0] // chunk, body, 0)


@functools.lru_cache(maxsize=None)
def _fft_constants(seq_len):
    n_fft = 2 * seq_len
    assert n_fft == FFT_N1 * FFT_N2
    k2 = np.arange(FFT_N2)
    fwd = []
    for k1 in range(FFT_SLICES):
        ang = -2.0 * np.pi * ((np.outer(FFT_N1 * k2 + k1, k2) % n_fft) / n_fft)
        er, ei = np.cos(ang), np.sin(ang)
        fwd.append(np.block([[er, -ei], [ei, er]]))
    fwd = np.stack(fwd, axis=0).astype(np.float32)
    inv = np.ascontiguousarray(np.transpose(fwd, (0, 2, 1)))
    return fwd, inv


@functools.lru_cache(maxsize=None)
def _filter_constants(seq_len, width):
    n_fft = 2 * seq_len
    t = np.linspace(0.0, 1.0, seq_len)
    bands = np.linspace(1e-4, FILTER_BANDS - 1, FILTER_BANDS)
    ang = (2.0 * np.pi / seq_len) * np.arange(seq_len)[:, None] * bands
    feats = np.concatenate([t[:, None], np.cos(ang), -np.sin(ang)], axis=-1)
    src = np.concatenate([np.arange(seq_len), [0], np.arange(seq_len - 1, 0, -1)])
    feats_ext = np.zeros((seq_len, 2 * LANES), np.float32)
    feats_ext[:, :FILTER_EMB] = feats[src[:seq_len]]
    feats_ext[:, LANES:LANES + FILTER_EMB] = feats[src[seq_len:]]
    t_mask = np.zeros((n_fft, 2), np.float32)
    t_mask[:, 0] = t[src]
    t_mask[:, 1] = 1.0
    t_mask[seq_len, 1] = 0.0
    deltas = np.abs(np.linspace(math.log(DECAY_TARGET) / SLOW_DECAY_PCT,
                                math.log(DECAY_TARGET) / FAST_DECAY_PCT, width))
    deltas = np.tile(deltas[None, :], (1, HYENA_ORDER)).astype(np.float32)
    return feats_ext, t_mask, deltas


@functools.lru_cache(maxsize=None)
def _attn_distance_tiles():
    n_off = Q_BLOCK // 2
    qi = np.arange(Q_BLOCK)[:, None]
    kj = np.arange(2 * Q_BLOCK)[None, :]
    tiles = []
    for shift in (0, n_off, 2 * n_off):
        off = np.abs(kj - shift - qi).astype(np.float32)
        tiles.append(np.where(off <= n_off, off, np.inf))
    off = np.abs(kj - qi).astype(np.float32)
    t3 = np.where(off <= n_off, off, np.inf)
    t3[:, Q_BLOCK:] = np.inf
    tiles.append(t3)
    return np.stack(tiles, axis=0).astype(np.float32)


def _alibi_slopes():
    n = N_GROUPS * HEADS_PER_GROUP
    return [2.0 ** (-ALIBI_MAX_EXP * j / n) for j in range(1, n + 1)]


def _radix8_half(z0, z1, z2, z3):
    s02, d02 = z0 + z2, z0 - z2
    s13, d13 = z1 + z3, z1 - z3
    ss, dd = s13 * RSQRT2, d13 * RSQRT2
    return {0: (s02 + s13, None), 4: (s02 - s13, None), 2: (d02, -d13),
            1: (z0 + dd, -z2 - ss), 3: (z0 - dd, z2 - ss)}


def _add(a, b, sign=1.0):
    if b is None:
        return a
    if a is None:
        return b if sign > 0 else -b
    return a + b if sign > 0 else a - b


def _cmul_const(re, im, wr, wi):
    def scaled(v, w):
        if v is None or w == 0.0:
            return None
        return v if w == 1.0 else (-v if w == -1.0 else v * w)
    if re is not None and im is not None and wr != 0.0 and abs(wr) == abs(wi):
        if wi == wr:
            return (re - im) * wr, (re + im) * wr
        return (re + im) * wr, (im - re) * wr
    return _add(scaled(re, wr), scaled(im, wi), -1.0), _add(scaled(re, wi), scaled(im, wr))


def _radix16_stage(blocks):
    even = _radix8_half(*blocks[0::2])
    odd = _radix8_half(*blocks[1::2])
    half = FFT_N1 // 2
    out = [None] * FFT_SLICES
    out[0] = (even[0][0] + odd[0][0], None)
    out[half] = (even[0][0] - odd[0][0], None)
    out[half // 2] = (even[4][0], -odd[4][0])
    for k in range(1, half // 2):
        (er, ei), (pr, pi) = even[k], _cmul_const(*odd[k], *W16[k])
        out[k] = (er + pr, ei + pi)
        out[half - k] = (er - pr, pi - ei)
    return out


def _assemble8(c0, c4, c1, c2, c3):
    (c1r, c1i), (c2r, c2i), (c3r, c3i) = c1, c2, c3
    e, o = c0 + c4, c0 - c4
    return [e + c1r + c2r + c3r,
            o + (c1r - c1i - c3r - c3i) * RSQRT2 - c2i,
            e - c1i - c2r + c3i,
            o + (c3r - c3i - c1r - c1i) * RSQRT2 + c2i]


def _inverse_radix16_stage(c):
    half = FFT_N1 // 2
    plus = [(c[k][0] + c[half - k][0], c[k][1] - c[half - k][1]) for k in range(1, half // 2)]
    minus = [_cmul_const(c[k][0] - c[half - k][0], c[k][1] + c[half - k][1], W16[k][0], -W16[k][1])
             for k in range(1, half // 2)]
    mid = c[half // 2]
    even = _assemble8(c[0][0] + c[half][0], mid[0], *plus)
    odd = _assemble8(c[0][0] - c[half][0], -mid[1], *minus)
    return [blk for pair_ in zip(even, odd) for blk in pair_]


def _slice_rows(k1):
    if k1 == 0:
        return 0, FFT_N2
    if k1 == FFT_N1 // 2:
        return FFT_N2, FFT_N2
    return 2 * FFT_N2 * k1, 2 * FFT_N2


def _store_slices(ref, off, rows, values, scale=None):
    for k1, (re, im) in enumerate(values):
        r0, n = _slice_rows(k1)
        parts = (re,) if n == FFT_N2 else (re, im)
        for j, part in enumerate(parts):
            if scale is not None:
                part = part * scale
            ref[pl.ds(r0 + j * FFT_N2 + off, rows), :] = part.astype(ref.dtype)


def _load_slices(ref, off, rows):
    out = []
    for k1 in range(FFT_SLICES):
        r0, n = _slice_rows(k1)
        re = ref[pl.ds(r0 + off, rows), :]
        out.append((re, ref[pl.ds(r0 + FFT_N2 + off, rows), :] if n > FFT_N2 else None))
    return out


def _forward_dft(b_ref, k1, mf_ref):
    r0, n = _slice_rows(k1)
    m = mf_ref[k1] if n > FFT_N2 else mf_ref[k1, :, 0:FFT_N2]
    return _dot(m, b_ref[r0:r0 + n, :])


def _inverse_dft(p, k1, mi_ref, c_ref):
    r0, n = _slice_rows(k1)
    m = mi_ref[k1] if n > FFT_N2 else mi_ref[k1, 0:FFT_N2, :]
    c_ref[r0:r0 + n, :] = _dot(m, p)


FILT_CW = 256
FILT_ROWS = 256
A_CHUNK = 16


def _filter_kernel(n_cast, feats_ref, tmask_ref, w0_ref, b0_ref, wi_ref, bi_ref, freq_ref, wf_ref, wb_ref,
                   delta_ref, mf_ref, *rest):
    cast_in, (h_ref, *cast_out), (hid_ref, ts_ref, b_ref) = (
        rest[:n_cast], rest[n_cast:2 * n_cast + 1], rest[2 * n_cast + 1:])
    _cast_blocks(cast_in, cast_out)
    seq_len = feats_ref.shape[0]
    n_fft = 2 * seq_len
    hp = lax.Precision.HIGHEST

    @pl.when(pl.program_id(0) == 0)
    def _():
        def body(i, carry):
            rows = pl.ds(pl.multiple_of(i * FILT_ROWS, FILT_ROWS), FILT_ROWS)
            freq = freq_ref[...]
            hid = jnp.sin(freq * (_dot(feats_ref[rows, :], w0_ref[...], hp) + b0_ref[...]))
            for layer in range(FILTER_INNER):
                hid = jnp.sin(freq * (_dot(hid, wi_ref[layer], hp) + bi_ref[layer]))
            hid_ref[rows, :] = hid
            return carry
        lax.fori_loop(0, seq_len // FILT_ROWS, body, 0)

    ssq = jnp.zeros((1, FILT_CW), F32)
    half_steps = seq_len // FILT_ROWS
    for half, w_ref in enumerate((wf_ref, wb_ref)):
        w_hi, w_lo = _split_bf16(w_ref[...])

        def filt_body(i, ssq, half=half, w_hi=w_hi, w_lo=w_lo):
            start = pl.multiple_of(i * FILT_ROWS, FILT_ROWS)
            rows = pl.ds(pl.multiple_of(half * seq_len + start, FILT_ROWS), FILT_ROWS)
            h_hi, h_lo = _split_bf16(hid_ref[pl.ds(start, FILT_ROWS), :])
            raw = _dot(h_hi, w_hi) + (_dot(h_hi, w_lo) + _dot(h_lo, w_hi))
            tm = tmask_ref[rows, :]
            val = raw * jnp.exp(-tm[:, 0:1] * delta_ref[...]) * tm[:, 1:2]
            ts_ref[rows, :] = val
            return ssq + jnp.sum(val * val, axis=0, keepdims=True)
        ssq = lax.fori_loop(0, half_steps, filt_body, ssq)
    scale = lax.rsqrt(ssq)

    def a_body(i, carry):
        off = pl.multiple_of(i * A_CHUNK, A_CHUNK)
        blk = [ts_ref[pl.ds(t1 * FFT_N2 + off, A_CHUNK), :] for t1 in range(FFT_N1)]
        lo = _radix16_stage(blk[:FFT_N1 // 2])
        hi = _radix16_stage(blk[FFT_N1 // 2:])
        vals = [(_add(lo[k][0], hi[k][0], 1.0 if k % 2 == 0 else -1.0),
                 _add(lo[k][1], hi[k][1], 1.0 if k % 2 == 0 else -1.0)) for k in range(FFT_SLICES)]
        _store_slices(b_ref, off, A_CHUNK, vals, scale)
        return carry
    lax.fori_loop(0, FFT_N2 // A_CHUNK, a_body, 0)

    for k1 in range(FFT_SLICES):
        s = (1.0 if k1 in (0, FFT_N1 // 2) else 2.0) / n_fft
        h_ref[k1] = _forward_dft(b_ref, k1, mf_ref) * s


def _filter_spectrum(seq_len, width, filt_w0, filt_b0, filt_w_inner, filt_b_inner, filt_w_out, filt_freq,
                     casts=()):
    n_fft = 2 * seq_len
    hid = LANES
    feats_ext, t_mask, deltas = _filter_constants(seq_len, width)
    mf = jnp.asarray(_fft_constants(seq_len)[0]).astype(BF16)
    assert 2 * FILTER_HIDDEN == hid and FILTER_EMB <= LANES
    n_cols = HYENA_ORDER * width
    steps = n_cols // FILT_CW
    zeros = lambda r, c: jnp.zeros((r, c), F32)
    w0p = jnp.pad(filt_w0.astype(F32), ((0, LANES - FILTER_EMB), (0, 0)))
    w0 = jnp.block([[w0p, zeros(LANES, FILTER_HIDDEN)], [zeros(LANES, FILTER_HIDDEN), w0p]])
    b0 = jnp.tile(filt_b0.astype(F32), 2)[None, :]
    wi = jnp.stack([jnp.block([[w, zeros(FILTER_HIDDEN, FILTER_HIDDEN)], [zeros(FILTER_HIDDEN, FILTER_HIDDEN), w]])
                    for w in filt_w_inner.astype(F32)])
    bi = jnp.tile(filt_b_inner.astype(F32), (1, 2))[:, None, :]
    freq = jnp.tile(filt_freq.astype(F32), 2)[None, :]
    w_out = filt_w_out.astype(F32)
    wl = jnp.concatenate([jnp.pad(w_out[:, :n_cols], ((0, FILTER_HIDDEN), (0, 0))),
                          jnp.pad(w_out[:, n_cols:], ((FILTER_HIDDEN, 0), (0, 0)))], axis=1)
    full = lambda *shape: pl.BlockSpec(shape, lambda j: (0,) * len(shape))
    cast_specs, cast_shapes = _cast_plan(casts, steps)
    spectrum, *cast_out = pl.pallas_call(
        functools.partial(_filter_kernel, len(casts)),
        out_shape=[jax.ShapeDtypeStruct((FFT_SLICES, 2 * FFT_N2, n_cols), F32)] + cast_shapes,
        grid=(steps,),
        in_specs=[
            full(seq_len, 2 * LANES), full(n_fft, 2), full(2 * LANES, hid), full(1, hid),
            full(FILTER_INNER, hid, hid), full(FILTER_INNER, 1, hid), full(1, hid),
            pl.BlockSpec((hid, FILT_CW), lambda j: (0, j)),
            pl.BlockSpec((hid, FILT_CW), lambda j: (0, steps + j)),
            pl.BlockSpec((1, FILT_CW), lambda j: (0, j)),
            full(FFT_SLICES, 2 * FFT_N2, 2 * FFT_N2),
        ] + cast_specs,
        out_specs=[pl.BlockSpec((FFT_SLICES, 2 * FFT_N2, FILT_CW), lambda j: (0, 0, j))] + cast_specs,
        scratch_shapes=[pltpu.VMEM((seq_len, hid), F32), pltpu.VMEM((n_fft, FILT_CW), F32),
                        pltpu.VMEM((FFT_ROWS, FILT_CW), BF16)],
        compiler_params=pltpu.CompilerParams(dimension_semantics=("arbitrary",),
                                             vmem_limit_bytes=VMEM_LIMIT_BYTES),
        name="hyena_filter",
    )(feats_ext, t_mask, w0, b0, wi, bi, freq, wl, wl, deltas, mf, *casts)
    return spectrum, cast_out


HY_CW = 256
CONV_ROWS = 64
PAD_ROWS = 8


def _hyena_kernel(x_ref, wv_ref, w1_ref, w2_ref, bv_ref, b1_ref, b2_ref, cwv_ref, cw1_ref, cw2_ref,
                  cbv_ref, cb1_ref, cb2_ref, skip_ref, h0_ref, h1_ref, mf_ref, mi_ref,
                  out_ref, u_ref, p_ref, b_ref, c_ref):
    seq_len = x_ref.shape[0]
    width = HY_CW
    n2 = FFT_N2
    xb = x_ref[...]

    n_slabs = width // LANES
    zeros = jnp.zeros((PAD_ROWS, LANES), F32)
    for slot in range(u_ref.shape[0]):
        for j in range(n_slabs):
            u_ref[slot, j, 0:PAD_ROWS, :] = zeros
            u_ref[slot, j, PAD_ROWS + seq_len:2 * PAD_ROWS + seq_len, :] = zeros
    parts = ((wv_ref, bv_ref, cwv_ref, cbv_ref), (w1_ref, b1_ref, cw1_ref, cb1_ref),
             (w2_ref, b2_ref, cw2_ref, cb2_ref))

    def project(p, slot):
        w_ref, bias_ref, _, _ = parts[p]
        half = seq_len // 2
        for r0 in (0, half):
            u = _dot(xb[r0:r0 + half], w_ref[...]) + bias_ref[...]
            for j in range(n_slabs):
                u_ref[slot, j, PAD_ROWS + r0:PAD_ROWS + r0 + half, :] = u[:, j * LANES:(j + 1) * LANES]

    def short_conv(p, slot):
        _, _, cw_ref, cb_ref = parts[p]

        def conv_body(i, carry):
            start = pl.multiple_of(i * CONV_ROWS, CONV_ROWS)
            cw = cw_ref[...]
            cb = cb_ref[...]
            for j in range(n_slabs):
                lanes = slice(j * LANES, (j + 1) * LANES)
                taps = [u_ref[slot, j, pl.ds(start + PAD_ROWS - 1 + k, CONV_ROWS, stride=1), :]
                        for k in range(3)]
                p_ref[p, pl.ds(start, CONV_ROWS), lanes] = (
                    cw[0:1, lanes] * taps[0] + cw[1:2, lanes] * taps[1] + cw[2:3, lanes] * taps[2]
                    + cb[:, lanes])
            return carry
        lax.fori_loop(0, seq_len // CONV_ROWS, conv_body, 0, unroll=True)

    def spectral_product(h_ref):
        x_next = _forward_dft(b_ref, 0, mf_ref)
        for k1 in range(FFT_SLICES):
            x = x_next
            if k1 + 1 < FFT_SLICES:
                x_next = _forward_dft(b_ref, k1 + 1, mf_ref)
            xr, xi = x[:n2], x[n2:]
            hr = h_ref[k1, 0:n2, :]
            hi = h_ref[k1, n2:2 * n2, :]
            prod = jnp.concatenate([(xr * hr - xi * hi).astype(BF16), (xr * hi + xi * hr).astype(BF16)], axis=0)
            _inverse_dft(prod, k1, mi_ref, c_ref)

    def gated_blocks(order, off):
        skip = skip_ref[order:order + 1, :]
        ys = _inverse_radix16_stage(_load_slices(c_ref, off, A_CHUNK))
        out = []
        for t1, y in enumerate(ys):
            rows = pl.ds(t1 * n2 + off, A_CHUNK)
            out.append(p_ref[order + 1, rows, :] * (y + skip * p_ref[0, rows, :]))
        return out

    project(0, 0)
    short_conv(0, 0)

    def a_body(i, carry):
        off = pl.multiple_of(i * A_CHUNK, A_CHUNK)
        blk = [p_ref[0, pl.ds(t1 * n2 + off, A_CHUNK), :] for t1 in range(FFT_N1 // 2)]
        _store_slices(b_ref, off, A_CHUNK, _radix16_stage(blk))
        return carry
    lax.fori_loop(0, n2 // A_CHUNK, a_body, 0, unroll=True)

    project(1, 0)
    project(2, 1)
    spectral_product(h0_ref)
    short_conv(1, 0)
    short_conv(2, 1)

    def mid_body(i, carry):
        off = pl.multiple_of(i * A_CHUNK, A_CHUNK)
        z = gated_blocks(0, off)
        for t1, blk in enumerate(z):
            p_ref[0, pl.ds(t1 * n2 + off, A_CHUNK), :] = blk
        _store_slices(b_ref, off, A_CHUNK, _radix16_stage(z))
        return carry
    lax.fori_loop(0, n2 // A_CHUNK, mid_body, 0)

    spectral_product(h1_ref)

    def out_body(i, carry):
        off = pl.multiple_of(i * A_CHUNK, A_CHUNK)
        for t1, blk in enumerate(gated_blocks(1, off)):
            out_ref[pl.ds(t1 * n2 + off, A_CHUNK), :] = blk.astype(out_ref.dtype)
        return carry
    lax.fori_loop(0, n2 // A_CHUNK, out_body, 0)


def _hyena_branch(xb, w_in, b_in, col0, conv_w, conv_b, hyena_skip, spectrum):
    bsz, seq_len, d_model = xb.shape
    width = hyena_skip.shape[-1]
    nblk = width // HY_CW
    mf, mi = (jnp.asarray(m).astype(BF16) for m in _fft_constants(seq_len))
    col = lambda part: (lambda c, b: (0, part * nblk + c))
    blk0, rem = divmod(col0, HY_CW)
    assert rem == 0
    in_col = lambda part: (lambda c, b: (0, blk0 + part * nblk + c))
    const = lambda *shape: pl.BlockSpec(shape, lambda c, b: (0,) * len(shape), pipeline_mode=pl.Buffered(1))
    spec_h = lambda order: pl.BlockSpec((FFT_SLICES, 2 * FFT_N2, HY_CW), lambda c, b: (0, 0, order * nblk + c),
                                        pipeline_mode=pl.Buffered(1))
    in_specs = [pl.BlockSpec((None, seq_len, d_model), lambda c, b: (b, 0, 0))]
    in_specs += [pl.BlockSpec((d_model, HY_CW), in_col(p)) for p in range(3)]
    in_specs += [pl.BlockSpec((1, HY_CW), in_col(p)) for p in range(3)]
    in_specs += [pl.BlockSpec((3, HY_CW), col(p)) for p in range(3)]
    in_specs += [pl.BlockSpec((1, HY_CW), col(p)) for p in range(3)]
    in_specs += [pl.BlockSpec((HYENA_ORDER, HY_CW), lambda c, b: (0, c)), spec_h(0), spec_h(1),
                 const(FFT_SLICES, 2 * FFT_N2, 2 * FFT_N2), const(FFT_SLICES, 2 * FFT_N2, 2 * FFT_N2)]
    return pl.pallas_call(
        _hyena_kernel,
        out_shape=jax.ShapeDtypeStruct((bsz, seq_len, width), BF16),
        grid=(nblk, bsz),
        in_specs=in_specs,
        out_specs=pl.BlockSpec((None, seq_len, HY_CW), lambda c, b: (b, 0, c)),
        scratch_shapes=[pltpu.VMEM((2, HY_CW // LANES, seq_len + 2 * PAD_ROWS, LANES), F32),
                        pltpu.VMEM((3, seq_len, HY_CW), F32),
                        pltpu.VMEM((FFT_ROWS, HY_CW), BF16),
                        pltpu.VMEM((FFT_ROWS, HY_CW), F32)],
        compiler_params=pltpu.CompilerParams(dimension_semantics=("arbitrary", "arbitrary"),
                                             vmem_limit_bytes=VMEM_LIMIT_BYTES),
        name="hyena_branch",
    )(xb, w_in, w_in, w_in, b_in, b_in, b_in, conv_w, conv_w, conv_w, conv_b, conv_b, conv_b,
      hyena_skip, spectrum, spectrum, mf, mi)


def _attn_kernel(x_ref, w_ref, b_ref, dist_ref, out_ref, qkv_ref, acc_ref, m_ref, l_ref,
                 s_ref, p_ref, al_ref):
    seq_len = x_ref.shape[0]
    xb = x_ref[...]
    slopes = _alibi_slopes()
    q_scale = 1.0 / math.sqrt(HEAD_DIM)
    heads = range(HEADS_PER_GROUP)

    def stage_scores(blk, coefs):
        qr, kr, dist, kc = blk
        for h in heads:
            s_ref[h, :, 0:kc] = _dot_nt(qkv_ref[0, h, qr, :].astype(BF16),
                                        qkv_ref[1, h, kr, :].astype(BF16)) + dist * coefs[h]

    def stage_softmax(blk, first, last):
        qr, _, _, kc = blk
        s = [s_ref[h, :, 0:kc] for h in heads]
        m_blk = [jnp.max(s[h], axis=-1, keepdims=True) for h in heads]
        if first:
            m_new = [jnp.broadcast_to(m_blk[h], (Q_BLOCK, HEAD_DIM)) for h in heads]
        else:
            m_old = [m_ref[h, qr, :] for h in heads]
            m_new = [jnp.maximum(m_old[h], m_blk[h]) for h in heads]
        m_wide = [jnp.concatenate([m_new[h]] * (kc // HEAD_DIM), axis=1) for h in heads]
        for h in heads:
            p_ref[h, :, 0:kc] = jnp.exp(s[h] - m_wide[h]).astype(BF16)
            if not first:
                al_ref[h] = jnp.exp(m_old[h] - m_new[h])
            if not last:
                m_ref[h, qr, :] = m_new[h]

    def stage_output(blk, first, last):
        qr, kr, _, kc = blk
        ones = jnp.ones((kc, HEAD_DIM), BF16)
        pv = [_dot(p_ref[h, :, 0:kc], jnp.concatenate([qkv_ref[2, h, kr, :].astype(BF16), ones], axis=1))
              for h in heads]
        acc = [r[:, :HEAD_DIM] for r in pv]
        l_new = [r[:, HEAD_DIM:] for r in pv]
        if not first:
            alpha = [al_ref[h] for h in heads]
            l_new = [alpha[h] * l_ref[h, qr, :] + l_new[h] for h in heads]
            acc = [alpha[h] * acc_ref[h, qr, :] + acc[h] for h in heads]
        for h in heads:
            if last:
                out_ref[qr, h * HEAD_DIM:(h + 1) * HEAD_DIM] = (acc[h] / l_new[h]).astype(out_ref.dtype)
            else:
                l_ref[h, qr, :] = l_new[h]
                acc_ref[h, qr, :] = acc[h]

    order = sorted(range(N_GROUPS), key=lambda g: -ATTN_GROUPS[g][1])
    assert ATTN_GROUPS[order[-1]][1] == 1
    for pos, g in enumerate(order):
        window, dil = ATTN_GROUPS[g]
        first, last = pos == 0, pos == N_GROUPS - 1
        n_off = (window // 2) // dil
        assert n_off == Q_BLOCK // 2
        stream = seq_len // dil
        nb = stream // Q_BLOCK
        n_blocks = dil * nb
        for part in range(3):
            c0 = part * ATTN_QKV_WIDTH + g * GROUP_WIDTH
            proj = _dot(xb, w_ref[:, c0:c0 + GROUP_WIDTH]) + b_ref[:, c0:c0 + GROUP_WIDTH]
            if part == 0:
                proj = proj * q_scale
            for h in heads:
                qkv_ref[part, h] = proj[:, h * HEAD_DIM:(h + 1) * HEAD_DIM]
        coefs = [-slopes[g * HEADS_PER_GROUP + h] * dil for h in heads]

        def block(i, dil=dil, nb=nb, stream=stream):
            if nb == 1:
                rows = pl.ds(i, Q_BLOCK, stride=dil) if dil > 1 else pl.ds(i, Q_BLOCK)
                return rows, rows, dist_ref[3, :, 0:Q_BLOCK], Q_BLOCK
            c = i // nb
            n = i % nb
            lo = jnp.clip(n * Q_BLOCK - n_off, 0, stream - 2 * Q_BLOCK)
            dist = dist_ref[jnp.where(n == 0, 0, jnp.where(n == nb - 1, 2, 1))]
            if dil > 1:
                return (pl.ds(c + n * Q_BLOCK * dil, Q_BLOCK, stride=dil),
                        pl.ds(c + lo * dil, 2 * Q_BLOCK, stride=dil), dist, 2 * Q_BLOCK)
            return (pl.ds(pl.multiple_of(n * Q_BLOCK, Q_BLOCK), Q_BLOCK),
                    pl.ds(pl.multiple_of(lo, n_off), 2 * Q_BLOCK), dist, 2 * Q_BLOCK)

        def steady(i, carry, block=block, coefs=coefs, first=first, last=last):
            stage_output(block(i - 2), first, last)
            stage_softmax(block(i - 1), first, last)
            stage_scores(block(i), coefs)
            return carry

        assert n_blocks >= 3
        stage_scores(block(0), coefs)
        stage_softmax(block(0), first, last)
        stage_scores(block(1), coefs)
        lax.fori_loop(2, n_blocks, steady, 0)
        stage_output(block(n_blocks - 2), first, last)
        stage_softmax(block(n_blocks - 1), first, last)
        stage_output(block(n_blocks - 1), first, last)


def _attention_branch(xb, w_qkv, b_qkv):
    bsz, seq_len, d_model = xb.shape
    dist = _attn_distance_tiles()
    qkv_w = 3 * ATTN_QKV_WIDTH
    return pl.pallas_call(
        _attn_kernel,
        out_shape=jax.ShapeDtypeStruct((bsz, seq_len, GROUP_WIDTH), BF16),
        grid=(bsz,),
        in_specs=[pl.BlockSpec((None, seq_len, d_model), lambda b: (b, 0, 0)),
                  pl.BlockSpec((d_model, qkv_w), lambda b: (0, 0), pipeline_mode=pl.Buffered(1)),
                  pl.BlockSpec((1, qkv_w), lambda b: (0, 0)),
                  pl.BlockSpec(dist.shape, lambda b: (0, 0, 0))],
        out_specs=pl.BlockSpec((None, seq_len, GROUP_WIDTH), lambda b: (b, 0, 0)),
        scratch_shapes=[pltpu.VMEM((3, HEADS_PER_GROUP, seq_len, HEAD_DIM), F32),
                        pltpu.VMEM((HEADS_PER_GROUP, seq_len, HEAD_DIM), F32),
                        pltpu.VMEM((HEADS_PER_GROUP, seq_len, HEAD_DIM), F32),
                        pltpu.VMEM((HEADS_PER_GROUP, seq_len, HEAD_DIM), F32),
                        pltpu.VMEM((HEADS_PER_GROUP, Q_BLOCK, 2 * Q_BLOCK), F32),
                        pltpu.VMEM((HEADS_PER_GROUP, Q_BLOCK, 2 * Q_BLOCK), BF16),
                        pltpu.VMEM((HEADS_PER_GROUP, Q_BLOCK, HEAD_DIM), F32)],
        compiler_params=pltpu.CompilerParams(dimension_semantics=("arbitrary",),
                                             vmem_limit_bytes=VMEM_LIMIT_BYTES),
        name="dilated_attention",
    )(xb, w_qkv, b_qkv, dist)


ROW_TILE = 1024
SUB_ROWS = 512


def _layer_norm(r, g, b):
    mu = jnp.mean(r, axis=-1, keepdims=True)
    d = r - mu
    var = jnp.mean(d * d, axis=-1, keepdims=True)
    return d * lax.rsqrt(var + LN_EPS) * g + b


def _sub_tiles(ref):
    return [slice(s * SUB_ROWS, (s + 1) * SUB_ROWS) for s in range(ref.shape[0] // SUB_ROWS)]


def _merge_kernel(alpha, x_ref, ya_ref, yh_ref, wg_ref, bg_ref, wa_ref, wh_ref, wo_ref, g_ref, b_ref, out_ref):
    d_model = x_ref.shape[-1]
    tiles = _sub_tiles(x_ref)
    x = [x_ref[r, :] for r in tiles]
    gates = [jax.nn.sigmoid(_dot(xs.astype(BF16), wg_ref[...]) + bg_ref[...]) for xs in x]
    merged = [gt[:, :d_model] * _dot(ya_ref[r, :], wa_ref[...])
              + gt[:, d_model:] * _dot(yh_ref[r, :], wh_ref[...]) for gt, r in zip(gates, tiles)]
    mix = [_dot(mg.astype(BF16), wo_ref[...]) for mg in merged]
    for r, xs, mx in zip(tiles, x, mix):
        out_ref[r, :] = _layer_norm(alpha * xs + mx, g_ref[...], b_ref[...])


def _ffn_kernel(alpha, h_ref, w1_ref, b1_ref, w2_ref, b2_ref, g_ref, b_ref, out_ref):
    tiles = _sub_tiles(h_ref)
    h = [h_ref[r, :] for r in tiles]
    hid = [jnp.maximum(_dot(hs.astype(BF16), w1_ref[...]) + b1_ref[...], 0.0) for hs in h]
    ff = [_dot((hd * hd).astype(BF16), w2_ref[...]) + b2_ref[...] for hd in hid]
    for r, hs, fs in zip(tiles, h, ff):
        out_ref[r, :] = _layer_norm(alpha * hs + fs, g_ref[...], b_ref[...])


def _row_tiled_call(kernel_fn, name, rows, d_model, tiled, resident, casts=()):
    steps = rows // ROW_TILE
    n_main = len(tiled) + len(resident)
    cast_specs, cast_shapes = _cast_plan(casts, steps)

    def body(*refs):
        cast_in, (out_ref, *cast_out) = refs[n_main:n_main + len(casts)], refs[n_main + len(casts):]
        _cast_blocks(cast_in, cast_out)
        kernel_fn(*refs[:n_main], out_ref)

    in_specs = [pl.BlockSpec((ROW_TILE, a.shape[1]), lambda i: (i, 0)) for a in tiled]
    in_specs += [pl.BlockSpec(a.shape, lambda i: (0, 0), pipeline_mode=pl.Buffered(1)) for a in resident]
    out, *cast_out = pl.pallas_call(
        body,
        out_shape=[jax.ShapeDtypeStruct((rows, d_model), F32)] + cast_shapes,
        grid=(steps,),
        in_specs=in_specs + cast_specs,
        out_specs=[pl.BlockSpec((ROW_TILE, d_model), lambda i: (i, 0))] + cast_specs,
        compiler_params=pltpu.CompilerParams(dimension_semantics=("arbitrary",),
                                             vmem_limit_bytes=VMEM_LIMIT_BYTES),
        name=name,
    )(*tiled, *resident, *casts)
    return out, cast_out


def kernel(x, w_in, b_in, conv_w, conv_b, filt_w0, filt_b0, filt_w_inner, filt_b_inner, filt_w_out,
           filt_freq, hyena_skip, w_branch_attn, w_branch_hyena, w_out, ln1_g, ln1_b, w_ff1, b_ff1,
           w_ff2, b_ff2, ln2_g, ln2_b):
    bsz, seq_len, d_model = x.shape
    depth = w_in.shape[0]
    alpha = (2 * depth) ** 0.25
    rows = bsz * seq_len
    qkv_w = 3 * ATTN_QKV_WIDTH
    hy_w = (HYENA_ORDER + 1) * hyena_skip.shape[-1]
    row = lambda a: a.astype(F32)[None, :]
    h = x
    for layer in range(depth):
        w_l = w_in[layer].astype(BF16)
        b_l = b_in[layer].astype(F32)[None, :]
        h_rows = h.reshape(rows, d_model).astype(F32)
        spectrum, (hb, wa_l, wh_l, wo_l) = _filter_spectrum(
            seq_len, hyena_skip.shape[-1], filt_w0[layer], filt_b0[layer], filt_w_inner[layer],
            filt_b_inner[layer], filt_w_out[layer], filt_freq[layer],
            casts=[h_rows, w_branch_attn[layer].astype(F32), w_branch_hyena[layer].astype(F32),
                   w_out[layer].astype(F32)])
        hb = hb.reshape(bsz, seq_len, d_model)
        y_attn = _attention_branch(hb, w_l, b_l)
        y_hyena = _hyena_branch(hb, w_l, b_l, qkv_w,
                                conv_w[layer].astype(F32), conv_b[layer].astype(F32)[None, :],
                                hyena_skip[layer].astype(F32), spectrum)
        h1, (w1_l, w2_l) = _row_tiled_call(
            functools.partial(_merge_kernel, alpha), "merge_ln", rows, d_model,
            [h_rows, y_attn.reshape(rows, -1), y_hyena.reshape(rows, -1)],
            [w_l[:, qkv_w + hy_w:], b_l[:, qkv_w + hy_w:], wa_l, wh_l, wo_l, row(ln1_g[layer]),
             row(ln1_b[layer])],
            casts=[w_ff1[layer].astype(F32), w_ff2[layer].astype(F32)])
        h2, _ = _row_tiled_call(
            functools.partial(_ffn_kernel, alpha), "ffn_ln", rows, d_model, [h1],
            [w1_l, row(b_ff1[layer]), w2_l, row(b_ff2[layer]), row(ln2_g[layer]), row(ln2_b[layer])])
        h = h2.reshape(bsz, seq_len, d_model)
    return h
```

```python
import functools
import math

import jax
import jax.numpy as jnp
import numpy as np
from jax import lax
from jax.experimental import pallas as pl
from jax.experimental.pallas import tpu as pltpu

F32 = jnp.float32
BF16 = jnp.bfloat16

ATTN_GROUPS = ((128, 1), (512, 4), (2048, 16))
N_GROUPS = len(ATTN_GROUPS)
HEADS_PER_GROUP = 4
HEAD_DIM = 128
GROUP_WIDTH = HEADS_PER_GROUP * HEAD_DIM
ATTN_QKV_WIDTH = N_GROUPS * GROUP_WIDTH
Q_BLOCK = 128
ALIBI_MAX_EXP = 8.0
HYENA_ORDER = 2
FILTER_BANDS = 16
FILTER_EMB = 1 + 2 * FILTER_BANDS
FILTER_HIDDEN = 64
FILTER_INNER = 2
DECAY_TARGET = 1e-2
FAST_DECAY_PCT = 0.3
SLOW_DECAY_PCT = 1.5
LN_EPS = 1e-5

LANES = 128
VMEM_LIMIT_BYTES = 58 * 1024 * 1024

FFT_N1 = 16
FFT_N2 = 256
FFT_SLICES = FFT_N1 // 2 + 1
FFT_ROWS = FFT_N1 * FFT_N2
RSQRT2 = 1.0 / math.sqrt(2.0)


def _unit_root(k, n):
    snap = lambda v: float(round(v)) if abs(v - round(v)) < 1e-12 else v
    return snap(math.cos(2.0 * math.pi * k / n)), snap(-math.sin(2.0 * math.pi * k / n))


W16 = tuple(_unit_root(k, FFT_N1) for k in range(FFT_SLICES))


def _dot(a, b, precision=None):
    return jnp.dot(a, b, preferred_element_type=F32, precision=precision)


def _split_bf16(a):
    hi = a.astype(BF16)
    return hi, (a - hi.astype(F32)).astype(BF16)


def _dot_nt(a, b):
    return lax.dot_general(a, b, (((1,), (1,)), ((), ())), preferred_element_type=F32)


CAST_ROWS = 256


def _cast_plan(arrays, steps):
    specs, shapes = [], []
    for a in arrays:
        rows, rem = divmod(a.shape[0], steps)
        assert rem == 0 and rows % 16 == 0, a.shape
        specs.append(pl.BlockSpec((rows, a.shape[1]), lambda i: (i, 0)))
        shapes.append(jax.ShapeDtypeStruct(a.shape, BF16))
    return specs, shapes


def _cast_blocks(src_refs, dst_refs):
    for src, dst in zip(src_refs, dst_refs):
        chunk = min(src.shape[0], CAST_ROWS)
        assert src.shape[0] % chunk == 0

        def body(i, carry, src=src, dst=dst, chunk=chunk):
            rows = pl.ds(pl.multiple_of(i * chunk, chunk), chunk)
            dst[rows, :] = src[rows, :].astype(dst.dtype)
            return carry
        lax.fori_loop(0, src.shape[0] // chunk, body, 0)


@functools.lru_cache(maxsize=None)
def _fft_constants(seq_len):
    n_fft = 2 * seq_len
    assert n_fft == FFT_N1 * FFT_N2
    k2 = np.arange(FFT_N2)
    fwd = []
    for k1 in range(FFT_SLICES):
        ang = -2.0 * np.pi * ((np.outer(FFT_N1 * k2 + k1, k2) % n_fft) / n_fft)
        er, ei = np.cos(ang), np.sin(ang)
        fwd.append(np.block([[er, -ei], [ei, er]]))
    fwd = np.stack(fwd, axis=0).astype(np.float32)
    inv = np.ascontiguousarray(np.transpose(fwd, (0, 2, 1)))
    return fwd, inv


@functools.lru_cache(maxsize=None)
def _filter_constants(seq_len, width):
    n_fft = 2 * seq_len
    t = np.linspace(0.0, 1.0, seq_len)
    bands = np.linspace(1e-4, FILTER_BANDS - 1, FILTER_BANDS)
    ang = (2.0 * np.pi / seq_len) * np.arange(seq_len)[:, None] * bands
    feats = np.concatenate([t[:, None], np.cos(ang), -np.sin(ang)], axis=-1)
    src = np.concatenate([np.arange(seq_len), [0], np.arange(seq_len - 1, 0, -1)])
    feats_ext = np.zeros((seq_len, 2 * LANES), np.float32)
    feats_ext[:, :FILTER_EMB] = feats[src[:seq_len]]
    feats_ext[:, LANES:LANES + FILTER_EMB] = feats[src[seq_len:]]
    t_mask = np.zeros((2, n_fft, LANES), np.float32)
    t_mask[0] = t[src][:, None]
    t_mask[1] = 1.0
    t_mask[1, seq_len] = 0.0
    deltas = np.abs(np.linspace(math.log(DECAY_TARGET) / SLOW_DECAY_PCT,
                                math.log(DECAY_TARGET) / FAST_DECAY_PCT, width))
    deltas = np.tile(deltas[None, :], (1, HYENA_ORDER)).astype(np.float32)
    return feats_ext, t_mask, deltas


@functools.lru_cache(maxsize=None)
def _attn_distance_tiles():
    n_off = Q_BLOCK // 2
    qi = np.arange(Q_BLOCK)[:, None]
    kj = np.arange(2 * Q_BLOCK)[None, :]
    tiles = []
    for shift in (0, n_off, 2 * n_off):
        off = np.abs(kj - shift - qi).astype(np.float32)
        tiles.append(np.where(off <= n_off, off, np.inf))
    off = np.abs(kj - qi).astype(np.float32)
    t3 = np.where(off <= n_off, off, np.inf)
    t3[:, Q_BLOCK:] = np.inf
    tiles.append(t3)
    return np.stack(tiles, axis=0).astype(np.float32)


def _alibi_slopes():
    n = N_GROUPS * HEADS_PER_GROUP
    return [2.0 ** (-ALIBI_MAX_EXP * j / n) for j in range(1, n + 1)]


def _radix8_half(z0, z1, z2, z3):
    s02, d02 = z0 + z2, z0 - z2
    s13, d13 = z1 + z3, z1 - z3
    ss, dd = s13 * RSQRT2, d13 * RSQRT2
    return {0: (s02 + s13, None), 4: (s02 - s13, None), 2: (d02, -d13),
            1: (z0 + dd, -z2 - ss), 3: (z0 - dd, z2 - ss)}


def _add(a, b, sign=1.0):
    if b is None:
        return a
    if a is None:
        return b if sign > 0 else -b
    return a + b if sign > 0 else a - b


def _cmul_const(re, im, wr, wi):
    def scaled(v, w):
        if v is None or w == 0.0:
            return None
        return v if w == 1.0 else (-v if w == -1.0 else v * w)
    if re is not None and im is not None and wr != 0.0 and abs(wr) == abs(wi):
        if wi == wr:
            return (re - im) * wr, (re + im) * wr
        return (re + im) * wr, (im - re) * wr
    return _add(scaled(re, wr), scaled(im, wi), -1.0), _add(scaled(re, wi), scaled(im, wr))


def _radix16_stage(blocks):
    even = _radix8_half(*blocks[0::2])
    odd = _radix8_half(*blocks[1::2])
    half = FFT_N1 // 2
    out = [None] * FFT_SLICES
    out[0] = (even[0][0] + odd[0][0], None)
    out[half] = (even[0][0] - odd[0][0], None)
    out[half // 2] = (even[4][0], -odd[4][0])
    for k in range(1, half // 2):
        (er, ei), (pr, pi) = even[k], _cmul_const(*odd[k], *W16[k])
        out[k] = (er + pr, ei + pi)
        out[half - k] = (er - pr, pi - ei)
    return out


def _assemble8(c0, c4, c1, c2, c3):
    (c1r, c1i), (c2r, c2i), (c3r, c3i) = c1, c2, c3
    e, o = c0 + c4, c0 - c4
    return [e + c1r + c2r + c3r,
            o + (c1r - c1i - c3r - c3i) * RSQRT2 - c2i,
            e - c1i - c2r + c3i,
            o + (c3r - c3i - c1r - c1i) * RSQRT2 + c2i]


def _inverse_radix16_stage(c):
    half = FFT_N1 // 2
    plus = [(c[k][0] + c[half - k][0], c[k][1] - c[half - k][1]) for k in range(1, half // 2)]
    minus = [_cmul_const(c[k][0] - c[half - k][0], c[k][1] + c[half - k][1], W16[k][0], -W16[k][1])
             for k in range(1, half // 2)]
    mid = c[half // 2]
    even = _assemble8(c[0][0] + c[half][0], mid[0], *plus)
    odd = _assemble8(c[0][0] - c[half][0], -mid[1], *minus)
    return [blk for pair_ in zip(even, odd) for blk in pair_]


def _slice_rows(k1):
    if k1 == 0:
        return 0, FFT_N2
    if k1 == FFT_N1 // 2:
        return FFT_N2, FFT_N2
    return 2 * FFT_N2 * k1, 2 * FFT_N2


def _store_slices(ref, off, rows, values, scale=None):
    for k1, (re, im) in enumerate(values):
        r0, n = _slice_rows(k1)
        parts = (re,) if n == FFT_N2 else (re, im)
        for j, part in enumerate(parts):
            if scale is not None:
                part = part * scale
            ref[pl.ds(r0 + j * FFT_N2 + off, rows), :] = part.astype(ref.dtype)


def _load_slices(ref, off, rows):
    out = []
    for k1 in range(FFT_SLICES):
        r0, n = _slice_rows(k1)
        re = ref[pl.ds(r0 + off, rows), :]
        out.append((re, ref[pl.ds(r0 + FFT_N2 + off, rows), :] if n > FFT_N2 else None))
    return out


def _forward_dft(b_ref, k1, mf_ref):
    r0, n = _slice_rows(k1)
    m = mf_ref[k1] if n > FFT_N2 else mf_ref[k1, :, 0:FFT_N2]
    return _dot(m, b_ref[r0:r0 + n, :])


def _inverse_dft(p, k1, mi_ref, c_ref):
    r0, n = _slice_rows(k1)
    m = mi_ref[k1] if n > FFT_N2 else mi_ref[k1, 0:FFT_N2, :]
    c_ref[r0:r0 + n, :] = _dot(m, p)


FILT_CW = 256
FILT_ROWS = 256
A_CHUNK = 16


def _filter_kernel(n_cast, feats_ref, tmask_ref, w0_ref, b0_ref, wi_ref, bi_ref, freq_ref, wf_ref, wb_ref,
                   delta_ref, mf_ref, *rest):
    cast_in, (h_ref, *cast_out), (hid_ref, ts_ref, b_ref) = (
        rest[:n_cast], rest[n_cast:2 * n_cast + 1], rest[2 * n_cast + 1:])
    _cast_blocks(cast_in, cast_out)
    seq_len = feats_ref.shape[0]
    n_fft = 2 * seq_len
    hp = lax.Precision.HIGHEST

    @pl.when(pl.program_id(0) == 0)
    def _():
        def body(i, carry):
            rows = pl.ds(pl.multiple_of(i * FILT_ROWS, FILT_ROWS), FILT_ROWS)
            freq = freq_ref[...]
            hid = jnp.sin(freq * (_dot(feats_ref[rows, :], w0_ref[...], hp) + b0_ref[...]))
            for layer in range(FILTER_INNER):
                hid = jnp.sin(freq * (_dot(hid, wi_ref[layer], hp) + bi_ref[layer]))
            hid_ref[rows, :] = hid
            return carry
        lax.fori_loop(0, seq_len // FILT_ROWS, body, 0)

    h_hi, h_lo = _split_bf16(hid_ref[...])
    h_cat = jnp.concatenate([h_hi, h_lo], axis=1)
    for half, w_ref in enumerate((wf_ref, wb_ref)):
        w_hi, w_lo = _split_bf16(w_ref[...])
        ts_ref[half * seq_len:(half + 1) * seq_len, :] = (
            _dot(h_cat, jnp.concatenate([w_hi, w_hi], axis=0)) + _dot(h_hi, w_lo))

    def window_body(i, ssq):
        rows = pl.ds(pl.multiple_of(i * FILT_ROWS, FILT_ROWS), FILT_ROWS)
        widen = lambda a: jnp.concatenate([a] * (FILT_CW // LANES), axis=1)
        val = (ts_ref[rows, :] * jnp.exp(-widen(tmask_ref[0, rows, :]) * delta_ref[...])
               * widen(tmask_ref[1, rows, :]))
        ts_ref[rows, :] = val
        return ssq + jnp.sum(val * val, axis=0, keepdims=True)
    ssq = lax.fori_loop(0, n_fft // FILT_ROWS, window_body, jnp.zeros((1, FILT_CW), F32))
    scale = lax.rsqrt(ssq)

    def a_body(i, carry):
        off = pl.multiple_of(i * A_CHUNK, A_CHUNK)
        blk = [ts_ref[pl.ds(t1 * FFT_N2 + off, A_CHUNK), :] for t1 in range(FFT_N1)]
        lo = _radix16_stage(blk[:FFT_N1 // 2])
        hi = _radix16_stage(blk[FFT_N1 // 2:])
        vals = [(_add(lo[k][0], hi[k][0], 1.0 if k % 2 == 0 else -1.0),
                 _add(lo[k][1], hi[k][1], 1.0 if k % 2 == 0 else -1.0)) for k in range(FFT_SLICES)]
        _store_slices(b_ref, off, A_CHUNK, vals, scale)
        return carry
    lax.fori_loop(0, FFT_N2 // A_CHUNK, a_body, 0)

    for k1 in range(FFT_SLICES):
        s = (1.0 if k1 in (0, FFT_N1 // 2) else 2.0) / n_fft
        h_ref[k1] = _forward_dft(b_ref, k1, mf_ref) * s


def _filter_spectrum(seq_len, width, filt_w0, filt_b0, filt_w_inner, filt_b_inner, filt_w_out, filt_freq,
                     casts=()):
    n_fft = 2 * seq_len
    hid = LANES
    feats_ext, t_mask, deltas = _filter_constants(seq_len, width)
    mf = jnp.asarray(_fft_constants(seq_len)[0]).astype(BF16)
    assert 2 * FILTER_HIDDEN == hid and FILTER_EMB <= LANES
    n_cols = HYENA_ORDER * width
    steps = n_cols // FILT_CW
    zeros = lambda r, c: jnp.zeros((r, c), F32)
    w0p = jnp.pad(filt_w0.astype(F32), ((0, LANES - FILTER_EMB), (0, 0)))
    w0 = jnp.block([[w0p, zeros(LANES, FILTER_HIDDEN)], [zeros(LANES, FILTER_HIDDEN), w0p]])
    b0 = jnp.tile(filt_b0.astype(F32), 2)[None, :]
    wi = jnp.stack([jnp.block([[w, zeros(FILTER_HIDDEN, FILTER_HIDDEN)], [zeros(FILTER_HIDDEN, FILTER_HIDDEN), w]])
                    for w in filt_w_inner.astype(F32)])
    bi = jnp.tile(filt_b_inner.astype(F32), (1, 2))[:, None, :]
    freq = jnp.tile(filt_freq.astype(F32), 2)[None, :]
    w_out = filt_w_out.astype(F32)
    wl = jnp.concatenate([jnp.pad(w_out[:, :n_cols], ((0, FILTER_HIDDEN), (0, 0))),
                          jnp.pad(w_out[:, n_cols:], ((FILTER_HIDDEN, 0), (0, 0)))], axis=1)
    full = lambda *shape: pl.BlockSpec(shape, lambda j: (0,) * len(shape))
    cast_specs, cast_shapes = _cast_plan(casts, steps)
    spectrum, *cast_out = pl.pallas_call(
        functools.partial(_filter_kernel, len(casts)),
        out_shape=[jax.ShapeDtypeStruct((FFT_SLICES, 2 * FFT_N2, n_cols), F32)] + cast_shapes,
        grid=(steps,),
        in_specs=[
            full(seq_len, 2 * LANES), full(2, n_fft, LANES), full(2 * LANES, hid), full(1, hid),
            full(FILTER_INNER, hid, hid), full(FILTER_INNER, 1, hid), full(1, hid),
            pl.BlockSpec((hid, FILT_CW), lambda j: (0, j)),
            pl.BlockSpec((hid, FILT_CW), lambda j: (0, steps + j)),
            pl.BlockSpec((1, FILT_CW), lambda j: (0, j)),
            full(FFT_SLICES, 2 * FFT_N2, 2 * FFT_N2),
        ] + cast_specs,
        out_specs=[pl.BlockSpec((FFT_SLICES, 2 * FFT_N2, FILT_CW), lambda j: (0, 0, j))] + cast_specs,
        scratch_shapes=[pltpu.VMEM((seq_len, hid), F32), pltpu.VMEM((n_fft, FILT_CW), F32),
                        pltpu.VMEM((FFT_ROWS, FILT_CW), BF16)],
        compiler_params=pltpu.CompilerParams(dimension_semantics=("arbitrary",),
                                             vmem_limit_bytes=VMEM_LIMIT_BYTES),
        name="hyena_filter",
    )(feats_ext, t_mask, w0, b0, wi, bi, freq, wl, wl, deltas, mf, *casts)
    return spectrum, cast_out


HY_CW = 256
CONV_ROWS = 64
PAD_ROWS = 8


def _hyena_kernel(x_ref, wv_ref, w1_ref, w2_ref, bv_ref, b1_ref, b2_ref, cwv_ref, cw1_ref, cw2_ref,
                  cbv_ref, cb1_ref, cb2_ref, skip_ref, h0_ref, h1_ref, mf_ref, mi_ref,
                  out_ref, u_ref, p_ref, b_ref, c_ref):
    seq_len = x_ref.shape[0]
    width = HY_CW
    n2 = FFT_N2
    xb = x_ref[...]

    n_slabs = width // LANES
    zeros = jnp.zeros((PAD_ROWS, LANES), F32)
    for slot in range(u_ref.shape[0]):
        for j in range(n_slabs):
            u_ref[slot, j, 0:PAD_ROWS, :] = zeros
            u_ref[slot, j, PAD_ROWS + seq_len:2 * PAD_ROWS + seq_len, :] = zeros
    parts = ((wv_ref, bv_ref, cwv_ref, cbv_ref), (w1_ref, b1_ref, cw1_ref, cb1_ref),
             (w2_ref, b2_ref, cw2_ref, cb2_ref))

    def project(p, slot):
        w_ref, bias_ref, _, _ = parts[p]
        half = seq_len // 2
        for r0 in (0, half):
            u = _dot(xb[r0:r0 + half], w_ref[...]) + bias_ref[...]
            for j in range(n_slabs):
                u_ref[slot, j, PAD_ROWS + r0:PAD_ROWS + r0 + half, :] = u[:, j * LANES:(j + 1) * LANES]

    def short_conv(p, slot):
        _, _, cw_ref, cb_ref = parts[p]

        def conv_body(i, carry):
            start = pl.multiple_of(i * CONV_ROWS, CONV_ROWS)
            cw = cw_ref[...]
            cb = cb_ref[...]
            for j in range(n_slabs):
                lanes = slice(j * LANES, (j + 1) * LANES)
                taps = [u_ref[slot, j, pl.ds(start + PAD_ROWS - 1 + k, CONV_ROWS, stride=1), :]
                        for k in range(3)]
                p_ref[p, pl.ds(start, CONV_ROWS), lanes] = (
                    cw[0:1, lanes] * taps[0] + cw[1:2, lanes] * taps[1] + cw[2:3, lanes] * taps[2]
                    + cb[:, lanes])
            return carry
        lax.fori_loop(0, seq_len // CONV_ROWS, conv_body, 0, unroll=True)

    def spectral_product(h_ref):
        x_next = _forward_dft(b_ref, 0, mf_ref)
        for k1 in range(FFT_SLICES):
            x = x_next
            if k1 + 1 < FFT_SLICES:
                x_next = _forward_dft(b_ref, k1 + 1, mf_ref)
            xr, xi = x[:n2], x[n2:]
            hr = h_ref[k1, 0:n2, :]
            hi = h_ref[k1, n2:2 * n2, :]
            prod = jnp.concatenate([(xr * hr - xi * hi).astype(BF16), (xr * hi + xi * hr).astype(BF16)], axis=0)
            _inverse_dft(prod, k1, mi_ref, c_ref)

    def gated_blocks(order, off):
        skip = skip_ref[order:order + 1, :]
        ys = _inverse_radix16_stage(_load_slices(c_ref, off, A_CHUNK))
        out = []
        for t1, y in enumerate(ys):
            rows = pl.ds(t1 * n2 + off, A_CHUNK)
            out.append(p_ref[order + 1, rows, :] * (y + skip * p_ref[0, rows, :]))
        return out

    project(0, 0)
    short_conv(0, 0)

    def a_body(i, carry):
        off = pl.multiple_of(i * A_CHUNK, A_CHUNK)
        blk = [p_ref[0, pl.ds(t1 * n2 + off, A_CHUNK), :] for t1 in range(FFT_N1 // 2)]
        _store_slices(b_ref, off, A_CHUNK, _radix16_stage(blk))
        return carry
    lax.fori_loop(0, n2 // A_CHUNK, a_body, 0, unroll=True)

    project(1, 0)
    project(2, 1)
    spectral_product(h0_ref)
    short_conv(1, 0)
    short_conv(2, 1)

    def mid_body(i, carry):
        off = pl.multiple_of(i * A_CHUNK, A_CHUNK)
        z = gated_blocks(0, off)
        for t1, blk in enumerate(z):
            p_ref[0, pl.ds(t1 * n2 + off, A_CHUNK), :] = blk
        _store_slices(b_ref, off, A_CHUNK, _radix16_stage(z))
        return carry
    lax.fori_loop(0, n2 // A_CHUNK, mid_body, 0)

    spectral_product(h1_ref)

    def out_body(i, carry):
        off = pl.multiple_of(i * A_CHUNK, A_CHUNK)
        for t1, blk in enumerate(gated_blocks(1, off)):
            out_ref[pl.ds(t1 * n2 + off, A_CHUNK), :] = blk.astype(out_ref.dtype)
        return carry
    lax.fori_loop(0, n2 // A_CHUNK, out_body, 0)


def _hyena_branch(xb, w_in, b_in, col0, conv_w, conv_b, hyena_skip, spectrum):
    bsz, seq_len, d_model = xb.shape
    width = hyena_skip.shape[-1]
    nblk = width // HY_CW
    mf, mi = (jnp.asarray(m).astype(BF16) for m in _fft_constants(seq_len))
    col = lambda part: (lambda c, b: (0, part * nblk + c))
    blk0, rem = divmod(col0, HY_CW)
    assert rem == 0
    in_col = lambda part: (lambda c, b: (0, blk0 + part * nblk + c))
    const = lambda *shape: pl.BlockSpec(shape, lambda c, b: (0,) * len(shape), pipeline_mode=pl.Buffered(1))
    spec_h = lambda order: pl.BlockSpec((FFT_SLICES, 2 * FFT_N2, HY_CW), lambda c, b: (0, 0, order * nblk + c),
                                        pipeline_mode=pl.Buffered(1))
    in_specs = [pl.BlockSpec((None, seq_len, d_model), lambda c, b: (b, 0, 0))]
    in_specs += [pl.BlockSpec((d_model, HY_CW), in_col(p)) for p in range(3)]
    in_specs += [pl.BlockSpec((1, HY_CW), in_col(p)) for p in range(3)]
    in_specs += [pl.BlockSpec((3, HY_CW), col(p)) for p in range(3)]
    in_specs += [pl.BlockSpec((1, HY_CW), col(p)) for p in range(3)]
    in_specs += [pl.BlockSpec((HYENA_ORDER, HY_CW), lambda c, b: (0, c)), spec_h(0), spec_h(1),
                 const(FFT_SLICES, 2 * FFT_N2, 2 * FFT_N2), const(FFT_SLICES, 2 * FFT_N2, 2 * FFT_N2)]
    return pl.pallas_call(
        _hyena_kernel,
        out_shape=jax.ShapeDtypeStruct((bsz, seq_len, width), BF16),
        grid=(nblk, bsz),
        in_specs=in_specs,
        out_specs=pl.BlockSpec((None, seq_len, HY_CW), lambda c, b: (b, 0, c)),
        scratch_shapes=[pltpu.VMEM((2, HY_CW // LANES, seq_len + 2 * PAD_ROWS, LANES), F32),
                        pltpu.VMEM((3, seq_len, HY_CW), F32),
                        pltpu.VMEM((FFT_ROWS, HY_CW), BF16),
                        pltpu.VMEM((FFT_ROWS, HY_CW), F32)],
        compiler_params=pltpu.CompilerParams(dimension_semantics=("arbitrary", "arbitrary"),
                                             vmem_limit_bytes=VMEM_LIMIT_BYTES),
        name="hyena_branch",
    )(xb, w_in, w_in, w_in, b_in, b_in, b_in, conv_w, conv_w, conv_w, conv_b, conv_b, conv_b,
      hyena_skip, spectrum, spectrum, mf, mi)


def _attn_kernel(x_ref, w_ref, b_ref, dist_ref, out_ref, qkv_ref, acc_ref, m_ref, l_ref,
                 s_ref, p_ref, al_ref):
    seq_len = x_ref.shape[0]
    xb = x_ref[...]
    slopes = _alibi_slopes()
    q_scale = 1.0 / math.sqrt(HEAD_DIM)
    heads = range(HEADS_PER_GROUP)

    def stage_scores(blk, coefs):
        qr, kr, dist, kc = blk
        for h in heads:
            s_ref[h, :, 0:kc] = _dot_nt(qkv_ref[0, h, qr, :].astype(BF16),
                                        qkv_ref[1, h, kr, :].astype(BF16)) + dist * coefs[h]

    def stage_softmax(blk, first, last):
        qr, _, _, kc = blk
        s = [s_ref[h, :, 0:kc] for h in heads]
        m_blk = [jnp.max(s[h], axis=-1, keepdims=True) for h in heads]
        if first:
            m_new = [jnp.broadcast_to(m_blk[h], (Q_BLOCK, HEAD_DIM)) for h in heads]
        else:
            m_old = [m_ref[h, qr, :] for h in heads]
            m_new = [jnp.maximum(m_old[h], m_blk[h]) for h in heads]
        m_wide = [jnp.concatenate([m_new[h]] * (kc // HEAD_DIM), axis=1) for h in heads]
        for h in heads:
            p_ref[h, :, 0:kc] = jnp.exp(s[h] - m_wide[h]).astype(BF16)
            if not first:
                al_ref[h] = jnp.exp(m_old[h] - m_new[h])
            if not last:
                m_ref[h, qr, :] = m_new[h]

    def stage_output(blk, first, last):
        qr, kr, _, kc = blk
        ones = jnp.ones((kc, HEAD_DIM), BF16)
        pv = [_dot(p_ref[h, :, 0:kc], jnp.concatenate([qkv_ref[2, h, kr, :].astype(BF16), ones], axis=1))
              for h in heads]
        acc = [r[:, :HEAD_DIM] for r in pv]
        l_new = [r[:, HEAD_DIM:] for r in pv]
        if not first:
            alpha = [al_ref[h] for h in heads]
            l_new = [alpha[h] * l_ref[h, qr, :] + l_new[h] for h in heads]
            acc = [alpha[h] * acc_ref[h, qr, :] + acc[h] for h in heads]
        for h in heads:
            if last:
                out_ref[qr, h * HEAD_DIM:(h + 1) * HEAD_DIM] = (acc[h] / l_new[h]).astype(out_ref.dtype)
            else:
                l_ref[h, qr, :] = l_new[h]
                acc_ref[h, qr, :] = acc[h]

    order = sorted(range(N_GROUPS), key=lambda g: -ATTN_GROUPS[g][1])
    assert ATTN_GROUPS[order[-1]][1] == 1
    for pos, g in enumerate(order):
        window, dil = ATTN_GROUPS[g]
        first, last = pos == 0, pos == N_GROUPS - 1
        n_off = (window // 2) // dil
        assert n_off == Q_BLOCK // 2
        stream = seq_len // dil
        nb = stream // Q_BLOCK
        n_blocks = dil * nb
        for part in range(3):
            c0 = part * ATTN_QKV_WIDTH + g * GROUP_WIDTH
            proj = _dot(xb, w_ref[:, c0:c0 + GROUP_WIDTH]) + b_ref[:, c0:c0 + GROUP_WIDTH]
            if part == 0:
                proj = proj * q_scale
            for h in heads:
                qkv_ref[part, h] = proj[:, h * HEAD_DIM:(h + 1) * HEAD_DIM]
        coefs = [-slopes[g * HEADS_PER_GROUP + h] * dil for h in heads]

        def block(i, dil=dil, nb=nb, stream=stream):
            if nb == 1:
                rows = pl.ds(i, Q_BLOCK, stride=dil) if dil > 1 else pl.ds(i, Q_BLOCK)
                return rows, rows, dist_ref[3, :, 0:Q_BLOCK], Q_BLOCK
            c = i // nb
            n = i % nb
            lo = jnp.clip(n * Q_BLOCK - n_off, 0, stream - 2 * Q_BLOCK)
            dist = dist_ref[jnp.where(n == 0, 0, jnp.where(n == nb - 1, 2, 1))]
            if dil > 1:
                return (pl.ds(c + n * Q_BLOCK * dil, Q_BLOCK, stride=dil),
                        pl.ds(c + lo * dil, 2 * Q_BLOCK, stride=dil), dist, 2 * Q_BLOCK)
            return (pl.ds(pl.multiple_of(n * Q_BLOCK, Q_BLOCK), Q_BLOCK),
                    pl.ds(pl.multiple_of(lo, n_off), 2 * Q_BLOCK), dist, 2 * Q_BLOCK)

        def steady(i, carry, block=block, coefs=coefs, first=first, last=last):
            stage_output(block(i - 2), first, last)
            stage_softmax(block(i - 1), first, last)
            stage_scores(block(i), coefs)
            return carry

        assert n_blocks >= 3
        stage_scores(block(0), coefs)
        stage_softmax(block(0), first, last)
        stage_scores(block(1), coefs)
        lax.fori_loop(2, n_blocks, steady, 0)
        stage_output(block(n_blocks - 2), first, last)
        stage_softmax(block(n_blocks - 1), first, last)
        stage_output(block(n_blocks - 1), first, last)


def _attention_branch(xb, w_qkv, b_qkv):
    bsz, seq_len, d_model = xb.shape
    dist = _attn_distance_tiles()
    qkv_w = 3 * ATTN_QKV_WIDTH
    return pl.pallas_call(
        _attn_kernel,
        out_shape=jax.ShapeDtypeStruct((bsz, seq_len, GROUP_WIDTH), BF16),
        grid=(bsz,),
        in_specs=[pl.BlockSpec((None, seq_len, d_model), lambda b: (b, 0, 0)),
                  pl.BlockSpec((d_model, qkv_w), lambda b: (0, 0), pipeline_mode=pl.Buffered(1)),
                  pl.BlockSpec((1, qkv_w), lambda b: (0, 0)),
                  pl.BlockSpec(dist.shape, lambda b: (0, 0, 0))],
        out_specs=pl.BlockSpec((None, seq_len, GROUP_WIDTH), lambda b: (b, 0, 0)),
        scratch_shapes=[pltpu.VMEM((3, HEADS_PER_GROUP, seq_len, HEAD_DIM), F32),
                        pltpu.VMEM((HEADS_PER_GROUP, seq_len, HEAD_DIM), F32),
                        pltpu.VMEM((HEADS_PER_GROUP, seq_len, HEAD_DIM), F32),
                        pltpu.VMEM((HEADS_PER_GROUP, seq_len, HEAD_DIM), F32),
                        pltpu.VMEM((HEADS_PER_GROUP, Q_BLOCK, 2 * Q_BLOCK), F32),
                        pltpu.VMEM((HEADS_PER_GROUP, Q_BLOCK, 2 * Q_BLOCK), BF16),
                        pltpu.VMEM((HEADS_PER_GROUP, Q_BLOCK, HEAD_DIM), F32)],
        compiler_params=pltpu.CompilerParams(dimension_semantics=("arbitrary",),
                                             vmem_limit_bytes=VMEM_LIMIT_BYTES),
        name="dilated_attention",
    )(xb, w_qkv, b_qkv, dist)


ROW_TILE = 1024
SUB_ROWS = 512


def _layer_norm(r, g, b):
    mu = jnp.mean(r, axis=-1, keepdims=True)
    d = r - mu
    var = jnp.mean(d * d, axis=-1, keepdims=True)
    return d * lax.rsqrt(var + LN_EPS) * g + b


def _sub_tiles(ref):
    return [slice(s * SUB_ROWS, (s + 1) * SUB_ROWS) for s in range(ref.shape[0] // SUB_ROWS)]


def _merge_kernel(alpha, x_ref, ya_ref, yh_ref, wg_ref, bg_ref, wa_ref, wh_ref, wo_ref, g_ref, b_ref, out_ref):
    d_model = x_ref.shape[-1]
    tiles = _sub_tiles(x_ref)
    x = [x_ref[r, :] for r in tiles]
    gates = [jax.nn.sigmoid(_dot(xs.astype(BF16), wg_ref[...]) + bg_ref[...]) for xs in x]
    merged = [gt[:, :d_model] * _dot(ya_ref[r, :], wa_ref[...])
              + gt[:, d_model:] * _dot(yh_ref[r, :], wh_ref[...]) for gt, r in zip(gates, tiles)]
    mix = [_dot(mg.astype(BF16), wo_ref[...]) for mg in merged]
    for r, xs, mx in zip(tiles, x, mix):
        out_ref[r, :] = _layer_norm(alpha * xs + mx, g_ref[...], b_ref[...])


def _ffn_kernel(alpha, h_ref, w1_ref, b1_ref, w2_ref, b2_ref, g_ref, b_ref, out_ref):
    tiles = _sub_tiles(h_ref)
    h = [h_ref[r, :] for r in tiles]
    hid = [jnp.maximum(_dot(hs.astype(BF16), w1_ref[...]) + b1_ref[...], 0.0) for hs in h]
    ff = [_dot((hd * hd).astype(BF16), w2_ref[...]) + b2_ref[...] for hd in hid]
    for r, hs, fs in zip(tiles, h, ff):
        out_ref[r, :] = _layer_norm(alpha * hs + fs, g_ref[...], b_ref[...])


def _row_tiled_call(kernel_fn, name, rows, d_model, tiled, resident, casts=()):
    steps = rows // ROW_TILE
    n_main = len(tiled) + len(resident)
    cast_specs, cast_shapes = _cast_plan(casts, steps)

    def body(*refs):
        cast_in, (out_ref, *cast_out) = refs[n_main:n_main + len(casts)], refs[n_main + len(casts):]
        _cast_blocks(cast_in, cast_out)
        kernel_fn(*refs[:n_main], out_ref)

    in_specs = [pl.BlockSpec((ROW_TILE, a.shape[1]), lambda i: (i, 0)) for a in tiled]
    in_specs += [pl.BlockSpec(a.shape, lambda i: (0, 0), pipeline_mode=pl.Buffered(1)) for a in resident]
    out, *cast_out = pl.pallas_call(
        body,
        out_shape=[jax.ShapeDtypeStruct((rows, d_model), F32)] + cast_shapes,
        grid=(steps,),
        in_specs=in_specs + cast_specs,
        out_specs=[pl.BlockSpec((ROW_TILE, d_model), lambda i: (i, 0))] + cast_specs,
        compiler_params=pltpu.CompilerParams(dimension_semantics=("arbitrary",),
                                             vmem_limit_bytes=VMEM_LIMIT_BYTES),
        name=name,
    )(*tiled, *resident, *casts)
    return out, cast_out


def kernel(x, w_in, b_in, conv_w, conv_b, filt_w0, filt_b0, filt_w_inner, filt_b_inner, filt_w_out,
           filt_freq, hyena_skip, w_branch_attn, w_branch_hyena, w_out, ln1_g, ln1_b, w_ff1, b_ff1,
           w_ff2, b_ff2, ln2_g, ln2_b):
    bsz, seq_len, d_model = x.shape
    depth = w_in.shape[0]
    alpha = (2 * depth) ** 0.25
    rows = bsz * seq_len
    qkv_w = 3 * ATTN_QKV_WIDTH
    hy_w = (HYENA_ORDER + 1) * hyena_skip.shape[-1]
    row = lambda a: a.astype(F32)[None, :]
    h = x
    for layer in range(depth):
        w_l = w_in[layer].astype(BF16)
        b_l = b_in[layer].astype(F32)[None, :]
        h_rows = h.reshape(rows, d_model).astype(F32)
        spectrum, (hb, wa_l, wh_l, wo_l) = _filter_spectrum(
            seq_len, hyena_skip.shape[-1], filt_w0[layer], filt_b0[layer], filt_w_inner[layer],
            filt_b_inner[layer], filt_w_out[layer], filt_freq[layer],
            casts=[h_rows, w_branch_attn[layer].astype(F32), w_branch_hyena[layer].astype(F32),
                   w_out[layer].astype(F32)])
        hb = hb.reshape(bsz, seq_len, d_model)
        y_attn = _attention_branch(hb, w_l, b_l)
        y_hyena = _hyena_branch(hb, w_l, b_l, qkv_w,
                                conv_w[layer].astype(F32), conv_b[layer].astype(F32)[None, :],
                                hyena_skip[layer].astype(F32), spectrum)
        h1, (w1_l, w2_l) = _row_tiled_call(
            functools.partial(_merge_kernel, alpha), "merge_ln", rows, d_model,
            [h_rows, y_attn.reshape(rows, -1), y_hyena.reshape(rows, -1)],
            [w_l[:, qkv_w + hy_w:], b_l[:, qkv_w + hy_w:], wa_l, wh_l, wo_l, row(ln1_g[layer]),
             row(ln1_b[layer])],
            casts=[w_ff1[layer].astype(F32), w_ff2[layer].astype(F32)])
        h2, _ = _row_tiled_call(
            functools.partial(_ffn_kernel, alpha), "ffn_ln", rows, d_model, [h1],
            [w1_l, row(b_ff1[layer]), w2_l, row(b_ff2[layer]), row(ln2_g[layer]), row(ln2_b[layer])])
        h = h2.reshape(bsz, seq_len, d_model)
    return h
```

```python
import functools
import math

import jax
import jax.numpy as jnp
import numpy as np
from jax import lax
from jax.experimental import pallas as pl
from jax.experimental.pallas import tpu as pltpu

F32 = jnp.float32
BF16 = jnp.bfloat16

ATTN_GROUPS = ((128, 1), (512, 4), (2048, 16))
N_GROUPS = len(ATTN_GROUPS)
HEADS_PER_GROUP = 4
HEAD_DIM = 128
GROUP_WIDTH = HEADS_PER_GROUP * HEAD_DIM
ATTN_QKV_WIDTH = N_GROUPS * GROUP_WIDTH
Q_BLOCK = 128
ALIBI_MAX_EXP = 8.0
HYENA_ORDER = 2
FILTER_BANDS = 16
FILTER_EMB = 1 + 2 * FILTER_BANDS
FILTER_HIDDEN = 64
FILTER_INNER = 2
DECAY_TARGET = 1e-2
FAST_DECAY_PCT = 0.3
SLOW_DECAY_PCT = 1.5
LN_EPS = 1e-5

LANES = 128
VMEM_LIMIT_BYTES = 58 * 1024 * 1024

FFT_N1 = 16
FFT_N2 = 256
FFT_SLICES = FFT_N1 // 2 + 1
FFT_ROWS = FFT_N1 * FFT_N2
RSQRT2 = 1.0 / math.sqrt(2.0)


def _unit_root(k, n):
    snap = lambda v: float(round(v)) if abs(v - round(v)) < 1e-12 else v
    return snap(math.cos(2.0 * math.pi * k / n)), snap(-math.sin(2.0 * math.pi * k / n))


W16 = tuple(_unit_root(k, FFT_N1) for k in range(FFT_SLICES))


def _dot(a, b):
    return jnp.dot(a, b, preferred_element_type=F32)


def _split_bf16(a):
    hi = a.astype(BF16)
    return hi, (a - hi.astype(F32)).astype(BF16)


def _dot3(a, b):
    a_hi, a_lo = _split_bf16(a)
    b_hi, b_lo = _split_bf16(b)
    return (_dot(jnp.concatenate([a_hi, a_lo], axis=1), jnp.concatenate([b_hi, b_hi], axis=0))
            + _dot(a_hi, b_lo))


def _dot_nt(a, b):
    return lax.dot_general(a, b, (((1,), (1,)), ((), ())), preferred_element_type=F32)


CAST_ROWS = 256


def _cast_plan(arrays, steps):
    specs, shapes = [], []
    for a in arrays:
        rows, rem = divmod(a.shape[0], steps)
        assert rem == 0 and rows % 16 == 0, a.shape
        specs.append(pl.BlockSpec((rows, a.shape[1]), lambda i: (i, 0)))
        shapes.append(jax.ShapeDtypeStruct(a.shape, BF16))
    return specs, shapes


def _cast_blocks(src_refs, dst_refs):
    for src, dst in zip(src_refs, dst_refs):
        chunk = min(src.shape[0], CAST_ROWS)
        assert src.shape[0] % chunk == 0

        def body(i, carry, src=src, dst=dst, chunk=chunk):
            rows = pl.ds(pl.multiple_of(i * chunk, chunk), chunk)
            dst[rows, :] = src[rows, :].astype(dst.dtype)
            return carry
        lax.fori_loop(0, src.shape[0] // chunk, body, 0)


@functools.lru_cache(maxsize=None)
def _fft_constants(seq_len):
    n_fft = 2 * seq_len
    assert n_fft == FFT_N1 * FFT_N2
    k2 = np.arange(FFT_N2)
    fwd = []
    for k1 in range(FFT_SLICES):
        ang = -2.0 * np.pi * ((np.outer(FFT_N1 * k2 + k1, k2) % n_fft) / n_fft)
        er, ei = np.cos(ang), np.sin(ang)
        fwd.append(np.block([[er, -ei], [ei, er]]))
    fwd = np.stack(fwd, axis=0).astype(np.float32)
    inv = np.ascontiguousarray(np.transpose(fwd, (0, 2, 1)))
    return fwd, inv


@functools.lru_cache(maxsize=None)
def _filter_constants(seq_len, width):
    n_fft = 2 * seq_len
    t = np.linspace(0.0, 1.0, seq_len)
    bands = np.linspace(1e-4, FILTER_BANDS - 1, FILTER_BANDS)
    ang = (2.0 * np.pi / seq_len) * np.arange(seq_len)[:, None] * bands
    feats = np.concatenate([t[:, None], np.cos(ang), -np.sin(ang)], axis=-1)
    src = np.concatenate([np.arange(seq_len), [0], np.arange(seq_len - 1, 0, -1)])
    feats_ext = np.zeros((seq_len, 2 * LANES), np.float32)
    feats_ext[:, :FILTER_EMB] = feats[src[:seq_len]]
    feats_ext[:, LANES:LANES + FILTER_EMB] = feats[src[seq_len:]]
    t_mask = np.zeros((2, n_fft, LANES), np.float32)
    t_mask[0] = t[src][:, None]
    t_mask[1] = 1.0
    t_mask[1, seq_len] = 0.0
    deltas = np.abs(np.linspace(math.log(DECAY_TARGET) / SLOW_DECAY_PCT,
                                math.log(DECAY_TARGET) / FAST_DECAY_PCT, width))
    deltas = np.tile(deltas[None, :], (1, HYENA_ORDER)).astype(np.float32)
    return feats_ext, t_mask, deltas


@functools.lru_cache(maxsize=None)
def _attn_distance_tiles():
    n_off = Q_BLOCK // 2
    qi = np.arange(Q_BLOCK)[:, None]
    kj = np.arange(2 * Q_BLOCK)[None, :]
    tiles = []
    for shift in (0, n_off, 2 * n_off):
        off = np.abs(kj - shift - qi).astype(np.float32)
        tiles.append(np.where(off <= n_off, off, np.inf))
    off = np.abs(kj - qi).astype(np.float32)
    t3 = np.where(off <= n_off, off, np.inf)
    t3[:, Q_BLOCK:] = np.inf
    tiles.append(t3)
    return np.stack(tiles, axis=0).astype(np.float32)


def _alibi_slopes():
    n = N_GROUPS * HEADS_PER_GROUP
    return [2.0 ** (-ALIBI_MAX_EXP * j / n) for j in range(1, n + 1)]


def _radix8_half(z0, z1, z2, z3):
    s02, d02 = z0 + z2, z0 - z2
    s13, d13 = z1 + z3, z1 - z3
    ss, dd = s13 * RSQRT2, d13 * RSQRT2
    return {0: (s02 + s13, None), 4: (s02 - s13, None), 2: (d02, -d13),
            1: (z0 + dd, -z2 - ss), 3: (z0 - dd, z2 - ss)}


def _add(a, b, sign=1.0):
    if b is None:
        return a
    if a is None:
        return b if sign > 0 else -b
    return a + b if sign > 0 else a - b


def _cmul_const(re, im, wr, wi):
    def scaled(v, w):
        if v is None or w == 0.0:
            return None
        return v if w == 1.0 else (-v if w == -1.0 else v * w)
    if re is not None and im is not None and wr != 0.0 and abs(wr) == abs(wi):
        if wi == wr:
            return (re - im) * wr, (re + im) * wr
        return (re + im) * wr, (im - re) * wr
    return _add(scaled(re, wr), scaled(im, wi), -1.0), _add(scaled(re, wi), scaled(im, wr))


def _radix16_stage(blocks):
    even = _radix8_half(*blocks[0::2])
    odd = _radix8_half(*blocks[1::2])
    half = FFT_N1 // 2
    out = [None] * FFT_SLICES
    out[0] = (even[0][0] + odd[0][0], None)
    out[half] = (even[0][0] - odd[0][0], None)
    out[half // 2] = (even[4][0], -odd[4][0])
    for k in range(1, half // 2):
        (er, ei), (pr, pi) = even[k], _cmul_const(*odd[k], *W16[k])
        out[k] = (er + pr, ei + pi)
        out[half - k] = (er - pr, pi - ei)
    return out


def _assemble8(c0, c4, c1, c2, c3):
    (c1r, c1i), (c2r, c2i), (c3r, c3i) = c1, c2, c3
    e, o = c0 + c4, c0 - c4
    return [e + c1r + c2r + c3r,
            o + (c1r - c1i - c3r - c3i) * RSQRT2 - c2i,
            e - c1i - c2r + c3i,
            o + (c3r - c3i - c1r - c1i) * RSQRT2 + c2i]


def _inverse_radix16_stage(c):
    half = FFT_N1 // 2
    plus = [(c[k][0] + c[half - k][0], c[k][1] - c[half - k][1]) for k in range(1, half // 2)]
    minus = [_cmul_const(c[k][0] - c[half - k][0], c[k][1] + c[half - k][1], W16[k][0], -W16[k][1])
             for k in range(1, half // 2)]
    mid = c[half // 2]
    even = _assemble8(c[0][0] + c[half][0], mid[0], *plus)
    odd = _assemble8(c[0][0] - c[half][0], -mid[1], *minus)
    return [blk for pair_ in zip(even, odd) for blk in pair_]


def _slice_rows(k1):
    if k1 == 0:
        return 0, FFT_N2
    if k1 == FFT_N1 // 2:
        return FFT_N2, FFT_N2
    return 2 * FFT_N2 * k1, 2 * FFT_N2


def _store_slices(ref, off, rows, values, scale=None):
    for k1, (re, im) in enumerate(values):
        r0, n = _slice_rows(k1)
        parts = (re,) if n == FFT_N2 else (re, im)
        for j, part in enumerate(parts):
            if scale is not None:
                part = part * scale
            ref[pl.ds(r0 + j * FFT_N2 + off, rows), :] = part.astype(ref.dtype)


def _load_slices(ref, off, rows):
    out = []
    for k1 in range(FFT_SLICES):
        r0, n = _slice_rows(k1)
        re = ref[pl.ds(r0 + off, rows), :]
        out.append((re, ref[pl.ds(r0 + FFT_N2 + off, rows), :] if n > FFT_N2 else None))
    return out


def _forward_dft(b_ref, k1, mf_ref):
    r0, n = _slice_rows(k1)
    m = mf_ref[k1] if n > FFT_N2 else mf_ref[k1, :, 0:FFT_N2]
    return _dot(m, b_ref[r0:r0 + n, :])


def _inverse_dft(p, k1, mi_ref, c_ref):
    r0, n = _slice_rows(k1)
    m = mi_ref[k1] if n > FFT_N2 else mi_ref[k1, 0:FFT_N2, :]
    c_ref[r0:r0 + n, :] = _dot(m, p)


FILT_CW = 256
FILT_ROWS = 256
A_CHUNK = 16


def _filter_kernel(n_cast, feats_ref, tmask_ref, w0_ref, b0_ref, wi_ref, bi_ref, freq_ref, wf_ref, wb_ref,
                   delta_ref, mf_ref, *rest):
    cast_in, (h_ref, *cast_out), (hid_ref, ts_ref, b_ref) = (
        rest[:n_cast], rest[n_cast:2 * n_cast + 1], rest[2 * n_cast + 1:])
    _cast_blocks(cast_in, cast_out)
    seq_len = feats_ref.shape[0]
    n_fft = 2 * seq_len

    @pl.when(pl.program_id(0) == 0)
    def _():
        def body(i, carry):
            rows = pl.ds(pl.multiple_of(i * FILT_ROWS, FILT_ROWS), FILT_ROWS)
            freq = freq_ref[...]
            hid = jnp.sin(freq * (_dot3(feats_ref[rows, :], w0_ref[...]) + b0_ref[...]))
            for layer in range(FILTER_INNER):
                hid = jnp.sin(freq * (_dot3(hid, wi_ref[layer]) + bi_ref[layer]))
            hid_ref[rows, :] = hid
            return carry
        lax.fori_loop(0, seq_len // FILT_ROWS, body, 0)

    h_hi, h_lo = _split_bf16(hid_ref[...])
    h_cat = jnp.concatenate([h_hi, h_lo], axis=1)
    for half, w_ref in enumerate((wf_ref, wb_ref)):
        w_hi, w_lo = _split_bf16(w_ref[...])
        ts_ref[half * seq_len:(half + 1) * seq_len, :] = (
            _dot(h_cat, jnp.concatenate([w_hi, w_hi], axis=0)) + _dot(h_hi, w_lo))

    def window_body(i, ssq):
        rows = pl.ds(pl.multiple_of(i * FILT_ROWS, FILT_ROWS), FILT_ROWS)
        widen = lambda a: jnp.concatenate([a] * (FILT_CW // LANES), axis=1)
        val = (ts_ref[rows, :] * jnp.exp(-widen(tmask_ref[0, rows, :]) * delta_ref[...])
               * widen(tmask_ref[1, rows, :]))
        ts_ref[rows, :] = val
        return ssq + jnp.sum(val * val, axis=0, keepdims=True)
    ssq = lax.fori_loop(0, n_fft // FILT_ROWS, window_body, jnp.zeros((1, FILT_CW), F32))
    scale = lax.rsqrt(ssq)

    def a_body(i, carry):
        off = pl.multiple_of(i * A_CHUNK, A_CHUNK)
        blk = [ts_ref[pl.ds(t1 * FFT_N2 + off, A_CHUNK), :] for t1 in range(FFT_N1)]
        lo = _radix16_stage(blk[:FFT_N1 // 2])
        hi = _radix16_stage(blk[FFT_N1 // 2:])
        vals = [(_add(lo[k][0], hi[k][0], 1.0 if k % 2 == 0 else -1.0),
                 _add(lo[k][1], hi[k][1], 1.0 if k % 2 == 0 else -1.0)) for k in range(FFT_SLICES)]
        _store_slices(b_ref, off, A_CHUNK, vals, scale)
        return carry
    lax.fori_loop(0, FFT_N2 // A_CHUNK, a_body, 0)

    for k1 in range(FFT_SLICES):
        s = (1.0 if k1 in (0, FFT_N1 // 2) else 2.0) / n_fft
        h_ref[k1] = _forward_dft(b_ref, k1, mf_ref) * s


def _filter_spectrum(seq_len, width, filt_w0, filt_b0, filt_w_inner, filt_b_inner, filt_w_out, filt_freq,
                     casts=()):
    n_fft = 2 * seq_len
    hid = LANES
    feats_ext, t_mask, deltas = _filter_constants(seq_len, width)
    mf = jnp.asarray(_fft_constants(seq_len)[0]).astype(BF16)
    assert 2 * FILTER_HIDDEN == hid and FILTER_EMB <= LANES
    n_cols = HYENA_ORDER * width
    steps = n_cols // FILT_CW
    zeros = lambda r, c: jnp.zeros((r, c), F32)
    w0p = jnp.pad(filt_w0.astype(F32), ((0, LANES - FILTER_EMB), (0, 0)))
    w0 = jnp.block([[w0p, zeros(LANES, FILTER_HIDDEN)], [zeros(LANES, FILTER_HIDDEN), w0p]])
    b0 = jnp.tile(filt_b0.astype(F32), 2)[None, :]
    wi = jnp.stack([jnp.block([[w, zeros(FILTER_HIDDEN, FILTER_HIDDEN)], [zeros(FILTER_HIDDEN, FILTER_HIDDEN), w]])
                    for w in filt_w_inner.astype(F32)])
    bi = jnp.tile(filt_b_inner.astype(F32), (1, 2))[:, None, :]
    freq = jnp.tile(filt_freq.astype(F32), 2)[None, :]
    w_out = filt_w_out.astype(F32)
    wl = jnp.concatenate([jnp.pad(w_out[:, :n_cols], ((0, FILTER_HIDDEN), (0, 0))),
                          jnp.pad(w_out[:, n_cols:], ((FILTER_HIDDEN, 0), (0, 0)))], axis=1)
    full = lambda *shape: pl.BlockSpec(shape, lambda j: (0,) * len(shape))
    cast_specs, cast_shapes = _cast_plan(casts, steps)
    spectrum, *cast_out = pl.pallas_call(
        functools.partial(_filter_kernel, len(casts)),
        out_shape=[jax.ShapeDtypeStruct((FFT_SLICES, 2 * FFT_N2, n_cols), F32)] + cast_shapes,
        grid=(steps,),
        in_specs=[
            full(seq_len, 2 * LANES), full(2, n_fft, LANES), full(2 * LANES, hid), full(1, hid),
            full(FILTER_INNER, hid, hid), full(FILTER_INNER, 1, hid), full(1, hid),
            pl.BlockSpec((hid, FILT_CW), lambda j: (0, j)),
            pl.BlockSpec((hid, FILT_CW), lambda j: (0, steps + j)),
            pl.BlockSpec((1, FILT_CW), lambda j: (0, j)),
            full(FFT_SLICES, 2 * FFT_N2, 2 * FFT_N2),
        ] + cast_specs,
        out_specs=[pl.BlockSpec((FFT_SLICES, 2 * FFT_N2, FILT_CW), lambda j: (0, 0, j))] + cast_specs,
        scratch_shapes=[pltpu.VMEM((seq_len, hid), F32), pltpu.VMEM((n_fft, FILT_CW), F32),
                        pltpu.VMEM((FFT_ROWS, FILT_CW), BF16)],
        compiler_params=pltpu.CompilerParams(dimension_semantics=("arbitrary",),
                                             vmem_limit_bytes=VMEM_LIMIT_BYTES),
        name="hyena_filter",
    )(feats_ext, t_mask, w0, b0, wi, bi, freq, wl, wl, deltas, mf, *casts)
    return spectrum, cast_out


HY_CW = 256
CONV_ROWS = 64
PAD_ROWS = 8


def _hyena_kernel(x_ref, wv_ref, w1_ref, w2_ref, bv_ref, b1_ref, b2_ref, cwv_ref, cw1_ref, cw2_ref,
                  cbv_ref, cb1_ref, cb2_ref, skip_ref, h0_ref, h1_ref, mf_ref, mi_ref,
                  out_ref, u_ref, p_ref, b_ref, c_ref):
    seq_len = x_ref.shape[0]
    width = HY_CW
    n2 = FFT_N2
    xb = x_ref[...]

    n_slabs = width // LANES
    zeros = jnp.zeros((PAD_ROWS, LANES), F32)
    for slot in range(u_ref.shape[0]):
        for j in range(n_slabs):
            u_ref[slot, j, 0:PAD_ROWS, :] = zeros
            u_ref[slot, j, PAD_ROWS + seq_len:2 * PAD_ROWS + seq_len, :] = zeros
    parts = ((wv_ref, bv_ref, cwv_ref, cbv_ref), (w1_ref, b1_ref, cw1_ref, cb1_ref),
             (w2_ref, b2_ref, cw2_ref, cb2_ref))

    def project(p, slot):
        w_ref, bias_ref, _, _ = parts[p]
        half = seq_len // 2
        for r0 in (0, half):
            u = _dot(xb[r0:r0 + half], w_ref[...]) + bias_ref[...]
            for j in range(n_slabs):
                u_ref[slot, j, PAD_ROWS + r0:PAD_ROWS + r0 + half, :] = u[:, j * LANES:(j + 1) * LANES]

    def short_conv(p, slot):
        _, _, cw_ref, cb_ref = parts[p]

        def conv_body(i, carry):
            start = pl.multiple_of(i * CONV_ROWS, CONV_ROWS)
            cw = cw_ref[...]
            cb = cb_ref[...]
            for j in range(n_slabs):
                lanes = slice(j * LANES, (j + 1) * LANES)
                taps = [u_ref[slot, j, pl.ds(start + PAD_ROWS - 1 + k, CONV_ROWS, stride=1), :]
                        for k in range(3)]
                p_ref[p, pl.ds(start, CONV_ROWS), lanes] = (
                    cw[0:1, lanes] * taps[0] + cw[1:2, lanes] * taps[1] + cw[2:3, lanes] * taps[2]
                    + cb[:, lanes])
            return carry
        lax.fori_loop(0, seq_len // CONV_ROWS, conv_body, 0, unroll=True)

    def spectral_product(h_ref):
        x_next = _forward_dft(b_ref, 0, mf_ref)
        for k1 in range(FFT_SLICES):
            x = x_next
            if k1 + 1 < FFT_SLICES:
                x_next = _forward_dft(b_ref, k1 + 1, mf_ref)
            xr, xi = x[:n2], x[n2:]
            hr = h_ref[k1, 0:n2, :]
            hi = h_ref[k1, n2:2 * n2, :]
            prod = jnp.concatenate([(xr * hr - xi * hi).astype(BF16), (xr * hi + xi * hr).astype(BF16)], axis=0)
            _inverse_dft(prod, k1, mi_ref, c_ref)

    def gated_blocks(order, off):
        skip = skip_ref[order:order + 1, :]
        ys = _inverse_radix16_stage(_load_slices(c_ref, off, A_CHUNK))
        out = []
        for t1, y in enumerate(ys):
            rows = pl.ds(t1 * n2 + off, A_CHUNK)
            out.append(p_ref[order + 1, rows, :] * (y + skip * p_ref[0, rows, :]))
        return out

    project(0, 0)
    short_conv(0, 0)

    def a_body(i, carry):
        off = pl.multiple_of(i * A_CHUNK, A_CHUNK)
        blk = [p_ref[0, pl.ds(t1 * n2 + off, A_CHUNK), :] for t1 in range(FFT_N1 // 2)]
        _store_slices(b_ref, off, A_CHUNK, _radix16_stage(blk))
        return carry
    lax.fori_loop(0, n2 // A_CHUNK, a_body, 0, unroll=True)

    project(1, 0)
    project(2, 1)
    spectral_product(h0_ref)
    short_conv(1, 0)
    short_conv(2, 1)

    def mid_body(i, carry):
        off = pl.multiple_of(i * A_CHUNK, A_CHUNK)
        z = gated_blocks(0, off)
        for t1, blk in enumerate(z):
            p_ref[0, pl.ds(t1 * n2 + off, A_CHUNK), :] = blk
        _store_slices(b_ref, off, A_CHUNK, _radix16_stage(z))
        return carry
    lax.fori_loop(0, n2 // A_CHUNK, mid_body, 0)

    spectral_product(h1_ref)

    def out_body(i, carry):
        off = pl.multiple_of(i * A_CHUNK, A_CHUNK)
        for t1, blk in enumerate(gated_blocks(1, off)):
            out_ref[pl.ds(t1 * n2 + off, A_CHUNK), :] = blk.astype(out_ref.dtype)
        return carry
    lax.fori_loop(0, n2 // A_CHUNK, out_body, 0)


def _hyena_branch(xb, w_in, b_in, col0, conv_w, conv_b, hyena_skip, spectrum):
    bsz, seq_len, d_model = xb.shape
    width = hyena_skip.shape[-1]
    nblk = width // HY_CW
    mf, mi = (jnp.asarray(m).astype(BF16) for m in _fft_constants(seq_len))
    col = lambda part: (lambda c, b: (0, part * nblk + c))
    blk0, rem = divmod(col0, HY_CW)
    assert rem == 0
    in_col = lambda part: (lambda c, b: (0, blk0 + part * nblk + c))
    const = lambda *shape: pl.BlockSpec(shape, lambda c, b: (0,) * len(shape), pipeline_mode=pl.Buffered(1))
    spec_h = lambda order: pl.BlockSpec((FFT_SLICES, 2 * FFT_N2, HY_CW), lambda c, b: (0, 0, order * nblk + c),
                                        pipeline_mode=pl.Buffered(1))
    in_specs = [pl.BlockSpec((None, seq_len, d_model), lambda c, b: (b, 0, 0))]
    in_specs += [pl.BlockSpec((d_model, HY_CW), in_col(p)) for p in range(3)]
    in_specs += [pl.BlockSpec((1, HY_CW), in_col(p)) for p in range(3)]
    in_specs += [pl.BlockSpec((3, HY_CW), col(p)) for p in range(3)]
    in_specs += [pl.BlockSpec((1, HY_CW), col(p)) for p in range(3)]
    in_specs += [pl.BlockSpec((HYENA_ORDER, HY_CW), lambda c, b: (0, c)), spec_h(0), spec_h(1),
                 const(FFT_SLICES, 2 * FFT_N2, 2 * FFT_N2), const(FFT_SLICES, 2 * FFT_N2, 2 * FFT_N2)]
    return pl.pallas_call(
        _hyena_kernel,
        out_shape=jax.ShapeDtypeStruct((bsz, seq_len, width), BF16),
        grid=(nblk, bsz),
        in_specs=in_specs,
        out_specs=pl.BlockSpec((None, seq_len, HY_CW), lambda c, b: (b, 0, c)),
        scratch_shapes=[pltpu.VMEM((2, HY_CW // LANES, seq_len + 2 * PAD_ROWS, LANES), F32),
                        pltpu.VMEM((3, seq_len, HY_CW), F32),
                        pltpu.VMEM((FFT_ROWS, HY_CW), BF16),
                        pltpu.VMEM((FFT_ROWS, HY_CW), F32)],
        compiler_params=pltpu.CompilerParams(dimension_semantics=("arbitrary", "arbitrary"),
                                             vmem_limit_bytes=VMEM_LIMIT_BYTES),
        name="hyena_branch",
    )(xb, w_in, w_in, w_in, b_in, b_in, b_in, conv_w, conv_w, conv_w, conv_b, conv_b, conv_b,
      hyena_skip, spectrum, spectrum, mf, mi)


def _attn_kernel(x_ref, w_ref, b_ref, dist_ref, out_ref, qkv_ref, acc_ref, m_ref, l_ref,
                 s_ref, p_ref, al_ref):
    seq_len = x_ref.shape[0]
    xb = x_ref[...]
    slopes = _alibi_slopes()
    q_scale = 1.0 / math.sqrt(HEAD_DIM)
    heads = range(HEADS_PER_GROUP)

    def stage_scores(blk, coefs):
        qr, kr, dist, kc = blk
        for h in heads:
            s_ref[h, :, 0:kc] = _dot_nt(qkv_ref[0, h, qr, :].astype(BF16),
                                        qkv_ref[1, h, kr, :].astype(BF16)) + dist * coefs[h]

    def stage_softmax(blk, first, last):
        qr, _, _, kc = blk
        s = [s_ref[h, :, 0:kc] for h in heads]
        m_blk = [jnp.max(s[h], axis=-1, keepdims=True) for h in heads]
        if first:
            m_new = [jnp.broadcast_to(m_blk[h], (Q_BLOCK, HEAD_DIM)) for h in heads]
        else:
            m_old = [m_ref[h, qr, :] for h in heads]
            m_new = [jnp.maximum(m_old[h], m_blk[h]) for h in heads]
        m_wide = [jnp.concatenate([m_new[h]] * (kc // HEAD_DIM), axis=1) for h in heads]
        for h in heads:
            p_ref[h, :, 0:kc] = jnp.exp(s[h] - m_wide[h]).astype(BF16)
            if not first:
                al_ref[h] = jnp.exp(m_old[h] - m_new[h])
            if not last:
                m_ref[h, qr, :] = m_new[h]

    def stage_output(blk, first, last):
        qr, kr, _, kc = blk
        ones = jnp.ones((kc, HEAD_DIM), BF16)
        pv = [_dot(p_ref[h, :, 0:kc], jnp.concatenate([qkv_ref[2, h, kr, :].astype(BF16), ones], axis=1))
              for h in heads]
        acc = [r[:, :HEAD_DIM] for r in pv]
        l_new = [r[:, HEAD_DIM:] for r in pv]
        if not first:
            alpha = [al_ref[h] for h in heads]
            l_new = [alpha[h] * l_ref[h, qr, :] + l_new[h] for h in heads]
            acc = [alpha[h] * acc_ref[h, qr, :] + acc[h] for h in heads]
        for h in heads:
            if last:
                out_ref[qr, h * HEAD_DIM:(h + 1) * HEAD_DIM] = (acc[h] / l_new[h]).astype(out_ref.dtype)
            else:
                l_ref[h, qr, :] = l_new[h]
                acc_ref[h, qr, :] = acc[h]

    order = sorted(range(N_GROUPS), key=lambda g: -ATTN_GROUPS[g][1])
    assert ATTN_GROUPS[order[-1]][1] == 1
    for pos, g in enumerate(order):
        window, dil = ATTN_GROUPS[g]
        first, last = pos == 0, pos == N_GROUPS - 1
        n_off = (window // 2) // dil
        assert n_off == Q_BLOCK // 2
        stream = seq_len // dil
        nb = stream // Q_BLOCK
        n_blocks = dil * nb
        for part in range(3):
            c0 = part * ATTN_QKV_WIDTH + g * GROUP_WIDTH
            proj = _dot(xb, w_ref[:, c0:c0 + GROUP_WIDTH]) + b_ref[:, c0:c0 + GROUP_WIDTH]
            if part == 0:
                proj = proj * q_scale
            for h in heads:
                qkv_ref[part, h] = proj[:, h * HEAD_DIM:(h + 1) * HEAD_DIM]
        coefs = [-slopes[g * HEADS_PER_GROUP + h] * dil for h in heads]

        def block(i, dil=dil, nb=nb, stream=stream):
            if nb == 1:
                rows = pl.ds(i, Q_BLOCK, stride=dil) if dil > 1 else pl.ds(i, Q_BLOCK)
                return rows, rows, dist_ref[3, :, 0:Q_BLOCK], Q_BLOCK
            c = i // nb
            n = i % nb
            lo = jnp.clip(n * Q_BLOCK - n_off, 0, stream - 2 * Q_BLOCK)
            dist = dist_ref[jnp.where(n == 0, 0, jnp.where(n == nb - 1, 2, 1))]
            if dil > 1:
                return (pl.ds(c + n * Q_BLOCK * dil, Q_BLOCK, stride=dil),
                        pl.ds(c + lo * dil, 2 * Q_BLOCK, stride=dil), dist, 2 * Q_BLOCK)
            return (pl.ds(pl.multiple_of(n * Q_BLOCK, Q_BLOCK), Q_BLOCK),
                    pl.ds(pl.multiple_of(lo, n_off), 2 * Q_BLOCK), dist, 2 * Q_BLOCK)

        def steady(i, carry, block=block, coefs=coefs, first=first, last=last):
            stage_output(block(i - 2), first, last)
            stage_softmax(block(i - 1), first, last)
            stage_scores(block(i), coefs)
            return carry

        assert n_blocks >= 3
        stage_scores(block(0), coefs)
        stage_softmax(block(0), first, last)
        stage_scores(block(1), coefs)
        lax.fori_loop(2, n_blocks, steady, 0)
        stage_output(block(n_blocks - 2), first, last)
        stage_softmax(block(n_blocks - 1), first, last)
        stage_output(block(n_blocks - 1), first, last)


def _attention_branch(xb, w_qkv, b_qkv):
    bsz, seq_len, d_model = xb.shape
    dist = _attn_distance_tiles()
    qkv_w = 3 * ATTN_QKV_WIDTH
    return pl.pallas_call(
        _attn_kernel,
        out_shape=jax.ShapeDtypeStruct((bsz, seq_len, GROUP_WIDTH), BF16),
        grid=(bsz,),
        in_specs=[pl.BlockSpec((None, seq_len, d_model), lambda b: (b, 0, 0)),
                  pl.BlockSpec((d_model, qkv_w), lambda b: (0, 0), pipeline_mode=pl.Buffered(1)),
                  pl.BlockSpec((1, qkv_w), lambda b: (0, 0)),
                  pl.BlockSpec(dist.shape, lambda b: (0, 0, 0))],
        out_specs=pl.BlockSpec((None, seq_len, GROUP_WIDTH), lambda b: (b, 0, 0)),
        scratch_shapes=[pltpu.VMEM((3, HEADS_PER_GROUP, seq_len, HEAD_DIM), F32),
                        pltpu.VMEM((HEADS_PER_GROUP, seq_len, HEAD_DIM), F32),
                        pltpu.VMEM((HEADS_PER_GROUP, seq_len, HEAD_DIM), F32),
                        pltpu.VMEM((HEADS_PER_GROUP, seq_len, HEAD_DIM), F32),
                        pltpu.VMEM((HEADS_PER_GROUP, Q_BLOCK, 2 * Q_BLOCK), F32),
                        pltpu.VMEM((HEADS_PER_GROUP, Q_BLOCK, 2 * Q_BLOCK), BF16),
                        pltpu.VMEM((HEADS_PER_GROUP, Q_BLOCK, HEAD_DIM), F32)],
        compiler_params=pltpu.CompilerParams(dimension_semantics=("arbitrary",),
                                             vmem_limit_bytes=VMEM_LIMIT_BYTES),
        name="dilated_attention",
    )(xb, w_qkv, b_qkv, dist)


ROW_TILE = 1024
SUB_ROWS = 512


def _layer_norm(r, g, b):
    mu = jnp.mean(r, axis=-1, keepdims=True)
    d = r - mu
    var = jnp.mean(d * d, axis=-1, keepdims=True)
    return d * lax.rsqrt(var + LN_EPS) * g + b


def _sub_tiles(ref):
    return [slice(s * SUB_ROWS, (s + 1) * SUB_ROWS) for s in range(ref.shape[0] // SUB_ROWS)]


def _merge_kernel(alpha, n_gate, x_ref, ya_ref, yh_ref, *rest):
    wg_refs, (bg_ref, wa_ref, wh_ref, wo_ref, g_ref, b_ref, out_ref) = rest[:n_gate], rest[n_gate:]
    d_model = x_ref.shape[-1]
    gw = wg_refs[0].shape[1]
    tiles = _sub_tiles(x_ref)
    x = [x_ref[r, :] for r in tiles]
    xb = [xs.astype(BF16) for xs in x]
    gates = [jnp.concatenate([jax.nn.sigmoid(_dot(xs, w[...]) + bg_ref[:, j * gw:(j + 1) * gw])
                              for j, w in enumerate(wg_refs)], axis=1) for xs in xb]
    merged = [gt[:, :d_model] * _dot(ya_ref[r, :], wa_ref[...])
              + gt[:, d_model:] * _dot(yh_ref[r, :], wh_ref[...]) for gt, r in zip(gates, tiles)]
    mix = [_dot(mg.astype(BF16), wo_ref[...]) for mg in merged]
    for r, xs, mx in zip(tiles, x, mix):
        out_ref[r, :] = _layer_norm(alpha * xs + mx, g_ref[...], b_ref[...])


def _ffn_kernel(alpha, h_ref, w1_ref, b1_ref, w2_ref, b2_ref, g_ref, b_ref, out_ref):
    tiles = _sub_tiles(h_ref)
    h = [h_ref[r, :] for r in tiles]
    hid = [jnp.maximum(_dot(hs.astype(BF16), w1_ref[...]) + b1_ref[...], 0.0) for hs in h]
    ff = [_dot((hd * hd).astype(BF16), w2_ref[...]) + b2_ref[...] for hd in hid]
    for r, hs, fs in zip(tiles, h, ff):
        out_ref[r, :] = _layer_norm(alpha * hs + fs, g_ref[...], b_ref[...])


def _row_tiled_call(kernel_fn, name, rows, d_model, tiled, resident, casts=()):
    steps = rows // ROW_TILE
    n_main = len(tiled) + len(resident)
    cast_specs, cast_shapes = _cast_plan(casts, steps)

    def resident_spec(item):
        if isinstance(item, tuple):
            a, width, j = item
            return pl.BlockSpec((a.shape[0], width), lambda i: (0, j), pipeline_mode=pl.Buffered(1))
        return pl.BlockSpec(item.shape, lambda i: (0, 0), pipeline_mode=pl.Buffered(1))

    def body(*refs):
        cast_in, (out_ref, *cast_out) = refs[n_main:n_main + len(casts)], refs[n_main + len(casts):]
        _cast_blocks(cast_in, cast_out)
        kernel_fn(*refs[:n_main], out_ref)

    in_specs = [pl.BlockSpec((ROW_TILE, a.shape[1]), lambda i: (i, 0)) for a in tiled]
    in_specs += [resident_spec(item) for item in resident]
    resident = [item[0] if isinstance(item, tuple) else item for item in resident]
    out, *cast_out = pl.pallas_call(
        body,
        out_shape=[jax.ShapeDtypeStruct((rows, d_model), F32)] + cast_shapes,
        grid=(steps,),
        in_specs=in_specs + cast_specs,
        out_specs=[pl.BlockSpec((ROW_TILE, d_model), lambda i: (i, 0))] + cast_specs,
        compiler_params=pltpu.CompilerParams(dimension_semantics=("arbitrary",),
                                             vmem_limit_bytes=VMEM_LIMIT_BYTES),
        name=name,
    )(*tiled, *resident, *casts)
    return out, cast_out


def kernel(x, w_in, b_in, conv_w, conv_b, filt_w0, filt_b0, filt_w_inner, filt_b_inner, filt_w_out,
           filt_freq, hyena_skip, w_branch_attn, w_branch_hyena, w_out, ln1_g, ln1_b, w_ff1, b_ff1,
           w_ff2, b_ff2, ln2_g, ln2_b):
    bsz, seq_len, d_model = x.shape
    depth = w_in.shape[0]
    alpha = (2 * depth) ** 0.25
    rows = bsz * seq_len
    qkv_w = 3 * ATTN_QKV_WIDTH
    hy_w = (HYENA_ORDER + 1) * hyena_skip.shape[-1]
    row = lambda a: a.astype(F32)[None, :]
    h = x
    for layer in range(depth):
        w_l = w_in[layer].astype(BF16)
        b_l = b_in[layer].astype(F32)[None, :]
        h_rows = h.reshape(rows, d_model).astype(F32)
        spectrum, (hb, wa_l, wh_l, wo_l) = _filter_spectrum(
            seq_len, hyena_skip.shape[-1], filt_w0[layer], filt_b0[layer], filt_w_inner[layer],
            filt_b_inner[layer], filt_w_out[layer], filt_freq[layer],
            casts=[h_rows, w_branch_attn[layer].astype(F32), w_branch_hyena[layer].astype(F32),
                   w_out[layer].astype(F32)])
        hb = hb.reshape(bsz, seq_len, d_model)
        y_attn = _attention_branch(hb, w_l, b_l)
        y_hyena = _hyena_branch(hb, w_l, b_l, qkv_w,
                                conv_w[layer].astype(F32), conv_b[layer].astype(F32)[None, :],
                                hyena_skip[layer].astype(F32), spectrum)
        gate0 = qkv_w + hy_w
        gate_w = math.gcd(gate0, 2 * d_model)
        gate_blocks = [(w_l, gate_w, gate0 // gate_w + j) for j in range(2 * d_model // gate_w)]
        h1, (w1_l, w2_l) = _row_tiled_call(
            functools.partial(_merge_kernel, alpha, len(gate_blocks)), "merge_ln", rows, d_model,
            [h_rows, y_attn.reshape(rows, -1), y_hyena.reshape(rows, -1)],
            gate_blocks + [b_l[:, gate0:], wa_l, wh_l, wo_l, row(ln1_g[layer]), row(ln1_b[layer])],
            casts=[w_ff1[layer].astype(F32), w_ff2[layer].astype(F32)])
        h2, _ = _row_tiled_call(
            functools.partial(_ffn_kernel, alpha), "ffn_ln", rows, d_model, [h1],
            [w1_l, row(b_ff1[layer]), w2_l, row(b_ff2[layer]), row(ln2_g[layer]), row(ln2_b[layer])])
        h = h2.reshape(bsz, seq_len, d_model)
    return h
```

```python
import functools
import math

import jax
import jax.numpy as jnp
import numpy as np
from jax import lax
from jax.experimental import pallas as pl
from jax.experimental.pallas import tpu as pltpu

F32 = jnp.float32
BF16 = jnp.bfloat16

ATTN_GROUPS = ((128, 1), (512, 4), (2048, 16))
N_GROUPS = len(ATTN_GROUPS)
HEADS_PER_GROUP = 4
HEAD_DIM = 128
GROUP_WIDTH = HEADS_PER_GROUP * HEAD_DIM
ATTN_QKV_WIDTH = N_GROUPS * GROUP_WIDTH
Q_BLOCK = 128
ALIBI_MAX_EXP = 8.0
HYENA_ORDER = 2
FILTER_BANDS = 16
FILTER_EMB = 1 + 2 * FILTER_BANDS
FILTER_HIDDEN = 64
FILTER_INNER = 2
DECAY_TARGET = 1e-2
FAST_DECAY_PCT = 0.3
SLOW_DECAY_PCT = 1.5
LN_EPS = 1e-5

LANES = 128
SUBLANES = 8
VMEM_LIMIT_BYTES = 58 * 1024 * 1024

FFT_N1 = 16
FFT_N2 = 256
FFT_SLICES = FFT_N1 // 2 + 1
FFT_ROWS = FFT_N1 * FFT_N2
RSQRT2 = 1.0 / math.sqrt(2.0)


def _unit_root(k, n):
    snap = lambda v: float(round(v)) if abs(v - round(v)) < 1e-12 else v
    return snap(math.cos(2.0 * math.pi * k / n)), snap(-math.sin(2.0 * math.pi * k / n))


W16 = tuple(_unit_root(k, FFT_N1) for k in range(FFT_SLICES))


def _dot(a, b):
    return jnp.dot(a, b, preferred_element_type=F32)


def _split_bf16(a):
    hi = a.astype(BF16)
    return hi, (a - hi.astype(F32)).astype(BF16)


def _dot3(a, b):
    a_hi, a_lo = _split_bf16(a)
    b_hi, b_lo = _split_bf16(b)
    return (_dot(jnp.concatenate([a_hi, a_lo], axis=1), jnp.concatenate([b_hi, b_hi], axis=0))
            + _dot(a_hi, b_lo))


def _dot_nt(a, b):
    return lax.dot_general(a, b, (((1,), (1,)), ((), ())), preferred_element_type=F32)


CAST_ROWS = 256


def _cast_plan(arrays, steps):
    specs, shapes = [], []
    for a in arrays:
        rows, rem = divmod(a.shape[0], steps)
        assert rem == 0 and rows % 16 == 0, a.shape
        specs.append(pl.BlockSpec((rows, a.shape[1]), lambda i: (i, 0)))
        shapes.append(jax.ShapeDtypeStruct(a.shape, BF16))
    return specs, shapes


def _cast_blocks(src_refs, dst_refs):
    for src, dst in zip(src_refs, dst_refs):
        chunk = min(src.shape[0], CAST_ROWS)
        assert src.shape[0] % chunk == 0

        def body(i, carry, src=src, dst=dst, chunk=chunk):
            rows = pl.ds(pl.multiple_of(i * chunk, chunk), chunk)
            dst[rows, :] = src[rows, :].astype(dst.dtype)
            return carry
        lax.fori_loop(0, src.shape[0] // chunk, body, 0)


@functools.lru_cache(maxsize=None)
def _fft_constants(seq_len):
    n_fft = 2 * seq_len
    assert n_fft == FFT_N1 * FFT_N2
    k2 = np.arange(FFT_N2)
    fwd = []
    for k1 in range(FFT_SLICES):
        ang = -2.0 * np.pi * ((np.outer(FFT_N1 * k2 + k1, k2) % n_fft) / n_fft)
        er, ei = np.cos(ang), np.sin(ang)
        fwd.append(np.block([[er, -ei], [ei, er]]))
    fwd = np.stack(fwd, axis=0).astype(np.float32)
    inv = np.ascontiguousarray(np.transpose(fwd, (0, 2, 1)))
    return fwd, inv


@functools.lru_cache(maxsize=None)
def _filter_constants(seq_len, width):
    n_fft = 2 * seq_len
    t = np.linspace(0.0, 1.0, seq_len)
    bands = np.linspace(1e-4, FILTER_BANDS - 1, FILTER_BANDS)
    ang = (2.0 * np.pi / seq_len) * np.arange(seq_len)[:, None] * bands
    feats = np.concatenate([t[:, None], np.cos(ang), -np.sin(ang)], axis=-1)
    src = np.concatenate([np.arange(seq_len), [0], np.arange(seq_len - 1, 0, -1)])
    feats_ext = np.zeros((seq_len, 2 * LANES), np.float32)
    feats_ext[:, :FILTER_EMB] = feats[src[:seq_len]]
    feats_ext[:, LANES:LANES + FILTER_EMB] = feats[src[seq_len:]]
    t_mask = np.zeros((2, n_fft, LANES), np.float32)
    t_mask[0] = t[src][:, None]
    t_mask[1] = 1.0
    t_mask[1, seq_len] = 0.0
    deltas = np.abs(np.linspace(math.log(DECAY_TARGET) / SLOW_DECAY_PCT,
                                math.log(DECAY_TARGET) / FAST_DECAY_PCT, width))
    deltas = np.tile(deltas[None, :], (1, HYENA_ORDER)).astype(np.float32)
    return feats_ext, t_mask, deltas


@functools.lru_cache(maxsize=None)
def _attn_distance_tiles():
    n_off = Q_BLOCK // 2
    qi = np.arange(Q_BLOCK)[:, None]
    kj = np.arange(2 * Q_BLOCK)[None, :]
    tiles = []
    for shift in (0, n_off, 2 * n_off):
        off = np.abs(kj - shift - qi).astype(np.float32)
        tiles.append(np.where(off <= n_off, off, np.inf))
    off = np.abs(kj - qi).astype(np.float32)
    t3 = np.where(off <= n_off, off, np.inf)
    t3[:, Q_BLOCK:] = np.inf
    tiles.append(t3)
    return np.stack(tiles, axis=0).astype(np.float32)


def _alibi_slopes():
    n = N_GROUPS * HEADS_PER_GROUP
    return [2.0 ** (-ALIBI_MAX_EXP * j / n) for j in range(1, n + 1)]


def _radix8_half(z0, z1, z2, z3):
    s02, d02 = z0 + z2, z0 - z2
    s13, d13 = z1 + z3, z1 - z3
    ss, dd = s13 * RSQRT2, d13 * RSQRT2
    return {0: (s02 + s13, None), 4: (s02 - s13, None), 2: (d02, -d13),
            1: (z0 + dd, -z2 - ss), 3: (z0 - dd, z2 - ss)}


def _add(a, b, sign=1.0):
    if b is None:
        return a
    if a is None:
        return b if sign > 0 else -b
    return a + b if sign > 0 else a - b


def _cmul_const(re, im, wr, wi):
    def scaled(v, w):
        if v is None or w == 0.0:
            return None
        return v if w == 1.0 else (-v if w == -1.0 else v * w)
    if re is not None and im is not None and wr != 0.0 and abs(wr) == abs(wi):
        if wi == wr:
            return (re - im) * wr, (re + im) * wr
        return (re + im) * wr, (im - re) * wr
    return _add(scaled(re, wr), scaled(im, wi), -1.0), _add(scaled(re, wi), scaled(im, wr))


def _radix16_stage(blocks):
    even = _radix8_half(*blocks[0::2])
    odd = _radix8_half(*blocks[1::2])
    half = FFT_N1 // 2
    out = [None] * FFT_SLICES
    out[0] = (even[0][0] + odd[0][0], None)
    out[half] = (even[0][0] - odd[0][0], None)
    out[half // 2] = (even[4][0], -odd[4][0])
    for k in range(1, half // 2):
        (er, ei), (pr, pi) = even[k], _cmul_const(*odd[k], *W16[k])
        out[k] = (er + pr, ei + pi)
        out[half - k] = (er - pr, pi - ei)
    return out


def _assemble8(c0, c4, c1, c2, c3):
    (c1r, c1i), (c2r, c2i), (c3r, c3i) = c1, c2, c3
    e, o = c0 + c4, c0 - c4
    return [e + c1r + c2r + c3r,
            o + (c1r - c1i - c3r - c3i) * RSQRT2 - c2i,
            e - c1i - c2r + c3i,
            o + (c3r - c3i - c1r - c1i) * RSQRT2 + c2i]


def _inverse_radix16_stage(c):
    half = FFT_N1 // 2
    plus = [(c[k][0] + c[half - k][0], c[k][1] - c[half - k][1]) for k in range(1, half // 2)]
    minus = [_cmul_const(c[k][0] - c[half - k][0], c[k][1] + c[half - k][1], W16[k][0], -W16[k][1])
             for k in range(1, half // 2)]
    mid = c[half // 2]
    even = _assemble8(c[0][0] + c[half][0], mid[0], *plus)
    odd = _assemble8(c[0][0] - c[half][0], -mid[1], *minus)
    return [blk for pair_ in zip(even, odd) for blk in pair_]


def _slice_rows(k1):
    if k1 == 0:
        return 0, FFT_N2
    if k1 == FFT_N1 // 2:
        return FFT_N2, FFT_N2
    return 2 * FFT_N2 * k1, 2 * FFT_N2


def _store_slices(ref, off, rows, values, scale=None):
    for k1, (re, im) in enumerate(values):
        r0, n = _slice_rows(k1)
        parts = (re,) if n == FFT_N2 else (re, im)
        for j, part in enumerate(parts):
            if scale is not None:
                part = part * scale
            ref[pl.ds(r0 + j * FFT_N2 + off, rows), :] = part.astype(ref.dtype)


def _load_slices(ref, off, rows):
    out = []
    for k1 in range(FFT_SLICES):
        r0, n = _slice_rows(k1)
        re = ref[pl.ds(r0 + off, rows), :]
        out.append((re, ref[pl.ds(r0 + FFT_N2 + off, rows), :] if n > FFT_N2 else None))
    return out


def _forward_dft(b_ref, k1, mf_ref):
    r0, n = _slice_rows(k1)
    m = mf_ref[k1] if n > FFT_N2 else mf_ref[k1, :, 0:FFT_N2]
    return _dot(m, b_ref[r0:r0 + n, :])


def _inverse_dft(p, k1, mi_ref, c_ref):
    r0, n = _slice_rows(k1)
    m = mi_ref[k1] if n > FFT_N2 else mi_ref[k1, 0:FFT_N2, :]
    c_ref[r0:r0 + n, :] = _dot(m, p)


FILT_CW = 256
FILT_ROWS = 256
A_CHUNK = 16


def _filter_kernel(n_cast, feats_ref, tmask_ref, w0_ref, b0_ref, wi_ref, bi_ref, freq_ref, wf_ref, wb_ref,
                   delta_ref, mf_ref, *rest):
    cast_in, (h_ref, *cast_out), (hid_ref, ts_ref, b_ref) = (
        rest[:n_cast], rest[n_cast:2 * n_cast + 1], rest[2 * n_cast + 1:])
    _cast_blocks(cast_in, cast_out)
    seq_len = feats_ref.shape[0]
    n_fft = 2 * seq_len

    @pl.when(pl.program_id(0) == 0)
    def _():
        def body(i, carry):
            rows = pl.ds(pl.multiple_of(i * FILT_ROWS, FILT_ROWS), FILT_ROWS)
            freq = freq_ref[...]
            hid = jnp.sin(freq * (_dot3(feats_ref[rows, :], w0_ref[...]) + b0_ref[...]))
            for layer in range(FILTER_INNER):
                hid = jnp.sin(freq * (_dot3(hid, wi_ref[layer]) + bi_ref[layer]))
            hid_ref[rows, :] = hid
            return carry
        lax.fori_loop(0, seq_len // FILT_ROWS, body, 0)

    h_hi, h_lo = _split_bf16(hid_ref[...])
    h_cat = jnp.concatenate([h_hi, h_lo], axis=1)
    for half, w_ref in enumerate((wf_ref, wb_ref)):
        w_hi, w_lo = _split_bf16(w_ref[...])
        ts_ref[half * seq_len:(half + 1) * seq_len, :] = (
            _dot(h_cat, jnp.concatenate([w_hi, w_hi], axis=0)) + _dot(h_hi, w_lo))

    def window_body(i, ssq):
        rows = pl.ds(pl.multiple_of(i * FILT_ROWS, FILT_ROWS), FILT_ROWS)
        widen = lambda a: jnp.concatenate([a] * (FILT_CW // LANES), axis=1)
        val = (ts_ref[rows, :] * jnp.exp(-widen(tmask_ref[0, rows, :]) * delta_ref[...])
               * widen(tmask_ref[1, rows, :]))
        ts_ref[rows, :] = val
        return ssq + jnp.sum(val * val, axis=0, keepdims=True)
    ssq = lax.fori_loop(0, n_fft // FILT_ROWS, window_body, jnp.zeros((1, FILT_CW), F32))
    scale = lax.rsqrt(ssq)

    def a_body(i, carry):
        off = pl.multiple_of(i * A_CHUNK, A_CHUNK)
        blk = [ts_ref[pl.ds(t1 * FFT_N2 + off, A_CHUNK), :] for t1 in range(FFT_N1)]
        lo = _radix16_stage(blk[:FFT_N1 // 2])
        hi = _radix16_stage(blk[FFT_N1 // 2:])
        vals = [(_add(lo[k][0], hi[k][0], 1.0 if k % 2 == 0 else -1.0),
                 _add(lo[k][1], hi[k][1], 1.0 if k % 2 == 0 else -1.0)) for k in range(FFT_SLICES)]
        _store_slices(b_ref, off, A_CHUNK, vals, scale)
        return carry
    lax.fori_loop(0, FFT_N2 // A_CHUNK, a_body, 0)

    for k1 in range(FFT_SLICES):
        s = (1.0 if k1 in (0, FFT_N1 // 2) else 2.0) / n_fft
        h_ref[k1] = _forward_dft(b_ref, k1, mf_ref) * s


def _filter_spectrum(seq_len, width, filt_w0, filt_b0, filt_w_inner, filt_b_inner, filt_w_out, filt_freq,
                     casts=()):
    n_fft = 2 * seq_len
    hid = LANES
    feats_ext, t_mask, deltas = _filter_constants(seq_len, width)
    mf = jnp.asarray(_fft_constants(seq_len)[0]).astype(BF16)
    assert 2 * FILTER_HIDDEN == hid and FILTER_EMB <= LANES
    n_cols = HYENA_ORDER * width
    steps = n_cols // FILT_CW
    zeros = lambda r, c: jnp.zeros((r, c), F32)
    w0p = jnp.pad(filt_w0.astype(F32), ((0, LANES - FILTER_EMB), (0, 0)))
    w0 = jnp.block([[w0p, zeros(LANES, FILTER_HIDDEN)], [zeros(LANES, FILTER_HIDDEN), w0p]])
    b0 = jnp.tile(filt_b0.astype(F32), 2)[None, :]
    wi = jnp.stack([jnp.block([[w, zeros(FILTER_HIDDEN, FILTER_HIDDEN)], [zeros(FILTER_HIDDEN, FILTER_HIDDEN), w]])
                    for w in filt_w_inner.astype(F32)])
    bi = jnp.tile(filt_b_inner.astype(F32), (1, 2))[:, None, :]
    freq = jnp.tile(filt_freq.astype(F32), 2)[None, :]
    w_out = filt_w_out.astype(F32)
    wl = jnp.concatenate([jnp.pad(w_out[:, :n_cols], ((0, FILTER_HIDDEN), (0, 0))),
                          jnp.pad(w_out[:, n_cols:], ((FILTER_HIDDEN, 0), (0, 0)))], axis=1)
    full = lambda *shape: pl.BlockSpec(shape, lambda j: (0,) * len(shape))
    cast_specs, cast_shapes = _cast_plan(casts, steps)
    spectrum, *cast_out = pl.pallas_call(
        functools.partial(_filter_kernel, len(casts)),
        out_shape=[jax.ShapeDtypeStruct((FFT_SLICES, 2 * FFT_N2, n_cols), F32)] + cast_shapes,
        grid=(steps,),
        in_specs=[
            full(seq_len, 2 * LANES), full(2, n_fft, LANES), full(2 * LANES, hid), full(1, hid),
            full(FILTER_INNER, hid, hid), full(FILTER_INNER, 1, hid), full(1, hid),
            pl.BlockSpec((hid, FILT_CW), lambda j: (0, j)),
            pl.BlockSpec((hid, FILT_CW), lambda j: (0, steps + j)),
            pl.BlockSpec((1, FILT_CW), lambda j: (0, j)),
            full(FFT_SLICES, 2 * FFT_N2, 2 * FFT_N2),
        ] + cast_specs,
        out_specs=[pl.BlockSpec((FFT_SLICES, 2 * FFT_N2, FILT_CW), lambda j: (0, 0, j))] + cast_specs,
        scratch_shapes=[pltpu.VMEM((seq_len, hid), F32), pltpu.VMEM((n_fft, FILT_CW), F32),
                        pltpu.VMEM((FFT_ROWS, FILT_CW), BF16)],
        compiler_params=pltpu.CompilerParams(dimension_semantics=("arbitrary",),
                                             vmem_limit_bytes=VMEM_LIMIT_BYTES),
        name="hyena_filter",
    )(feats_ext, t_mask, w0, b0, wi, bi, freq, wl, wl, deltas, mf, *casts)
    return spectrum, cast_out


HY_CW = 256
CONV_ROWS = 64
PAD_ROWS = 8


def _hyena_kernel(x_ref, wv_ref, w1_ref, w2_ref, bv_ref, b1_ref, b2_ref, cwv_ref, cw1_ref, cw2_ref,
                  cbv_ref, cb1_ref, cb2_ref, skip_ref, h0_ref, h1_ref, mf_ref, mi_ref,
                  out_ref, u_ref, p_ref, b_ref, c_ref):
    seq_len = x_ref.shape[0]
    width = HY_CW
    n2 = FFT_N2
    xb = x_ref[...]

    n_slabs = width // LANES
    zeros = jnp.zeros((PAD_ROWS, LANES), F32)
    for slot in range(u_ref.shape[0]):
        for j in range(n_slabs):
            u_ref[slot, j, 0:PAD_ROWS, :] = zeros
            u_ref[slot, j, PAD_ROWS + seq_len:2 * PAD_ROWS + seq_len, :] = zeros
    parts = ((wv_ref, bv_ref, cwv_ref, cbv_ref), (w1_ref, b1_ref, cw1_ref, cb1_ref),
             (w2_ref, b2_ref, cw2_ref, cb2_ref))

    def project(p, slot):
        w_ref, bias_ref, _, _ = parts[p]
        half = seq_len // 2
        for r0 in (0, half):
            u = _dot(xb[r0:r0 + half], w_ref[...]) + bias_ref[...]
            for j in range(n_slabs):
                u_ref[slot, j, PAD_ROWS + r0:PAD_ROWS + r0 + half, :] = u[:, j * LANES:(j + 1) * LANES]

    def short_conv(p, slot):
        _, _, cw_ref, cb_ref = parts[p]

        def conv_body(i, carry):
            start = pl.multiple_of(i * CONV_ROWS, CONV_ROWS)
            cw = cw_ref[...]
            cb = cb_ref[...]
            for j in range(n_slabs):
                lanes = slice(j * LANES, (j + 1) * LANES)
                taps = [u_ref[slot, j, pl.ds(start + PAD_ROWS - 1 + k, CONV_ROWS, stride=1), :]
                        for k in range(3)]
                p_ref[p, pl.ds(start, CONV_ROWS), lanes] = (
                    cw[0:1, lanes] * taps[0] + cw[1:2, lanes] * taps[1] + cw[2:3, lanes] * taps[2]
                    + cb[:, lanes])
            return carry
        lax.fori_loop(0, seq_len // CONV_ROWS, conv_body, 0, unroll=True)

    def spectral_product(h_ref):
        x_next = _forward_dft(b_ref, 0, mf_ref)
        for k1 in range(FFT_SLICES):
            x = x_next
            if k1 + 1 < FFT_SLICES:
                x_next = _forward_dft(b_ref, k1 + 1, mf_ref)
            xr, xi = x[:n2], x[n2:]
            hr = h_ref[k1, 0:n2, :]
            hi = h_ref[k1, n2:2 * n2, :]
            prod = jnp.concatenate([(xr * hr - xi * hi).astype(BF16), (xr * hi + xi * hr).astype(BF16)], axis=0)
            _inverse_dft(prod, k1, mi_ref, c_ref)

    def gated_blocks(order, off):
        skip = skip_ref[order:order + 1, :]
        ys = _inverse_radix16_stage(_load_slices(c_ref, off, A_CHUNK))
        out = []
        for t1, y in enumerate(ys):
            rows = pl.ds(t1 * n2 + off, A_CHUNK)
            out.append(p_ref[order + 1, rows, :] * (y + skip * p_ref[0, rows, :]))
        return out

    project(0, 0)
    short_conv(0, 0)

    def a_body(i, carry):
        off = pl.multiple_of(i * A_CHUNK, A_CHUNK)
        blk = [p_ref[0, pl.ds(t1 * n2 + off, A_CHUNK), :] for t1 in range(FFT_N1 // 2)]
        _store_slices(b_ref, off, A_CHUNK, _radix16_stage(blk))
        return carry
    lax.fori_loop(0, n2 // A_CHUNK, a_body, 0, unroll=True)

    project(1, 0)
    project(2, 1)
    spectral_product(h0_ref)
    short_conv(1, 0)
    short_conv(2, 1)

    def mid_body(i, carry):
        off = pl.multiple_of(i * A_CHUNK, A_CHUNK)
        z = gated_blocks(0, off)
        for t1, blk in enumerate(z):
            p_ref[0, pl.ds(t1 * n2 + off, A_CHUNK), :] = blk
        _store_slices(b_ref, off, A_CHUNK, _radix16_stage(z))
        return carry
    lax.fori_loop(0, n2 // A_CHUNK, mid_body, 0)

    spectral_product(h1_ref)

    def out_body(i, carry):
        off = pl.multiple_of(i * A_CHUNK, A_CHUNK)
        for t1, blk in enumerate(gated_blocks(1, off)):
            out_ref[pl.ds(t1 * n2 + off, A_CHUNK), :] = blk.astype(out_ref.dtype)
        return carry
    lax.fori_loop(0, n2 // A_CHUNK, out_body, 0)


def _hyena_branch(xb, w_in, b_in, col0, conv_w, conv_b, hyena_skip, spectrum):
    bsz, seq_len, d_model = xb.shape
    width = hyena_skip.shape[-1]
    nblk = width // HY_CW
    mf, mi = (jnp.asarray(m).astype(BF16) for m in _fft_constants(seq_len))
    col = lambda part: (lambda c, b: (0, part * nblk + c))
    blk0, rem = divmod(col0, HY_CW)
    assert rem == 0
    in_col = lambda part: (lambda c, b: (0, blk0 + part * nblk + c))
    const = lambda *shape: pl.BlockSpec(shape, lambda c, b: (0,) * len(shape), pipeline_mode=pl.Buffered(1))
    spec_h = lambda order: pl.BlockSpec((FFT_SLICES, 2 * FFT_N2, HY_CW), lambda c, b: (0, 0, order * nblk + c),
                                        pipeline_mode=pl.Buffered(1))
    in_specs = [pl.BlockSpec((None, seq_len, d_model), lambda c, b: (b, 0, 0))]
    in_specs += [pl.BlockSpec((d_model, HY_CW), in_col(p)) for p in range(3)]
    in_specs += [pl.BlockSpec((1, HY_CW), in_col(p)) for p in range(3)]
    in_specs += [pl.BlockSpec((3, HY_CW), col(p)) for p in range(3)]
    in_specs += [pl.BlockSpec((1, HY_CW), col(p)) for p in range(3)]
    in_specs += [pl.BlockSpec((HYENA_ORDER, HY_CW), lambda c, b: (0, c)), spec_h(0), spec_h(1),
                 const(FFT_SLICES, 2 * FFT_N2, 2 * FFT_N2), const(FFT_SLICES, 2 * FFT_N2, 2 * FFT_N2)]
    return pl.pallas_call(
        _hyena_kernel,
        out_shape=jax.ShapeDtypeStruct((bsz, seq_len, width), BF16),
        grid=(nblk, bsz),
        in_specs=in_specs,
        out_specs=pl.BlockSpec((None, seq_len, HY_CW), lambda c, b: (b, 0, c)),
        scratch_shapes=[pltpu.VMEM((2, HY_CW // LANES, seq_len + 2 * PAD_ROWS, LANES), F32),
                        pltpu.VMEM((3, seq_len, HY_CW), F32),
                        pltpu.VMEM((FFT_ROWS, HY_CW), BF16),
                        pltpu.VMEM((FFT_ROWS, HY_CW), F32)],
        compiler_params=pltpu.CompilerParams(dimension_semantics=("arbitrary", "arbitrary"),
                                             vmem_limit_bytes=VMEM_LIMIT_BYTES),
        name="hyena_branch",
    )(xb, w_in, w_in, w_in, b_in, b_in, b_in, conv_w, conv_w, conv_w, conv_b, conv_b, conv_b,
      hyena_skip, spectrum, spectrum, mf, mi)


def _stream_pitch(dil):
    return dil + 1 if dil % SUBLANES == 0 else dil


def _attn_kernel(x_ref, w_ref, b_ref, dist_ref, out_ref, qkv_ref, acc_ref, m_ref, l_ref,
                 s_ref, p_ref, al_ref):
    seq_len = x_ref.shape[0]
    xb = x_ref[...]
    slopes = _alibi_slopes()
    q_scale = 1.0 / math.sqrt(HEAD_DIM)
    heads = range(HEADS_PER_GROUP)

    def stage_scores(blk, coefs):
        qr, kr, _, dist, kc = blk
        for h in heads:
            s_ref[h, :, 0:kc] = _dot_nt(qkv_ref[0, h, qr, :].astype(BF16),
                                        qkv_ref[1, h, kr, :].astype(BF16)) + dist * coefs[h]

    def stage_softmax(blk, first, last):
        _, _, ar, _, kc = blk
        s = [s_ref[h, :, 0:kc] for h in heads]
        m_blk = [jnp.max(s[h], axis=-1, keepdims=True) for h in heads]
        if first:
            m_new = [jnp.broadcast_to(m_blk[h], (Q_BLOCK, HEAD_DIM)) for h in heads]
        else:
            m_old = [m_ref[h, ar, :] for h in heads]
            m_new = [jnp.maximum(m_old[h], m_blk[h]) for h in heads]
        m_wide = [jnp.concatenate([m_new[h]] * (kc // HEAD_DIM), axis=1) for h in heads]
        for h in heads:
            p_ref[h, :, 0:kc] = jnp.exp(s[h] - m_wide[h]).astype(BF16)
            if not first:
                al_ref[h] = jnp.exp(m_old[h] - m_new[h])
            if not last:
                m_ref[h, ar, :] = m_new[h]

    def stage_output(blk, first, last):
        _, kr, ar, _, kc = blk
        ones = jnp.ones((kc, HEAD_DIM), BF16)
        pv = [_dot(p_ref[h, :, 0:kc], jnp.concatenate([qkv_ref[2, h, kr, :].astype(BF16), ones], axis=1))
              for h in heads]
        acc = [r[:, :HEAD_DIM] for r in pv]
        l_new = [r[:, HEAD_DIM:] for r in pv]
        if not first:
            alpha = [al_ref[h] for h in heads]
            l_new = [alpha[h] * l_ref[h, ar, :] + l_new[h] for h in heads]
            acc = [alpha[h] * acc_ref[h, ar, :] + acc[h] for h in heads]
        for h in heads:
            if last:
                out_ref[ar, h * HEAD_DIM:(h + 1) * HEAD_DIM] = (acc[h] / l_new[h]).astype(out_ref.dtype)
            else:
                l_ref[h, ar, :] = l_new[h]
                acc_ref[h, ar, :] = acc[h]

    order = sorted(range(N_GROUPS), key=lambda g: -ATTN_GROUPS[g][1])
    assert ATTN_GROUPS[order[-1]][1] == 1
    for pos, g in enumerate(order):
        window, dil = ATTN_GROUPS[g]
        first, last = pos == 0, pos == N_GROUPS - 1
        n_off = (window // 2) // dil
        assert n_off == Q_BLOCK // 2
        stream = seq_len // dil
        nb = stream // Q_BLOCK
        n_blocks = dil * nb
        pitch = _stream_pitch(dil)
        for part in range(3):
            c0 = part * ATTN_QKV_WIDTH + g * GROUP_WIDTH
            proj = _dot(xb, w_ref[:, c0:c0 + GROUP_WIDTH]) + b_ref[:, c0:c0 + GROUP_WIDTH]
            if part == 0:
                proj = proj * q_scale
            for h in heads:
                slab = proj[:, h * HEAD_DIM:(h + 1) * HEAD_DIM]
                if pitch == dil:
                    qkv_ref[part, h, 0:seq_len, :] = slab
                else:
                    for i in range(stream):
                        qkv_ref[part, h, pl.ds(pitch * i, dil, stride=1), :] = slab[dil * i:dil * (i + 1)]
        coefs = [-slopes[g * HEADS_PER_GROUP + h] * dil for h in heads]

        def block(i, dil=dil, nb=nb, stream=stream, pitch=pitch):
            def rows(row_pitch, c, first_pos, count):
                if row_pitch == 1:
                    return pl.ds(pl.multiple_of(first_pos, n_off), count)
                return pl.ds(c + row_pitch * first_pos, count, stride=row_pitch)
            if nb == 1:
                q = rows(pitch, i, 0, Q_BLOCK)
                return q, q, rows(dil, i, 0, Q_BLOCK), dist_ref[3, :, 0:Q_BLOCK], Q_BLOCK
            c = i // nb
            n = i % nb
            lo = jnp.clip(n * Q_BLOCK - n_off, 0, stream - 2 * Q_BLOCK)
            dist = dist_ref[jnp.where(n == 0, 0, jnp.where(n == nb - 1, 2, 1))]
            return (rows(pitch, c, n * Q_BLOCK, Q_BLOCK), rows(pitch, c, lo, 2 * Q_BLOCK),
                    rows(dil, c, n * Q_BLOCK, Q_BLOCK), dist, 2 * Q_BLOCK)

        def steady(i, carry, block=block, coefs=coefs, first=first, last=last):
            stage_output(block(i - 2), first, last)
            stage_softmax(block(i - 1), first, last)
            stage_scores(block(i), coefs)
            return carry

        assert n_blocks >= 3
        stage_scores(block(0), coefs)
        stage_softmax(block(0), first, last)
        stage_scores(block(1), coefs)
        lax.fori_loop(2, n_blocks, steady, 0)
        stage_output(block(n_blocks - 2), first, last)
        stage_softmax(block(n_blocks - 1), first, last)
        stage_output(block(n_blocks - 1), first, last)


def _attention_branch(xb, w_qkv, b_qkv):
    bsz, seq_len, d_model = xb.shape
    dist = _attn_distance_tiles()
    qkv_w = 3 * ATTN_QKV_WIDTH
    qkv_rows = max((seq_len // dil) * _stream_pitch(dil) for _, dil in ATTN_GROUPS)
    return pl.pallas_call(
        _attn_kernel,
        out_shape=jax.ShapeDtypeStruct((bsz, seq_len, GROUP_WIDTH), BF16),
        grid=(bsz,),
        in_specs=[pl.BlockSpec((None, seq_len, d_model), lambda b: (b, 0, 0)),
                  pl.BlockSpec((d_model, qkv_w), lambda b: (0, 0), pipeline_mode=pl.Buffered(1)),
                  pl.BlockSpec((1, qkv_w), lambda b: (0, 0)),
                  pl.BlockSpec(dist.shape, lambda b: (0, 0, 0))],
        out_specs=pl.BlockSpec((None, seq_len, GROUP_WIDTH), lambda b: (b, 0, 0)),
        scratch_shapes=[pltpu.VMEM((3, HEADS_PER_GROUP, qkv_rows, HEAD_DIM), F32),
                        pltpu.VMEM((HEADS_PER_GROUP, seq_len, HEAD_DIM), F32),
                        pltpu.VMEM((HEADS_PER_GROUP, seq_len, HEAD_DIM), F32),
                        pltpu.VMEM((HEADS_PER_GROUP, seq_len, HEAD_DIM), F32),
                        pltpu.VMEM((HEADS_PER_GROUP, Q_BLOCK, 2 * Q_BLOCK), F32),
                        pltpu.VMEM((HEADS_PER_GROUP, Q_BLOCK, 2 * Q_BLOCK), BF16),
                        pltpu.VMEM((HEADS_PER_GROUP, Q_BLOCK, HEAD_DIM), F32)],
        compiler_params=pltpu.CompilerParams(dimension_semantics=("arbitrary",),
                                             vmem_limit_bytes=VMEM_LIMIT_BYTES),
        name="dilated_attention",
    )(xb, w_qkv, b_qkv, dist)


ROW_TILE = 1024
SUB_ROWS = 512


def _layer_norm(r, g, b):
    mu = jnp.mean(r, axis=-1, keepdims=True)
    d = r - mu
    var = jnp.mean(d * d, axis=-1, keepdims=True)
    return d * lax.rsqrt(var + LN_EPS) * g + b


def _sub_tiles(ref):
    return [slice(s * SUB_ROWS, (s + 1) * SUB_ROWS) for s in range(ref.shape[0] // SUB_ROWS)]


def _merge_kernel(alpha, n_gate, x_ref, ya_ref, yh_ref, *rest):
    wg_refs, (bg_ref, wa_ref, wh_ref, wo_ref, g_ref, b_ref, out_ref) = rest[:n_gate], rest[n_gate:]
    d_model = x_ref.shape[-1]
    gw = wg_refs[0].shape[1]
    tiles = _sub_tiles(x_ref)
    x = [x_ref[r, :] for r in tiles]
    xb = [xs.astype(BF16) for xs in x]
    gates = [jnp.concatenate([jax.nn.sigmoid(_dot(xs, w[...]) + bg_ref[:, j * gw:(j + 1) * gw])
                              for j, w in enumerate(wg_refs)], axis=1) for xs in xb]
    merged = [gt[:, :d_model] * _dot(ya_ref[r, :], wa_ref[...])
              + gt[:, d_model:] * _dot(yh_ref[r, :], wh_ref[...]) for gt, r in zip(gates, tiles)]
    mix = [_dot(mg.astype(BF16), wo_ref[...]) for mg in merged]
    for r, xs, mx in zip(tiles, x, mix):
        out_ref[r, :] = _layer_norm(alpha * xs + mx, g_ref[...], b_ref[...])


def _ffn_kernel(alpha, h_ref, w1_ref, b1_ref, w2_ref, b2_ref, g_ref, b_ref, out_ref):
    tiles = _sub_tiles(h_ref)
    h = [h_ref[r, :] for r in tiles]
    hid = [jnp.maximum(_dot(hs.astype(BF16), w1_ref[...]) + b1_ref[...], 0.0) for hs in h]
    ff = [_dot((hd * hd).astype(BF16), w2_ref[...]) + b2_ref[...] for hd in hid]
    for r, hs, fs in zip(tiles, h, ff):
        out_ref[r, :] = _layer_norm(alpha * hs + fs, g_ref[...], b_ref[...])


def _row_tiled_call(kernel_fn, name, rows, d_model, tiled, resident, casts=()):
    steps = rows // ROW_TILE
    n_main = len(tiled) + len(resident)
    cast_specs, cast_shapes = _cast_plan(casts, steps)

    def resident_spec(item):
        if isinstance(item, tuple):
            a, width, j = item
            return pl.BlockSpec((a.shape[0], width), lambda i: (0, j), pipeline_mode=pl.Buffered(1))
        return pl.BlockSpec(item.shape, lambda i: (0, 0), pipeline_mode=pl.Buffered(1))

    def body(*refs):
        cast_in, (out_ref, *cast_out) = refs[n_main:n_main + len(casts)], refs[n_main + len(casts):]
        _cast_blocks(cast_in, cast_out)
        kernel_fn(*refs[:n_main], out_ref)

    in_specs = [pl.BlockSpec((ROW_TILE, a.shape[1]), lambda i: (i, 0)) for a in tiled]
    in_specs += [resident_spec(item) for item in resident]
    resident = [item[0] if isinstance(item, tuple) else item for item in resident]
    out, *cast_out = pl.pallas_call(
        body,
        out_shape=[jax.ShapeDtypeStruct((rows, d_model), F32)] + cast_shapes,
        grid=(steps,),
        in_specs=in_specs + cast_specs,
        out_specs=[pl.BlockSpec((ROW_TILE, d_model), lambda i: (i, 0))] + cast_specs,
        compiler_params=pltpu.CompilerParams(dimension_semantics=("arbitrary",),
                                             vmem_limit_bytes=VMEM_LIMIT_BYTES),
        name=name,
    )(*tiled, *resident, *casts)
    return out, cast_out


def kernel(x, w_in, b_in, conv_w, conv_b, filt_w0, filt_b0, filt_w_inner, filt_b_inner, filt_w_out,
           filt_freq, hyena_skip, w_branch_attn, w_branch_hyena, w_out, ln1_g, ln1_b, w_ff1, b_ff1,
           w_ff2, b_ff2, ln2_g, ln2_b):
    bsz, seq_len, d_model = x.shape
    depth = w_in.shape[0]
    alpha = (2 * depth) ** 0.25
    rows = bsz * seq_len
    qkv_w = 3 * ATTN_QKV_WIDTH
    hy_w = (HYENA_ORDER + 1) * hyena_skip.shape[-1]
    row = lambda a: a.astype(F32)[None, :]
    h = x
    for layer in range(depth):
        w_l = w_in[layer].astype(BF16)
        b_l = b_in[layer].astype(F32)[None, :]
        h_rows = h.reshape(rows, d_model).astype(F32)
        spectrum, (hb, wa_l, wh_l, wo_l) = _filter_spectrum(
            seq_len, hyena_skip.shape[-1], filt_w0[layer], filt_b0[layer], filt_w_inner[layer],
            filt_b_inner[layer], filt_w_out[layer], filt_freq[layer],
            casts=[h_rows, w_branch_attn[layer].astype(F32), w_branch_hyena[layer].astype(F32),
                   w_out[layer].astype(F32)])
        hb = hb.reshape(bsz, seq_len, d_model)
        y_attn = _attention_branch(hb, w_l, b_l)
        y_hyena = _hyena_branch(hb, w_l, b_l, qkv_w,
                                conv_w[layer].astype(F32), conv_b[layer].astype(F32)[None, :],
                                hyena_skip[layer].astype(F32), spectrum)
        gate0 = qkv_w + hy_w
        gate_w = math.gcd(gate0, 2 * d_model)
        gate_blocks = [(w_l, gate_w, gate0 // gate_w + j) for j in range(2 * d_model // gate_w)]
        h1, (w1_l, w2_l) = _row_tiled_call(
            functools.partial(_merge_kernel, alpha, len(gate_blocks)), "merge_ln", rows, d_model,
            [h_rows, y_attn.reshape(rows, -1), y_hyena.reshape(rows, -1)],
            gate_blocks + [b_l[:, gate0:], wa_l, wh_l, wo_l, row(ln1_g[layer]), row(ln1_b[layer])],
            casts=[w_ff1[layer].astype(F32), w_ff2[layer].astype(F32)])
        h2, _ = _row_tiled_call(
            functools.partial(_ffn_kernel, alpha), "ffn_ln", rows, d_model, [h1],
            [w1_l, row(b_ff1[layer]), w2_l, row(b_ff2[layer]), row(ln2_g[layer]), row(ln2_b[layer])])
        h = h2.reshape(bsz, seq_len, d_model)
    return h
```

```python
import functools
import math

import jax
import jax.numpy as jnp
import numpy as np
from jax import lax
from jax.experimental import pallas as pl
from jax.experimental.pallas import tpu as pltpu

F32 = jnp.float32
BF16 = jnp.bfloat16

ATTN_GROUPS = ((128, 1), (512, 4), (2048, 16))
N_GROUPS = len(ATTN_GROUPS)
HEADS_PER_GROUP = 4
HEAD_DIM = 128
GROUP_WIDTH = HEADS_PER_GROUP * HEAD_DIM
ATTN_QKV_WIDTH = N_GROUPS * GROUP_WIDTH
Q_BLOCK = 128
ALIBI_MAX_EXP = 8.0
HYENA_ORDER = 2
FILTER_BANDS = 16
FILTER_EMB = 1 + 2 * FILTER_BANDS
FILTER_HIDDEN = 64
FILTER_INNER = 2
DECAY_TARGET = 1e-2
FAST_DECAY_PCT = 0.3
SLOW_DECAY_PCT = 1.5
LN_EPS = 1e-5

LANES = 128
SUBLANES = 8
VMEM_LIMIT_BYTES = 58 * 1024 * 1024

FFT_N1 = 16
FFT_N2 = 256
FFT_SLICES = FFT_N1 // 2 + 1
FFT_ROWS = FFT_N1 * FFT_N2
RSQRT2 = 1.0 / math.sqrt(2.0)


def _unit_root(k, n):
    snap = lambda v: float(round(v)) if abs(v - round(v)) < 1e-12 else v
    return snap(math.cos(2.0 * math.pi * k / n)), snap(-math.sin(2.0 * math.pi * k / n))


W16 = tuple(_unit_root(k, FFT_N1) for k in range(FFT_SLICES))


def _dot(a, b):
    return jnp.dot(a, b, preferred_element_type=F32)


def _split_bf16(a):
    hi = a.astype(BF16)
    return hi, (a - hi.astype(F32)).astype(BF16)


def _dot3(a, b):
    a_hi, a_lo = _split_bf16(a)
    b_hi, b_lo = _split_bf16(b)
    return (_dot(jnp.concatenate([a_hi, a_lo], axis=1), jnp.concatenate([b_hi, b_hi], axis=0))
            + _dot(a_hi, b_lo))


def _dot_nt(a, b):
    return lax.dot_general(a, b, (((1,), (1,)), ((), ())), preferred_element_type=F32)


CAST_ROWS = 256


def _cast_plan(arrays, steps):
    specs, shapes = [], []
    for a in arrays:
        rows, rem = divmod(a.shape[0], steps)
        assert rem == 0 and rows % 16 == 0, a.shape
        specs.append(pl.BlockSpec((rows, a.shape[1]), lambda i: (i, 0)))
        shapes.append(jax.ShapeDtypeStruct(a.shape, BF16))
    return specs, shapes


def _cast_blocks(src_refs, dst_refs):
    for src, dst in zip(src_refs, dst_refs):
        chunk = min(src.shape[0], CAST_ROWS)
        assert src.shape[0] % chunk == 0

        def body(i, carry, src=src, dst=dst, chunk=chunk):
            rows = pl.ds(pl.multiple_of(i * chunk, chunk), chunk)
            dst[rows, :] = src[rows, :].astype(dst.dtype)
            return carry
        lax.fori_loop(0, src.shape[0] // chunk, body, 0)


@functools.lru_cache(maxsize=None)
def _fft_constants(seq_len):
    n_fft = 2 * seq_len
    assert n_fft == FFT_N1 * FFT_N2
    k2 = np.arange(FFT_N2)
    fwd = []
    for k1 in range(FFT_SLICES):
        ang = -2.0 * np.pi * ((np.outer(FFT_N1 * k2 + k1, k2) % n_fft) / n_fft)
        er, ei = np.cos(ang), np.sin(ang)
        fwd.append(np.block([[er, -ei], [ei, er]]))
    fwd = np.stack(fwd, axis=0).astype(np.float32)
    inv = np.ascontiguousarray(np.transpose(fwd, (0, 2, 1)))
    return fwd, inv


@functools.lru_cache(maxsize=None)
def _filter_constants(seq_len, width):
    n_fft = 2 * seq_len
    t = np.linspace(0.0, 1.0, seq_len)
    bands = np.linspace(1e-4, FILTER_BANDS - 1, FILTER_BANDS)
    ang = (2.0 * np.pi / seq_len) * np.arange(seq_len)[:, None] * bands
    feats = np.concatenate([t[:, None], np.cos(ang), -np.sin(ang)], axis=-1)
    src = np.concatenate([np.arange(seq_len), [0], np.arange(seq_len - 1, 0, -1)])
    feats_ext = np.zeros((seq_len, 2 * LANES), np.float32)
    feats_ext[:, :FILTER_EMB] = feats[src[:seq_len]]
    feats_ext[:, LANES:LANES + FILTER_EMB] = feats[src[seq_len:]]
    t_mask = np.zeros((2, n_fft, LANES), np.float32)
    t_mask[0] = t[src][:, None]
    t_mask[1] = 1.0
    t_mask[1, seq_len] = 0.0
    deltas = np.abs(np.linspace(math.log(DECAY_TARGET) / SLOW_DECAY_PCT,
                                math.log(DECAY_TARGET) / FAST_DECAY_PCT, width))
    deltas = np.tile(deltas[None, :], (1, HYENA_ORDER)).astype(np.float32)
    return feats_ext, t_mask, deltas


@functools.lru_cache(maxsize=None)
def _attn_distance_tiles():
    n_off = Q_BLOCK // 2
    qi = np.arange(Q_BLOCK)[:, None]
    kj = np.arange(2 * Q_BLOCK)[None, :]
    tiles = []
    for shift in (0, n_off, 2 * n_off):
        off = np.abs(kj - shift - qi).astype(np.float32)
        tiles.append(np.where(off <= n_off, off, np.inf))
    off = np.abs(kj - qi).astype(np.float32)
    t3 = np.where(off <= n_off, off, np.inf)
    t3[:, Q_BLOCK:] = np.inf
    tiles.append(t3)
    return np.stack(tiles, axis=0).astype(np.float32)


def _alibi_slopes():
    n = N_GROUPS * HEADS_PER_GROUP
    return [2.0 ** (-ALIBI_MAX_EXP * j / n) for j in range(1, n + 1)]


def _radix8_half(z0, z1, z2, z3):
    s02, d02 = z0 + z2, z0 - z2
    s13, d13 = z1 + z3, z1 - z3
    ss, dd = s13 * RSQRT2, d13 * RSQRT2
    return {0: (s02 + s13, None), 4: (s02 - s13, None), 2: (d02, -d13),
            1: (z0 + dd, -z2 - ss), 3: (z0 - dd, z2 - ss)}


def _add(a, b, sign=1.0):
    if b is None:
        return a
    if a is None:
        return b if sign > 0 else -b
    return a + b if sign > 0 else a - b


def _cmul_const(re, im, wr, wi):
    def scaled(v, w):
        if v is None or w == 0.0:
            return None
        return v if w == 1.0 else (-v if w == -1.0 else v * w)
    if re is not None and im is not None and wr != 0.0 and abs(wr) == abs(wi):
        if wi == wr:
            return (re - im) * wr, (re + im) * wr
        return (re + im) * wr, (im - re) * wr
    return _add(scaled(re, wr), scaled(im, wi), -1.0), _add(scaled(re, wi), scaled(im, wr))


def _radix16_stage(blocks):
    even = _radix8_half(*blocks[0::2])
    odd = _radix8_half(*blocks[1::2])
    half = FFT_N1 // 2
    out = [None] * FFT_SLICES
    out[0] = (even[0][0] + odd[0][0], None)
    out[half] = (even[0][0] - odd[0][0], None)
    out[half // 2] = (even[4][0], -odd[4][0])
    for k in range(1, half // 2):
        (er, ei), (pr, pi) = even[k], _cmul_const(*odd[k], *W16[k])
        out[k] = (er + pr, ei + pi)
        out[half - k] = (er - pr, pi - ei)
    return out


def _assemble8(c0, c4, c1, c2, c3):
    (c1r, c1i), (c2r, c2i), (c3r, c3i) = c1, c2, c3
    e, o = c0 + c4, c0 - c4
    return [e + c1r + c2r + c3r,
            o + (c1r - c1i - c3r - c3i) * RSQRT2 - c2i,
            e - c1i - c2r + c3i,
            o + (c3r - c3i - c1r - c1i) * RSQRT2 + c2i]


def _inverse_radix16_stage(c):
    half = FFT_N1 // 2
    plus = [(c[k][0] + c[half - k][0], c[k][1] - c[half - k][1]) for k in range(1, half // 2)]
    minus = [_cmul_const(c[k][0] - c[half - k][0], c[k][1] + c[half - k][1], W16[k][0], -W16[k][1])
             for k in range(1, half // 2)]
    mid = c[half // 2]
    even = _assemble8(c[0][0] + c[half][0], mid[0], *plus)
    odd = _assemble8(c[0][0] - c[half][0], -mid[1], *minus)
    return [blk for pair_ in zip(even, odd) for blk in pair_]


def _slice_rows(k1):
    if k1 == 0:
        return 0, FFT_N2
    if k1 == FFT_N1 // 2:
        return FFT_N2, FFT_N2
    return 2 * FFT_N2 * k1, 2 * FFT_N2


def _store_slices(ref, off, rows, values, scale=None):
    for k1, (re, im) in enumerate(values):
        r0, n = _slice_rows(k1)
        parts = (re,) if n == FFT_N2 else (re, im)
        for j, part in enumerate(parts):
            if scale is not None:
                part = part * scale
            ref[pl.ds(r0 + j * FFT_N2 + off, rows), :] = part.astype(ref.dtype)


def _load_slices(ref, off, rows):
    out = []
    for k1 in range(FFT_SLICES):
        r0, n = _slice_rows(k1)
        re = ref[pl.ds(r0 + off, rows), :]
        out.append((re, ref[pl.ds(r0 + FFT_N2 + off, rows), :] if n > FFT_N2 else None))
    return out


def _forward_dft(b_ref, k1, mf_ref):
    r0, n = _slice_rows(k1)
    b = b_ref[r0:r0 + n, :]
    return jnp.concatenate([_dot(mf_ref[k1, h0:h0 + FFT_N2, 0:n], b) for h0 in (0, FFT_N2)], axis=0)


def _inverse_dft(p, k1, mi_ref, c_ref):
    r0, n = _slice_rows(k1)
    for h0 in range(0, n, FFT_N2):
        c_ref[r0 + h0:r0 + h0 + FFT_N2, :] = _dot(mi_ref[k1, h0:h0 + FFT_N2, :], p)


FILT_CW = 256
FILT_ROWS = 256
A_CHUNK = 16


def _filter_kernel(n_cast, feats_ref, tmask_ref, w0_ref, b0_ref, wi_ref, bi_ref, freq_ref, wf_ref, wb_ref,
                   delta_ref, mf_ref, *rest):
    cast_in, (h_ref, *cast_out), (hid_ref, ts_ref, b_ref) = (
        rest[:n_cast], rest[n_cast:2 * n_cast + 1], rest[2 * n_cast + 1:])
    _cast_blocks(cast_in, cast_out)
    seq_len = feats_ref.shape[0]
    n_fft = 2 * seq_len

    @pl.when(pl.program_id(0) == 0)
    def _():
        def body(i, carry):
            rows = pl.ds(pl.multiple_of(i * FILT_ROWS, FILT_ROWS), FILT_ROWS)
            freq = freq_ref[...]
            hid = jnp.sin(freq * (_dot3(feats_ref[rows, :], w0_ref[...]) + b0_ref[...]))
            for layer in range(FILTER_INNER):
                hid = jnp.sin(freq * (_dot3(hid, wi_ref[layer]) + bi_ref[layer]))
            hid_ref[rows, :] = hid
            return carry
        lax.fori_loop(0, seq_len // FILT_ROWS, body, 0)

    h_hi, h_lo = _split_bf16(hid_ref[...])
    h_cat = jnp.concatenate([h_hi, h_lo], axis=1)
    for half, w_ref in enumerate((wf_ref, wb_ref)):
        w_hi, w_lo = _split_bf16(w_ref[...])
        ts_ref[half * seq_len:(half + 1) * seq_len, :] = (
            _dot(h_cat, jnp.concatenate([w_hi, w_hi], axis=0)) + _dot(h_hi, w_lo))

    def window_body(i, ssq):
        rows = pl.ds(pl.multiple_of(i * FILT_ROWS, FILT_ROWS), FILT_ROWS)
        widen = lambda a: jnp.concatenate([a] * (FILT_CW // LANES), axis=1)
        val = (ts_ref[rows, :] * jnp.exp(-widen(tmask_ref[0, rows, :]) * delta_ref[...])
               * widen(tmask_ref[1, rows, :]))
        ts_ref[rows, :] = val
        return ssq + jnp.sum(val * val, axis=0, keepdims=True)
    ssq = lax.fori_loop(0, n_fft // FILT_ROWS, window_body, jnp.zeros((1, FILT_CW), F32))
    scale = lax.rsqrt(ssq)

    def a_body(i, carry):
        off = pl.multiple_of(i * A_CHUNK, A_CHUNK)
        blk = [ts_ref[pl.ds(t1 * FFT_N2 + off, A_CHUNK), :] for t1 in range(FFT_N1)]
        lo = _radix16_stage(blk[:FFT_N1 // 2])
        hi = _radix16_stage(blk[FFT_N1 // 2:])
        vals = [(_add(lo[k][0], hi[k][0], 1.0 if k % 2 == 0 else -1.0),
                 _add(lo[k][1], hi[k][1], 1.0 if k % 2 == 0 else -1.0)) for k in range(FFT_SLICES)]
        _store_slices(b_ref, off, A_CHUNK, vals, scale)
        return carry
    lax.fori_loop(0, FFT_N2 // A_CHUNK, a_body, 0)

    for k1 in range(FFT_SLICES):
        s = (1.0 if k1 in (0, FFT_N1 // 2) else 2.0) / n_fft
        h_ref[k1] = _forward_dft(b_ref, k1, mf_ref) * s


def _filter_spectrum(seq_len, width, filt_w0, filt_b0, filt_w_inner, filt_b_inner, filt_w_out, filt_freq,
                     casts=()):
    n_fft = 2 * seq_len
    hid = LANES
    feats_ext, t_mask, deltas = _filter_constants(seq_len, width)
    mf = jnp.asarray(_fft_constants(seq_len)[0]).astype(BF16)
    assert 2 * FILTER_HIDDEN == hid and FILTER_EMB <= LANES
    n_cols = HYENA_ORDER * width
    steps = n_cols // FILT_CW
    zeros = lambda r, c: jnp.zeros((r, c), F32)
    w0p = jnp.pad(filt_w0.astype(F32), ((0, LANES - FILTER_EMB), (0, 0)))
    w0 = jnp.block([[w0p, zeros(LANES, FILTER_HIDDEN)], [zeros(LANES, FILTER_HIDDEN), w0p]])
    b0 = jnp.tile(filt_b0.astype(F32), 2)[None, :]
    wi = jnp.stack([jnp.block([[w, zeros(FILTER_HIDDEN, FILTER_HIDDEN)], [zeros(FILTER_HIDDEN, FILTER_HIDDEN), w]])
                    for w in filt_w_inner.astype(F32)])
    bi = jnp.tile(filt_b_inner.astype(F32), (1, 2))[:, None, :]
    freq = jnp.tile(filt_freq.astype(F32), 2)[None, :]
    w_out = filt_w_out.astype(F32)
    wl = jnp.concatenate([jnp.pad(w_out[:, :n_cols], ((0, FILTER_HIDDEN), (0, 0))),
                          jnp.pad(w_out[:, n_cols:], ((FILTER_HIDDEN, 0), (0, 0)))], axis=1)
    full = lambda *shape: pl.BlockSpec(shape, lambda j: (0,) * len(shape))
    cast_specs, cast_shapes = _cast_plan(casts, steps)
    spectrum, *cast_out = pl.pallas_call(
        functools.partial(_filter_kernel, len(casts)),
        out_shape=[jax.ShapeDtypeStruct((FFT_SLICES, 2 * FFT_N2, n_cols), F32)] + cast_shapes,
        grid=(steps,),
        in_specs=[
            full(seq_len, 2 * LANES), full(2, n_fft, LANES), full(2 * LANES, hid), full(1, hid),
            full(FILTER_INNER, hid, hid), full(FILTER_INNER, 1, hid), full(1, hid),
            pl.BlockSpec((hid, FILT_CW), lambda j: (0, j)),
            pl.BlockSpec((hid, FILT_CW), lambda j: (0, steps + j)),
            pl.BlockSpec((1, FILT_CW), lambda j: (0, j)),
            full(FFT_SLICES, 2 * FFT_N2, 2 * FFT_N2),
        ] + cast_specs,
        out_specs=[pl.BlockSpec((FFT_SLICES, 2 * FFT_N2, FILT_CW), lambda j: (0, 0, j))] + cast_specs,
        scratch_shapes=[pltpu.VMEM((seq_len, hid), F32), pltpu.VMEM((n_fft, FILT_CW), F32),
                        pltpu.VMEM((FFT_ROWS, FILT_CW), BF16)],
        compiler_params=pltpu.CompilerParams(dimension_semantics=("arbitrary",),
                                             vmem_limit_bytes=VMEM_LIMIT_BYTES),
        name="hyena_filter",
    )(feats_ext, t_mask, w0, b0, wi, bi, freq, wl, wl, deltas, mf, *casts)
    return spectrum, cast_out


HY_CW = 256
CONV_ROWS = 64
PAD_ROWS = 8


def _hyena_kernel(x_ref, wv_ref, w1_ref, w2_ref, bv_ref, b1_ref, b2_ref, cwv_ref, cw1_ref, cw2_ref,
                  cbv_ref, cb1_ref, cb2_ref, skip_ref, h0_ref, h1_ref, mf_ref, mi_ref,
                  out_ref, u_ref, p_ref, b_ref, c_ref):
    seq_len = x_ref.shape[0]
    width = HY_CW
    n2 = FFT_N2
    xb = x_ref[...]

    n_slabs = width // LANES
    zeros = jnp.zeros((PAD_ROWS, LANES), F32)
    for slot in range(u_ref.shape[0]):
        for j in range(n_slabs):
            u_ref[slot, j, 0:PAD_ROWS, :] = zeros
            u_ref[slot, j, PAD_ROWS + seq_len:2 * PAD_ROWS + seq_len, :] = zeros
    parts = ((wv_ref, bv_ref, cwv_ref, cbv_ref), (w1_ref, b1_ref, cw1_ref, cb1_ref),
             (w2_ref, b2_ref, cw2_ref, cb2_ref))

    def project(p, slot):
        w_ref, bias_ref, _, _ = parts[p]
        half = seq_len // 2
        for r0 in (0, half):
            u = _dot(xb[r0:r0 + half], w_ref[...]) + bias_ref[...]
            for j in range(n_slabs):
                u_ref[slot, j, PAD_ROWS + r0:PAD_ROWS + r0 + half, :] = u[:, j * LANES:(j + 1) * LANES]

    def short_conv(p, slot):
        _, _, cw_ref, cb_ref = parts[p]

        def conv_body(i, carry):
            start = pl.multiple_of(i * CONV_ROWS, CONV_ROWS)
            cw = cw_ref[...]
            cb = cb_ref[...]
            for j in range(n_slabs):
                lanes = slice(j * LANES, (j + 1) * LANES)
                taps = [u_ref[slot, j, pl.ds(start + PAD_ROWS - 1 + k, CONV_ROWS, stride=1), :]
                        for k in range(3)]
                p_ref[p, pl.ds(start, CONV_ROWS), lanes] = (
                    cw[0:1, lanes] * taps[0] + cw[1:2, lanes] * taps[1] + cw[2:3, lanes] * taps[2]
                    + cb[:, lanes])
            return carry
        lax.fori_loop(0, seq_len // CONV_ROWS, conv_body, 0, unroll=True)

    def spectral_product(h_ref):
        x_next = _forward_dft(b_ref, 0, mf_ref)
        for k1 in range(FFT_SLICES):
            x = x_next
            if k1 + 1 < FFT_SLICES:
                x_next = _forward_dft(b_ref, k1 + 1, mf_ref)
            xr, xi = x[:n2], x[n2:]
            hr = h_ref[k1, 0:n2, :]
            hi = h_ref[k1, n2:2 * n2, :]
            prod = jnp.concatenate([(xr * hr - xi * hi).astype(BF16), (xr * hi + xi * hr).astype(BF16)], axis=0)
            _inverse_dft(prod, k1, mi_ref, c_ref)

    def gated_blocks(order, off):
        skip = skip_ref[order:order + 1, :]
        ys = _inverse_radix16_stage(_load_slices(c_ref, off, A_CHUNK))
        out = []
        for t1, y in enumerate(ys):
            rows = pl.ds(t1 * n2 + off, A_CHUNK)
            out.append(p_ref[order + 1, rows, :] * (y + skip * p_ref[0, rows, :]))
        return out

    project(0, 0)
    short_conv(0, 0)

    def a_body(i, carry):
        off = pl.multiple_of(i * A_CHUNK, A_CHUNK)
        blk = [p_ref[0, pl.ds(t1 * n2 + off, A_CHUNK), :] for t1 in range(FFT_N1 // 2)]
        _store_slices(b_ref, off, A_CHUNK, _radix16_stage(blk))
        return carry
    lax.fori_loop(0, n2 // A_CHUNK, a_body, 0, unroll=True)

    project(1, 0)
    project(2, 1)
    spectral_product(h0_ref)
    short_conv(1, 0)
    short_conv(2, 1)

    def mid_body(i, carry):
        off = pl.multiple_of(i * A_CHUNK, A_CHUNK)
        z = gated_blocks(0, off)
        for t1, blk in enumerate(z):
            p_ref[0, pl.ds(t1 * n2 + off, A_CHUNK), :] = blk
        _store_slices(b_ref, off, A_CHUNK, _radix16_stage(z))
        return carry
    lax.fori_loop(0, n2 // A_CHUNK, mid_body, 0)

    spectral_product(h1_ref)

    def out_body(i, carry):
        off = pl.multiple_of(i * A_CHUNK, A_CHUNK)
        for t1, blk in enumerate(gated_blocks(1, off)):
            out_ref[pl.ds(t1 * n2 + off, A_CHUNK), :] = blk.astype(out_ref.dtype)
        return carry
    lax.fori_loop(0, n2 // A_CHUNK, out_body, 0)


def _hyena_branch(xb, w_in, b_in, col0, conv_w, conv_b, hyena_skip, spectrum):
    bsz, seq_len, d_model = xb.shape
    width = hyena_skip.shape[-1]
    nblk = width // HY_CW
    mf, mi = (jnp.asarray(m).astype(BF16) for m in _fft_constants(seq_len))
    col = lambda part: (lambda c, b: (0, part * nblk + c))
    blk0, rem = divmod(col0, HY_CW)
    assert rem == 0
    in_col = lambda part: (lambda c, b: (0, blk0 + part * nblk + c))
    const = lambda *shape: pl.BlockSpec(shape, lambda c, b: (0,) * len(shape), pipeline_mode=pl.Buffered(1))
    spec_h = lambda order: pl.BlockSpec((FFT_SLICES, 2 * FFT_N2, HY_CW), lambda c, b: (0, 0, order * nblk + c),
                                        pipeline_mode=pl.Buffered(1))
    in_specs = [pl.BlockSpec((None, seq_len, d_model), lambda c, b: (b, 0, 0))]
    in_specs += [pl.BlockSpec((d_model, HY_CW), in_col(p)) for p in range(3)]
    in_specs += [pl.BlockSpec((1, HY_CW), in_col(p)) for p in range(3)]
    in_specs += [pl.BlockSpec((3, HY_CW), col(p)) for p in range(3)]
    in_specs += [pl.BlockSpec((1, HY_CW), col(p)) for p in range(3)]
    in_specs += [pl.BlockSpec((HYENA_ORDER, HY_CW), lambda c, b: (0, c)), spec_h(0), spec_h(1),
                 const(FFT_SLICES, 2 * FFT_N2, 2 * FFT_N2), const(FFT_SLICES, 2 * FFT_N2, 2 * FFT_N2)]
    return pl.pallas_call(
        _hyena_kernel,
        out_shape=jax.ShapeDtypeStruct((bsz, seq_len, width), BF16),
        grid=(nblk, bsz),
        in_specs=in_specs,
        out_specs=pl.BlockSpec((None, seq_len, HY_CW), lambda c, b: (b, 0, c)),
        scratch_shapes=[pltpu.VMEM((2, HY_CW // LANES, seq_len + 2 * PAD_ROWS, LANES), F32),
                        pltpu.VMEM((3, seq_len, HY_CW), F32),
                        pltpu.VMEM((FFT_ROWS, HY_CW), BF16),
                        pltpu.VMEM((FFT_ROWS, HY_CW), F32)],
        compiler_params=pltpu.CompilerParams(dimension_semantics=("arbitrary", "arbitrary"),
                                             vmem_limit_bytes=VMEM_LIMIT_BYTES),
        name="hyena_branch",
    )(xb, w_in, w_in, w_in, b_in, b_in, b_in, conv_w, conv_w, conv_w, conv_b, conv_b, conv_b,
      hyena_skip, spectrum, spectrum, mf, mi)


def _stream_pitch(dil):
    return dil + 1 if dil % SUBLANES == 0 else dil


def _attn_kernel(x_ref, w_ref, b_ref, dist_ref, out_ref, qkv_ref, acc_ref, m_ref, l_ref,
                 s_ref, p_ref, al_ref):
    seq_len = x_ref.shape[0]
    xb = x_ref[...]
    slopes = _alibi_slopes()
    q_scale = 1.0 / math.sqrt(HEAD_DIM)
    heads = range(HEADS_PER_GROUP)

    def stage_scores(blk, coefs):
        qr, kr, _, dist, kc = blk
        for h in heads:
            s_ref[h, :, 0:kc] = _dot_nt(qkv_ref[0, h, qr, :].astype(BF16),
                                        qkv_ref[1, h, kr, :].astype(BF16)) + dist * coefs[h]

    def stage_softmax(blk, first, last):
        _, _, ar, _, kc = blk
        s = [s_ref[h, :, 0:kc] for h in heads]
        m_blk = [jnp.max(s[h], axis=-1, keepdims=True) for h in heads]
        if first:
            m_new = [jnp.broadcast_to(m_blk[h], (Q_BLOCK, HEAD_DIM)) for h in heads]
        else:
            m_old = [m_ref[h, ar, :] for h in heads]
            m_new = [jnp.maximum(m_old[h], m_blk[h]) for h in heads]
        m_wide = [jnp.concatenate([m_new[h]] * (kc // HEAD_DIM), axis=1) for h in heads]
        for h in heads:
            p_ref[h, :, 0:kc] = jnp.exp(s[h] - m_wide[h]).astype(BF16)
            if not first:
                al_ref[h] = jnp.exp(m_old[h] - m_new[h])
            if not last:
                m_ref[h, ar, :] = m_new[h]

    def stage_output(blk, first, last):
        _, kr, ar, _, kc = blk
        ones = jnp.ones((kc, HEAD_DIM), BF16)
        pv = [_dot(p_ref[h, :, 0:kc], jnp.concatenate([qkv_ref[2, h, kr, :].astype(BF16), ones], axis=1))
              for h in heads]
        acc = [r[:, :HEAD_DIM] for r in pv]
        l_new = [r[:, HEAD_DIM:] for r in pv]
        if not first:
            alpha = [al_ref[h] for h in heads]
            l_new = [alpha[h] * l_ref[h, ar, :] + l_new[h] for h in heads]
            acc = [alpha[h] * acc_ref[h, ar, :] + acc[h] for h in heads]
        for h in heads:
            if last:
                out_ref[ar, h * HEAD_DIM:(h + 1) * HEAD_DIM] = (acc[h] / l_new[h]).astype(out_ref.dtype)
            else:
                l_ref[h, ar, :] = l_new[h]
                acc_ref[h, ar, :] = acc[h]

    order = sorted(range(N_GROUPS), key=lambda g: -ATTN_GROUPS[g][1])
    assert ATTN_GROUPS[order[-1]][1] == 1
    for pos, g in enumerate(order):
        window, dil = ATTN_GROUPS[g]
        first, last = pos == 0, pos == N_GROUPS - 1
        n_off = (window // 2) // dil
        assert n_off == Q_BLOCK // 2
        stream = seq_len // dil
        nb = stream // Q_BLOCK
        n_blocks = dil * nb
        pitch = _stream_pitch(dil)
        for part in range(3):
            c0 = part * ATTN_QKV_WIDTH + g * GROUP_WIDTH
            proj = _dot(xb, w_ref[:, c0:c0 + GROUP_WIDTH]) + b_ref[:, c0:c0 + GROUP_WIDTH]
            if part == 0:
                proj = proj * q_scale
            for h in heads:
                slab = proj[:, h * HEAD_DIM:(h + 1) * HEAD_DIM]
                if pitch == dil:
                    qkv_ref[part, h, 0:seq_len, :] = slab
                else:
                    for i in range(stream):
                        qkv_ref[part, h, pl.ds(pitch * i, dil, stride=1), :] = slab[dil * i:dil * (i + 1)]
        coefs = [-slopes[g * HEADS_PER_GROUP + h] * dil for h in heads]

        def block(i, dil=dil, nb=nb, stream=stream, pitch=pitch):
            def rows(row_pitch, c, first_pos, count):
                if row_pitch == 1:
                    return pl.ds(pl.multiple_of(first_pos, n_off), count)
                return pl.ds(c + row_pitch * first_pos, count, stride=row_pitch)
            if nb == 1:
                q = rows(pitch, i, 0, Q_BLOCK)
                return q, q, rows(dil, i, 0, Q_BLOCK), dist_ref[3, :, 0:Q_BLOCK], Q_BLOCK
            c = i // nb
            n = i % nb
            lo = jnp.clip(n * Q_BLOCK - n_off, 0, stream - 2 * Q_BLOCK)
            dist = dist_ref[jnp.where(n == 0, 0, jnp.where(n == nb - 1, 2, 1))]
            return (rows(pitch, c, n * Q_BLOCK, Q_BLOCK), rows(pitch, c, lo, 2 * Q_BLOCK),
                    rows(dil, c, n * Q_BLOCK, Q_BLOCK), dist, 2 * Q_BLOCK)

        def steady(i, carry, block=block, coefs=coefs, first=first, last=last):
            stage_output(block(i - 2), first, last)
            stage_softmax(block(i - 1), first, last)
            stage_scores(block(i), coefs)
            return carry

        assert n_blocks >= 3
        stage_scores(block(0), coefs)
        stage_softmax(block(0), first, last)
        stage_scores(block(1), coefs)
        lax.fori_loop(2, n_blocks, steady, 0)
        stage_output(block(n_blocks - 2), first, last)
        stage_softmax(block(n_blocks - 1), first, last)
        stage_output(block(n_blocks - 1), first, last)


def _attention_branch(xb, w_qkv, b_qkv):
    bsz, seq_len, d_model = xb.shape
    dist = _attn_distance_tiles()
    qkv_w = 3 * ATTN_QKV_WIDTH
    qkv_rows = max((seq_len // dil) * _stream_pitch(dil) for _, dil in ATTN_GROUPS)
    return pl.pallas_call(
        _attn_kernel,
        out_shape=jax.ShapeDtypeStruct((bsz, seq_len, GROUP_WIDTH), BF16),
        grid=(bsz,),
        in_specs=[pl.BlockSpec((None, seq_len, d_model), lambda b: (b, 0, 0)),
                  pl.BlockSpec((d_model, qkv_w), lambda b: (0, 0), pipeline_mode=pl.Buffered(1)),
                  pl.BlockSpec((1, qkv_w), lambda b: (0, 0)),
                  pl.BlockSpec(dist.shape, lambda b: (0, 0, 0))],
        out_specs=pl.BlockSpec((None, seq_len, GROUP_WIDTH), lambda b: (b, 0, 0)),
        scratch_shapes=[pltpu.VMEM((3, HEADS_PER_GROUP, qkv_rows, HEAD_DIM), F32),
                        pltpu.VMEM((HEADS_PER_GROUP, seq_len, HEAD_DIM), F32),
                        pltpu.VMEM((HEADS_PER_GROUP, seq_len, HEAD_DIM), F32),
                        pltpu.VMEM((HEADS_PER_GROUP, seq_len, HEAD_DIM), F32),
                        pltpu.VMEM((HEADS_PER_GROUP, Q_BLOCK, 2 * Q_BLOCK), F32),
                        pltpu.VMEM((HEADS_PER_GROUP, Q_BLOCK, 2 * Q_BLOCK), BF16),
                        pltpu.VMEM((HEADS_PER_GROUP, Q_BLOCK, HEAD_DIM), F32)],
        compiler_params=pltpu.CompilerParams(dimension_semantics=("arbitrary",),
                                             vmem_limit_bytes=VMEM_LIMIT_BYTES),
        name="dilated_attention",
    )(xb, w_qkv, b_qkv, dist)


ROW_TILE = 1024
SUB_ROWS = 512


def _layer_norm(r, g, b):
    mu = jnp.mean(r, axis=-1, keepdims=True)
    d = r - mu
    var = jnp.mean(d * d, axis=-1, keepdims=True)
    return d * lax.rsqrt(var + LN_EPS) * g + b


def _sub_tiles(ref):
    return [slice(s * SUB_ROWS, (s + 1) * SUB_ROWS) for s in range(ref.shape[0] // SUB_ROWS)]


def _merge_kernel(alpha, n_gate, x_ref, ya_ref, yh_ref, *rest):
    wg_refs, (bg_ref, wa_ref, wh_ref, wo_ref, g_ref, b_ref, out_ref) = rest[:n_gate], rest[n_gate:]
    d_model = x_ref.shape[-1]
    gw = wg_refs[0].shape[1]
    tiles = _sub_tiles(x_ref)
    x = [x_ref[r, :] for r in tiles]
    xb = [xs.astype(BF16) for xs in x]
    gates = [jnp.concatenate([jax.nn.sigmoid(_dot(xs, w[...]) + bg_ref[:, j * gw:(j + 1) * gw])
                              for j, w in enumerate(wg_refs)], axis=1) for xs in xb]
    merged = [gt[:, :d_model] * _dot(ya_ref[r, :], wa_ref[...])
              + gt[:, d_model:] * _dot(yh_ref[r, :], wh_ref[...]) for gt, r in zip(gates, tiles)]
    mix = [_dot(mg.astype(BF16), wo_ref[...]) for mg in merged]
    for r, xs, mx in zip(tiles, x, mix):
        out_ref[r, :] = _layer_norm(alpha * xs + mx, g_ref[...], b_ref[...])


def _ffn_kernel(alpha, h_ref, w1_ref, b1_ref, w2_ref, b2_ref, g_ref, b_ref, out_ref):
    tiles = _sub_tiles(h_ref)
    h = [h_ref[r, :] for r in tiles]
    hid = [jnp.maximum(_dot(hs.astype(BF16), w1_ref[...]) + b1_ref[...], 0.0) for hs in h]
    ff = [_dot((hd * hd).astype(BF16), w2_ref[...]) + b2_ref[...] for hd in hid]
    for r, hs, fs in zip(tiles, h, ff):
        out_ref[r, :] = _layer_norm(alpha * hs + fs, g_ref[...], b_ref[...])


def _row_tiled_call(kernel_fn, name, rows, d_model, tiled, resident, casts=()):
    steps = rows // ROW_TILE
    n_main = len(tiled) + len(resident)
    cast_specs, cast_shapes = _cast_plan(casts, steps)

    def resident_spec(item):
        if isinstance(item, tuple):
            a, width, j = item
            return pl.BlockSpec((a.shape[0], width), lambda i: (0, j), pipeline_mode=pl.Buffered(1))
        return pl.BlockSpec(item.shape, lambda i: (0, 0), pipeline_mode=pl.Buffered(1))

    def body(*refs):
        cast_in, (out_ref, *cast_out) = refs[n_main:n_main + len(casts)], refs[n_main + len(casts):]
        _cast_blocks(cast_in, cast_out)
        kernel_fn(*refs[:n_main], out_ref)

    in_specs = [pl.BlockSpec((ROW_TILE, a.shape[1]), lambda i: (i, 0)) for a in tiled]
    in_specs += [resident_spec(item) for item in resident]
    resident = [item[0] if isinstance(item, tuple) else item for item in resident]
    out, *cast_out = pl.pallas_call(
        body,
        out_shape=[jax.ShapeDtypeStruct((rows, d_model), F32)] + cast_shapes,
        grid=(steps,),
        in_specs=in_specs + cast_specs,
        out_specs=[pl.BlockSpec((ROW_TILE, d_model), lambda i: (i, 0))] + cast_specs,
        compiler_params=pltpu.CompilerParams(dimension_semantics=("arbitrary",),
                                             vmem_limit_bytes=VMEM_LIMIT_BYTES),
        name=name,
    )(*tiled, *resident, *casts)
    return out, cast_out


def kernel(x, w_in, b_in, conv_w, conv_b, filt_w0, filt_b0, filt_w_inner, filt_b_inner, filt_w_out,
           filt_freq, hyena_skip, w_branch_attn, w_branch_hyena, w_out, ln1_g, ln1_b, w_ff1, b_ff1,
           w_ff2, b_ff2, ln2_g, ln2_b):
    bsz, seq_len, d_model = x.shape
    depth = w_in.shape[0]
    alpha = (2 * depth) ** 0.25
    rows = bsz * seq_len
    qkv_w = 3 * ATTN_QKV_WIDTH
    hy_w = (HYENA_ORDER + 1) * hyena_skip.shape[-1]
    row = lambda a: a.astype(F32)[None, :]
    h = x
    for layer in range(depth):
        w_l = w_in[layer].astype(BF16)
        b_l = b_in[layer].astype(F32)[None, :]
        h_rows = h.reshape(rows, d_model).astype(F32)
        spectrum, (hb, wa_l, wh_l, wo_l) = _filter_spectrum(
            seq_len, hyena_skip.shape[-1], filt_w0[layer], filt_b0[layer], filt_w_inner[layer],
            filt_b_inner[layer], filt_w_out[layer], filt_freq[layer],
            casts=[h_rows, w_branch_attn[layer].astype(F32), w_branch_hyena[layer].astype(F32),
                   w_out[layer].astype(F32)])
        hb = hb.reshape(bsz, seq_len, d_model)
        y_attn = _attention_branch(hb, w_l, b_l)
        y_hyena = _hyena_branch(hb, w_l, b_l, qkv_w,
                                conv_w[layer].astype(F32), conv_b[layer].astype(F32)[None, :],
                                hyena_skip[layer].astype(F32), spectrum)
        gate0 = qkv_w + hy_w
        gate_w = math.gcd(gate0, 2 * d_model)
        gate_blocks = [(w_l, gate_w, gate0 // gate_w + j) for j in range(2 * d_model // gate_w)]
        h1, (w1_l, w2_l) = _row_tiled_call(
            functools.partial(_merge_kernel, alpha, len(gate_blocks)), "merge_ln", rows, d_model,
            [h_rows, y_attn.reshape(rows, -1), y_hyena.reshape(rows, -1)],
            gate_blocks + [b_l[:, gate0:], wa_l, wh_l, wo_l, row(ln1_g[layer]), row(ln1_b[layer])],
            casts=[w_ff1[layer].astype(F32), w_ff2[layer].astype(F32)])
        h2, _ = _row_tiled_call(
            functools.partial(_ffn_kernel, alpha), "ffn_ln", rows, d_model, [h1],
            [w1_l, row(b_ff1[layer]), w2_l, row(b_ff2[layer]), row(ln2_g[layer]), row(ln2_b[layer])])
        h = h2.reshape(bsz, seq_len, d_model)
    return h
```

```python
import functools
import math

import jax
import jax.numpy as jnp
import numpy as np
from jax import lax
from jax.experimental import pallas as pl
from jax.experimental.pallas import tpu as pltpu

F32 = jnp.float32
BF16 = jnp.bfloat16

ATTN_GROUPS = ((128, 1), (512, 4), (2048, 16))
N_GROUPS = len(ATTN_GROUPS)
HEADS_PER_GROUP = 4
HEAD_DIM = 128
GROUP_WIDTH = HEADS_PER_GROUP * HEAD_DIM
ATTN_QKV_WIDTH = N_GROUPS * GROUP_WIDTH
Q_BLOCK = 128
ALIBI_MAX_EXP = 8.0
HYENA_ORDER = 2
FILTER_BANDS = 16
FILTER_EMB = 1 + 2 * FILTER_BANDS
FILTER_HIDDEN = 64
FILTER_INNER = 2
DECAY_TARGET = 1e-2
FAST_DECAY_PCT = 0.3
SLOW_DECAY_PCT = 1.5
LN_EPS = 1e-5

LANES = 128
SUBLANES = 8
BF16_ROWS = 2 * SUBLANES
VMEM_LIMIT_BYTES = 58 * 1024 * 1024

FFT_N1 = 16
FFT_N2 = 256
FFT_SLICES = FFT_N1 // 2 + 1
FFT_ROWS = FFT_N1 * FFT_N2
RSQRT2 = 1.0 / math.sqrt(2.0)


def _unit_root(k, n):
    snap = lambda v: float(round(v)) if abs(v - round(v)) < 1e-12 else v
    return snap(math.cos(2.0 * math.pi * k / n)), snap(-math.sin(2.0 * math.pi * k / n))


W16 = tuple(_unit_root(k, FFT_N1) for k in range(FFT_SLICES))


def _dot(a, b):
    return jnp.dot(a, b, preferred_element_type=F32)


def _split_bf16(a):
    hi = a.astype(BF16)
    return hi, (a - hi.astype(F32)).astype(BF16)


def _dot3(a, b):
    a_hi, a_lo = _split_bf16(a)
    b_hi, b_lo = _split_bf16(b)
    return (_dot(jnp.concatenate([a_hi, a_lo], axis=1), jnp.concatenate([b_hi, b_hi], axis=0))
            + _dot(a_hi, b_lo))


def _dot_nt(a, b):
    return lax.dot_general(a, b, (((1,), (1,)), ((), ())), preferred_element_type=F32)


CAST_ROWS = 256


def _cast_plan(arrays, steps):
    specs, shapes = [], []
    for a in arrays:
        rows, rem = divmod(a.shape[0], steps)
        assert rem == 0 and rows % BF16_ROWS == 0, a.shape
        specs.append(pl.BlockSpec((rows, a.shape[1]), lambda i: (i, 0)))
        shapes.append(jax.ShapeDtypeStruct(a.shape, BF16))
    return specs, shapes


def _cast_blocks(src_refs, dst_refs):
    for src, dst in zip(src_refs, dst_refs):
        chunk = min(src.shape[0], CAST_ROWS)
        assert src.shape[0] % chunk == 0

        def body(i, carry, src=src, dst=dst, chunk=chunk):
            rows = pl.ds(pl.multiple_of(i * chunk, chunk), chunk)
            dst[rows, :] = src[rows, :].astype(dst.dtype)
            return carry
        lax.fori_loop(0, src.shape[0] // chunk, body, 0)


@functools.lru_cache(maxsize=None)
def _fft_constants(seq_len):
    n_fft = 2 * seq_len
    assert n_fft == FFT_N1 * FFT_N2
    k2 = np.arange(FFT_N2)
    fwd = []
    for k1 in range(FFT_SLICES):
        ang = -2.0 * np.pi * ((np.outer(FFT_N1 * k2 + k1, k2) % n_fft) / n_fft)
        er, ei = np.cos(ang), np.sin(ang)
        fwd.append(np.block([[er, -ei], [ei, er]]))
    fwd = np.stack(fwd, axis=0).astype(np.float32)
    inv = np.ascontiguousarray(np.transpose(fwd, (0, 2, 1)))
    return fwd, inv


@functools.lru_cache(maxsize=None)
def _filter_constants(seq_len, width):
    n_fft = 2 * seq_len
    t = np.linspace(0.0, 1.0, seq_len)
    bands = np.linspace(1e-4, FILTER_BANDS - 1, FILTER_BANDS)
    ang = (2.0 * np.pi / seq_len) * np.arange(seq_len)[:, None] * bands
    feats = np.concatenate([t[:, None], np.cos(ang), -np.sin(ang)], axis=-1)
    src = np.concatenate([np.arange(seq_len), [0], np.arange(seq_len - 1, 0, -1)])
    feats_ext = np.zeros((seq_len, 2 * LANES), np.float32)
    feats_ext[:, :FILTER_EMB] = feats[src[:seq_len]]
    feats_ext[:, LANES:LANES + FILTER_EMB] = feats[src[seq_len:]]
    t_mask = np.zeros((2, n_fft, LANES), np.float32)
    t_mask[0] = t[src][:, None]
    t_mask[1] = 1.0
    t_mask[1, seq_len] = 0.0
    deltas = np.abs(np.linspace(math.log(DECAY_TARGET) / SLOW_DECAY_PCT,
                                math.log(DECAY_TARGET) / FAST_DECAY_PCT, width))
    deltas = np.tile(deltas[None, :], (1, HYENA_ORDER)).astype(np.float32)
    return feats_ext, t_mask, deltas


@functools.lru_cache(maxsize=None)
def _attn_distance_tiles():
    n_off = Q_BLOCK // 2
    qi = np.arange(Q_BLOCK)[:, None]
    kj = np.arange(2 * Q_BLOCK)[None, :]
    tiles = []
    for shift in (0, n_off, 2 * n_off):
        off = np.abs(kj - shift - qi).astype(np.float32)
        tiles.append(np.where(off <= n_off, off, np.inf))
    off = np.abs(kj - qi).astype(np.float32)
    t3 = np.where(off <= n_off, off, np.inf)
    t3[:, Q_BLOCK:] = np.inf
    tiles.append(t3)
    def permute(n):
        return (4 * np.arange(n // 4)[None, :] + np.arange(4)[:, None]).reshape(-1)
    rows, cols = permute(Q_BLOCK), permute(2 * Q_BLOCK)
    tiles += [tiles[v][rows][:, cols] for v in range(3)]
    return np.stack(tiles, axis=0).astype(np.float32)


def _alibi_slopes():
    n = N_GROUPS * HEADS_PER_GROUP
    return [2.0 ** (-ALIBI_MAX_EXP * j / n) for j in range(1, n + 1)]


def _radix8_half(z0, z1, z2, z3):
    s02, d02 = z0 + z2, z0 - z2
    s13, d13 = z1 + z3, z1 - z3
    ss, dd = s13 * RSQRT2, d13 * RSQRT2
    return {0: (s02 + s13, None), 4: (s02 - s13, None), 2: (d02, -d13),
            1: (z0 + dd, -z2 - ss), 3: (z0 - dd, z2 - ss)}


def _add(a, b, sign=1.0):
    if b is None:
        return a
    if a is None:
        return b if sign > 0 else -b
    return a + b if sign > 0 else a - b


def _cmul_const(re, im, wr, wi):
    def scaled(v, w):
        if v is None or w == 0.0:
            return None
        return v if w == 1.0 else (-v if w == -1.0 else v * w)
    if re is not None and im is not None and wr != 0.0 and abs(wr) == abs(wi):
        if wi == wr:
            return (re - im) * wr, (re + im) * wr
        return (re + im) * wr, (im - re) * wr
    return _add(scaled(re, wr), scaled(im, wi), -1.0), _add(scaled(re, wi), scaled(im, wr))


def _radix16_stage(blocks):
    even = _radix8_half(*blocks[0::2])
    odd = _radix8_half(*blocks[1::2])
    half = FFT_N1 // 2
    out = [None] * FFT_SLICES
    out[0] = (even[0][0] + odd[0][0], None)
    out[half] = (even[0][0] - odd[0][0], None)
    out[half // 2] = (even[4][0], -odd[4][0])
    for k in range(1, half // 2):
        (er, ei), (pr, pi) = even[k], _cmul_const(*odd[k], *W16[k])
        out[k] = (er + pr, ei + pi)
        out[half - k] = (er - pr, pi - ei)
    return out


def _assemble8(c0, c4, c1, c2, c3):
    (c1r, c1i), (c2r, c2i), (c3r, c3i) = c1, c2, c3
    e, o = c0 + c4, c0 - c4
    return [e + c1r + c2r + c3r,
            o + (c1r - c1i - c3r - c3i) * RSQRT2 - c2i,
            e - c1i - c2r + c3i,
            o + (c3r - c3i - c1r - c1i) * RSQRT2 + c2i]


def _inverse_radix16_stage(c):
    half = FFT_N1 // 2
    plus = [(c[k][0] + c[half - k][0], c[k][1] - c[half - k][1]) for k in range(1, half // 2)]
    minus = [_cmul_const(c[k][0] - c[half - k][0], c[k][1] + c[half - k][1], W16[k][0], -W16[k][1])
             for k in range(1, half // 2)]
    mid = c[half // 2]
    even = _assemble8(c[0][0] + c[half][0], mid[0], *plus)
    odd = _assemble8(c[0][0] - c[half][0], -mid[1], *minus)
    return [blk for pair_ in zip(even, odd) for blk in pair_]


def _slice_rows(k1):
    if k1 == 0:
        return 0, FFT_N2
    if k1 == FFT_N1 // 2:
        return FFT_N2, FFT_N2
    return 2 * FFT_N2 * k1, 2 * FFT_N2


def _store_slices(ref, off, rows, values, scale=None):
    for k1, (re, im) in enumerate(values):
        r0, n = _slice_rows(k1)
        parts = (re,) if n == FFT_N2 else (re, im)
        for j, part in enumerate(parts):
            if scale is not None:
                part = part * scale
            ref[pl.ds(r0 + j * FFT_N2 + off, rows), :] = part.astype(ref.dtype)


def _load_slices(ref, off, rows):
    out = []
    for k1 in range(FFT_SLICES):
        r0, n = _slice_rows(k1)
        re = ref[pl.ds(r0 + off, rows), :]
        out.append((re, ref[pl.ds(r0 + FFT_N2 + off, rows), :] if n > FFT_N2 else None))
    return out


def _forward_dft(b_ref, k1, mf_ref):
    r0, n = _slice_rows(k1)
    b = b_ref[r0:r0 + n, :]
    return jnp.concatenate([_dot(mf_ref[k1, h0:h0 + FFT_N2, 0:n], b) for h0 in (0, FFT_N2)], axis=0)


def _inverse_dft(p, k1, mi_ref, c_ref):
    r0, n = _slice_rows(k1)
    for h0 in range(0, n, FFT_N2):
        c_ref[r0 + h0:r0 + h0 + FFT_N2, :] = _dot(mi_ref[k1, h0:h0 + FFT_N2, :], p)


FILT_CW = 256
FILT_ROWS = 256
A_CHUNK = 16


def _filter_kernel(n_cast, feats_ref, tmask_ref, w0_ref, b0_ref, wi_ref, bi_ref, freq_ref, wf_ref, wb_ref,
                   delta_ref, mf_ref, *rest):
    cast_in, (h_ref, *cast_out), (hid_ref, ts_ref, b_ref) = (
        rest[:n_cast], rest[n_cast:2 * n_cast + 1], rest[2 * n_cast + 1:])
    _cast_blocks(cast_in, cast_out)
    seq_len = feats_ref.shape[0]
    n_fft = 2 * seq_len

    @pl.when(pl.program_id(0) == 0)
    def _():
        def body(i, carry):
            rows = pl.ds(pl.multiple_of(i * FILT_ROWS, FILT_ROWS), FILT_ROWS)
            freq = freq_ref[...]
            hid = jnp.sin(freq * (_dot3(feats_ref[rows, :], w0_ref[...]) + b0_ref[...]))
            for layer in range(FILTER_INNER):
                hid = jnp.sin(freq * (_dot3(hid, wi_ref[layer]) + bi_ref[layer]))
            hid_ref[rows, :] = hid
            return carry
        lax.fori_loop(0, seq_len // FILT_ROWS, body, 0)

    h_hi, h_lo = _split_bf16(hid_ref[...])
    h_cat = jnp.concatenate([h_hi, h_lo], axis=1)
    for half, w_ref in enumerate((wf_ref, wb_ref)):
        w_hi, w_lo = _split_bf16(w_ref[...])
        ts_ref[half * seq_len:(half + 1) * seq_len, :] = (
            _dot(h_cat, jnp.concatenate([w_hi, w_hi], axis=0)) + _dot(h_hi, w_lo))

    def window_body(i, ssq):
        rows = pl.ds(pl.multiple_of(i * FILT_ROWS, FILT_ROWS), FILT_ROWS)
        widen = lambda a: jnp.concatenate([a] * (FILT_CW // LANES), axis=1)
        val = (ts_ref[rows, :] * jnp.exp(-widen(tmask_ref[0, rows, :]) * delta_ref[...])
               * widen(tmask_ref[1, rows, :]))
        ts_ref[rows, :] = val
        return ssq + jnp.sum(val * val, axis=0, keepdims=True)
    ssq = lax.fori_loop(0, n_fft // FILT_ROWS, window_body, jnp.zeros((1, FILT_CW), F32))
    scale = lax.rsqrt(ssq)

    def a_body(i, carry):
        off = pl.multiple_of(i * A_CHUNK, A_CHUNK)
        blk = [ts_ref[pl.ds(t1 * FFT_N2 + off, A_CHUNK), :] for t1 in range(FFT_N1)]
        lo = _radix16_stage(blk[:FFT_N1 // 2])
        hi = _radix16_stage(blk[FFT_N1 // 2:])
        vals = [(_add(lo[k][0], hi[k][0], 1.0 if k % 2 == 0 else -1.0),
                 _add(lo[k][1], hi[k][1], 1.0 if k % 2 == 0 else -1.0)) for k in range(FFT_SLICES)]
        _store_slices(b_ref, off, A_CHUNK, vals, scale)
        return carry
    lax.fori_loop(0, FFT_N2 // A_CHUNK, a_body, 0)

    for k1 in range(FFT_SLICES):
        s = (1.0 if k1 in (0, FFT_N1 // 2) else 2.0) / n_fft
        h_ref[k1] = (_forward_dft(b_ref, k1, mf_ref) * s).astype(h_ref.dtype)


def _filter_spectrum(seq_len, width, filt_w0, filt_b0, filt_w_inner, filt_b_inner, filt_w_out, filt_freq,
                     casts=()):
    n_fft = 2 * seq_len
    hid = LANES
    feats_ext, t_mask, deltas = _filter_constants(seq_len, width)
    mf = jnp.asarray(_fft_constants(seq_len)[0]).astype(BF16)
    assert 2 * FILTER_HIDDEN == hid and FILTER_EMB <= LANES
    n_cols = HYENA_ORDER * width
    steps = n_cols // FILT_CW
    zeros = lambda r, c: jnp.zeros((r, c), F32)
    w0p = jnp.pad(filt_w0.astype(F32), ((0, LANES - FILTER_EMB), (0, 0)))
    w0 = jnp.block([[w0p, zeros(LANES, FILTER_HIDDEN)], [zeros(LANES, FILTER_HIDDEN), w0p]])
    b0 = jnp.tile(filt_b0.astype(F32), 2)[None, :]
    wi = jnp.stack([jnp.block([[w, zeros(FILTER_HIDDEN, FILTER_HIDDEN)], [zeros(FILTER_HIDDEN, FILTER_HIDDEN), w]])
                    for w in filt_w_inner.astype(F32)])
    bi = jnp.tile(filt_b_inner.astype(F32), (1, 2))[:, None, :]
    freq = jnp.tile(filt_freq.astype(F32), 2)[None, :]
    w_out = filt_w_out.astype(F32)
    wl = jnp.concatenate([jnp.pad(w_out[:, :n_cols], ((0, FILTER_HIDDEN), (0, 0))),
                          jnp.pad(w_out[:, n_cols:], ((FILTER_HIDDEN, 0), (0, 0)))], axis=1)
    full = lambda *shape: pl.BlockSpec(shape, lambda j: (0,) * len(shape))
    cast_specs, cast_shapes = _cast_plan(casts, steps)
    spectrum, *cast_out = pl.pallas_call(
        functools.partial(_filter_kernel, len(casts)),
        out_shape=[jax.ShapeDtypeStruct((FFT_SLICES, 2 * FFT_N2, n_cols), BF16)] + cast_shapes,
        grid=(steps,),
        in_specs=[
            full(seq_len, 2 * LANES), full(2, n_fft, LANES), full(2 * LANES, hid), full(1, hid),
            full(FILTER_INNER, hid, hid), full(FILTER_INNER, 1, hid), full(1, hid),
            pl.BlockSpec((hid, FILT_CW), lambda j: (0, j)),
            pl.BlockSpec((hid, FILT_CW), lambda j: (0, steps + j)),
            pl.BlockSpec((1, FILT_CW), lambda j: (0, j)),
            full(FFT_SLICES, 2 * FFT_N2, 2 * FFT_N2),
        ] + cast_specs,
        out_specs=[pl.BlockSpec((FFT_SLICES, 2 * FFT_N2, FILT_CW), lambda j: (0, 0, j))] + cast_specs,
        scratch_shapes=[pltpu.VMEM((seq_len, hid), F32), pltpu.VMEM((n_fft, FILT_CW), F32),
                        pltpu.VMEM((FFT_ROWS, FILT_CW), BF16)],
        compiler_params=pltpu.CompilerParams(dimension_semantics=("arbitrary",),
                                             vmem_limit_bytes=VMEM_LIMIT_BYTES),
        name="hyena_filter",
    )(feats_ext, t_mask, w0, b0, wi, bi, freq, wl, wl, deltas, mf, *casts)
    return spectrum, cast_out


HY_CW = 256
CONV_ROWS = 64
PAD_ROWS = 8


def _hyena_kernel(x_ref, wv_ref, w1_ref, w2_ref, bv_ref, b1_ref, b2_ref, cwv_ref, cw1_ref, cw2_ref,
                  cbv_ref, cb1_ref, cb2_ref, skip_ref, h0_ref, h1_ref, mf_ref, mi_ref,
                  out_ref, u_ref, p_ref, b_ref, c_ref):
    seq_len = x_ref.shape[0]
    width = HY_CW
    n2 = FFT_N2
    xb = x_ref[...]

    n_slabs = width // LANES
    zeros = jnp.zeros((PAD_ROWS, LANES), F32)
    for slot in range(u_ref.shape[0]):
        for j in range(n_slabs):
            u_ref[slot, j, 0:PAD_ROWS, :] = zeros
            u_ref[slot, j, PAD_ROWS + seq_len:2 * PAD_ROWS + seq_len, :] = zeros
    parts = ((wv_ref, bv_ref, cwv_ref, cbv_ref), (w1_ref, b1_ref, cw1_ref, cb1_ref),
             (w2_ref, b2_ref, cw2_ref, cb2_ref))

    def project(p, slot):
        w_ref, bias_ref, _, _ = parts[p]
        half = seq_len // 2
        for r0 in (0, half):
            u = _dot(xb[r0:r0 + half], w_ref[...]) + bias_ref[...]
            for j in range(n_slabs):
                u_ref[slot, j, PAD_ROWS + r0:PAD_ROWS + r0 + half, :] = u[:, j * LANES:(j + 1) * LANES]

    def short_conv(p, slot):
        _, _, cw_ref, cb_ref = parts[p]

        def conv_body(i, carry):
            start = pl.multiple_of(i * CONV_ROWS, CONV_ROWS)
            cw = cw_ref[...]
            cb = cb_ref[...]
            for j in range(n_slabs):
                lanes = slice(j * LANES, (j + 1) * LANES)
                taps = [u_ref[slot, j, pl.ds(start + PAD_ROWS - 1 + k, CONV_ROWS, stride=1), :]
                        for k in range(3)]
                p_ref[p, pl.ds(start, CONV_ROWS), lanes] = (
                    cw[0:1, lanes] * taps[0] + cw[1:2, lanes] * taps[1] + cw[2:3, lanes] * taps[2]
                    + cb[:, lanes])
            return carry
        lax.fori_loop(0, seq_len // CONV_ROWS, conv_body, 0, unroll=True)

    def spectral_product(h_ref):
        x_next = _forward_dft(b_ref, 0, mf_ref)
        for k1 in range(FFT_SLICES):
            x = x_next
            if k1 + 1 < FFT_SLICES:
                x_next = _forward_dft(b_ref, k1 + 1, mf_ref)
            xr, xi = x[:n2].astype(BF16), x[n2:].astype(BF16)
            hr = h_ref[k1, 0:n2, :]
            hi = h_ref[k1, n2:2 * n2, :]
            prod = jnp.concatenate([xr * hr - xi * hi, xr * hi + xi * hr], axis=0)
            _inverse_dft(prod, k1, mi_ref, c_ref)

    def gated_blocks(order, off):
        skip = skip_ref[order:order + 1, :]
        ys = _inverse_radix16_stage(_load_slices(c_ref, off, A_CHUNK))
        out = []
        for t1, y in enumerate(ys):
            rows = pl.ds(t1 * n2 + off, A_CHUNK)
            out.append(p_ref[order + 1, rows, :] * (y + skip * p_ref[0, rows, :]))
        return out

    project(0, 0)
    short_conv(0, 0)

    def a_body(i, carry):
        off = pl.multiple_of(i * A_CHUNK, A_CHUNK)
        blk = [p_ref[0, pl.ds(t1 * n2 + off, A_CHUNK), :] for t1 in range(FFT_N1 // 2)]
        _store_slices(b_ref, off, A_CHUNK, _radix16_stage(blk))
        return carry
    lax.fori_loop(0, n2 // A_CHUNK, a_body, 0, unroll=True)

    project(1, 0)
    project(2, 1)
    spectral_product(h0_ref)
    short_conv(1, 0)
    short_conv(2, 1)

    def mid_body(i, carry):
        off = pl.multiple_of(i * A_CHUNK, A_CHUNK)
        z = gated_blocks(0, off)
        for t1, blk in enumerate(z):
            p_ref[0, pl.ds(t1 * n2 + off, A_CHUNK), :] = blk
        _store_slices(b_ref, off, A_CHUNK, _radix16_stage(z))
        return carry
    lax.fori_loop(0, n2 // A_CHUNK, mid_body, 0)

    spectral_product(h1_ref)

    def out_body(i, carry):
        off = pl.multiple_of(i * A_CHUNK, A_CHUNK)
        for t1, blk in enumerate(gated_blocks(1, off)):
            out_ref[pl.ds(t1 * n2 + off, A_CHUNK), :] = blk.astype(out_ref.dtype)
        return carry
    lax.fori_loop(0, n2 // A_CHUNK, out_body, 0)


def _hyena_branch(xb, w_in, b_in, col0, conv_w, conv_b, hyena_skip, spectrum):
    bsz, seq_len, d_model = xb.shape
    width = hyena_skip.shape[-1]
    nblk = width // HY_CW
    mf, mi = (jnp.asarray(m).astype(BF16) for m in _fft_constants(seq_len))
    col = lambda part: (lambda c, b: (0, part * nblk + c))
    blk0, rem = divmod(col0, HY_CW)
    assert rem == 0
    in_col = lambda part: (lambda c, b: (0, blk0 + part * nblk + c))
    const = lambda *shape: pl.BlockSpec(shape, lambda c, b: (0,) * len(shape), pipeline_mode=pl.Buffered(1))
    spec_h = lambda order: pl.BlockSpec((FFT_SLICES, 2 * FFT_N2, HY_CW), lambda c, b: (0, 0, order * nblk + c),
                                        pipeline_mode=pl.Buffered(1))
    in_specs = [pl.BlockSpec((None, seq_len, d_model), lambda c, b: (b, 0, 0))]
    in_specs += [pl.BlockSpec((d_model, HY_CW), in_col(p)) for p in range(3)]
    in_specs += [pl.BlockSpec((1, HY_CW), in_col(p)) for p in range(3)]
    in_specs += [pl.BlockSpec((3, HY_CW), col(p)) for p in range(3)]
    in_specs += [pl.BlockSpec((1, HY_CW), col(p)) for p in range(3)]
    in_specs += [pl.BlockSpec((HYENA_ORDER, HY_CW), lambda c, b: (0, c)), spec_h(0), spec_h(1),
                 const(FFT_SLICES, 2 * FFT_N2, 2 * FFT_N2), const(FFT_SLICES, 2 * FFT_N2, 2 * FFT_N2)]
    return pl.pallas_call(
        _hyena_kernel,
        out_shape=jax.ShapeDtypeStruct((bsz, seq_len, width), BF16),
        grid=(nblk, bsz),
        in_specs=in_specs,
        out_specs=pl.BlockSpec((None, seq_len, HY_CW), lambda c, b: (b, 0, c)),
        scratch_shapes=[pltpu.VMEM((2, HY_CW // LANES, seq_len + 2 * PAD_ROWS, LANES), F32),
                        pltpu.VMEM((3, seq_len, HY_CW), F32),
                        pltpu.VMEM((FFT_ROWS, HY_CW), BF16),
                        pltpu.VMEM((FFT_ROWS, HY_CW), F32)],
        compiler_params=pltpu.CompilerParams(dimension_semantics=("arbitrary", "arbitrary"),
                                             vmem_limit_bytes=VMEM_LIMIT_BYTES),
        name="hyena_branch",
    )(xb, w_in, w_in, w_in, b_in, b_in, b_in, conv_w, conv_w, conv_w, conv_b, conv_b, conv_b,
      hyena_skip, spectrum, spectrum, mf, mi)


TOKEN_GROUP = 16
TOKEN_PITCH = TOKEN_GROUP + 1


def _padded_rows(seq_len):
    return (seq_len // TOKEN_GROUP) * TOKEN_PITCH


def _attn_kernel(x_ref, w_ref, b_ref, dist_ref, out_ref, qkv_ref, acc_ref, m_ref, l_ref,
                 s_ref, p_ref, al_ref):
    seq_len = x_ref.shape[0]
    xb = x_ref[...]
    slopes = _alibi_slopes()
    q_scale = 1.0 / math.sqrt(HEAD_DIM)
    heads = range(HEADS_PER_GROUP)

    def load(ref, lead, pieces):
        parts = [ref[(*lead, piece, slice(None))] for piece in pieces]
        return parts[0] if len(parts) == 1 else jnp.concatenate(parts, axis=0)

    def store(ref, lead, pieces, value):
        r0 = 0
        for piece in pieces:
            ref[(*lead, piece, slice(None))] = value[r0:r0 + piece.size]
            r0 += piece.size

    def stage_scores(blk, coefs):
        qr, kr, _, dist, kc = blk
        for h in heads:
            s_ref[h, :, 0:kc] = _dot_nt(load(qkv_ref, (0, h), qr).astype(BF16),
                                        load(qkv_ref, (1, h), kr).astype(BF16)) + dist * coefs[h]

    def stage_softmax(blk, first, last):
        _, _, ar, _, kc = blk
        s = [s_ref[h, :, 0:kc] for h in heads]
        m_blk = [jnp.max(s[h], axis=-1, keepdims=True) for h in heads]
        if first:
            m_new = [jnp.broadcast_to(m_blk[h], (Q_BLOCK, HEAD_DIM)) for h in heads]
        else:
            m_old = [load(m_ref, (h,), ar) for h in heads]
            m_new = [jnp.maximum(m_old[h], m_blk[h]) for h in heads]
        m_wide = [jnp.concatenate([m_new[h]] * (kc // HEAD_DIM), axis=1) for h in heads]
        for h in heads:
            p_ref[h, :, 0:kc] = jnp.exp(s[h] - m_wide[h]).astype(BF16)
            if not first:
                al_ref[h] = jnp.exp(m_old[h] - m_new[h])
            if not last:
                store(m_ref, (h,), ar, m_new[h])

    def stage_output(blk, first, last, out_rows):
        _, kr, ar, _, kc = blk
        ones = jnp.ones((kc, HEAD_DIM), BF16)
        pv = [_dot(p_ref[h, :, 0:kc], jnp.concatenate([load(qkv_ref, (2, h), kr).astype(BF16), ones], axis=1))
              for h in heads]
        acc = [r[:, :HEAD_DIM] for r in pv]
        l_new = [r[:, HEAD_DIM:] for r in pv]
        if not first:
            alpha = [al_ref[h] for h in heads]
            l_new = [alpha[h] * load(l_ref, (h,), ar) + l_new[h] for h in heads]
            acc = [alpha[h] * load(acc_ref, (h,), ar) + acc[h] for h in heads]
        for h in heads:
            if last:
                out_ref[out_rows, h * HEAD_DIM:(h + 1) * HEAD_DIM] = (acc[h] / l_new[h]).astype(out_ref.dtype)
            else:
                store(l_ref, (h,), ar, l_new[h])
                store(acc_ref, (h,), ar, acc[h])

    order = sorted(range(N_GROUPS), key=lambda g: -ATTN_GROUPS[g][1])
    assert ATTN_GROUPS[order[-1]][1] == 1 and ATTN_GROUPS[order[0]][1] == TOKEN_GROUP
    for pos, g in enumerate(order):
        window, dil = ATTN_GROUPS[g]
        first, last = pos == 0, pos == N_GROUPS - 1
        n_off = (window // 2) // dil
        assert n_off == Q_BLOCK // 2 and TOKEN_GROUP % dil == 0
        stream = seq_len // dil
        nb = stream // Q_BLOCK
        n_blocks = dil * nb
        sub = TOKEN_GROUP // dil
        for part in range(3):
            c0 = part * ATTN_QKV_WIDTH + g * GROUP_WIDTH
            proj = _dot(xb, w_ref[:, c0:c0 + GROUP_WIDTH]) + b_ref[:, c0:c0 + GROUP_WIDTH]
            if part == 0:
                proj = proj * q_scale
            for h in heads:
                slab = proj[:, h * HEAD_DIM:(h + 1) * HEAD_DIM]
                if dil == 1:
                    qkv_ref[part, h, 0:seq_len, :] = slab
                else:
                    for i in range(seq_len // TOKEN_GROUP):
                        qkv_ref[part, h, pl.ds(TOKEN_PITCH * i, TOKEN_GROUP, stride=1), :] = (
                            slab[TOKEN_GROUP * i:TOKEN_GROUP * (i + 1)])
        coefs = [-slopes[g * HEADS_PER_GROUP + h] * dil for h in heads]

        def block(i, dil=dil, nb=nb, stream=stream, sub=sub):
            def padded(c, first_pos, count):
                return [pl.ds(c + dil * r + TOKEN_PITCH * (first_pos // sub), count // sub, stride=TOKEN_PITCH)
                        for r in range(sub)]
            c = i // nb
            n = i % nb
            if nb == 1:
                rows = padded(i, 0, Q_BLOCK)
                return rows, rows, rows, dist_ref[3, :, 0:Q_BLOCK], Q_BLOCK, None
            lo = jnp.clip(n * Q_BLOCK - n_off, 0, stream - 2 * Q_BLOCK)
            variant = jnp.where(n == 0, 0, jnp.where(n == nb - 1, 2, 1))
            if dil > 1:
                rows = padded(c, n * Q_BLOCK, Q_BLOCK)
                return rows, padded(c, lo, 2 * Q_BLOCK), rows, dist_ref[4 + variant], 2 * Q_BLOCK, None
            q0 = pl.multiple_of(n * Q_BLOCK, Q_BLOCK)
            acc_rows = [pl.ds(TOKEN_PITCH * (n * (Q_BLOCK // TOKEN_GROUP) + j), TOKEN_GROUP, stride=1)
                        for j in range(Q_BLOCK // TOKEN_GROUP)]
            return ([pl.ds(q0, Q_BLOCK)], [pl.ds(pl.multiple_of(lo, n_off), 2 * Q_BLOCK)], acc_rows,
                    dist_ref[variant], 2 * Q_BLOCK, pl.ds(q0, Q_BLOCK))

        def steady(i, carry, block=block, coefs=coefs, first=first, last=last):
            old = block(i - 2)
            stage_output(old[:5], first, last, old[5])
            stage_softmax(block(i - 1)[:5], first, last)
            stage_scores(block(i)[:5], coefs)
            return carry

        assert n_blocks >= 3
        stage_scores(block(0)[:5], coefs)
        stage_softmax(block(0)[:5], first, last)
        stage_scores(block(1)[:5], coefs)
        lax.fori_loop(2, n_blocks, steady, 0)
        for i in (n_blocks - 2, n_blocks - 1):
            if i == n_blocks - 1:
                stage_softmax(block(i)[:5], first, last)
            blk = block(i)
            stage_output(blk[:5], first, last, blk[5])


def _attention_branch(xb, w_qkv, b_qkv):
    bsz, seq_len, d_model = xb.shape
    dist = _attn_distance_tiles()
    qkv_w = 3 * ATTN_QKV_WIDTH
    qkv_rows = _padded_rows(seq_len)
    return pl.pallas_call(
        _attn_kernel,
        out_shape=jax.ShapeDtypeStruct((bsz, seq_len, GROUP_WIDTH), BF16),
        grid=(bsz,),
        in_specs=[pl.BlockSpec((None, seq_len, d_model), lambda b: (b, 0, 0)),
                  pl.BlockSpec((d_model, qkv_w), lambda b: (0, 0), pipeline_mode=pl.Buffered(1)),
                  pl.BlockSpec((1, qkv_w), lambda b: (0, 0)),
                  pl.BlockSpec(dist.shape, lambda b: (0, 0, 0))],
        out_specs=pl.BlockSpec((None, seq_len, GROUP_WIDTH), lambda b: (b, 0, 0)),
        scratch_shapes=[pltpu.VMEM((3, HEADS_PER_GROUP, qkv_rows, HEAD_DIM), F32),
                        pltpu.VMEM((HEADS_PER_GROUP, qkv_rows, HEAD_DIM), F32),
                        pltpu.VMEM((HEADS_PER_GROUP, qkv_rows, HEAD_DIM), F32),
                        pltpu.VMEM((HEADS_PER_GROUP, qkv_rows, HEAD_DIM), F32),
                        pltpu.VMEM((HEADS_PER_GROUP, Q_BLOCK, 2 * Q_BLOCK), F32),
                        pltpu.VMEM((HEADS_PER_GROUP, Q_BLOCK, 2 * Q_BLOCK), BF16),
                        pltpu.VMEM((HEADS_PER_GROUP, Q_BLOCK, HEAD_DIM), F32)],
        compiler_params=pltpu.CompilerParams(dimension_semantics=("arbitrary",),
                                             vmem_limit_bytes=VMEM_LIMIT_BYTES),
        name="dilated_attention",
    )(xb, w_qkv, b_qkv, dist)


ROW_TILE = 1024
SUB_ROWS = 512


def _layer_norm(r, g, b):
    mu = jnp.mean(r, axis=-1, keepdims=True)
    d = r - mu
    var = jnp.mean(d * d, axis=-1, keepdims=True)
    return d * lax.rsqrt(var + LN_EPS) * g + b


def _sub_tiles(ref):
    return [slice(s * SUB_ROWS, (s + 1) * SUB_ROWS) for s in range(ref.shape[0] // SUB_ROWS)]


def _merge_kernel(alpha, n_gate, x_ref, ya_ref, yh_ref, *rest):
    wg_refs, (bg_ref, wa_ref, wh_ref, wo_ref, g_ref, b_ref, out_ref) = rest[:n_gate], rest[n_gate:]
    d_model = x_ref.shape[-1]
    gw = wg_refs[0].shape[1]
    tiles = _sub_tiles(x_ref)
    x = [x_ref[r, :] for r in tiles]
    xb = [xs.astype(BF16) for xs in x]
    gates = [jnp.concatenate([jax.nn.sigmoid(_dot(xs, w[...]) + bg_ref[:, j * gw:(j + 1) * gw])
                              for j, w in enumerate(wg_refs)], axis=1) for xs in xb]
    merged = [gt[:, :d_model] * _dot(ya_ref[r, :], wa_ref[...])
              + gt[:, d_model:] * _dot(yh_ref[r, :], wh_ref[...]) for gt, r in zip(gates, tiles)]
    mix = [_dot(mg.astype(BF16), wo_ref[...]) for mg in merged]
    for r, xs, mx in zip(tiles, x, mix):
        out_ref[r, :] = _layer_norm(alpha * xs + mx, g_ref[...], b_ref[...])


def _ffn_kernel(alpha, h_ref, w1_ref, b1_ref, w2_ref, b2_ref, g_ref, b_ref, out_ref):
    tiles = _sub_tiles(h_ref)
    h = [h_ref[r, :] for r in tiles]
    hid = [jnp.maximum(_dot(hs.astype(BF16), w1_ref[...]) + b1_ref[...], 0.0) for hs in h]
    ff = [_dot((hd * hd).astype(BF16), w2_ref[...]) + b2_ref[...] for hd in hid]
    for r, hs, fs in zip(tiles, h, ff):
        out_ref[r, :] = _layer_norm(alpha * hs + fs, g_ref[...], b_ref[...])


def _row_tiled_call(kernel_fn, name, rows, d_model, tiled, resident, casts=()):
    steps = rows // ROW_TILE
    n_main = len(tiled) + len(resident)
    cast_specs, cast_shapes = _cast_plan(casts, steps)

    def resident_spec(item):
        if isinstance(item, tuple):
            a, width, j = item
            return pl.BlockSpec((a.shape[0], width), lambda i: (0, j), pipeline_mode=pl.Buffered(1))
        return pl.BlockSpec(item.shape, lambda i: (0, 0), pipeline_mode=pl.Buffered(1))

    def body(*refs):
        cast_in, (out_ref, *cast_out) = refs[n_main:n_main + len(casts)], refs[n_main + len(casts):]
        _cast_blocks(cast_in, cast_out)
        kernel_fn(*refs[:n_main], out_ref)

    in_specs = [pl.BlockSpec((ROW_TILE, a.shape[1]), lambda i: (i, 0)) for a in tiled]
    in_specs += [resident_spec(item) for item in resident]
    resident = [item[0] if isinstance(item, tuple) else item for item in resident]
    out, *cast_out = pl.pallas_call(
        body,
        out_shape=[jax.ShapeDtypeStruct((rows, d_model), F32)] + cast_shapes,
        grid=(steps,),
        in_specs=in_specs + cast_specs,
        out_specs=[pl.BlockSpec((ROW_TILE, d_model), lambda i: (i, 0))] + cast_specs,
        compiler_params=pltpu.CompilerParams(dimension_semantics=("arbitrary",),
                                             vmem_limit_bytes=VMEM_LIMIT_BYTES),
        name=name,
    )(*tiled, *resident, *casts)
    return out, cast_out


def kernel(x, w_in, b_in, conv_w, conv_b, filt_w0, filt_b0, filt_w_inner, filt_b_inner, filt_w_out,
           filt_freq, hyena_skip, w_branch_attn, w_branch_hyena, w_out, ln1_g, ln1_b, w_ff1, b_ff1,
           w_ff2, b_ff2, ln2_g, ln2_b):
    bsz, seq_len, d_model = x.shape
    depth = w_in.shape[0]
    alpha = (2 * depth) ** 0.25
    rows = bsz * seq_len
    qkv_w = 3 * ATTN_QKV_WIDTH
    hy_w = (HYENA_ORDER + 1) * hyena_skip.shape[-1]
    row = lambda a: a.astype(F32)[None, :]
    h = x
    for layer in range(depth):
        w_l = w_in[layer].astype(BF16)
        b_l = b_in[layer].astype(F32)[None, :]
        h_rows = h.reshape(rows, d_model).astype(F32)
        spectrum, (hb, wa_l, wh_l, wo_l) = _filter_spectrum(
            seq_len, hyena_skip.shape[-1], filt_w0[layer], filt_b0[layer], filt_w_inner[layer],
            filt_b_inner[layer], filt_w_out[layer], filt_freq[layer],
            casts=[h_rows, w_branch_attn[layer].astype(F32), w_branch_hyena[layer].astype(F32),
                   w_out[layer].astype(F32)])
        hb = hb.reshape(bsz, seq_len, d_model)
        y_attn = _attention_branch(hb, w_l, b_l)
        y_hyena = _hyena_branch(hb, w_l, b_l, qkv_w,
                                conv_w[layer].astype(F32), conv_b[layer].astype(F32)[None, :],
                                hyena_skip[layer].astype(F32), spectrum)
        gate0 = qkv_w + hy_w
        gate_w = math.gcd(gate0, 2 * d_model)
        gate_blocks = [(w_l, gate_w, gate0 // gate_w + j) for j in range(2 * d_model // gate_w)]
        h1, (w1_l, w2_l) = _row_tiled_call(
            functools.partial(_merge_kernel, alpha, len(gate_blocks)), "merge_ln", rows, d_model,
            [h_rows, y_attn.reshape(rows, -1), y_hyena.reshape(rows, -1)],
            gate_blocks + [b_l[:, gate0:], wa_l, wh_l, wo_l, row(ln1_g[layer]), row(ln1_b[layer])],
            casts=[w_ff1[layer].astype(F32), w_ff2[layer].astype(F32)])
        h2, _ = _row_tiled_call(
            functools.partial(_ffn_kernel, alpha), "ffn_ln", rows, d_model, [h1],
            [w1_l, row(b_ff1[layer]), w2_l, row(b_ff2[layer]), row(ln2_g[layer]), row(ln2_b[layer])])
        h = h2.reshape(bsz, seq_len, d_model)
    return h
```

```python
import functools
import math

import jax
import jax.numpy as jnp
import numpy as np
from jax import lax
from jax.experimental import pallas as pl
from jax.experimental.pallas import tpu as pltpu

F32 = jnp.float32
BF16 = jnp.bfloat16

ATTN_GROUPS = ((128, 1), (512, 4), (2048, 16))
N_GROUPS = len(ATTN_GROUPS)
HEADS_PER_GROUP = 4
HEAD_DIM = 128
GROUP_WIDTH = HEADS_PER_GROUP * HEAD_DIM
ATTN_QKV_WIDTH = N_GROUPS * GROUP_WIDTH
Q_BLOCK = 128
ALIBI_MAX_EXP = 8.0
HYENA_ORDER = 2
FILTER_BANDS = 16
FILTER_EMB = 1 + 2 * FILTER_BANDS
FILTER_HIDDEN = 64
FILTER_INNER = 2
DECAY_TARGET = 1e-2
FAST_DECAY_PCT = 0.3
SLOW_DECAY_PCT = 1.5
LN_EPS = 1e-5

LANES = 128
SUBLANES = 8
BF16_ROWS = 2 * SUBLANES
VMEM_LIMIT_BYTES = 58 * 1024 * 1024

FFT_N1 = 16
FFT_N2 = 256
FFT_SLICES = FFT_N1 // 2 + 1
FFT_ROWS = FFT_N1 * FFT_N2
RSQRT2 = 1.0 / math.sqrt(2.0)


def _unit_root(k, n):
    snap = lambda v: float(round(v)) if abs(v - round(v)) < 1e-12 else v
    return snap(math.cos(2.0 * math.pi * k / n)), snap(-math.sin(2.0 * math.pi * k / n))


W16 = tuple(_unit_root(k, FFT_N1) for k in range(FFT_SLICES))


def _dot(a, b):
    return jnp.dot(a, b, preferred_element_type=F32)


def _split_bf16(a):
    hi = a.astype(BF16)
    return hi, (a - hi.astype(F32)).astype(BF16)


def _dot3(a, b):
    a_hi, a_lo = _split_bf16(a)
    b_hi, b_lo = _split_bf16(b)
    return (_dot(jnp.concatenate([a_hi, a_lo], axis=1), jnp.concatenate([b_hi, b_hi], axis=0))
            + _dot(a_hi, b_lo))


def _dot_nt(a, b):
    return lax.dot_general(a, b, (((1,), (1,)), ((), ())), preferred_element_type=F32)


CAST_ROWS = 256


def _cast_plan(arrays, steps):
    specs, shapes = [], []
    for a in arrays:
        rows, rem = divmod(a.shape[0], steps)
        assert rem == 0 and rows % BF16_ROWS == 0, a.shape
        specs.append(pl.BlockSpec((rows, a.shape[1]), lambda i: (i, 0)))
        shapes.append(jax.ShapeDtypeStruct(a.shape, BF16))
    return specs, shapes


def _cast_blocks(src_refs, dst_refs):
    for src, dst in zip(src_refs, dst_refs):
        chunk = min(src.shape[0], CAST_ROWS)
        assert src.shape[0] % chunk == 0

        def body(i, carry, src=src, dst=dst, chunk=chunk):
            rows = pl.ds(pl.multiple_of(i * chunk, chunk), chunk)
            dst[rows, :] = src[rows, :].astype(dst.dtype)
            return carry
        lax.fori_loop(0, src.shape[0] // chunk, body, 0)


@functools.lru_cache(maxsize=None)
def _fft_constants(seq_len):
    n_fft = 2 * seq_len
    assert n_fft == FFT_N1 * FFT_N2
    k2 = np.arange(FFT_N2)
    fwd = []
    for k1 in range(FFT_SLICES):
        ang = -2.0 * np.pi * ((np.outer(FFT_N1 * k2 + k1, k2) % n_fft) / n_fft)
        er, ei = np.cos(ang), np.sin(ang)
        fwd.append(np.block([[er, -ei], [ei, er]]))
    fwd = np.stack(fwd, axis=0).astype(np.float32)
    inv = np.ascontiguousarray(np.transpose(fwd, (0, 2, 1)))
    return fwd, inv


@functools.lru_cache(maxsize=None)
def _filter_constants(seq_len, width):
    n_fft = 2 * seq_len
    t = np.linspace(0.0, 1.0, seq_len)
    bands = np.linspace(1e-4, FILTER_BANDS - 1, FILTER_BANDS)
    ang = (2.0 * np.pi / seq_len) * np.arange(seq_len)[:, None] * bands
    feats = np.concatenate([t[:, None], np.cos(ang), -np.sin(ang)], axis=-1)
    src = np.concatenate([np.arange(seq_len), [0], np.arange(seq_len - 1, 0, -1)])
    feats_ext = np.zeros((seq_len, 2 * LANES), np.float32)
    feats_ext[:, :FILTER_EMB] = feats[src[:seq_len]]
    feats_ext[:, LANES:LANES + FILTER_EMB] = feats[src[seq_len:]]
    t_mask = np.zeros((2, n_fft, LANES), np.float32)
    t_mask[0] = t[src][:, None]
    t_mask[1] = 1.0
    t_mask[1, seq_len] = 0.0
    deltas = np.abs(np.linspace(math.log(DECAY_TARGET) / SLOW_DECAY_PCT,
                                math.log(DECAY_TARGET) / FAST_DECAY_PCT, width))
    deltas = np.tile(deltas[None, :], (1, HYENA_ORDER)).astype(np.float32)
    return feats_ext, t_mask, deltas


@functools.lru_cache(maxsize=None)
def _attn_distance_tiles():
    n_off = Q_BLOCK // 2
    qi = np.arange(Q_BLOCK)[:, None]
    kj = np.arange(2 * Q_BLOCK)[None, :]
    tiles = []
    for shift in (0, n_off, 2 * n_off):
        off = np.abs(kj - shift - qi).astype(np.float32)
        tiles.append(np.where(off <= n_off, off, np.inf))
    off = np.abs(kj - qi).astype(np.float32)
    t3 = np.where(off <= n_off, off, np.inf)
    t3[:, Q_BLOCK:] = np.inf
    tiles.append(t3)
    def permute(n):
        return (4 * np.arange(n // 4)[None, :] + np.arange(4)[:, None]).reshape(-1)
    rows, cols = permute(Q_BLOCK), permute(2 * Q_BLOCK)
    tiles += [tiles[v][rows][:, cols] for v in range(3)]
    return np.stack(tiles, axis=0).astype(np.float32)


def _alibi_slopes():
    n = N_GROUPS * HEADS_PER_GROUP
    return [2.0 ** (-ALIBI_MAX_EXP * j / n) for j in range(1, n + 1)]


def _radix8_half(z0, z1, z2, z3):
    s02, d02 = z0 + z2, z0 - z2
    s13, d13 = z1 + z3, z1 - z3
    ss, dd = s13 * RSQRT2, d13 * RSQRT2
    return {0: (s02 + s13, None), 4: (s02 - s13, None), 2: (d02, -d13),
            1: (z0 + dd, -z2 - ss), 3: (z0 - dd, z2 - ss)}


def _add(a, b, sign=1.0):
    if b is None:
        return a
    if a is None:
        return b if sign > 0 else -b
    return a + b if sign > 0 else a - b


def _cmul_const(re, im, wr, wi):
    def scaled(v, w):
        if v is None or w == 0.0:
            return None
        return v if w == 1.0 else (-v if w == -1.0 else v * w)
    if re is not None and im is not None and wr != 0.0 and abs(wr) == abs(wi):
        if wi == wr:
            return (re - im) * wr, (re + im) * wr
        return (re + im) * wr, (im - re) * wr
    return _add(scaled(re, wr), scaled(im, wi), -1.0), _add(scaled(re, wi), scaled(im, wr))


def _radix16_stage(blocks):
    even = _radix8_half(*blocks[0::2])
    odd = _radix8_half(*blocks[1::2])
    half = FFT_N1 // 2
    out = [None] * FFT_SLICES
    out[0] = (even[0][0] + odd[0][0], None)
    out[half] = (even[0][0] - odd[0][0], None)
    out[half // 2] = (even[4][0], -odd[4][0])
    for k in range(1, half // 2):
        (er, ei), (pr, pi) = even[k], _cmul_const(*odd[k], *W16[k])
        out[k] = (er + pr, ei + pi)
        out[half - k] = (er - pr, pi - ei)
    return out


def _assemble8(c0, c4, c1, c2, c3):
    (c1r, c1i), (c2r, c2i), (c3r, c3i) = c1, c2, c3
    e, o = c0 + c4, c0 - c4
    return [e + c1r + c2r + c3r,
            o + (c1r - c1i - c3r - c3i) * RSQRT2 - c2i,
            e - c1i - c2r + c3i,
            o + (c3r - c3i - c1r - c1i) * RSQRT2 + c2i]


def _inverse_radix16_stage(c):
    half = FFT_N1 // 2
    plus = [(c[k][0] + c[half - k][0], c[k][1] - c[half - k][1]) for k in range(1, half // 2)]
    minus = [_cmul_const(c[k][0] - c[half - k][0], c[k][1] + c[half - k][1], W16[k][0], -W16[k][1])
             for k in range(1, half // 2)]
    mid = c[half // 2]
    even = _assemble8(c[0][0] + c[half][0], mid[0], *plus)
    odd = _assemble8(c[0][0] - c[half][0], -mid[1], *minus)
    return [blk for pair_ in zip(even, odd) for blk in pair_]


def _slice_rows(k1):
    if k1 == 0:
        return 0, FFT_N2
    if k1 == FFT_N1 // 2:
        return FFT_N2, FFT_N2
    return 2 * FFT_N2 * k1, 2 * FFT_N2


def _store_slices(ref, off, rows, values, scale=None):
    for k1, (re, im) in enumerate(values):
        r0, n = _slice_rows(k1)
        parts = (re,) if n == FFT_N2 else (re, im)
        for j, part in enumerate(parts):
            if scale is not None:
                part = part * scale
            ref[pl.ds(r0 + j * FFT_N2 + off, rows), :] = part.astype(ref.dtype)


def _load_slices(ref, off, rows):
    out = []
    for k1 in range(FFT_SLICES):
        r0, n = _slice_rows(k1)
        re = ref[pl.ds(r0 + off, rows), :]
        out.append((re, ref[pl.ds(r0 + FFT_N2 + off, rows), :] if n > FFT_N2 else None))
    return out


def _forward_dft(b_ref, k1, mf_ref):
    r0, n = _slice_rows(k1)
    b = b_ref[r0:r0 + n, :]
    return jnp.concatenate([_dot(mf_ref[k1, h0:h0 + FFT_N2, 0:n], b) for h0 in (0, FFT_N2)], axis=0)


def _inverse_dft(p, k1, mi_ref, c_ref):
    r0, n = _slice_rows(k1)
    for h0 in range(0, n, FFT_N2):
        c_ref[r0 + h0:r0 + h0 + FFT_N2, :] = _dot(mi_ref[k1, h0:h0 + FFT_N2, :], p)


FILT_CW = 256
FILT_ROWS = 256
A_CHUNK = 16


def _filter_kernel(n_cast, feats_ref, tmask_ref, w0_ref, b0_ref, wi_ref, bi_ref, freq_ref, wf_ref, wb_ref,
                   delta_ref, mf_ref, *rest):
    cast_in, (h_ref, *cast_out), (hid_ref, ts_ref, b_ref) = (
        rest[:n_cast], rest[n_cast:2 * n_cast + 1], rest[2 * n_cast + 1:])
    _cast_blocks(cast_in, cast_out)
    seq_len = feats_ref.shape[0]
    n_fft = 2 * seq_len

    @pl.when(pl.program_id(0) == 0)
    def _():
        def body(i, carry):
            rows = pl.ds(pl.multiple_of(i * FILT_ROWS, FILT_ROWS), FILT_ROWS)
            freq = freq_ref[...]
            hid = jnp.sin(freq * (_dot3(feats_ref[rows, :], w0_ref[...]) + b0_ref[...]))
            for layer in range(FILTER_INNER):
                hid = jnp.sin(freq * (_dot3(hid, wi_ref[layer]) + bi_ref[layer]))
            hid_ref[rows, :] = hid
            return carry
        lax.fori_loop(0, seq_len // FILT_ROWS, body, 0)

    h_hi, h_lo = _split_bf16(hid_ref[...])
    h_cat = jnp.concatenate([h_hi, h_lo], axis=1)
    for half, w_ref in enumerate((wf_ref, wb_ref)):
        w_hi, w_lo = _split_bf16(w_ref[...])
        ts_ref[half * seq_len:(half + 1) * seq_len, :] = (
            _dot(h_cat, jnp.concatenate([w_hi, w_hi], axis=0)) + _dot(h_hi, w_lo))

    def window_body(i, ssq):
        rows = pl.ds(pl.multiple_of(i * FILT_ROWS, FILT_ROWS), FILT_ROWS)
        widen = lambda a: jnp.concatenate([a] * (FILT_CW // LANES), axis=1)
        val = (ts_ref[rows, :] * jnp.exp(-widen(tmask_ref[0, rows, :]) * delta_ref[...])
               * widen(tmask_ref[1, rows, :]))
        ts_ref[rows, :] = val
        return ssq + jnp.sum(val * val, axis=0, keepdims=True)
    ssq = lax.fori_loop(0, n_fft // FILT_ROWS, window_body, jnp.zeros((1, FILT_CW), F32))
    scale = lax.rsqrt(ssq)

    def a_body(i, carry):
        off = pl.multiple_of(i * A_CHUNK, A_CHUNK)
        blk = [ts_ref[pl.ds(t1 * FFT_N2 + off, A_CHUNK), :] for t1 in range(FFT_N1)]
        lo = _radix16_stage(blk[:FFT_N1 // 2])
        hi = _radix16_stage(blk[FFT_N1 // 2:])
        vals = [(_add(lo[k][0], hi[k][0], 1.0 if k % 2 == 0 else -1.0),
                 _add(lo[k][1], hi[k][1], 1.0 if k % 2 == 0 else -1.0)) for k in range(FFT_SLICES)]
        _store_slices(b_ref, off, A_CHUNK, vals, scale)
        return carry
    lax.fori_loop(0, FFT_N2 // A_CHUNK, a_body, 0)

    for k1 in range(FFT_SLICES):
        s = (1.0 if k1 in (0, FFT_N1 // 2) else 2.0) / n_fft
        h_ref[k1] = (_forward_dft(b_ref, k1, mf_ref) * s).astype(h_ref.dtype)


def _filter_spectrum(seq_len, width, filt_w0, filt_b0, filt_w_inner, filt_b_inner, filt_w_out, filt_freq,
                     casts=()):
    n_fft = 2 * seq_len
    hid = LANES
    feats_ext, t_mask, deltas = _filter_constants(seq_len, width)
    mf = jnp.asarray(_fft_constants(seq_len)[0]).astype(BF16)
    assert 2 * FILTER_HIDDEN == hid and FILTER_EMB <= LANES
    n_cols = HYENA_ORDER * width
    steps = n_cols // FILT_CW
    zeros = lambda r, c: jnp.zeros((r, c), F32)
    w0p = jnp.pad(filt_w0.astype(F32), ((0, LANES - FILTER_EMB), (0, 0)))
    w0 = jnp.block([[w0p, zeros(LANES, FILTER_HIDDEN)], [zeros(LANES, FILTER_HIDDEN), w0p]])
    b0 = jnp.tile(filt_b0.astype(F32), 2)[None, :]
    wi = jnp.stack([jnp.block([[w, zeros(FILTER_HIDDEN, FILTER_HIDDEN)], [zeros(FILTER_HIDDEN, FILTER_HIDDEN), w]])
                    for w in filt_w_inner.astype(F32)])
    bi = jnp.tile(filt_b_inner.astype(F32), (1, 2))[:, None, :]
    freq = jnp.tile(filt_freq.astype(F32), 2)[None, :]
    w_out = filt_w_out.astype(F32)
    wl = jnp.concatenate([jnp.pad(w_out[:, :n_cols], ((0, FILTER_HIDDEN), (0, 0))),
                          jnp.pad(w_out[:, n_cols:], ((FILTER_HIDDEN, 0), (0, 0)))], axis=1)
    full = lambda *shape: pl.BlockSpec(shape, lambda j: (0,) * len(shape))
    cast_specs, cast_shapes = _cast_plan(casts, steps)
    spectrum, *cast_out = pl.pallas_call(
        functools.partial(_filter_kernel, len(casts)),
        out_shape=[jax.ShapeDtypeStruct((FFT_SLICES, 2 * FFT_N2, n_cols), BF16)] + cast_shapes,
        grid=(steps,),
        in_specs=[
            full(seq_len, 2 * LANES), full(2, n_fft, LANES), full(2 * LANES, hid), full(1, hid),
            full(FILTER_INNER, hid, hid), full(FILTER_INNER, 1, hid), full(1, hid),
            pl.BlockSpec((hid, FILT_CW), lambda j: (0, j)),
            pl.BlockSpec((hid, FILT_CW), lambda j: (0, steps + j)),
            pl.BlockSpec((1, FILT_CW), lambda j: (0, j)),
            full(FFT_SLICES, 2 * FFT_N2, 2 * FFT_N2),
        ] + cast_specs,
        out_specs=[pl.BlockSpec((FFT_SLICES, 2 * FFT_N2, FILT_CW), lambda j: (0, 0, j))] + cast_specs,
        scratch_shapes=[pltpu.VMEM((seq_len, hid), F32), pltpu.VMEM((n_fft, FILT_CW), F32),
                        pltpu.VMEM((FFT_ROWS, FILT_CW), BF16)],
        compiler_params=pltpu.CompilerParams(dimension_semantics=("arbitrary",),
                                             vmem_limit_bytes=VMEM_LIMIT_BYTES),
        name="hyena_filter",
    )(feats_ext, t_mask, w0, b0, wi, bi, freq, wl, wl, deltas, mf, *casts)
    return spectrum, cast_out


HY_CW = 256
CONV_ROWS = 64
PAD_ROWS = 8
PROJ_SPLIT = 4


def _hyena_kernel(x_ref, wv_ref, w1_ref, w2_ref, bv_ref, b1_ref, b2_ref, cwv_ref, cw1_ref, cw2_ref,
                  cbv_ref, cb1_ref, cb2_ref, skip_ref, h0_ref, h1_ref, mf_ref, mi_ref,
                  out_ref, u_ref, p_ref, b_ref, c_ref):
    seq_len = x_ref.shape[0]
    width = HY_CW
    n2 = FFT_N2
    xb = x_ref[...]

    n_slabs = width // LANES
    zeros = jnp.zeros((PAD_ROWS, LANES), F32)
    for slot in range(u_ref.shape[0]):
        for j in range(n_slabs):
            u_ref[slot, j, 0:PAD_ROWS, :] = zeros
            u_ref[slot, j, PAD_ROWS + seq_len:2 * PAD_ROWS + seq_len, :] = zeros
    parts = ((wv_ref, bv_ref, cwv_ref, cbv_ref), (w1_ref, b1_ref, cw1_ref, cb1_ref),
             (w2_ref, b2_ref, cw2_ref, cb2_ref))

    def project(p, slot):
        w_ref, bias_ref, _, _ = parts[p]
        half = seq_len // PROJ_SPLIT
        for r0 in range(0, seq_len, half):
            u = _dot(xb[r0:r0 + half], w_ref[...]) + bias_ref[...]
            for j in range(n_slabs):
                u_ref[slot, j, PAD_ROWS + r0:PAD_ROWS + r0 + half, :] = u[:, j * LANES:(j + 1) * LANES]

    def short_conv(p, slot):
        _, _, cw_ref, cb_ref = parts[p]

        def conv_body(i, carry):
            start = pl.multiple_of(i * CONV_ROWS, CONV_ROWS)
            cw = cw_ref[...]
            cb = cb_ref[...]
            for j in range(n_slabs):
                lanes = slice(j * LANES, (j + 1) * LANES)
                taps = [u_ref[slot, j, pl.ds(start + PAD_ROWS - 1 + k, CONV_ROWS, stride=1), :]
                        for k in range(3)]
                p_ref[p, pl.ds(start, CONV_ROWS), lanes] = (
                    cw[0:1, lanes] * taps[0] + cw[1:2, lanes] * taps[1] + cw[2:3, lanes] * taps[2]
                    + cb[:, lanes])
            return carry
        lax.fori_loop(0, seq_len // CONV_ROWS, conv_body, 0, unroll=True)

    def spectral_product(h_ref):
        x_next = _forward_dft(b_ref, 0, mf_ref)
        for k1 in range(FFT_SLICES):
            x = x_next
            if k1 + 1 < FFT_SLICES:
                x_next = _forward_dft(b_ref, k1 + 1, mf_ref)
            xr, xi = x[:n2].astype(BF16), x[n2:].astype(BF16)
            hr = h_ref[k1, 0:n2, :]
            hi = h_ref[k1, n2:2 * n2, :]
            prod = jnp.concatenate([xr * hr - xi * hi, xr * hi + xi * hr], axis=0)
            _inverse_dft(prod, k1, mi_ref, c_ref)

    def gated_blocks(order, off):
        skip = skip_ref[order:order + 1, :]
        ys = _inverse_radix16_stage(_load_slices(c_ref, off, A_CHUNK))
        out = []
        for t1, y in enumerate(ys):
            rows = pl.ds(t1 * n2 + off, A_CHUNK)
            out.append(p_ref[order + 1, rows, :] * (y + skip * p_ref[0, rows, :]))
        return out

    project(0, 0)
    short_conv(0, 0)

    def a_body(i, carry):
        off = pl.multiple_of(i * A_CHUNK, A_CHUNK)
        blk = [p_ref[0, pl.ds(t1 * n2 + off, A_CHUNK), :] for t1 in range(FFT_N1 // 2)]
        _store_slices(b_ref, off, A_CHUNK, _radix16_stage(blk))
        return carry
    lax.fori_loop(0, n2 // A_CHUNK, a_body, 0, unroll=True)

    project(1, 0)
    project(2, 1)
    spectral_product(h0_ref)
    short_conv(1, 0)
    short_conv(2, 1)

    def mid_body(i, carry):
        off = pl.multiple_of(i * A_CHUNK, A_CHUNK)
        z = gated_blocks(0, off)
        for t1, blk in enumerate(z):
            p_ref[0, pl.ds(t1 * n2 + off, A_CHUNK), :] = blk
        _store_slices(b_ref, off, A_CHUNK, _radix16_stage(z))
        return carry
    lax.fori_loop(0, n2 // A_CHUNK, mid_body, 0)

    spectral_product(h1_ref)

    def out_body(i, carry):
        off = pl.multiple_of(i * A_CHUNK, A_CHUNK)
        for t1, blk in enumerate(gated_blocks(1, off)):
            out_ref[pl.ds(t1 * n2 + off, A_CHUNK), :] = blk.astype(out_ref.dtype)
        return carry
    lax.fori_loop(0, n2 // A_CHUNK, out_body, 0)


def _hyena_branch(xb, w_in, b_in, col0, conv_w, conv_b, hyena_skip, spectrum):
    bsz, seq_len, d_model = xb.shape
    width = hyena_skip.shape[-1]
    nblk = width // HY_CW
    mf, mi = (jnp.asarray(m).astype(BF16) for m in _fft_constants(seq_len))
    col = lambda part: (lambda c, b: (0, part * nblk + c))
    blk0, rem = divmod(col0, HY_CW)
    assert rem == 0
    in_col = lambda part: (lambda c, b: (0, blk0 + part * nblk + c))
    const = lambda *shape: pl.BlockSpec(shape, lambda c, b: (0,) * len(shape), pipeline_mode=pl.Buffered(1))
    spec_h = lambda order: pl.BlockSpec((FFT_SLICES, 2 * FFT_N2, HY_CW), lambda c, b: (0, 0, order * nblk + c),
                                        pipeline_mode=pl.Buffered(1))
    in_specs = [pl.BlockSpec((None, seq_len, d_model), lambda c, b: (b, 0, 0))]
    in_specs += [pl.BlockSpec((d_model, HY_CW), in_col(p)) for p in range(3)]
    in_specs += [pl.BlockSpec((1, HY_CW), in_col(p)) for p in range(3)]
    in_specs += [pl.BlockSpec((3, HY_CW), col(p)) for p in range(3)]
    in_specs += [pl.BlockSpec((1, HY_CW), col(p)) for p in range(3)]
    in_specs += [pl.BlockSpec((HYENA_ORDER, HY_CW), lambda c, b: (0, c)), spec_h(0), spec_h(1),
                 const(FFT_SLICES, 2 * FFT_N2, 2 * FFT_N2), const(FFT_SLICES, 2 * FFT_N2, 2 * FFT_N2)]
    return pl.pallas_call(
        _hyena_kernel,
        out_shape=jax.ShapeDtypeStruct((bsz, seq_len, width), BF16),
        grid=(nblk, bsz),
        in_specs=in_specs,
        out_specs=pl.BlockSpec((None, seq_len, HY_CW), lambda c, b: (b, 0, c)),
        scratch_shapes=[pltpu.VMEM((2, HY_CW // LANES, seq_len + 2 * PAD_ROWS, LANES), F32),
                        pltpu.VMEM((3, seq_len, HY_CW), F32),
                        pltpu.VMEM((FFT_ROWS, HY_CW), BF16),
                        pltpu.VMEM((FFT_ROWS, HY_CW), F32)],
        compiler_params=pltpu.CompilerParams(dimension_semantics=("arbitrary", "arbitrary"),
                                             vmem_limit_bytes=VMEM_LIMIT_BYTES),
        name="hyena_branch",
    )(xb, w_in, w_in, w_in, b_in, b_in, b_in, conv_w, conv_w, conv_w, conv_b, conv_b, conv_b,
      hyena_skip, spectrum, spectrum, mf, mi)


TOKEN_GROUP = 16
TOKEN_PITCH = TOKEN_GROUP + 1
ATTN_PROJ_SPLIT = 2


def _padded_rows(seq_len):
    return (seq_len // TOKEN_GROUP) * TOKEN_PITCH


def _attn_kernel(x_ref, w_ref, b_ref, dist_ref, out_ref, qkv_ref, acc_ref, m_ref, l_ref,
                 s_ref, p_ref, al_ref):
    seq_len = x_ref.shape[0]
    xb = x_ref[...]
    slopes = _alibi_slopes()
    q_scale = 1.0 / math.sqrt(HEAD_DIM)
    heads = range(HEADS_PER_GROUP)

    def load(ref, lead, pieces):
        parts = [ref[(*lead, piece, slice(None))] for piece in pieces]
        return parts[0] if len(parts) == 1 else jnp.concatenate(parts, axis=0)

    def store(ref, lead, pieces, value):
        r0 = 0
        for piece in pieces:
            ref[(*lead, piece, slice(None))] = value[r0:r0 + piece.size]
            r0 += piece.size

    def stage_scores(blk, coefs):
        qr, kr, _, dist, kc = blk
        for h in heads:
            s_ref[h, :, 0:kc] = _dot_nt(load(qkv_ref, (0, h), qr).astype(BF16),
                                        load(qkv_ref, (1, h), kr).astype(BF16)) + dist * coefs[h]

    def stage_softmax(blk, first, last):
        _, _, ar, _, kc = blk
        s = [s_ref[h, :, 0:kc] for h in heads]
        m_blk = [jnp.max(s[h], axis=-1, keepdims=True) for h in heads]
        if first:
            m_new = [jnp.broadcast_to(m_blk[h], (Q_BLOCK, HEAD_DIM)) for h in heads]
        else:
            m_old = [load(m_ref, (h,), ar) for h in heads]
            m_new = [jnp.maximum(m_old[h], m_blk[h]) for h in heads]
        m_wide = [jnp.concatenate([m_new[h]] * (kc // HEAD_DIM), axis=1) for h in heads]
        for h in heads:
            p_ref[h, :, 0:kc] = jnp.exp(s[h] - m_wide[h]).astype(BF16)
            if not first:
                al_ref[h] = jnp.exp(m_old[h] - m_new[h])
            if not last:
                store(m_ref, (h,), ar, m_new[h])

    def stage_output(blk, first, last, out_rows):
        _, kr, ar, _, kc = blk
        ones = jnp.ones((kc, HEAD_DIM), BF16)
        pv = [_dot(p_ref[h, :, 0:kc], jnp.concatenate([load(qkv_ref, (2, h), kr).astype(BF16), ones], axis=1))
              for h in heads]
        acc = [r[:, :HEAD_DIM] for r in pv]
        l_new = [r[:, HEAD_DIM:] for r in pv]
        if not first:
            alpha = [al_ref[h] for h in heads]
            l_new = [alpha[h] * load(l_ref, (h,), ar) + l_new[h] for h in heads]
            acc = [alpha[h] * load(acc_ref, (h,), ar) + acc[h] for h in heads]
        for h in heads:
            if last:
                out_ref[out_rows, h * HEAD_DIM:(h + 1) * HEAD_DIM] = (acc[h] / l_new[h]).astype(out_ref.dtype)
            else:
                store(l_ref, (h,), ar, l_new[h])
                store(acc_ref, (h,), ar, acc[h])

    order = sorted(range(N_GROUPS), key=lambda g: -ATTN_GROUPS[g][1])
    assert ATTN_GROUPS[order[-1]][1] == 1 and ATTN_GROUPS[order[0]][1] == TOKEN_GROUP
    for pos, g in enumerate(order):
        window, dil = ATTN_GROUPS[g]
        first, last = pos == 0, pos == N_GROUPS - 1
        n_off = (window // 2) // dil
        assert n_off == Q_BLOCK // 2 and TOKEN_GROUP % dil == 0
        stream = seq_len // dil
        nb = stream // Q_BLOCK
        n_blocks = dil * nb
        sub = TOKEN_GROUP // dil
        for part in range(3):
            c0 = part * ATTN_QKV_WIDTH + g * GROUP_WIDTH
            rblk = seq_len // ATTN_PROJ_SPLIT
            for r0 in range(0, seq_len, rblk):
                proj = _dot(xb[r0:r0 + rblk], w_ref[:, c0:c0 + GROUP_WIDTH]) + b_ref[:, c0:c0 + GROUP_WIDTH]
                if part == 0:
                    proj = proj * q_scale
                for h in heads:
                    slab = proj[:, h * HEAD_DIM:(h + 1) * HEAD_DIM]
                    if dil == 1:
                        qkv_ref[part, h, r0:r0 + rblk, :] = slab
                    else:
                        for i in range(rblk // TOKEN_GROUP):
                            qkv_ref[part, h, pl.ds(TOKEN_PITCH * (r0 // TOKEN_GROUP + i), TOKEN_GROUP, stride=1), :] = (
                                slab[TOKEN_GROUP * i:TOKEN_GROUP * (i + 1)])
        coefs = [-slopes[g * HEADS_PER_GROUP + h] * dil for h in heads]

        def block(i, dil=dil, nb=nb, stream=stream, sub=sub):
            def padded(c, first_pos, count):
                return [pl.ds(c + dil * r + TOKEN_PITCH * (first_pos // sub), count // sub, stride=TOKEN_PITCH)
                        for r in range(sub)]
            c = i // nb
            n = i % nb
            if nb == 1:
                rows = padded(i, 0, Q_BLOCK)
                return rows, rows, rows, dist_ref[3, :, 0:Q_BLOCK], Q_BLOCK, None
            lo = jnp.clip(n * Q_BLOCK - n_off, 0, stream - 2 * Q_BLOCK)
            variant = jnp.where(n == 0, 0, jnp.where(n == nb - 1, 2, 1))
            if dil > 1:
                rows = padded(c, n * Q_BLOCK, Q_BLOCK)
                return rows, padded(c, lo, 2 * Q_BLOCK), rows, dist_ref[4 + variant], 2 * Q_BLOCK, None
            q0 = pl.multiple_of(n * Q_BLOCK, Q_BLOCK)
            acc_rows = [pl.ds(TOKEN_PITCH * (n * (Q_BLOCK // TOKEN_GROUP) + j), TOKEN_GROUP, stride=1)
                        for j in range(Q_BLOCK // TOKEN_GROUP)]
            return ([pl.ds(q0, Q_BLOCK)], [pl.ds(pl.multiple_of(lo, n_off), 2 * Q_BLOCK)], acc_rows,
                    dist_ref[variant], 2 * Q_BLOCK, pl.ds(q0, Q_BLOCK))

        def steady(i, carry, block=block, coefs=coefs, first=first, last=last):
            old = block(i - 2)
            stage_output(old[:5], first, last, old[5])
            stage_softmax(block(i - 1)[:5], first, last)
            stage_scores(block(i)[:5], coefs)
            return carry

        assert n_blocks >= 3
        stage_scores(block(0)[:5], coefs)
        stage_softmax(block(0)[:5], first, last)
        stage_scores(block(1)[:5], coefs)
        lax.fori_loop(2, n_blocks, steady, 0)
        for i in (n_blocks - 2, n_blocks - 1):
            if i == n_blocks - 1:
                stage_softmax(block(i)[:5], first, last)
            blk = block(i)
            stage_output(blk[:5], first, last, blk[5])


def _attention_branch(xb, w_qkv, b_qkv):
    bsz, seq_len, d_model = xb.shape
    dist = _attn_distance_tiles()
    qkv_w = 3 * ATTN_QKV_WIDTH
    qkv_rows = _padded_rows(seq_len)
    return pl.pallas_call(
        _attn_kernel,
        out_shape=jax.ShapeDtypeStruct((bsz, seq_len, GROUP_WIDTH), BF16),
        grid=(bsz,),
        in_specs=[pl.BlockSpec((None, seq_len, d_model), lambda b: (b, 0, 0)),
                  pl.BlockSpec((d_model, qkv_w), lambda b: (0, 0), pipeline_mode=pl.Buffered(1)),
                  pl.BlockSpec((1, qkv_w), lambda b: (0, 0)),
                  pl.BlockSpec(dist.shape, lambda b: (0, 0, 0))],
        out_specs=pl.BlockSpec((None, seq_len, GROUP_WIDTH), lambda b: (b, 0, 0)),
        scratch_shapes=[pltpu.VMEM((3, HEADS_PER_GROUP, qkv_rows, HEAD_DIM), F32),
                        pltpu.VMEM((HEADS_PER_GROUP, qkv_rows, HEAD_DIM), F32),
                        pltpu.VMEM((HEADS_PER_GROUP, qkv_rows, HEAD_DIM), F32),
                        pltpu.VMEM((HEADS_PER_GROUP, qkv_rows, HEAD_DIM), F32),
                        pltpu.VMEM((HEADS_PER_GROUP, Q_BLOCK, 2 * Q_BLOCK), F32),
                        pltpu.VMEM((HEADS_PER_GROUP, Q_BLOCK, 2 * Q_BLOCK), BF16),
                        pltpu.VMEM((HEADS_PER_GROUP, Q_BLOCK, HEAD_DIM), F32)],
        compiler_params=pltpu.CompilerParams(dimension_semantics=("arbitrary",),
                                             vmem_limit_bytes=VMEM_LIMIT_BYTES),
        name="dilated_attention",
    )(xb, w_qkv, b_qkv, dist)


ROW_TILE = 1024
SUB_ROWS = 256


def _layer_norm(r, g, b):
    mu = jnp.mean(r, axis=-1, keepdims=True)
    d = r - mu
    var = jnp.mean(d * d, axis=-1, keepdims=True)
    return d * lax.rsqrt(var + LN_EPS) * g + b


def _sub_tiles(ref):
    return [slice(s * SUB_ROWS, (s + 1) * SUB_ROWS) for s in range(ref.shape[0] // SUB_ROWS)]


def _merge_kernel(alpha, n_gate, x_ref, ya_ref, yh_ref, *rest):
    wg_refs, (bg_ref, wa_ref, wh_ref, wo_ref, g_ref, b_ref, out_ref) = rest[:n_gate], rest[n_gate:]
    d_model = x_ref.shape[-1]
    gw = wg_refs[0].shape[1]
    tiles = _sub_tiles(x_ref)
    x = [x_ref[r, :] for r in tiles]
    xb = [xs.astype(BF16) for xs in x]
    gates = [jnp.concatenate([jax.nn.sigmoid(_dot(xs, w[...]) + bg_ref[:, j * gw:(j + 1) * gw])
                              for j, w in enumerate(wg_refs)], axis=1) for xs in xb]
    merged = [gt[:, :d_model] * _dot(ya_ref[r, :], wa_ref[...])
              + gt[:, d_model:] * _dot(yh_ref[r, :], wh_ref[...]) for gt, r in zip(gates, tiles)]
    mix = [_dot(mg.astype(BF16), wo_ref[...]) for mg in merged]
    for r, xs, mx in zip(tiles, x, mix):
        out_ref[r, :] = _layer_norm(alpha * xs + mx, g_ref[...], b_ref[...])


def _ffn_kernel(alpha, h_ref, w1_ref, b1_ref, w2_ref, b2_ref, g_ref, b_ref, out_ref):
    tiles = _sub_tiles(h_ref)
    h = [h_ref[r, :] for r in tiles]
    hid = [jnp.maximum(_dot(hs.astype(BF16), w1_ref[...]) + b1_ref[...], 0.0) for hs in h]
    ff = [_dot((hd * hd).astype(BF16), w2_ref[...]) + b2_ref[...] for hd in hid]
    for r, hs, fs in zip(tiles, h, ff):
        out_ref[r, :] = _layer_norm(alpha * hs + fs, g_ref[...], b_ref[...])


def _row_tiled_call(kernel_fn, name, rows, d_model, tiled, resident, casts=()):
    steps = rows // ROW_TILE
    n_main = len(tiled) + len(resident)
    cast_specs, cast_shapes = _cast_plan(casts, steps)

    def resident_spec(item):
        if isinstance(item, tuple):
            a, width, j = item
            return pl.BlockSpec((a.shape[0], width), lambda i: (0, j), pipeline_mode=pl.Buffered(1))
        return pl.BlockSpec(item.shape, lambda i: (0, 0), pipeline_mode=pl.Buffered(1))

    def body(*refs):
        cast_in, (out_ref, *cast_out) = refs[n_main:n_main + len(casts)], refs[n_main + len(casts):]
        _cast_blocks(cast_in, cast_out)
        kernel_fn(*refs[:n_main], out_ref)

    in_specs = [pl.BlockSpec((ROW_TILE, a.shape[1]), lambda i: (i, 0)) for a in tiled]
    in_specs += [resident_spec(item) for item in resident]
    resident = [item[0] if isinstance(item, tuple) else item for item in resident]
    out, *cast_out = pl.pallas_call(
        body,
        out_shape=[jax.ShapeDtypeStruct((rows, d_model), F32)] + cast_shapes,
        grid=(steps,),
        in_specs=in_specs + cast_specs,
        out_specs=[pl.BlockSpec((ROW_TILE, d_model), lambda i: (i, 0))] + cast_specs,
        compiler_params=pltpu.CompilerParams(dimension_semantics=("arbitrary",),
                                             vmem_limit_bytes=VMEM_LIMIT_BYTES),
        name=name,
    )(*tiled, *resident, *casts)
    return out, cast_out


def kernel(x, w_in, b_in, conv_w, conv_b, filt_w0, filt_b0, filt_w_inner, filt_b_inner, filt_w_out,
           filt_freq, hyena_skip, w_branch_attn, w_branch_hyena, w_out, ln1_g, ln1_b, w_ff1, b_ff1,
           w_ff2, b_ff2, ln2_g, ln2_b):
    bsz, seq_len, d_model = x.shape
    depth = w_in.shape[0]
    alpha = (2 * depth) ** 0.25
    rows = bsz * seq_len
    qkv_w = 3 * ATTN_QKV_WIDTH
    hy_w = (HYENA_ORDER + 1) * hyena_skip.shape[-1]
    row = lambda a: a.astype(F32)[None, :]
    h = x
    for layer in range(depth):
        w_l = w_in[layer].astype(BF16)
        b_l = b_in[layer].astype(F32)[None, :]
        h_rows = h.reshape(rows, d_model).astype(F32)
        spectrum, (hb, wa_l, wh_l, wo_l) = _filter_spectrum(
            seq_len, hyena_skip.shape[-1], filt_w0[layer], filt_b0[layer], filt_w_inner[layer],
            filt_b_inner[layer], filt_w_out[layer], filt_freq[layer],
            casts=[h_rows, w_branch_attn[layer].astype(F32), w_branch_hyena[layer].astype(F32),
                   w_out[layer].astype(F32)])
        hb = hb.reshape(bsz, seq_len, d_model)
        y_attn = _attention_branch(hb, w_l, b_l)
        y_hyena = _hyena_branch(hb, w_l, b_l, qkv_w,
                                conv_w[layer].astype(F32), conv_b[layer].astype(F32)[None, :],
                                hyena_skip[layer].astype(F32), spectrum)
        gate0 = qkv_w + hy_w
        gate_w = math.gcd(gate0, 2 * d_model)
        gate_blocks = [(w_l, gate_w, gate0 // gate_w + j) for j in range(2 * d_model // gate_w)]
        h1, (w1_l, w2_l) = _row_tiled_call(
            functools.partial(_merge_kernel, alpha, len(gate_blocks)), "merge_ln", rows, d_model,
            [h_rows, y_attn.reshape(rows, -1), y_hyena.reshape(rows, -1)],
            gate_blocks + [b_l[:, gate0:], wa_l, wh_l, wo_l, row(ln1_g[layer]), row(ln1_b[layer])],
            casts=[w_ff1[layer].astype(F32), w_ff2[layer].astype(F32)])
        h2, _ = _row_tiled_call(
            functools.partial(_ffn_kernel, alpha), "ffn_ln", rows, d_model, [h1],
            [w1_l, row(b_ff1[layer]), w2_l, row(b_ff2[layer]), row(ln2_g[layer]), row(ln2_b[layer])])
        h = h2.reshape(bsz, seq_len, d_model)
    return h
```

```python
import functools
import math

import jax
import jax.numpy as jnp
import numpy as np
from jax import lax
from jax.experimental import pallas as pl
from jax.experimental.pallas import tpu as pltpu

F32 = jnp.float32
BF16 = jnp.bfloat16

ATTN_GROUPS = ((128, 1), (512, 4), (2048, 16))
N_GROUPS = len(ATTN_GROUPS)
HEADS_PER_GROUP = 4
HEAD_DIM = 128
GROUP_WIDTH = HEADS_PER_GROUP * HEAD_DIM
ATTN_QKV_WIDTH = N_GROUPS * GROUP_WIDTH
Q_BLOCK = 128
ALIBI_MAX_EXP = 8.0
HYENA_ORDER = 2
FILTER_BANDS = 16
FILTER_EMB = 1 + 2 * FILTER_BANDS
FILTER_HIDDEN = 64
FILTER_INNER = 2
DECAY_TARGET = 1e-2
FAST_DECAY_PCT = 0.3
SLOW_DECAY_PCT = 1.5
LN_EPS = 1e-5

LANES = 128
SUBLANES = 8
BF16_ROWS = 2 * SUBLANES
VMEM_LIMIT_BYTES = 58 * 1024 * 1024

FFT_N1 = 16
FFT_N2 = 256
FFT_SLICES = FFT_N1 // 2 + 1
FFT_ROWS = FFT_N1 * FFT_N2
RSQRT2 = 1.0 / math.sqrt(2.0)


def _unit_root(k, n):
    snap = lambda v: float(round(v)) if abs(v - round(v)) < 1e-12 else v
    return snap(math.cos(2.0 * math.pi * k / n)), snap(-math.sin(2.0 * math.pi * k / n))


W16 = tuple(_unit_root(k, FFT_N1) for k in range(FFT_SLICES))


def _dot(a, b):
    return jnp.dot(a, b, preferred_element_type=F32)


def _split_bf16(a):
    hi = a.astype(BF16)
    return hi, (a - hi.astype(F32)).astype(BF16)


def _dot3(a, b):
    a_hi, a_lo = _split_bf16(a)
    b_hi, b_lo = _split_bf16(b)
    return (_dot(jnp.concatenate([a_hi, a_lo], axis=1), jnp.concatenate([b_hi, b_hi], axis=0))
            + _dot(a_hi, b_lo))


def _dot_nt(a, b):
    return lax.dot_general(a, b, (((1,), (1,)), ((), ())), preferred_element_type=F32)


CAST_ROWS = 256


def _cast_plan(arrays, steps):
    specs, shapes = [], []
    for a in arrays:
        rows, rem = divmod(a.shape[0], steps)
        assert rem == 0 and rows % BF16_ROWS == 0, a.shape
        specs.append(pl.BlockSpec((rows, a.shape[1]), lambda i: (i, 0)))
        shapes.append(jax.ShapeDtypeStruct(a.shape, BF16))
    return specs, shapes


def _cast_blocks(src_refs, dst_refs):
    for src, dst in zip(src_refs, dst_refs):
        chunk = min(src.shape[0], CAST_ROWS)
        assert src.shape[0] % chunk == 0

        def body(i, carry, src=src, dst=dst, chunk=chunk):
            rows = pl.ds(pl.multiple_of(i * chunk, chunk), chunk)
            dst[rows, :] = src[rows, :].astype(dst.dtype)
            return carry
        lax.fori_loop(0, src.shape[0] // chunk, body, 0)


@functools.lru_cache(maxsize=None)
def _fft_constants(seq_len):
    n_fft = 2 * seq_len
    assert n_fft == FFT_N1 * FFT_N2
    k2 = np.arange(FFT_N2)
    fwd = []
    for k1 in range(FFT_SLICES):
        ang = -2.0 * np.pi * ((np.outer(FFT_N1 * k2 + k1, k2) % n_fft) / n_fft)
        er, ei = np.cos(ang), np.sin(ang)
        fwd.append(np.block([[er, -ei], [ei, er]]))
    fwd = np.stack(fwd, axis=0).astype(np.float32)
    inv = np.ascontiguousarray(np.transpose(fwd, (0, 2, 1)))
    return fwd, inv


@functools.lru_cache(maxsize=None)
def _filter_constants(seq_len, width):
    n_fft = 2 * seq_len
    t = np.linspace(0.0, 1.0, seq_len)
    bands = np.linspace(1e-4, FILTER_BANDS - 1, FILTER_BANDS)
    ang = (2.0 * np.pi / seq_len) * np.arange(seq_len)[:, None] * bands
    feats = np.concatenate([t[:, None], np.cos(ang), -np.sin(ang)], axis=-1)
    src = np.concatenate([np.arange(seq_len), [0], np.arange(seq_len - 1, 0, -1)])
    feats_ext = np.zeros((seq_len, 2 * LANES), np.float32)
    feats_ext[:, :FILTER_EMB] = feats[src[:seq_len]]
    feats_ext[:, LANES:LANES + FILTER_EMB] = feats[src[seq_len:]]
    t_mask = np.zeros((2, n_fft, LANES), np.float32)
    t_mask[0] = t[src][:, None]
    t_mask[1] = 1.0
    t_mask[1, seq_len] = 0.0
    deltas = np.abs(np.linspace(math.log(DECAY_TARGET) / SLOW_DECAY_PCT,
                                math.log(DECAY_TARGET) / FAST_DECAY_PCT, width))
    deltas = np.tile(deltas[None, :], (1, HYENA_ORDER)).astype(np.float32)
    return feats_ext, t_mask, deltas


@functools.lru_cache(maxsize=None)
def _attn_distance_tiles():
    n_off = Q_BLOCK // 2
    qi = np.arange(Q_BLOCK)[:, None]
    kj = np.arange(2 * Q_BLOCK)[None, :]
    tiles = []
    for shift in (0, n_off, 2 * n_off):
        off = np.abs(kj - shift - qi).astype(np.float32)
        tiles.append(np.where(off <= n_off, off, np.inf))
    off = np.abs(kj - qi).astype(np.float32)
    t3 = np.where(off <= n_off, off, np.inf)
    t3[:, Q_BLOCK:] = np.inf
    tiles.append(t3)
    def permute(n):
        return (4 * np.arange(n // 4)[None, :] + np.arange(4)[:, None]).reshape(-1)
    rows, cols = permute(Q_BLOCK), permute(2 * Q_BLOCK)
    tiles += [tiles[v][rows][:, cols] for v in range(3)]
    return np.stack(tiles, axis=0).astype(np.float32)


def _alibi_slopes():
    n = N_GROUPS * HEADS_PER_GROUP
    return [2.0 ** (-ALIBI_MAX_EXP * j / n) for j in range(1, n + 1)]


def _radix8_half(z0, z1, z2, z3):
    s02, d02 = z0 + z2, z0 - z2
    s13, d13 = z1 + z3, z1 - z3
    ss, dd = s13 * RSQRT2, d13 * RSQRT2
    return {0: (s02 + s13, None), 4: (s02 - s13, None), 2: (d02, -d13),
            1: (z0 + dd, -z2 - ss), 3: (z0 - dd, z2 - ss)}


def _add(a, b, sign=1.0):
    if b is None:
        return a
    if a is None:
        return b if sign > 0 else -b
    return a + b if sign > 0 else a - b


def _cmul_const(re, im, wr, wi):
    def scaled(v, w):
        if v is None or w == 0.0:
            return None
        return v if w == 1.0 else (-v if w == -1.0 else v * w)
    if re is not None and im is not None and wr != 0.0 and abs(wr) == abs(wi):
        if wi == wr:
            return (re - im) * wr, (re + im) * wr
        return (re + im) * wr, (im - re) * wr
    return _add(scaled(re, wr), scaled(im, wi), -1.0), _add(scaled(re, wi), scaled(im, wr))


def _radix16_stage(blocks):
    even = _radix8_half(*blocks[0::2])
    odd = _radix8_half(*blocks[1::2])
    half = FFT_N1 // 2
    out = [None] * FFT_SLICES
    out[0] = (even[0][0] + odd[0][0], None)
    out[half] = (even[0][0] - odd[0][0], None)
    out[half // 2] = (even[4][0], -odd[4][0])
    for k in range(1, half // 2):
        (er, ei), (pr, pi) = even[k], _cmul_const(*odd[k], *W16[k])
        out[k] = (er + pr, ei + pi)
        out[half - k] = (er - pr, pi - ei)
    return out


def _assemble8(c0, c4, c1, c2, c3):
    (c1r, c1i), (c2r, c2i), (c3r, c3i) = c1, c2, c3
    e, o = c0 + c4, c0 - c4
    return [e + c1r + c2r + c3r,
            o + (c1r - c1i - c3r - c3i) * RSQRT2 - c2i,
            e - c1i - c2r + c3i,
            o + (c3r - c3i - c1r - c1i) * RSQRT2 + c2i]


def _inverse_radix16_stage(c):
    half = FFT_N1 // 2
    plus = [(c[k][0] + c[half - k][0], c[k][1] - c[half - k][1]) for k in range(1, half // 2)]
    minus = [_cmul_const(c[k][0] - c[half - k][0], c[k][1] + c[half - k][1], W16[k][0], -W16[k][1])
             for k in range(1, half // 2)]
    mid = c[half // 2]
    even = _assemble8(c[0][0] + c[half][0], mid[0], *plus)
    odd = _assemble8(c[0][0] - c[half][0], -mid[1], *minus)
    return [blk for pair_ in zip(even, odd) for blk in pair_]


def _slice_rows(k1):
    if k1 == 0:
        return 0, FFT_N2
    if k1 == FFT_N1 // 2:
        return FFT_N2, FFT_N2
    return 2 * FFT_N2 * k1, 2 * FFT_N2


def _store_slices(ref, off, rows, values, scale=None):
    for k1, (re, im) in enumerate(values):
        r0, n = _slice_rows(k1)
        parts = (re,) if n == FFT_N2 else (re, im)
        for j, part in enumerate(parts):
            if scale is not None:
                part = part * scale
            ref[pl.ds(r0 + j * FFT_N2 + off, rows), :] = part.astype(ref.dtype)


def _load_slices(ref, off, rows):
    out = []
    for k1 in range(FFT_SLICES):
        r0, n = _slice_rows(k1)
        re = ref[pl.ds(r0 + off, rows), :]
        out.append((re, ref[pl.ds(r0 + FFT_N2 + off, rows), :] if n > FFT_N2 else None))
    return out


def _forward_dft(b_ref, k1, mf_ref):
    r0, n = _slice_rows(k1)
    b = b_ref[r0:r0 + n, :]
    return jnp.concatenate([_dot(mf_ref[k1, h0:h0 + FFT_N2, 0:n], b) for h0 in (0, FFT_N2)], axis=0)


def _inverse_dft(p, k1, mi_ref, c_ref):
    r0, n = _slice_rows(k1)
    for h0 in range(0, n, FFT_N2):
        c_ref[r0 + h0:r0 + h0 + FFT_N2, :] = _dot(mi_ref[k1, h0:h0 + FFT_N2, :], p)


FILT_CW = 256
FILT_ROWS = 256
A_CHUNK = 16


def _filter_kernel(n_cast, feats_ref, tmask_ref, w0_ref, b0_ref, wi_ref, bi_ref, freq_ref, wf_ref, wb_ref,
                   delta_ref, skip_ref, mf_ref, *rest):
    cast_in, (h_ref, *cast_out), (hid_ref, ts_ref, b_ref) = (
        rest[:n_cast], rest[n_cast:2 * n_cast + 1], rest[2 * n_cast + 1:])
    _cast_blocks(cast_in, cast_out)
    seq_len = feats_ref.shape[0]
    n_fft = 2 * seq_len

    @pl.when(pl.program_id(0) == 0)
    def _():
        def body(i, carry):
            rows = pl.ds(pl.multiple_of(i * FILT_ROWS, FILT_ROWS), FILT_ROWS)
            freq = freq_ref[...]
            hid = jnp.sin(freq * (_dot3(feats_ref[rows, :], w0_ref[...]) + b0_ref[...]))
            for layer in range(FILTER_INNER):
                hid = jnp.sin(freq * (_dot3(hid, wi_ref[layer]) + bi_ref[layer]))
            hid_ref[rows, :] = hid
            return carry
        lax.fori_loop(0, seq_len // FILT_ROWS, body, 0)

    h_hi, h_lo = _split_bf16(hid_ref[...])
    h_cat = jnp.concatenate([h_hi, h_lo], axis=1)
    for half, w_ref in enumerate((wf_ref, wb_ref)):
        w_hi, w_lo = _split_bf16(w_ref[...])
        ts_ref[half * seq_len:(half + 1) * seq_len, :] = (
            _dot(h_cat, jnp.concatenate([w_hi, w_hi], axis=0)) + _dot(h_hi, w_lo))

    def window_body(i, ssq):
        rows = pl.ds(pl.multiple_of(i * FILT_ROWS, FILT_ROWS), FILT_ROWS)
        widen = lambda a: jnp.concatenate([a] * (FILT_CW // LANES), axis=1)
        val = (ts_ref[rows, :] * jnp.exp(-widen(tmask_ref[0, rows, :]) * delta_ref[...])
               * widen(tmask_ref[1, rows, :]))
        ts_ref[rows, :] = val
        return ssq + jnp.sum(val * val, axis=0, keepdims=True)
    ssq = lax.fori_loop(0, n_fft // FILT_ROWS, window_body, jnp.zeros((1, FILT_CW), F32))
    scale = lax.rsqrt(ssq)

    def a_body(i, carry):
        off = pl.multiple_of(i * A_CHUNK, A_CHUNK)
        blk = [ts_ref[pl.ds(t1 * FFT_N2 + off, A_CHUNK), :] for t1 in range(FFT_N1)]
        lo = _radix16_stage(blk[:FFT_N1 // 2])
        hi = _radix16_stage(blk[FFT_N1 // 2:])
        vals = [(_add(lo[k][0], hi[k][0], 1.0 if k % 2 == 0 else -1.0),
                 _add(lo[k][1], hi[k][1], 1.0 if k % 2 == 0 else -1.0)) for k in range(FFT_SLICES)]
        _store_slices(b_ref, off, A_CHUNK, vals, scale)
        return carry
    lax.fori_loop(0, FFT_N2 // A_CHUNK, a_body, 0)

    for k1 in range(FFT_SLICES):
        s = (1.0 if k1 in (0, FFT_N1 // 2) else 2.0) / n_fft
        x = _forward_dft(b_ref, k1, mf_ref)
        h_ref[k1] = (jnp.concatenate([x[:FFT_N2] + skip_ref[...], x[FFT_N2:]], axis=0) * s).astype(h_ref.dtype)


def _filter_spectrum(seq_len, width, filt_w0, filt_b0, filt_w_inner, filt_b_inner, filt_w_out, filt_freq,
                     skip, casts=()):
    n_fft = 2 * seq_len
    hid = LANES
    feats_ext, t_mask, deltas = _filter_constants(seq_len, width)
    mf = jnp.asarray(_fft_constants(seq_len)[0]).astype(BF16)
    assert 2 * FILTER_HIDDEN == hid and FILTER_EMB <= LANES
    n_cols = HYENA_ORDER * width
    steps = n_cols // FILT_CW
    zeros = lambda r, c: jnp.zeros((r, c), F32)
    w0p = jnp.pad(filt_w0.astype(F32), ((0, LANES - FILTER_EMB), (0, 0)))
    w0 = jnp.block([[w0p, zeros(LANES, FILTER_HIDDEN)], [zeros(LANES, FILTER_HIDDEN), w0p]])
    b0 = jnp.tile(filt_b0.astype(F32), 2)[None, :]
    wi = jnp.stack([jnp.block([[w, zeros(FILTER_HIDDEN, FILTER_HIDDEN)], [zeros(FILTER_HIDDEN, FILTER_HIDDEN), w]])
                    for w in filt_w_inner.astype(F32)])
    bi = jnp.tile(filt_b_inner.astype(F32), (1, 2))[:, None, :]
    freq = jnp.tile(filt_freq.astype(F32), 2)[None, :]
    w_out = filt_w_out.astype(F32)
    wl = jnp.concatenate([jnp.pad(w_out[:, :n_cols], ((0, FILTER_HIDDEN), (0, 0))),
                          jnp.pad(w_out[:, n_cols:], ((FILTER_HIDDEN, 0), (0, 0)))], axis=1)
    full = lambda *shape: pl.BlockSpec(shape, lambda j: (0,) * len(shape))
    cast_specs, cast_shapes = _cast_plan(casts, steps)
    spectrum, *cast_out = pl.pallas_call(
        functools.partial(_filter_kernel, len(casts)),
        out_shape=[jax.ShapeDtypeStruct((FFT_SLICES, 2 * FFT_N2, n_cols), BF16)] + cast_shapes,
        grid=(steps,),
        in_specs=[
            full(seq_len, 2 * LANES), full(2, n_fft, LANES), full(2 * LANES, hid), full(1, hid),
            full(FILTER_INNER, hid, hid), full(FILTER_INNER, 1, hid), full(1, hid),
            pl.BlockSpec((hid, FILT_CW), lambda j: (0, j)),
            pl.BlockSpec((hid, FILT_CW), lambda j: (0, steps + j)),
            pl.BlockSpec((1, FILT_CW), lambda j: (0, j)),
            pl.BlockSpec((1, FILT_CW), lambda j: (0, j)),
            full(FFT_SLICES, 2 * FFT_N2, 2 * FFT_N2),
        ] + cast_specs,
        out_specs=[pl.BlockSpec((FFT_SLICES, 2 * FFT_N2, FILT_CW), lambda j: (0, 0, j))] + cast_specs,
        scratch_shapes=[pltpu.VMEM((seq_len, hid), F32), pltpu.VMEM((n_fft, FILT_CW), F32),
                        pltpu.VMEM((FFT_ROWS, FILT_CW), BF16)],
        compiler_params=pltpu.CompilerParams(dimension_semantics=("arbitrary",),
                                             vmem_limit_bytes=VMEM_LIMIT_BYTES),
        name="hyena_filter",
    )(feats_ext, t_mask, w0, b0, wi, bi, freq, wl, wl, deltas, skip.astype(F32).reshape(1, n_cols), mf, *casts)
    return spectrum, cast_out


HY_CW = 256
CONV_ROWS = 64
PAD_ROWS = 8
PROJ_SPLIT = 4


def _hyena_kernel(x_ref, wv_ref, w1_ref, w2_ref, bv_ref, b1_ref, b2_ref, cwv_ref, cw1_ref, cw2_ref,
                  cbv_ref, cb1_ref, cb2_ref, h0_ref, h1_ref, mf_ref, mi_ref,
                  out_ref, u_ref, p_ref, b_ref, c_ref):
    seq_len = x_ref.shape[0]
    width = HY_CW
    n2 = FFT_N2
    xb = x_ref[...]

    n_slabs = width // LANES
    zeros = jnp.zeros((PAD_ROWS, LANES), F32)
    for slot in range(u_ref.shape[0]):
        for j in range(n_slabs):
            u_ref[slot, j, 0:PAD_ROWS, :] = zeros
            u_ref[slot, j, PAD_ROWS + seq_len:2 * PAD_ROWS + seq_len, :] = zeros
    parts = ((wv_ref, bv_ref, cwv_ref, cbv_ref), (w1_ref, b1_ref, cw1_ref, cb1_ref),
             (w2_ref, b2_ref, cw2_ref, cb2_ref))

    def project(p, slot):
        w_ref, bias_ref, _, _ = parts[p]
        half = seq_len // PROJ_SPLIT
        for r0 in range(0, seq_len, half):
            u = _dot(xb[r0:r0 + half], w_ref[...]) + bias_ref[...]
            for j in range(n_slabs):
                u_ref[slot, j, PAD_ROWS + r0:PAD_ROWS + r0 + half, :] = u[:, j * LANES:(j + 1) * LANES]

    def short_conv(p, slot):
        _, _, cw_ref, cb_ref = parts[p]

        def conv_body(i, carry):
            start = pl.multiple_of(i * CONV_ROWS, CONV_ROWS)
            cw = cw_ref[...]
            cb = cb_ref[...]
            for j in range(n_slabs):
                lanes = slice(j * LANES, (j + 1) * LANES)
                taps = [u_ref[slot, j, pl.ds(start + PAD_ROWS - 1 + k, CONV_ROWS, stride=1), :]
                        for k in range(3)]
                p_ref[p, pl.ds(start, CONV_ROWS), lanes] = (
                    cw[0:1, lanes] * taps[0] + cw[1:2, lanes] * taps[1] + cw[2:3, lanes] * taps[2]
                    + cb[:, lanes])
            return carry
        lax.fori_loop(0, seq_len // CONV_ROWS, conv_body, 0, unroll=True)

    def spectral_product(h_ref):
        x_next = _forward_dft(b_ref, 0, mf_ref)
        for k1 in range(FFT_SLICES):
            x = x_next
            if k1 + 1 < FFT_SLICES:
                x_next = _forward_dft(b_ref, k1 + 1, mf_ref)
            xr, xi = x[:n2].astype(BF16), x[n2:].astype(BF16)
            hr = h_ref[k1, 0:n2, :]
            hi = h_ref[k1, n2:2 * n2, :]
            prod = jnp.concatenate([xr * hr - xi * hi, xr * hi + xi * hr], axis=0)
            _inverse_dft(prod, k1, mi_ref, c_ref)

    def gated_blocks(order, off):
        ys = _inverse_radix16_stage(_load_slices(c_ref, off, A_CHUNK))
        return [p_ref[order + 1, pl.ds(t1 * n2 + off, A_CHUNK), :] * y for t1, y in enumerate(ys)]

    project(0, 0)
    short_conv(0, 0)

    def a_body(i, carry):
        off = pl.multiple_of(i * A_CHUNK, A_CHUNK)
        blk = [p_ref[0, pl.ds(t1 * n2 + off, A_CHUNK), :] for t1 in range(FFT_N1 // 2)]
        _store_slices(b_ref, off, A_CHUNK, _radix16_stage(blk))
        return carry
    lax.fori_loop(0, n2 // A_CHUNK, a_body, 0, unroll=True)

    project(1, 0)
    project(2, 1)
    spectral_product(h0_ref)
    short_conv(1, 0)
    short_conv(2, 1)

    def mid_body(i, carry):
        off = pl.multiple_of(i * A_CHUNK, A_CHUNK)
        _store_slices(b_ref, off, A_CHUNK, _radix16_stage(gated_blocks(0, off)))
        return carry
    lax.fori_loop(0, n2 // A_CHUNK, mid_body, 0)

    spectral_product(h1_ref)

    def out_body(i, carry):
        off = pl.multiple_of(i * A_CHUNK, A_CHUNK)
        for t1, blk in enumerate(gated_blocks(1, off)):
            out_ref[pl.ds(t1 * n2 + off, A_CHUNK), :] = blk.astype(out_ref.dtype)
        return carry
    lax.fori_loop(0, n2 // A_CHUNK, out_body, 0)


def _hyena_branch(xb, w_in, b_in, col0, conv_w, conv_b, spectrum):
    bsz, seq_len, d_model = xb.shape
    width = spectrum.shape[-1] // HYENA_ORDER
    nblk = width // HY_CW
    mf, mi = (jnp.asarray(m).astype(BF16) for m in _fft_constants(seq_len))
    col = lambda part: (lambda c, b: (0, part * nblk + c))
    blk0, rem = divmod(col0, HY_CW)
    assert rem == 0
    in_col = lambda part: (lambda c, b: (0, blk0 + part * nblk + c))
    const = lambda *shape: pl.BlockSpec(shape, lambda c, b: (0,) * len(shape), pipeline_mode=pl.Buffered(1))
    spec_h = lambda order: pl.BlockSpec((FFT_SLICES, 2 * FFT_N2, HY_CW), lambda c, b: (0, 0, order * nblk + c),
                                        pipeline_mode=pl.Buffered(1))
    in_specs = [pl.BlockSpec((None, seq_len, d_model), lambda c, b: (b, 0, 0))]
    in_specs += [pl.BlockSpec((d_model, HY_CW), in_col(p)) for p in range(3)]
    in_specs += [pl.BlockSpec((1, HY_CW), in_col(p)) for p in range(3)]
    in_specs += [pl.BlockSpec((3, HY_CW), col(p)) for p in range(3)]
    in_specs += [pl.BlockSpec((1, HY_CW), col(p)) for p in range(3)]
    in_specs += [spec_h(0), spec_h(1),
                 const(FFT_SLICES, 2 * FFT_N2, 2 * FFT_N2), const(FFT_SLICES, 2 * FFT_N2, 2 * FFT_N2)]
    return pl.pallas_call(
        _hyena_kernel,
        out_shape=jax.ShapeDtypeStruct((bsz, seq_len, width), BF16),
        grid=(nblk, bsz),
        in_specs=in_specs,
        out_specs=pl.BlockSpec((None, seq_len, HY_CW), lambda c, b: (b, 0, c)),
        scratch_shapes=[pltpu.VMEM((2, HY_CW // LANES, seq_len + 2 * PAD_ROWS, LANES), F32),
                        pltpu.VMEM((3, seq_len, HY_CW), F32),
                        pltpu.VMEM((FFT_ROWS, HY_CW), BF16),
                        pltpu.VMEM((FFT_ROWS, HY_CW), F32)],
        compiler_params=pltpu.CompilerParams(dimension_semantics=("arbitrary", "arbitrary"),
                                             vmem_limit_bytes=VMEM_LIMIT_BYTES),
        name="hyena_branch",
    )(xb, w_in, w_in, w_in, b_in, b_in, b_in, conv_w, conv_w, conv_w, conv_b, conv_b, conv_b,
      spectrum, spectrum, mf, mi)


TOKEN_GROUP = 16
TOKEN_PITCH = TOKEN_GROUP + 1


def _padded_rows(seq_len):
    return (seq_len // TOKEN_GROUP) * TOKEN_PITCH


def _attn_kernel(x_ref, w_ref, b_ref, dist_ref, out_ref, qkv_ref, acc_ref, m_ref, l_ref,
                 s_ref, p_ref, al_ref):
    seq_len = x_ref.shape[0]
    xb = x_ref[...]
    slopes = _alibi_slopes()
    q_scale = 1.0 / math.sqrt(HEAD_DIM)
    heads = range(HEADS_PER_GROUP)

    def load(ref, lead, pieces):
        parts = [ref[(*lead, piece, slice(None))] for piece in pieces]
        return parts[0] if len(parts) == 1 else jnp.concatenate(parts, axis=0)

    def store(ref, lead, pieces, value):
        r0 = 0
        for piece in pieces:
            ref[(*lead, piece, slice(None))] = value[r0:r0 + piece.size]
            r0 += piece.size

    def stage_scores(blk, coefs):
        qr, kr, _, dist, kc = blk
        for h in heads:
            s_ref[h, :, 0:kc] = _dot_nt(load(qkv_ref, (0, h), qr).astype(BF16),
                                        load(qkv_ref, (1, h), kr).astype(BF16)) + dist * coefs[h]

    def stage_softmax(blk, first, last):
        _, _, ar, _, kc = blk
        s = [s_ref[h, :, 0:kc] for h in heads]
        m_blk = [jnp.max(s[h], axis=-1, keepdims=True) for h in heads]
        if first:
            m_new = [jnp.broadcast_to(m_blk[h], (Q_BLOCK, HEAD_DIM)) for h in heads]
        else:
            m_old = [load(m_ref, (h,), ar) for h in heads]
            m_new = [jnp.maximum(m_old[h], m_blk[h]) for h in heads]
        m_wide = [jnp.concatenate([m_new[h]] * (kc // HEAD_DIM), axis=1) for h in heads]
        for h in heads:
            p_ref[h, :, 0:kc] = jnp.exp(s[h] - m_wide[h]).astype(BF16)
            if not first:
                al_ref[h] = jnp.exp(m_old[h] - m_new[h])
            if not last:
                store(m_ref, (h,), ar, m_new[h])

    def stage_output(blk, first, last, out_rows):
        _, kr, ar, _, kc = blk
        ones = jnp.ones((kc, HEAD_DIM), BF16)
        pv = [_dot(p_ref[h, :, 0:kc], jnp.concatenate([load(qkv_ref, (2, h), kr).astype(BF16), ones], axis=1))
              for h in heads]
        acc = [r[:, :HEAD_DIM] for r in pv]
        l_new = [r[:, HEAD_DIM:] for r in pv]
        if not first:
            alpha = [al_ref[h] for h in heads]
            l_new = [alpha[h] * load(l_ref, (h,), ar) + l_new[h] for h in heads]
            acc = [alpha[h] * load(acc_ref, (h,), ar) + acc[h] for h in heads]
        for h in heads:
            if last:
                out_ref[out_rows, h * HEAD_DIM:(h + 1) * HEAD_DIM] = (acc[h] / l_new[h]).astype(out_ref.dtype)
            else:
                store(l_ref, (h,), ar, l_new[h])
                store(acc_ref, (h,), ar, acc[h])

    order = sorted(range(N_GROUPS), key=lambda g: -ATTN_GROUPS[g][1])
    assert ATTN_GROUPS[order[-1]][1] == 1 and ATTN_GROUPS[order[0]][1] == TOKEN_GROUP
    for pos, g in enumerate(order):
        window, dil = ATTN_GROUPS[g]
        first, last = pos == 0, pos == N_GROUPS - 1
        n_off = (window // 2) // dil
        assert n_off == Q_BLOCK // 2 and TOKEN_GROUP % dil == 0
        stream = seq_len // dil
        nb = stream // Q_BLOCK
        n_blocks = dil * nb
        sub = TOKEN_GROUP // dil
        for part in range(3):
            c0 = part * ATTN_QKV_WIDTH + g * GROUP_WIDTH
            proj = _dot(xb, w_ref[:, c0:c0 + GROUP_WIDTH]) + b_ref[:, c0:c0 + GROUP_WIDTH]
            if part == 0:
                proj = proj * q_scale
            for h in heads:
                slab = proj[:, h * HEAD_DIM:(h + 1) * HEAD_DIM]
                if dil == 1:
                    qkv_ref[part, h, 0:seq_len, :] = slab
                else:
                    for i in range(seq_len // TOKEN_GROUP):
                        qkv_ref[part, h, pl.ds(TOKEN_PITCH * i, TOKEN_GROUP, stride=1), :] = (
                            slab[TOKEN_GROUP * i:TOKEN_GROUP * (i + 1)])
        coefs = [-slopes[g * HEADS_PER_GROUP + h] * dil for h in heads]

        def block(i, dil=dil, nb=nb, stream=stream, sub=sub):
            def padded(c, first_pos, count):
                return [pl.ds(c + dil * r + TOKEN_PITCH * (first_pos // sub), count // sub, stride=TOKEN_PITCH)
                        for r in range(sub)]
            c = i // nb
            n = i % nb
            if nb == 1:
                rows = padded(i, 0, Q_BLOCK)
                return rows, rows, rows, dist_ref[3, :, 0:Q_BLOCK], Q_BLOCK, None
            lo = jnp.clip(n * Q_BLOCK - n_off, 0, stream - 2 * Q_BLOCK)
            variant = jnp.where(n == 0, 0, jnp.where(n == nb - 1, 2, 1))
            if dil > 1:
                rows = padded(c, n * Q_BLOCK, Q_BLOCK)
                return rows, padded(c, lo, 2 * Q_BLOCK), rows, dist_ref[4 + variant], 2 * Q_BLOCK, None
            q0 = pl.multiple_of(n * Q_BLOCK, Q_BLOCK)
            acc_rows = [pl.ds(TOKEN_PITCH * (n * (Q_BLOCK // TOKEN_GROUP) + j), TOKEN_GROUP, stride=1)
                        for j in range(Q_BLOCK // TOKEN_GROUP)]
            return ([pl.ds(q0, Q_BLOCK)], [pl.ds(pl.multiple_of(lo, n_off), 2 * Q_BLOCK)], acc_rows,
                    dist_ref[variant], 2 * Q_BLOCK, pl.ds(q0, Q_BLOCK))

        def steady(i, carry, block=block, coefs=coefs, first=first, last=last):
            old = block(i - 2)
            stage_output(old[:5], first, last, old[5])
            stage_softmax(block(i - 1)[:5], first, last)
            stage_scores(block(i)[:5], coefs)
            return carry

        assert n_blocks >= 3
        stage_scores(block(0)[:5], coefs)
        stage_softmax(block(0)[:5], first, last)
        stage_scores(block(1)[:5], coefs)
        lax.fori_loop(2, n_blocks, steady, 0)
        for i in (n_blocks - 2, n_blocks - 1):
            if i == n_blocks - 1:
                stage_softmax(block(i)[:5], first, last)
            blk = block(i)
            stage_output(blk[:5], first, last, blk[5])


def _attention_branch(xb, w_qkv, b_qkv):
    bsz, seq_len, d_model = xb.shape
    dist = _attn_distance_tiles()
    qkv_w = 3 * ATTN_QKV_WIDTH
    qkv_rows = _padded_rows(seq_len)
    return pl.pallas_call(
        _attn_kernel,
        out_shape=jax.ShapeDtypeStruct((bsz, seq_len, GROUP_WIDTH), BF16),
        grid=(bsz,),
        in_specs=[pl.BlockSpec((None, seq_len, d_model), lambda b: (b, 0, 0)),
                  pl.BlockSpec((d_model, qkv_w), lambda b: (0, 0), pipeline_mode=pl.Buffered(1)),
                  pl.BlockSpec((1, qkv_w), lambda b: (0, 0)),
                  pl.BlockSpec(dist.shape, lambda b: (0, 0, 0))],
        out_specs=pl.BlockSpec((None, seq_len, GROUP_WIDTH), lambda b: (b, 0, 0)),
        scratch_shapes=[pltpu.VMEM((3, HEADS_PER_GROUP, qkv_rows, HEAD_DIM), F32),
                        pltpu.VMEM((HEADS_PER_GROUP, qkv_rows, HEAD_DIM), F32),
                        pltpu.VMEM((HEADS_PER_GROUP, qkv_rows, HEAD_DIM), F32),
                        pltpu.VMEM((HEADS_PER_GROUP, qkv_rows, HEAD_DIM), F32),
                        pltpu.VMEM((HEADS_PER_GROUP, Q_BLOCK, 2 * Q_BLOCK), F32),
                        pltpu.VMEM((HEADS_PER_GROUP, Q_BLOCK, 2 * Q_BLOCK), BF16),
                        pltpu.VMEM((HEADS_PER_GROUP, Q_BLOCK, HEAD_DIM), F32)],
        compiler_params=pltpu.CompilerParams(dimension_semantics=("arbitrary",),
                                             vmem_limit_bytes=VMEM_LIMIT_BYTES),
        name="dilated_attention",
    )(xb, w_qkv, b_qkv, dist)


ROW_TILE = 1024
SUB_ROWS = 512


def _layer_norm(r, g, b):
    mu = jnp.mean(r, axis=-1, keepdims=True)
    d = r - mu
    var = jnp.mean(d * d, axis=-1, keepdims=True)
    return d * lax.rsqrt(var + LN_EPS) * g + b


def _sub_tiles(ref):
    return [slice(s * SUB_ROWS, (s + 1) * SUB_ROWS) for s in range(ref.shape[0] // SUB_ROWS)]


def _merge_kernel(alpha, n_gate, x_ref, ya_ref, yh_ref, *rest):
    wg_refs, (bg_ref, wa_ref, wh_ref, wo_ref, g_ref, b_ref, out_ref) = rest[:n_gate], rest[n_gate:]
    d_model = x_ref.shape[-1]
    gw = wg_refs[0].shape[1]
    tiles = _sub_tiles(x_ref)
    x = [x_ref[r, :] for r in tiles]
    xb = [xs.astype(BF16) for xs in x]
    gates = [jnp.concatenate([jax.nn.sigmoid(_dot(xs, w[...]) + bg_ref[:, j * gw:(j + 1) * gw])
                              for j, w in enumerate(wg_refs)], axis=1) for xs in xb]
    merged = [gt[:, :d_model] * _dot(ya_ref[r, :], wa_ref[...])
              + gt[:, d_model:] * _dot(yh_ref[r, :], wh_ref[...]) for gt, r in zip(gates, tiles)]
    mix = [_dot(mg.astype(BF16), wo_ref[...]) for mg in merged]
    for r, xs, mx in zip(tiles, x, mix):
        out_ref[r, :] = _layer_norm(alpha * xs + mx, g_ref[...], b_ref[...])


def _ffn_kernel(alpha, h_ref, w1_ref, b1_ref, w2_ref, b2_ref, g_ref, b_ref, out_ref):
    tiles = _sub_tiles(h_ref)
    h = [h_ref[r, :] for r in tiles]
    hid = [jnp.maximum(_dot(hs.astype(BF16), w1_ref[...]) + b1_ref[...], 0.0) for hs in h]
    ff = [_dot((hd * hd).astype(BF16), w2_ref[...]) + b2_ref[...] for hd in hid]
    for r, hs, fs in zip(tiles, h, ff):
        out_ref[r, :] = _layer_norm(alpha * hs + fs, g_ref[...], b_ref[...])


def _row_tiled_call(kernel_fn, name, rows, d_model, tiled, resident, casts=()):
    steps = rows // ROW_TILE
    n_main = len(tiled) + len(resident)
    cast_specs, cast_shapes = _cast_plan(casts, steps)

    def resident_spec(item):
        if isinstance(item, tuple):
            a, width, j = item
            return pl.BlockSpec((a.shape[0], width), lambda i: (0, j), pipeline_mode=pl.Buffered(1))
        return pl.BlockSpec(item.shape, lambda i: (0, 0), pipeline_mode=pl.Buffered(1))

    def body(*refs):
        cast_in, (out_ref, *cast_out) = refs[n_main:n_main + len(casts)], refs[n_main + len(casts):]
        _cast_blocks(cast_in, cast_out)
        kernel_fn(*refs[:n_main], out_ref)

    in_specs = [pl.BlockSpec((ROW_TILE, a.shape[1]), lambda i: (i, 0)) for a in tiled]
    in_specs += [resident_spec(item) for item in resident]
    resident = [item[0] if isinstance(item, tuple) else item for item in resident]
    out, *cast_out = pl.pallas_call(
        body,
        out_shape=[jax.ShapeDtypeStruct((rows, d_model), F32)] + cast_shapes,
        grid=(steps,),
        in_specs=in_specs + cast_specs,
        out_specs=[pl.BlockSpec((ROW_TILE, d_model), lambda i: (i, 0))] + cast_specs,
        compiler_params=pltpu.CompilerParams(dimension_semantics=("arbitrary",),
                                             vmem_limit_bytes=VMEM_LIMIT_BYTES),
        name=name,
    )(*tiled, *resident, *casts)
    return out, cast_out


def kernel(x, w_in, b_in, conv_w, conv_b, filt_w0, filt_b0, filt_w_inner, filt_b_inner, filt_w_out,
           filt_freq, hyena_skip, w_branch_attn, w_branch_hyena, w_out, ln1_g, ln1_b, w_ff1, b_ff1,
           w_ff2, b_ff2, ln2_g, ln2_b):
    bsz, seq_len, d_model = x.shape
    depth = w_in.shape[0]
    alpha = (2 * depth) ** 0.25
    rows = bsz * seq_len
    qkv_w = 3 * ATTN_QKV_WIDTH
    hy_w = (HYENA_ORDER + 1) * hyena_skip.shape[-1]
    row = lambda a: a.astype(F32)[None, :]
    h = x
    for layer in range(depth):
        w_l = w_in[layer].astype(BF16)
        b_l = b_in[layer].astype(F32)[None, :]
        h_rows = h.reshape(rows, d_model).astype(F32)
        spectrum, (hb, wa_l, wh_l, wo_l) = _filter_spectrum(
            seq_len, hyena_skip.shape[-1], filt_w0[layer], filt_b0[layer], filt_w_inner[layer],
            filt_b_inner[layer], filt_w_out[layer], filt_freq[layer], hyena_skip[layer],
            casts=[h_rows, w_branch_attn[layer].astype(F32), w_branch_hyena[layer].astype(F32),
                   w_out[layer].astype(F32)])
        hb = hb.reshape(bsz, seq_len, d_model)
        y_attn = _attention_branch(hb, w_l, b_l)
        y_hyena = _hyena_branch(hb, w_l, b_l, qkv_w,
                                conv_w[layer].astype(F32), conv_b[layer].astype(F32)[None, :], spectrum)
        gate0 = qkv_w + hy_w
        gate_w = math.gcd(gate0, 2 * d_model)
        gate_blocks = [(w_l, gate_w, gate0 // gate_w + j) for j in range(2 * d_model // gate_w)]
        h1, (w1_l, w2_l) = _row_tiled_call(
            functools.partial(_merge_kernel, alpha, len(gate_blocks)), "merge_ln", rows, d_model,
            [h_rows, y_attn.reshape(rows, -1), y_hyena.reshape(rows, -1)],
            gate_blocks + [b_l[:, gate0:], wa_l, wh_l, wo_l, row(ln1_g[layer]), row(ln1_b[layer])],
            casts=[w_ff1[layer].astype(F32), w_ff2[layer].astype(F32)])
        h2, _ = _row_tiled_call(
            functools.partial(_ffn_kernel, alpha), "ffn_ln", rows, d_model, [h1],
            [w1_l, row(b_ff1[layer]), w2_l, row(b_ff2[layer]), row(ln2_g[layer]), row(ln2_b[layer])])
        h = h2.reshape(bsz, seq_len, d_model)
    return h
```
